```python
import math
import jax, jax.numpy as jnp
from jax import lax
import numpy as np

D_MODEL = 2048
BATCH = 4
SEQ = 4096
DEPTH = 2

EPS = 1e-6
NEG = -1e30
F32 = jnp.float32

A_HEADS = 8
A_HEAD_DIM = 128
A_BLOCK = 256
A_TOPK = 3
A_QCHUNK = 64
T5_BUCKETS = 32
T5_MAX_DIST = 128
B_HEADS = 4
B_QK_DIM = 128
B_V_DIM = 256
B_CONV = 4
B_CHUNK = 64
C_HEADS = 8
C_Q_RANK = 512
C_KV_RANK = 256
C_NOPE = 128
C_ROPE = 64
C_V_DIM = 128
C_QBLOCK = 128
ROPE_THETA = 10000.0

A_W = A_HEADS * A_HEAD_DIM
B_QK_W = B_HEADS * B_QK_DIM
B_V_W = B_HEADS * B_V_DIM
C_W = C_HEADS * C_V_DIM
BRANCH_WIDTH = 1024
N_BRANCH = 3

SPLIT_SIZES = (A_W, A_W, A_W, A_W,
               B_QK_W, B_QK_W, B_V_W, B_HEADS, B_HEADS,
               B_V_W, B_V_W,
               C_Q_RANK, C_KV_RANK, C_ROPE, C_W,
               N_BRANCH * D_MODEL)
D_IN = int(sum(SPLIT_SIZES))
SPLIT_POINTS = tuple(int(v) for v in np.cumsum(SPLIT_SIZES)[:-1])

kernel_name = 'hybrid_moba_mlstm_mla_block'


def rmsnorm(x, g):
    xf = x.astype(F32)
    y = xf * lax.rsqrt(jnp.mean(xf * xf, axis=-1, keepdims=True) + EPS)
    return (y * g.astype(F32)).astype(x.dtype)


def to_heads(t, n_heads):
    b, s, w = t.shape
    return t.reshape(b, s, n_heads, w // n_heads).transpose(0, 2, 1, 3)


def from_heads(t):
    b, h, s, d = t.shape
    return t.transpose(0, 2, 1, 3).reshape(b, s, h * d)


def t5_bucket(dist):
    max_exact = T5_BUCKETS // 2
    d = jnp.maximum(dist, 0)
    large = max_exact + (jnp.log(jnp.maximum(d, 1).astype(F32) / max_exact)
                         / math.log(T5_MAX_DIST / max_exact) * (T5_BUCKETS - max_exact)).astype(jnp.int32)
    large = jnp.minimum(large, T5_BUCKETS - 1)
    return jnp.where(d < max_exact, d, large)


def rope(x, pos):
    half = x.shape[-1] // 2
    inv = ROPE_THETA ** (-jnp.arange(half, dtype=F32) / half)
    ang = pos.astype(F32)[:, None] * inv[None, :]
    cos, sin = jnp.cos(ang), jnp.sin(ang)
    x1 = x[..., :half].astype(F32)
    x2 = x[..., half:].astype(F32)
    return jnp.concatenate([x1 * cos - x2 * sin, x1 * sin + x2 * cos], axis=-1).astype(x.dtype)


def causal_conv(x, w, b):
    k = w.shape[0]
    s = x.shape[1]
    xp = jnp.pad(x, ((0, 0), (k - 1, 0), (0, 0)))
    return sum(xp[:, j:j + s] * w[j] for j in range(k)) + b


def moba_attention(q, k, v, t5_table):
    bsz, h, s, hd = q.shape
    nb = -(-s // A_BLOCK)
    pad = nb * A_BLOCK - s
    padw = ((0, 0), (0, 0), (0, pad), (0, 0))
    kb = jnp.pad(k, padw).reshape(bsz, h, nb, A_BLOCK, hd)
    vb = jnp.pad(v, padw).reshape(bsz, h, nb, A_BLOCK, hd)
    k_mean = jnp.mean(kb.astype(F32), axis=3)
    q_block = jnp.arange(s) // A_BLOCK
    gate = jnp.einsum('bhsd,bhnd->bhsn', q.astype(F32), k_mean)
    is_past = jnp.arange(nb)[None, :] < q_block[:, None]
    gate = jnp.where(is_past, gate, NEG)
    n_sel = min(A_TOPK, nb)
    _, sel = lax.top_k(gate, n_sel)
    nc = s // A_QCHUNK
    q_c = q.reshape(bsz, h, nc, A_QCHUNK, hd).transpose(2, 0, 1, 3, 4)
    sel_c = sel.reshape(bsz, h, nc, A_QCHUNK, n_sel).transpose(2, 0, 1, 3, 4)
    bias_table = t5_table.T.astype(F32)
    b_idx = jnp.arange(bsz)[:, None, None, None]
    h_idx = jnp.arange(h)[None, :, None, None]
    offs = jnp.arange(A_BLOCK)
    scale = hd ** -0.5

    def one_chunk(args):
        ci, qq, ss = args
        qpos = ci * A_QCHUNK + jnp.arange(A_QCHUNK)
        own = (ci * A_QCHUNK) // A_BLOCK
        k_sel = kb[b_idx, h_idx, ss]
        v_sel = vb[b_idx, h_idx, ss]
        kpos_sel = ss[..., None] * A_BLOCK + offs
        bias_sel = bias_table[h_idx[..., None], t5_bucket(qpos[:, None, None] - kpos_sel)]
        valid = (jnp.arange(n_sel)[None, :] < (qpos // A_BLOCK)[:, None])[:, :, None]
        s_sel = jnp.einsum('bhqd,bhqnkd->bhqnk', qq, k_sel, preferred_element_type=F32) * scale + bias_sel
        s_sel = jnp.where(valid, s_sel, NEG)
        k_own = lax.dynamic_index_in_dim(kb, own, axis=2, keepdims=False)
        v_own = lax.dynamic_index_in_dim(vb, own, axis=2, keepdims=False)
        dist_own = qpos[:, None] - (own * A_BLOCK + offs)[None, :]
        s_own = (jnp.einsum('bhqd,bhkd->bhqk', qq, k_own, preferred_element_type=F32) * scale
                 + bias_table[:, t5_bucket(dist_own)])
        s_own = jnp.where(dist_own >= 0, s_own, NEG)
        s_all = jnp.concatenate([s_sel.reshape(bsz, h, A_QCHUNK, n_sel * A_BLOCK), s_own], axis=-1)
        p = jax.nn.softmax(s_all, axis=-1)
        p_sel = p[..., :n_sel * A_BLOCK].reshape(bsz, h, A_QCHUNK, n_sel, A_BLOCK).astype(v.dtype)
        p_own = p[..., n_sel * A_BLOCK:].astype(v.dtype)
        return (jnp.einsum('bhqnk,bhqnkd->bhqd', p_sel, v_sel, preferred_element_type=F32)
                + jnp.einsum('bhqk,bhkd->bhqd', p_own, v_own, preferred_element_type=F32))

    out = lax.map(one_chunk, (jnp.arange(nc), q_c, sel_c))
    return out.transpose(1, 2, 0, 3, 4).reshape(bsz, h, s, hd)


def mlstm_chunkwise(q, k, v, i_pre, f_pre):
    bsz, h, s, dk = q.shape
    dv = v.shape[-1]
    L = B_CHUNK
    nc = s // L

    def chunks(t):
        return jnp.moveaxis(t.reshape(bsz, h, nc, L, *t.shape[3:]), 2, 0)

    qc = chunks(q.astype(F32))
    kc = chunks(k.astype(F32) * (dk ** -0.5))
    vc = chunks(v.astype(F32))
    lic = chunks(i_pre.astype(F32))
    lfc = chunks(jax.nn.log_sigmoid(f_pre.astype(F32)))
    causal = jnp.tril(jnp.ones((L, L), dtype=bool))

    def step(carry, xs):
        C, n, m = carry
        qq, kk, vv, li, lf = xs
        b = jnp.cumsum(lf, axis=-1)
        a = b + m[..., None]
        D = jnp.where(causal, b[..., :, None] - b[..., None, :] + li[..., None, :], -jnp.inf)
        m_t = jnp.maximum(a, jnp.max(D, axis=-1))
        w_inter = jnp.exp(a - m_t)
        sc = jnp.einsum('bhld,bhsd->bhls', qq, kk) * jnp.exp(D - m_t[..., None])
        num = (w_inter[..., None] * jnp.einsum('bhld,bhde->bhle', qq, C)
               + jnp.einsum('bhls,bhse->bhle', sc, vv))
        den = w_inter * jnp.einsum('bhld,bhd->bhl', qq, n) + jnp.sum(sc, axis=-1)
        hh = num / jnp.maximum(jnp.abs(den), jnp.exp(-m_t))[..., None]
        b_last = b[..., -1]
        g = b_last[..., None] - b + li
        m_new = jnp.maximum(b_last + m, jnp.max(g, axis=-1))
        decay = jnp.exp(b_last + m - m_new)
        wk = jnp.exp(g - m_new[..., None])[..., None] * kk
        C_new = decay[..., None, None] * C + jnp.einsum('bhsd,bhse->bhde', wk, vv)
        n_new = decay[..., None] * n + jnp.sum(wk, axis=-2)
        return (C_new, n_new, m_new), hh

    init = (jnp.zeros((bsz, h, dk, dv), F32), jnp.zeros((bsz, h, dk), F32), jnp.zeros((bsz, h), F32))
    _, hs = lax.scan(step, init, (qc, kc, vc, lic, lfc))
    return jnp.moveaxis(hs, 0, 2).reshape(bsz, h, s, dv)


def head_layernorm(hh, g):
    t = hh.transpose(0, 2, 1, 3).astype(F32)
    mu = jnp.mean(t, axis=-1, keepdims=True)
    var = jnp.mean(jnp.square(t - mu), axis=-1, keepdims=True)
    t = (t - mu) * lax.rsqrt(var + EPS)
    b, s, h, d = t.shape
    return t.reshape(b, s, h * d) * g.astype(F32)


def causal_block_attention(q, k, v):
    bsz, h, s, dq = q.shape
    nq = s // C_QBLOCK
    qb = q.reshape(bsz, h, nq, C_QBLOCK, dq).transpose(2, 0, 1, 3, 4)
    kpos = jnp.arange(s)
    scale = dq ** -0.5

    def blk(args):
        bi, q_b = args
        qpos = bi * C_QBLOCK + jnp.arange(C_QBLOCK)
        logits = jnp.einsum('bhqd,bhkd->bhqk', q_b, k, preferred_element_type=F32) * scale
        logits = jnp.where(kpos[None, :] <= qpos[:, None], logits, NEG)
        p = jax.nn.softmax(logits, axis=-1).astype(v.dtype)
        return jnp.einsum('bhqk,bhkd->bhqd', p, v, preferred_element_type=F32)

    out = lax.map(blk, (jnp.arange(nq), qb))
    return out.transpose(1, 2, 0, 3, 4).reshape(bsz, h, s, -1)


def mla_attention(cq, ckv, kr, q_norm, kv_norm, w_uq, w_ukv, pos):
    bsz, s, _ = cq.shape
    q = jnp.einsum('bsr,re->bse', rmsnorm(cq, q_norm), w_uq)
    q = q.reshape(bsz, s, C_HEADS, C_NOPE + C_ROPE).transpose(0, 2, 1, 3)
    kv = jnp.einsum('bsr,re->bse', rmsnorm(ckv, kv_norm), w_ukv)
    kv = kv.reshape(bsz, s, C_HEADS, C_NOPE + C_V_DIM).transpose(0, 2, 1, 3)
    q = jnp.concatenate([q[..., :C_NOPE], rope(q[..., C_NOPE:], pos)], axis=-1)
    k_rope = rope(kr[:, None], pos)
    k = jnp.concatenate([kv[..., :C_NOPE], jnp.broadcast_to(k_rope, (bsz, C_HEADS, s, C_ROPE))], axis=-1)
    v = kv[..., C_NOPE:]
    return causal_block_attention(q, k, v)


def setup_inputs(seed: int = 0) -> dict:
    key = jax.random.key(seed)
    ks = jax.random.split(key, 17)
    nrm = jax.random.normal
    return {
        'x': nrm(ks[0], (BATCH, SEQ, D_MODEL), F32),
        'norm_gain': 1.0 + 0.02 * nrm(ks[1], (DEPTH, D_MODEL), F32),
        'w_in': nrm(ks[2], (DEPTH, D_MODEL, D_IN), F32) * D_MODEL ** -0.5,
        't5_table': 0.5 * nrm(ks[3], (T5_BUCKETS, A_HEADS), F32),
        'mlstm_conv_w': nrm(ks[4], (DEPTH, B_CONV, 2 * B_QK_W), F32) * B_CONV ** -0.5,
        'mlstm_conv_b': 0.01 * nrm(ks[5], (DEPTH, 2 * B_QK_W), F32),
        'mlstm_i_bias': 0.1 * nrm(ks[6], (DEPTH, B_HEADS), F32),
        'mlstm_f_bias': jnp.linspace(3.0, 6.0, B_HEADS, dtype=F32)[None, :] + 0.1 * nrm(ks[7], (DEPTH, B_HEADS), F32),
        'mlstm_out_norm': 1.0 + 0.02 * nrm(ks[8], (DEPTH, B_V_W), F32),
        'mla_q_norm': 1.0 + 0.02 * nrm(ks[9], (DEPTH, C_Q_RANK), F32),
        'mla_kv_norm': 1.0 + 0.02 * nrm(ks[10], (DEPTH, C_KV_RANK), F32),
        'mla_w_uq': nrm(ks[11], (DEPTH, C_Q_RANK, C_HEADS * (C_NOPE + C_ROPE)), F32) * C_Q_RANK ** -0.5,
        'mla_w_ukv': nrm(ks[12], (DEPTH, C_KV_RANK, C_HEADS * (C_NOPE + C_V_DIM)), F32) * C_KV_RANK ** -0.5,
        'w_branch': nrm(ks[13], (DEPTH, N_BRANCH, BRANCH_WIDTH, D_MODEL), F32) * BRANCH_WIDTH ** -0.5,
        'w_out': nrm(ks[14], (DEPTH, D_MODEL, D_MODEL), F32) * D_MODEL ** -0.5,
        'final_norm': 1.0 + 0.02 * nrm(ks[15], (D_MODEL,), F32),
    }


def reference(x, norm_gain, w_in, t5_table, mlstm_conv_w, mlstm_conv_b, mlstm_i_bias, mlstm_f_bias,
              mlstm_out_norm, mla_q_norm, mla_kv_norm, mla_w_uq, mla_w_ukv, w_branch, w_out, final_norm):
    bsz, s, _ = x.shape
    pos = jnp.arange(s, dtype=jnp.int32)
    for l in range(DEPTH):
        h = rmsnorm(x, norm_gain[l])
        proj = jnp.einsum('bsd,de->bse', h, w_in[l])
        (qa, ka, va, za, qb, kb, vb, ib, fb, ob, zb, cq, ckv, kr, zc, gt) = jnp.split(proj, SPLIT_POINTS, axis=-1)
        ya = moba_attention(to_heads(qa, A_HEADS), to_heads(ka, A_HEADS), to_heads(va, A_HEADS), t5_table)
        ya = from_heads(ya) * jax.nn.silu(za.astype(F32))
        qk = jax.nn.silu(causal_conv(jnp.concatenate([qb, kb], axis=-1), mlstm_conv_w[l], mlstm_conv_b[l]))
        qb2, kb2 = jnp.split(qk, 2, axis=-1)
        i_pre = (ib + mlstm_i_bias[l]).transpose(0, 2, 1)
        f_pre = (fb + mlstm_f_bias[l]).transpose(0, 2, 1)
        hb = mlstm_chunkwise(to_heads(qb2, B_HEADS), to_heads(kb2, B_HEADS), to_heads(vb, B_HEADS), i_pre, f_pre)
        yb = (head_layernorm(hb, mlstm_out_norm[l]) * jax.nn.sigmoid(ob.astype(F32))
              * jax.nn.silu(zb.astype(F32)))
        hc = mla_attention(cq, ckv, kr, mla_q_norm[l], mla_kv_norm[l], mla_w_uq[l], mla_w_ukv[l], pos)
        yc = from_heads(hc) * jax.nn.silu(zc.astype(F32))
        ys = jnp.stack([ya, yb, yc], axis=2).astype(h.dtype)
        branch = jnp.einsum('bsnw,nwd->bsnd', ys, w_branch[l])
        gates = jax.nn.sigmoid(gt.reshape(bsz, s, N_BRANCH, D_MODEL))
        merged = jnp.sum(gates * branch, axis=2)
        x = (x + jnp.einsum('bsd,de->bse', merged, w_out[l])).astype(x.dtype)
    return rmsnorm(x, final_norm)
```

```python
import functools
import math

import jax
import jax.numpy as jnp
import numpy as np
from jax import lax
from jax.experimental import pallas as pl
from jax.experimental.pallas import tpu as pltpu

F32 = jnp.float32
BF16 = jnp.bfloat16

D_MODEL = 2048
DEPTH = 2
EPS = 1e-6
NEG = -1e30

A_HEADS = 8
A_HEAD_DIM = 128
A_BLOCK = 256
A_TOPK = 3
T5_BUCKETS = 32
T5_MAX_DIST = 128
B_HEADS = 4
B_QK_DIM = 128
B_V_DIM = 256
B_CONV = 4
C_HEADS = 8
C_Q_RANK = 512
C_KV_RANK = 256
C_NOPE = 128
C_ROPE = 64
C_V_DIM = 128
ROPE_THETA = 10000.0
N_BRANCH = 3

A_W = A_HEADS * A_HEAD_DIM
B_QK_W = B_HEADS * B_QK_DIM
B_V_W = B_HEADS * B_V_DIM
C_W = C_HEADS * C_V_DIM

LANES = 128
MXU_DIM = 256
VMEM_LIMIT_BYTES = 56 * 1024 * 1024

OFF_QA = 0
OFF_KA = OFF_QA + A_W
OFF_VA = OFF_KA + A_W
OFF_ZA = OFF_VA + A_W
OFF_QKB = OFF_ZA + A_W
OFF_VB = OFF_QKB + 2 * B_QK_W
OFF_OB = OFF_VB + B_V_W
OFF_ZB = OFF_OB + B_V_W
OFF_ZC = OFF_ZB + B_V_W
OFF_CQ = OFF_ZC + C_W
OFF_CKV = OFF_CQ + C_Q_RANK
OFF_IF = OFF_CKV + C_KV_RANK
OFF_KR = OFF_IF + LANES
OFF_GT = OFF_KR + LANES
D_SLAB = OFF_GT + N_BRANCH * D_MODEL
assert D_SLAB == 16384

MLA_QK_PAD = 2 * LANES


def _cparams(n_axes):
    return pltpu.CompilerParams(dimension_semantics=("arbitrary",) * n_axes,
                                vmem_limit_bytes=VMEM_LIMIT_BYTES)


def _silu(t):
    return t * jax.nn.sigmoid(t)


def _rmsnorm_body(x_ref, g_ref, o_ref):
    xf = x_ref[...]
    y = xf * lax.rsqrt(jnp.mean(xf * xf, axis=-1, keepdims=True) + EPS)
    o_ref[...] = (y * g_ref[...]).astype(o_ref.dtype)


def rmsnorm_rows(x2d, gain, tm=512):
    m, d = x2d.shape
    return pl.pallas_call(
        _rmsnorm_body,
        grid=(m // tm,),
        in_specs=[pl.BlockSpec((tm, d), lambda i: (i, 0)),
                  pl.BlockSpec((1, d), lambda i: (0, 0))],
        out_specs=pl.BlockSpec((tm, d), lambda i: (i, 0)),
        out_shape=jax.ShapeDtypeStruct((m, d), BF16),
        compiler_params=_cparams(1),
        name="rmsnorm",
    )(x2d, gain.reshape(1, d))


def _proj_body(h_ref, w_ref, o_ref):
    o_ref[...] = jnp.dot(h_ref[...], w_ref[...], preferred_element_type=F32).astype(o_ref.dtype)


def input_projection(h2d, w_slab, tm=1024, tn=1024):
    m, d = h2d.shape
    n = w_slab.shape[1]
    return pl.pallas_call(
        _proj_body,
        grid=(n // tn, m // tm),
        in_specs=[pl.BlockSpec((tm, d), lambda j, i: (i, 0)),
                  pl.BlockSpec((d, tn), lambda j, i: (0, j))],
        out_specs=pl.BlockSpec((tm, tn), lambda j, i: (i, j)),
        out_shape=jax.ShapeDtypeStruct((m, n), BF16),
        compiler_params=_cparams(2),
        name="input_proj",
    )(h2d, w_slab)


def _t5_thresholds():
    max_exact = T5_BUCKETS // 2
    d = np.arange(0, 4 * T5_MAX_DIST, dtype=np.int64)
    dd = np.maximum(d, 1).astype(np.float32)
    large = max_exact + (np.log(dd / np.float32(max_exact)) / np.float32(math.log(T5_MAX_DIST / max_exact))
                         * np.float32(T5_BUCKETS - max_exact)).astype(np.int32)
    large = np.minimum(large, T5_BUCKETS - 1)
    bucket = np.where(d < max_exact, d, large)
    assert np.all(np.diff(bucket) >= 0) and bucket[-1] == T5_BUCKETS - 1
    return [int(np.argmax(bucket >= k)) for k in range(T5_BUCKETS)]


_T5_THRESH = _t5_thresholds()


def _moba_body(t5_ref, q_ref, k_ref, v_ref, z_ref, o_ref, kmean_s, bias_own_s, bias_prev_s, *, n_blocks):
    h = pl.program_id(0)
    b = pl.program_id(1)
    i = pl.program_id(2)
    blk = A_BLOCK

    row = lax.broadcasted_iota(jnp.int32, (blk, blk), 0)
    col = lax.broadcasted_iota(jnp.int32, (blk, blk), 1)

    @pl.when((b == 0) & (i == 0))
    def _build_bias():
        for dist, dst in ((row - col, bias_own_s), (blk + row - col, bias_prev_s)):
            bias = jnp.full((blk, blk), t5_ref[0, h], F32)
            for kk in range(1, T5_BUCKETS):
                bias = jnp.where(dist >= _T5_THRESH[kk], t5_ref[kk, h], bias)
            dst[...] = bias

    @pl.when(i == 0)
    def _block_means():
        kmean_s[...] = jnp.zeros_like(kmean_s)
        for nb in range(n_blocks):
            kb = k_ref[nb * blk:(nb + 1) * blk, :].astype(F32)
            kmean_s[nb:nb + 1, :] = jnp.mean(kb, axis=0, keepdims=True)

    q = q_ref[...]

    lane = lax.broadcasted_iota(jnp.int32, (blk, LANES), 1)
    gate = lax.dot_general(q.astype(F32), kmean_s[...], (((1,), (1,)), ((), ())),
                           precision=lax.Precision.HIGHEST, preferred_element_type=F32)
    cur = jnp.where(lane < i, gate, NEG)
    sel = jnp.zeros((blk, LANES), F32)
    for _ in range(A_TOPK):
        mx = jnp.max(cur, axis=-1, keepdims=True)
        idx = jnp.min(jnp.where(cur == mx, lane, LANES), axis=-1, keepdims=True)
        pick = lane == idx
        sel = jnp.where(pick, 1.0, sel)
        cur = jnp.where(pick, -jnp.inf, cur)
    sel = jnp.where(lane < i, sel, 0.0)

    def scores(j):
        kj = k_ref[pl.ds(pl.multiple_of(j * blk, blk), blk), :]
        return lax.dot_general(q, kj, (((1,), (1,)), ((), ())), preferred_element_type=F32)

    def block_selected(j):
        return jnp.max(jnp.where(lane == j, sel, 0.0), axis=-1, keepdims=True) > 0.5

    def pv(p, j):
        vj = v_ref[pl.ds(pl.multiple_of(j * blk, blk), blk), :]
        return jnp.dot(p.astype(BF16), vj, preferred_element_type=F32)

    s = jnp.where(col <= row, scores(i) + bias_own_s[...], NEG)
    m = jnp.max(s, axis=-1, keepdims=True)
    p = jnp.exp(s - m)
    l = jnp.sum(p, axis=-1, keepdims=True)
    acc = pv(p, i)

    def update(carry, s, j):
        m, l, acc = carry
        m_new = jnp.maximum(m, jnp.max(s, axis=-1, keepdims=True))
        alpha = jnp.exp(m - m_new)
        p = jnp.exp(s - m_new)
        l = alpha * l + jnp.sum(p, axis=-1, keepdims=True)
        acc = alpha * acc + pv(p, j)
        return m_new, l, acc

    jp = jnp.maximum(i - 1, 0)
    s = jnp.where(block_selected(i - 1), scores(jp) + bias_prev_s[...], NEG)
    carry = update((m, l, acc), s, jp)

    far_bias = t5_ref[T5_BUCKETS - 1, h]

    def far_block(j, carry):
        s = jnp.where(block_selected(j), scores(j) + far_bias, NEG)
        return update(carry, s, j)

    m, l, acc = lax.fori_loop(0, jnp.maximum(i - 1, 0), far_block, carry)

    o_ref[...] = (acc / l * _silu(z_ref[...].astype(F32))).astype(o_ref.dtype)


def moba_branch(slab, t5_table):
    bsz, s, _ = slab.shape
    blk, hd = A_BLOCK, A_HEAD_DIM
    nb = s // blk
    assert nb * blk == s and nb <= LANES
    assert _T5_THRESH[T5_BUCKETS - 1] <= blk + 1
    cq, ck, cv, cz = (off // hd for off in (OFF_QA, OFF_KA, OFF_VA, OFF_ZA))
    return pl.pallas_call(
        functools.partial(_moba_body, n_blocks=nb),
        grid=(A_HEADS, bsz, nb),
        in_specs=[pl.BlockSpec(memory_space=pltpu.SMEM),
                  pl.BlockSpec((None, blk, hd), lambda h, b, i: (b, i, cq + h)),
                  pl.BlockSpec((None, s, hd), lambda h, b, i: (b, 0, ck + h)),
                  pl.BlockSpec((None, s, hd), lambda h, b, i: (b, 0, cv + h)),
                  pl.BlockSpec((None, blk, hd), lambda h, b, i: (b, i, cz + h))],
        out_specs=pl.BlockSpec((None, blk, hd), lambda h, b, i: (b, i, h)),
        out_shape=jax.ShapeDtypeStruct((bsz, s, A_W), BF16),
        scratch_shapes=[pltpu.VMEM((LANES, hd), F32),
                        pltpu.VMEM((blk, blk), F32),
                        pltpu.VMEM((blk, blk), F32)],
        compiler_params=_cparams(3),
        name="moba",
    )(t5_table, slab, slab, slab, slab)


MLSTM_CHUNK = 256
CONV_HALO = 8


def _log_sigmoid(t):
    return jnp.minimum(t, 0.0) - jnp.log1p(jnp.exp(-jnp.abs(t)))


def _mlstm_body(qk_ref, v_ref, ob_ref, zb_ref, if_ref, cw_ref, cb_ref, gb_ref, gn_ref, o_ref,
                xe_s, c_s, n_s, m_s):
    c = pl.program_id(1)
    L = MLSTM_CHUNK
    dk, dv = B_QK_DIM, B_V_DIM

    @pl.when(c == 0)
    def _reset():
        xe_s[0:CONV_HALO, :] = jnp.zeros((CONV_HALO, 2 * B_QK_W), F32)
        c_s[...] = jnp.zeros_like(c_s)
        n_s[...] = jnp.zeros_like(n_s)
        m_s[...] = jnp.zeros_like(m_s)

    xe_s[CONV_HALO:CONV_HALO + L, :] = qk_ref[...].astype(F32)
    conv = cb_ref[...]
    for j in range(B_CONV):
        conv = conv + cw_ref[j:j + 1, :] * xe_s[pl.ds(CONV_HALO - (B_CONV - 1) + j, L), :]
    xe_s[0:CONV_HALO, :] = xe_s[L:L + CONV_HALO, :]
    qk = _silu(conv)

    gi = if_ref[...].astype(F32) + gb_ref[...]
    lf = _log_sigmoid(gi)
    row = lax.broadcasted_iota(jnp.int32, (L, L), 0)
    col = lax.broadcasted_iota(jnp.int32, (L, L), 1)
    causal = col <= row
    tri = jnp.where(causal, 1.0, 0.0).astype(F32)
    b_cols = jnp.dot(tri, lf, precision=lax.Precision.HIGHEST, preferred_element_type=F32)
    b_rows = b_cols.T
    li_rows = gi.T

    for h in range(B_HEADS):
        q_h = qk[:, h * dk:(h + 1) * dk]
        k_h = qk[:, B_QK_W + h * dk:B_QK_W + (h + 1) * dk] * (dk ** -0.5)
        v_h = v_ref[:, h * dv:(h + 1) * dv]
        b_c = b_cols[:, B_HEADS + h:B_HEADS + h + 1]
        b_r = b_rows[B_HEADS + h:B_HEADS + h + 1, :]
        li_c = gi[:, h:h + 1]
        li_r = li_rows[h:h + 1, :]
        m_prev = m_s[h:h + 1, 0:1]
        c_prev = c_s[h]
        n_prev = n_s[h:h + 1, :]

        a_c = b_c + m_prev
        dmat = jnp.where(causal, b_c - b_r + li_r, -jnp.inf)
        m_t = jnp.maximum(a_c, jnp.max(dmat, axis=-1, keepdims=True))
        w_inter = jnp.exp(a_c - m_t)
        q_b = q_h.astype(BF16)
        k_b = k_h.astype(BF16)
        sc = (lax.dot_general(q_b, k_b, (((1,), (1,)), ((), ())), preferred_element_type=F32)
              * jnp.exp(dmat - m_t))
        num = (w_inter * jnp.dot(q_b, c_prev.astype(BF16), preferred_element_type=F32)
               + jnp.dot(sc.astype(BF16), v_h, preferred_element_type=F32))
        den = (w_inter * jnp.sum(q_h * n_prev, axis=-1, keepdims=True)
               + jnp.sum(sc, axis=-1, keepdims=True))
        hh = num / jnp.maximum(jnp.abs(den), jnp.exp(-m_t))

        b_last = b_c[L - 1:L, :]
        g_c = b_last - b_c + li_c
        m_new = jnp.maximum(b_last + m_prev, jnp.max(g_c, axis=0, keepdims=True))
        decay = jnp.exp(b_last + m_prev - m_new)
        wk = jnp.exp(g_c - m_new) * k_h
        c_s[h] = decay * c_prev + lax.dot_general(wk.astype(BF16), v_h, (((0,), (0,)), ((), ())),
                                                  preferred_element_type=F32)
        n_s[h:h + 1, :] = decay * n_prev + jnp.sum(wk, axis=0, keepdims=True)
        m_s[h:h + 1, :] = jnp.broadcast_to(m_new, (1, LANES))

        mu = jnp.mean(hh, axis=-1, keepdims=True)
        dlt = hh - mu
        var = jnp.mean(dlt * dlt, axis=-1, keepdims=True)
        y = dlt * lax.rsqrt(var + EPS) * gn_ref[:, h * dv:(h + 1) * dv]
        y = (y * jax.nn.sigmoid(ob_ref[:, h * dv:(h + 1) * dv].astype(F32))
             * _silu(zb_ref[:, h * dv:(h + 1) * dv].astype(F32)))
        o_ref[:, h * dv:(h + 1) * dv] = y.astype(o_ref.dtype)


def mlstm_branch(slab, conv_w, conv_b, i_bias, f_bias, out_norm):
    bsz, s, _ = slab.shape
    L = MLSTM_CHUNK
    assert s % L == 0
    gate_bias = jnp.zeros((1, LANES), F32).at[0, :B_HEADS].set(i_bias).at[0, B_HEADS:2 * B_HEADS].set(f_bias)
    w2 = 2 * B_QK_W
    full = lambda shape: pl.BlockSpec(shape, lambda b, c: (0,) * len(shape))
    return pl.pallas_call(
        _mlstm_body,
        grid=(bsz, s // L),
        in_specs=[pl.BlockSpec((None, L, w2), lambda b, c: (b, c, OFF_QKB // w2)),
                  pl.BlockSpec((None, L, B_V_W), lambda b, c: (b, c, OFF_VB // B_V_W)),
                  pl.BlockSpec((None, L, B_V_W), lambda b, c: (b, c, OFF_OB // B_V_W)),
                  pl.BlockSpec((None, L, B_V_W), lambda b, c: (b, c, OFF_ZB // B_V_W)),
                  pl.BlockSpec((None, L, LANES), lambda b, c: (b, c, OFF_IF // LANES)),
                  full((B_CONV, w2)), full((1, w2)), full((1, LANES)), full((1, B_V_W))],
        out_specs=pl.BlockSpec((None, L, B_V_W), lambda b, c: (b, c, 0)),
        out_shape=jax.ShapeDtypeStruct((bsz, s, B_V_W), BF16),
        scratch_shapes=[pltpu.VMEM((L + CONV_HALO, w2), F32),
                        pltpu.VMEM((B_HEADS, B_QK_DIM, B_V_DIM), F32),
                        pltpu.VMEM((8, B_QK_DIM), F32),
                        pltpu.VMEM((8, LANES), F32)],
        compiler_params=_cparams(2),
        name="mlstm",
    )(slab, slab, slab, slab, slab, conv_w, conv_b.reshape(1, w2), gate_bias, out_norm.reshape(1, B_V_W))


def _mla_prep_body(cq_ref, ckv_ref, kr_ref, qg_ref, kg_ref, wqn_ref, wqr_ref, wqs_ref, wkn_ref, wv_ref,
                   cos_ref, sin_ref, qf_ref, kf_ref, vf_ref):
    def normed(ref, g_ref):
        t = ref[...].astype(F32)
        y = t * lax.rsqrt(jnp.mean(t * t, axis=-1, keepdims=True) + EPS)
        return (y * g_ref[...]).astype(BF16)

    cqn = normed(cq_ref, qg_ref)
    ckvn = normed(ckv_ref, kg_ref)
    cos = cos_ref[...]
    sin = sin_ref[...]

    q_nope = jnp.dot(cqn, wqn_ref[...], preferred_element_type=F32)
    q_rope = jnp.dot(cqn, wqr_ref[...], preferred_element_type=F32)
    q_swap = jnp.dot(cqn, wqs_ref[...], preferred_element_type=F32)
    k_nope = jnp.dot(ckvn, wkn_ref[...], preferred_element_type=F32)
    vf_ref[...] = jnp.dot(ckvn, wv_ref[...], preferred_element_type=F32).astype(vf_ref.dtype)

    kr = kr_ref[...].astype(F32)
    k_rot = (kr * cos + pltpu.roll(kr, LANES // 2, axis=1) * sin).astype(kf_ref.dtype)

    for h in range(C_HEADS):
        lo = h * MLA_QK_PAD
        sl = slice(h * LANES, (h + 1) * LANES)
        qf_ref[:, lo:lo + LANES] = q_nope[:, sl].astype(qf_ref.dtype)
        qf_ref[:, lo + LANES:lo + 2 * LANES] = (q_rope[:, sl] * cos + q_swap[:, sl] * sin).astype(qf_ref.dtype)
        kf_ref[:, lo:lo + LANES] = k_nope[:, sl].astype(kf_ref.dtype)
        kf_ref[:, lo + LANES:lo + 2 * LANES] = k_rot


def mla_prep(slab, q_norm, kv_norm, w_uq, w_ukv, tm=512):
    bsz, s, _ = slab.shape
    half = C_ROPE // 2
    scale = (C_NOPE + C_ROPE) ** -0.5
    wq = (w_uq * scale).reshape(C_Q_RANK, C_HEADS, C_NOPE + C_ROPE)
    pad = jnp.zeros((C_Q_RANK, C_HEADS, LANES - C_ROPE), F32)
    x1, x2 = wq[..., C_NOPE:C_NOPE + half], wq[..., C_NOPE + half:]
    wqn = wq[..., :C_NOPE].reshape(C_Q_RANK, C_HEADS * LANES).astype(BF16)
    wqr = jnp.concatenate([x1, x2, pad], axis=-1).reshape(C_Q_RANK, C_HEADS * LANES).astype(BF16)
    wqs = jnp.concatenate([-x2, x1, pad], axis=-1).reshape(C_Q_RANK, C_HEADS * LANES).astype(BF16)
    wkv = w_ukv.reshape(C_KV_RANK, C_HEADS, C_NOPE + C_V_DIM)
    wkn = wkv[..., :C_NOPE].reshape(C_KV_RANK, C_HEADS * C_NOPE).astype(BF16)
    wv = wkv[..., C_NOPE:].reshape(C_KV_RANK, C_W).astype(BF16)

    pos = jnp.arange(s, dtype=jnp.int32)
    inv = ROPE_THETA ** (-jnp.arange(half, dtype=F32) / half)
    ang = pos.astype(F32)[:, None] * inv[None, :]
    zpad = jnp.zeros((s, LANES - C_ROPE), F32)
    cos_t = jnp.concatenate([jnp.cos(ang), jnp.cos(ang), zpad], axis=-1)
    sin_t = jnp.concatenate([jnp.sin(ang), jnp.sin(ang), zpad], axis=-1)

    full = lambda shape: pl.BlockSpec(shape, lambda b, i: (0,) * len(shape))
    return pl.pallas_call(
        _mla_prep_body,
        grid=(bsz, s // tm),
        in_specs=[pl.BlockSpec((None, tm, C_Q_RANK), lambda b, i: (b, i, OFF_CQ // C_Q_RANK)),
                  pl.BlockSpec((None, tm, C_KV_RANK), lambda b, i: (b, i, OFF_CKV // C_KV_RANK)),
                  pl.BlockSpec((None, tm, LANES), lambda b, i: (b, i, OFF_KR // LANES)),
                  full((1, C_Q_RANK)), full((1, C_KV_RANK)),
                  full(wqn.shape), full(wqr.shape), full(wqs.shape), full(wkn.shape), full(wv.shape),
                  pl.BlockSpec((tm, LANES), lambda b, i: (i, 0)),
                  pl.BlockSpec((tm, LANES), lambda b, i: (i, 0))],
        out_specs=[pl.BlockSpec((None, tm, C_HEADS * MLA_QK_PAD), lambda b, i: (b, i, 0)),
                   pl.BlockSpec((None, tm, C_HEADS * MLA_QK_PAD), lambda b, i: (b, i, 0)),
                   pl.BlockSpec((None, tm, C_W), lambda b, i: (b, i, 0))],
        out_shape=[jax.ShapeDtypeStruct((bsz, s, C_HEADS * MLA_QK_PAD), BF16),
                   jax.ShapeDtypeStruct((bsz, s, C_HEADS * MLA_QK_PAD), BF16),
                   jax.ShapeDtypeStruct((bsz, s, C_W), BF16)],
        compiler_params=_cparams(2),
        name="mla_prep",
    )(slab, slab, slab, q_norm.reshape(1, C_Q_RANK), kv_norm.reshape(1, C_KV_RANK),
      wqn, wqr, wqs, wkn, wv, cos_t, sin_t)


MLA_TILE = 256


def _mla_attn_body(q_ref, k_ref, v_ref, z_ref, o_ref):
    i = pl.program_id(2)
    t = MLA_TILE
    q = q_ref[...]

    def scores(j):
        kj = k_ref[pl.ds(pl.multiple_of(j * t, t), t), :]
        return lax.dot_general(q, kj, (((1,), (1,)), ((), ())), preferred_element_type=F32)

    def pv(p, j):
        vj = v_ref[pl.ds(pl.multiple_of(j * t, t), t), :]
        return jnp.dot(p.astype(BF16), vj, preferred_element_type=F32)

    row = lax.broadcasted_iota(jnp.int32, (t, t), 0)
    col = lax.broadcasted_iota(jnp.int32, (t, t), 1)
    s = jnp.where(col <= row, scores(i), NEG)
    m = jnp.max(s, axis=-1, keepdims=True)
    p = jnp.exp(s - m)
    l = jnp.sum(p, axis=-1, keepdims=True)
    acc = pv(p, i)

    def past_block(j, carry):
        m, l, acc = carry
        s = scores(j)
        m_new = jnp.maximum(m, jnp.max(s, axis=-1, keepdims=True))
        alpha = jnp.exp(m - m_new)
        p = jnp.exp(s - m_new)
        return m_new, alpha * l + jnp.sum(p, axis=-1, keepdims=True), alpha * acc + pv(p, j)

    m, l, acc = lax.fori_loop(0, i, past_block, (m, l, acc))
    o_ref[...] = (acc / l * _silu(z_ref[...].astype(F32))).astype(o_ref.dtype)


def mla_attention(qf, kf, vf, slab):
    bsz, s, _ = qf.shape
    t = MLA_TILE
    cz = OFF_ZC // C_V_DIM
    return pl.pallas_call(
        _mla_attn_body,
        grid=(bsz, C_HEADS, s // t),
        in_specs=[pl.BlockSpec((None, t, MLA_QK_PAD), lambda b, h, i: (b, i, h)),
                  pl.BlockSpec((None, s, MLA_QK_PAD), lambda b, h, i: (b, 0, h)),
                  pl.BlockSpec((None, s, C_V_DIM), lambda b, h, i: (b, 0, h)),
                  pl.BlockSpec((None, t, C_V_DIM), lambda b, h, i: (b, i, cz + h))],
        out_specs=pl.BlockSpec((None, t, C_V_DIM), lambda b, h, i: (b, i, h)),
        out_shape=jax.ShapeDtypeStruct((bsz, s, C_W), BF16),
        compiler_params=_cparams(3),
        name="mla_attn",
    )(qf, kf, vf, slab)


def _merge_body(ya_ref, yb_ref, yc_ref, wa_ref, wb_ref, wc_ref, ga_ref, gb_ref, gc_ref, o_ref):
    acc = None
    for y_ref, w_ref, g_ref in ((ya_ref, wa_ref, ga_ref), (yb_ref, wb_ref, gb_ref), (yc_ref, wc_ref, gc_ref)):
        term = (jax.nn.sigmoid(g_ref[...].astype(F32))
                * jnp.dot(y_ref[...], w_ref[...], preferred_element_type=F32))
        acc = term if acc is None else acc + term
    o_ref[...] = acc.astype(o_ref.dtype)


def branch_merge(ya, yb, yc, w_branch, slab2d, tm=1024, tn=1024):
    m, w = ya.shape
    d = w_branch.shape[-1]
    g0 = OFF_GT // tn
    gper = d // tn
    y_spec = pl.BlockSpec((tm, w), lambda j, i: (i, 0))
    w_specs = [pl.BlockSpec((None, w, tn), functools.partial(lambda j, i, n: (n, 0, j), n=n)) for n in range(N_BRANCH)]
    g_specs = [pl.BlockSpec((tm, tn), functools.partial(lambda j, i, n: (i, g0 + n * gper + j), n=n))
               for n in range(N_BRANCH)]
    return pl.pallas_call(
        _merge_body,
        grid=(d // tn, m // tm),
        in_specs=[y_spec, y_spec, y_spec] + w_specs + g_specs,
        out_specs=pl.BlockSpec((tm, tn), lambda j, i: (i, j)),
        out_shape=jax.ShapeDtypeStruct((m, d), BF16),
        compiler_params=_cparams(2),
        name="branch_merge",
    )(ya, yb, yc, w_branch, w_branch, w_branch, slab2d, slab2d, slab2d)


def _out_body(mg_ref, w_ref, x_ref, g_ref, *out_refs, last):
    x_new = x_ref[...] + jnp.dot(mg_ref[...], w_ref[...], preferred_element_type=F32)
    y = x_new * lax.rsqrt(jnp.mean(x_new * x_new, axis=-1, keepdims=True) + EPS) * g_ref[...]
    if last:
        out_refs[0][...] = y
    else:
        out_refs[0][...] = x_new
        out_refs[1][...] = y.astype(out_refs[1].dtype)


def out_projection(merged, w_out, x2d, gain, last, tm=512):
    m, d = x2d.shape
    row = pl.BlockSpec((tm, d), lambda i: (i, 0))
    if last:
        out_specs, out_shape = row, jax.ShapeDtypeStruct((m, d), F32)
    else:
        out_specs = [row, row]
        out_shape = [jax.ShapeDtypeStruct((m, d), F32), jax.ShapeDtypeStruct((m, d), BF16)]
    return pl.pallas_call(
        functools.partial(_out_body, last=last),
        grid=(m // tm,),
        in_specs=[row, pl.BlockSpec((d, d), lambda i: (0, 0)), row, pl.BlockSpec((1, d), lambda i: (0, 0))],
        out_specs=out_specs,
        out_shape=out_shape,
        compiler_params=_cparams(1),
        name="out_proj",
    )(merged, w_out, x2d, gain.reshape(1, d))


def _slab_weights(w_in_l):
    sizes = (A_W, A_W, A_W, A_W, B_QK_W, B_QK_W, B_V_W, B_HEADS, B_HEADS, B_V_W, B_V_W,
             C_Q_RANK, C_KV_RANK, C_ROPE, C_W, N_BRANCH * D_MODEL)
    points = np.cumsum(sizes)[:-1].tolist()
    (qa, ka, va, za, qb, kb, vb, ib, fb, ob, zb, cq, ckv, kr, zc, gt) = jnp.split(w_in_l, points, axis=1)
    d = w_in_l.shape[0]
    half = C_ROPE // 2
    if_group = jnp.concatenate([ib, fb, jnp.zeros((d, LANES - 2 * B_HEADS), F32)], axis=1)
    kr_group = jnp.concatenate([kr, -kr[:, half:], kr[:, :half]], axis=1)
    cols = [qa * (A_HEAD_DIM ** -0.5), ka, va, za, qb, kb, vb, ob, zb, zc, cq, ckv, if_group, kr_group, gt]
    return jnp.concatenate(cols, axis=1).astype(BF16)


def kernel(x, norm_gain, w_in, t5_table, mlstm_conv_w, mlstm_conv_b, mlstm_i_bias, mlstm_f_bias,
           mlstm_out_norm, mla_q_norm, mla_kv_norm, mla_w_uq, mla_w_ukv, w_branch, w_out, final_norm):
    bsz, s, d = x.shape
    m = bsz * s
    x2d = x.reshape(m, d)
    h = rmsnorm_rows(x2d, norm_gain[0])
    out = None
    for l in range(DEPTH):
        slab2d = input_projection(h, _slab_weights(w_in[l]))
        slab = slab2d.reshape(bsz, s, D_SLAB)
        ya = moba_branch(slab, t5_table)
        yb = mlstm_branch(slab, mlstm_conv_w[l], mlstm_conv_b[l], mlstm_i_bias[l], mlstm_f_bias[l],
                          mlstm_out_norm[l])
        qf, kf, vf = mla_prep(slab, mla_q_norm[l], mla_kv_norm[l], mla_w_uq[l], mla_w_ukv[l])
        yc = mla_attention(qf, kf, vf, slab)
        merged = branch_merge(ya.reshape(m, A_W), yb.reshape(m, B_V_W), yc.reshape(m, C_W),
                              w_branch[l].astype(BF16), slab2d)
        last = l == DEPTH - 1
        gain = final_norm if last else norm_gain[l + 1]
        res = out_projection(merged, w_out[l].astype(BF16), x2d, gain, last)
        if last:
            out = res
        else:
            x2d, h = res
    return out.reshape(bsz, s, d)
```

```python
import functools
import math

import jax
import jax.numpy as jnp
import numpy as np
from jax import lax
from jax.experimental import pallas as pl
from jax.experimental.pallas import tpu as pltpu

F32 = jnp.float32
BF16 = jnp.bfloat16

D_MODEL = 2048
DEPTH = 2
EPS = 1e-6
NEG = -1e30
LOG2E = math.log2(math.e)

A_HEADS = 8
A_HEAD_DIM = 128
A_BLOCK = 256
A_TOPK = 3
T5_BUCKETS = 32
T5_MAX_DIST = 128
B_HEADS = 4
B_QK_DIM = 128
B_V_DIM = 256
B_CONV = 4
C_HEADS = 8
C_Q_RANK = 512
C_KV_RANK = 256
C_NOPE = 128
C_ROPE = 64
C_V_DIM = 128
ROPE_THETA = 10000.0
N_BRANCH = 3

A_W = A_HEADS * A_HEAD_DIM
B_QK_W = B_HEADS * B_QK_DIM
B_V_W = B_HEADS * B_V_DIM
C_W = C_HEADS * C_V_DIM

LANES = 128
SUBLANES = 8
VMEM_LIMIT_BYTES = 56 * 1024 * 1024

OFF_KA = 0
OFF_ZA = OFF_KA + A_W
OFF_QKB = OFF_ZA + A_W
OFF_VB = OFF_QKB + 2 * B_QK_W
OFF_OB = OFF_VB + B_V_W
OFF_ZB = OFF_OB + B_V_W
OFF_ZC = OFF_ZB + B_V_W
OFF_CQ = OFF_ZC + C_W
OFF_CKV = OFF_CQ + C_Q_RANK
OFF_IF = OFF_CKV + C_KV_RANK
OFF_KR = OFF_IF + LANES
OFF_GT = OFF_KR + LANES
D_SLAB = OFF_GT + N_BRANCH * D_MODEL
assert D_SLAB == 14336

ATTN_TILE = 256
ATTN_QK_PAD = 2 * LANES
ATTN_HEADS_PER_STEP = 4
assert ATTN_TILE == A_BLOCK


def _cparams(n_axes):
    return pltpu.CompilerParams(dimension_semantics=("arbitrary",) * n_axes,
                                vmem_limit_bytes=VMEM_LIMIT_BYTES)


def _silu(t):
    return t * jax.nn.sigmoid(t)


_NT = (((1,), (1,)), ((), ()))


def _rmsnorm_body(x_ref, g_ref, o_ref):
    xf = x_ref[...]
    y = xf * lax.rsqrt(jnp.mean(xf * xf, axis=-1, keepdims=True) + EPS)
    o_ref[...] = (y * g_ref[...]).astype(o_ref.dtype)


def rmsnorm_rows(x2d, gain, tm=512):
    m, d = x2d.shape
    return pl.pallas_call(
        _rmsnorm_body,
        grid=(m // tm,),
        in_specs=[pl.BlockSpec((tm, d), lambda i: (i, 0)),
                  pl.BlockSpec((1, d), lambda i: (0, 0))],
        out_specs=pl.BlockSpec((tm, d), lambda i: (i, 0)),
        out_shape=jax.ShapeDtypeStruct((m, d), BF16),
        compiler_params=_cparams(1),
        name="rmsnorm",
    )(x2d, gain.reshape(1, d))


def _proj_body(h_ref, w_ref, o_ref):
    o_ref[...] = jnp.dot(h_ref[...], w_ref[...], preferred_element_type=F32).astype(o_ref.dtype)


def input_projection(h2d, w_slab, tm=1024, tn=1024):
    m, d = h2d.shape
    n = w_slab.shape[1]
    return pl.pallas_call(
        _proj_body,
        grid=(n // tn, m // tm),
        in_specs=[pl.BlockSpec((tm, d), lambda j, i: (i, 0)),
                  pl.BlockSpec((d, tn), lambda j, i: (0, j))],
        out_specs=pl.BlockSpec((tm, tn), lambda j, i: (i, j)),
        out_shape=jax.ShapeDtypeStruct((m, n), BF16),
        compiler_params=_cparams(2),
        name="input_proj",
    )(h2d, w_slab)


def _proj_t_body(h_ref, wt_ref, qt_ref, vt_ref, *, n_sub):
    t = ATTN_TILE
    res = lax.dot_general(wt_ref[...], h_ref[...], _NT, preferred_element_type=F32)
    for c in range(n_sub):
        qt_ref[c] = res[:A_W, c * t:(c + 1) * t].astype(qt_ref.dtype)
        vt_ref[c] = res[A_W:, c * t:(c + 1) * t].astype(vt_ref.dtype)


def moba_qv_projection(h2d, wt_qv, bsz, tm=512):
    m, d = h2d.shape
    t = ATTN_TILE
    tm = min(tm, m // bsz)
    n_sub = tm // t
    nt = m // bsz // t
    steps = m // bsz // tm
    out = jax.ShapeDtypeStruct((bsz, nt, A_W, t), BF16)
    out_spec = pl.BlockSpec((None, n_sub, A_W, t), lambda b, i: (b, i, 0, 0))
    return pl.pallas_call(
        functools.partial(_proj_t_body, n_sub=n_sub),
        grid=(bsz, steps),
        in_specs=[pl.BlockSpec((tm, d), lambda b, i: (b * steps + i, 0)),
                  pl.BlockSpec((2 * A_W, d), lambda b, i: (0, 0))],
        out_specs=[out_spec, out_spec],
        out_shape=[out, out],
        compiler_params=_cparams(2),
        name="moba_qv_proj",
    )(h2d, wt_qv)


def _softmax_init(m_s, l_s, acc_s):
    m_s[...] = jnp.full(m_s.shape, -jnp.inf, F32)
    l_s[...] = jnp.zeros(l_s.shape, F32)
    acc_s[...] = jnp.zeros(acc_s.shape, F32)


def _softmax_step_t(s, vt_blk, g, m_s, l_s, acc_s):
    m_prev = m_s[g]
    m_new = jnp.maximum(m_prev, jnp.max(s, axis=0, keepdims=True))
    alpha = jnp.exp2(m_prev - m_new)
    p = jnp.exp2(s - m_new)
    l_s[g] = alpha * l_s[g] + jnp.sum(p, axis=0, keepdims=True)
    acc_s[g] = alpha * acc_s[g] + jnp.dot(vt_blk, p.astype(BF16), preferred_element_type=F32)
    m_s[g] = m_new


def _attn_scratch(g_heads, t, dv):
    state = [pltpu.VMEM((g_heads, 1, t), F32), pltpu.VMEM((g_heads, 1, t), F32), pltpu.VMEM((g_heads, dv, t), F32)]
    return state + [pltpu.VMEM((2, t, t), F32)] * g_heads


def _skewed_blocks(lead, n_uniform, uniform_scores, consume):
    lead[0][0](0)
    pend_blk, pend_slot = lead[0][1], 0
    for fn, blk in lead[1:]:
        fn(1 - pend_slot)
        consume(pend_blk, pend_slot)
        pend_blk, pend_slot = blk, 1 - pend_slot
    a, b = pend_slot, 1 - pend_slot

    def two_blocks(jj, pend):
        uniform_scores(2 * jj, b)
        consume(pend, a)
        uniform_scores(2 * jj + 1, a)
        consume(2 * jj, b)
        return 2 * jj + 1

    pend = lax.fori_loop(0, n_uniform // 2, two_blocks, pend_blk)

    @pl.when(n_uniform % 2 == 1)
    def _odd_tail():
        uniform_scores(n_uniform - 1, b)
        consume(pend, a)
        consume(n_uniform - 1, b)

    @pl.when(n_uniform % 2 == 0)
    def _even_tail():
        consume(pend, a)


def _attn_finish(z_ref, o_ref, l_s, acc_s, g_heads, dv):
    for g in range(g_heads):
        out_t = acc_s[g] / l_s[g]
        o_ref[:, g * dv:(g + 1) * dv] = (out_t.T * _silu(z_ref[:, g * dv:(g + 1) * dv].astype(F32))
                                         ).astype(o_ref.dtype)


def _t5_thresholds():
    max_exact = T5_BUCKETS // 2
    d = np.arange(0, 4 * T5_MAX_DIST, dtype=np.int64)
    dd = np.maximum(d, 1).astype(np.float32)
    large = max_exact + (np.log(dd / np.float32(max_exact)) / np.float32(math.log(T5_MAX_DIST / max_exact))
                         * np.float32(T5_BUCKETS - max_exact)).astype(np.int32)
    large = np.minimum(large, T5_BUCKETS - 1)
    bucket = np.where(d < max_exact, d, large)
    assert np.all(np.diff(bucket) >= 0) and bucket[-1] == T5_BUCKETS - 1
    return [int(np.argmax(bucket >= k)) for k in range(T5_BUCKETS)]


_T5_THRESH = _t5_thresholds()

MASK_BIG = 1e30


def _moba_body(t5_ref, qt_ref, k_ref, vt_ref, z_ref, o_ref,
               kaug_s, km_s, bias_own_s, bias_prev_s, qa_s, m_s, l_s, acc_s, *s_refs, n_blocks, g_heads):
    hg = pl.program_id(0)
    b = pl.program_id(1)
    i = pl.program_id(2)
    t, hd = A_BLOCK, A_HEAD_DIM
    seq = n_blocks * t
    nb_pad = -(-n_blocks // SUBLANES) * SUBLANES

    @pl.when((hg == 0) & (b == 0) & (i == 0))
    def _mask_columns():
        key_blk = lax.broadcasted_iota(jnp.int32, (seq, LANES), 0) // t
        lane = lax.broadcasted_iota(jnp.int32, (seq, LANES), 1)
        pattern = jnp.where(lane == key_blk, -MASK_BIG, 0.0).astype(kaug_s.dtype)
        for g in range(g_heads):
            kaug_s[g, :, hd:] = pattern

    @pl.when((b == 0) & (i == 0))
    def _build_bias():
        key = lax.broadcasted_iota(jnp.int32, (t, t), 0)
        qry = lax.broadcasted_iota(jnp.int32, (t, t), 1)
        for g in range(g_heads):
            head = hg * g_heads + g
            far = t5_ref[T5_BUCKETS - 1, head]
            for dist, dst in ((qry - key, bias_own_s), (t + qry - key, bias_prev_s)):
                bias = jnp.full((t, t), t5_ref[0, head], F32)
                for kk in range(1, T5_BUCKETS):
                    bias = jnp.where(dist >= _T5_THRESH[kk], t5_ref[kk, head], bias)
                dst[g] = (bias - far) * LOG2E

    @pl.when(i == 0)
    def _new_sequence():
        for g in range(g_heads):
            kaug_s[g, :, :hd] = k_ref[:, g * hd:(g + 1) * hd]
            km_s[g] = jnp.zeros(km_s.shape[1:], km_s.dtype)
            for nb in range(n_blocks):
                mean = jnp.mean(k_ref[nb * t:(nb + 1) * t, g * hd:(g + 1) * hd].astype(F32), axis=0, keepdims=True)
                hi = mean.astype(BF16)
                km_s[g, nb:nb + 1, :] = hi
                km_s[g, LANES + nb:LANES + nb + 1, :] = (mean - hi.astype(F32)).astype(BF16)

    blk_row = lax.broadcasted_iota(jnp.int32, (nb_pad, t), 0)
    past = blk_row < i
    for g in range(g_heads):
        qt = qt_ref[g * hd:(g + 1) * hd, :]
        g2 = jnp.dot(km_s[g], qt, preferred_element_type=F32)
        gate = g2[:nb_pad, :] + g2[LANES:LANES + nb_pad, :]
        cur = jnp.where(past, gate, NEG)
        keep = blk_row == i
        for _ in range(A_TOPK):
            mx = jnp.max(cur, axis=0, keepdims=True)
            idx = jnp.min(jnp.where(cur == mx, blk_row, nb_pad), axis=0, keepdims=True)
            pick = blk_row == idx
            keep = keep | (pick & past)
            cur = jnp.where(pick, -jnp.inf, cur)
        unsel = jnp.where(keep, 0.0, 1.0)
        qa_s[g, :hd, :] = qt
        qa_s[g, hd:, :] = jnp.concatenate([unsel, jnp.zeros((LANES - nb_pad, t), F32)], axis=0).astype(qa_s.dtype)

    _softmax_init(m_s, l_s, acc_s)

    def raw_scores(g, j):
        start = pl.multiple_of(j * t, t)
        return jnp.dot(kaug_s[g, pl.ds(start, t), :], qa_s[g], preferred_element_type=F32)

    def own_scores(slot):
        key = lax.broadcasted_iota(jnp.int32, (t, t), 0)
        qry = lax.broadcasted_iota(jnp.int32, (t, t), 1)
        for g in range(g_heads):
            s_refs[g][slot] = jnp.where(key <= qry, raw_scores(g, i) + bias_own_s[g], NEG)

    j_prev = jnp.maximum(i - 1, 0)
    no_prev = jnp.where(i == 0, -MASK_BIG, 0.0)

    def prev_scores(slot):
        for g in range(g_heads):
            s_refs[g][slot] = raw_scores(g, j_prev) + (bias_prev_s[g] + no_prev)

    def far_scores(j, slot):
        for g in range(g_heads):
            s_refs[g][slot] = raw_scores(g, j)

    def consume(j, slot):
        for g in range(g_heads):
            _softmax_step_t(s_refs[g][slot], vt_ref[j, g * hd:(g + 1) * hd, :], g, m_s, l_s, acc_s)

    _skewed_blocks([(own_scores, i), (prev_scores, j_prev)], j_prev, far_scores, consume)
    _attn_finish(z_ref, o_ref, l_s, acc_s, g_heads, hd)


def moba_branch(qt, vt, slab, t5_table, g_heads=ATTN_HEADS_PER_STEP):
    bsz, nb, _, t = qt.shape
    s = nb * t
    hd = A_HEAD_DIM
    gw = g_heads * hd
    assert t == A_BLOCK and nb <= LANES and A_HEADS % g_heads == 0
    assert OFF_KA % gw == 0 and OFF_ZA % gw == 0
    assert _T5_THRESH[T5_BUCKETS - 1] <= t + 1
    ck, cz = OFF_KA // gw, OFF_ZA // gw
    return pl.pallas_call(
        functools.partial(_moba_body, n_blocks=nb, g_heads=g_heads),
        grid=(A_HEADS // g_heads, bsz, nb),
        in_specs=[pl.BlockSpec(memory_space=pltpu.SMEM),
                  pl.BlockSpec((None, None, gw, t), lambda h, b, i: (b, i, h, 0)),
                  pl.BlockSpec((None, s, gw), lambda h, b, i: (b, 0, ck + h)),
                  pl.BlockSpec((None, nb, gw, t), lambda h, b, i: (b, 0, h, 0)),
                  pl.BlockSpec((None, t, gw), lambda h, b, i: (b, i, cz + h))],
        out_specs=pl.BlockSpec((None, t, gw), lambda h, b, i: (b, i, h)),
        out_shape=jax.ShapeDtypeStruct((bsz, s, A_W), BF16),
        scratch_shapes=[pltpu.VMEM((g_heads, s, ATTN_QK_PAD), BF16),
                        pltpu.VMEM((g_heads, 2 * LANES, hd), BF16),
                        pltpu.VMEM((g_heads, t, t), F32),
                        pltpu.VMEM((g_heads, t, t), F32),
                        pltpu.VMEM((g_heads, ATTN_QK_PAD, t), BF16)] + _attn_scratch(g_heads, t, hd),
        compiler_params=_cparams(3),
        name="moba",
    )(t5_table, qt, slab, vt, slab)


MLSTM_CHUNK = 256
CONV_HALO = 8


def _log_sigmoid(t):
    return jnp.minimum(t, 0.0) - jnp.log1p(jnp.exp(-jnp.abs(t)))


def _mlstm_body(qk_ref, v_ref, ob_ref, zb_ref, if_ref, cw_ref, cb_ref, gb_ref, gn_ref, o_ref,
                xe_s, c_s, n_s, m_s):
    c = pl.program_id(1)
    L = MLSTM_CHUNK
    dk, dv = B_QK_DIM, B_V_DIM

    @pl.when(c == 0)
    def _reset():
        xe_s[0:CONV_HALO, :] = jnp.zeros((CONV_HALO, 2 * B_QK_W), F32)
        c_s[...] = jnp.zeros_like(c_s)
        n_s[...] = jnp.zeros_like(n_s)
        m_s[...] = jnp.zeros_like(m_s)

    xe_s[CONV_HALO:CONV_HALO + L, :] = qk_ref[...].astype(F32)
    conv = cb_ref[...]
    for j in range(B_CONV):
        conv = conv + cw_ref[j:j + 1, :] * xe_s[pl.ds(CONV_HALO - (B_CONV - 1) + j, L), :]
    xe_s[0:CONV_HALO, :] = xe_s[L:L + CONV_HALO, :]
    qk = _silu(conv)

    gi = if_ref[...].astype(F32) + gb_ref[...]
    lf = _log_sigmoid(gi)
    row = lax.broadcasted_iota(jnp.int32, (L, L), 0)
    col = lax.broadcasted_iota(jnp.int32, (L, L), 1)
    causal = col <= row
    tri = jnp.where(causal, 1.0, 0.0).astype(F32)
    b_cols = jnp.dot(tri, lf, precision=lax.Precision.HIGHEST, preferred_element_type=F32)
    b_rows = b_cols.T
    li_rows = gi.T

    for h in range(B_HEADS):
        q_h = qk[:, h * dk:(h + 1) * dk]
        k_h = qk[:, B_QK_W + h * dk:B_QK_W + (h + 1) * dk] * (dk ** -0.5)
        v_h = v_ref[:, h * dv:(h + 1) * dv]
        b_c = b_cols[:, B_HEADS + h:B_HEADS + h + 1]
        b_r = b_rows[B_HEADS + h:B_HEADS + h + 1, :]
        li_c = gi[:, h:h + 1]
        li_r = li_rows[h:h + 1, :]
        m_prev = m_s[h:h + 1, 0:1]
        c_prev = c_s[h]
        n_prev = n_s[h:h + 1, :]

        a_c = b_c + m_prev
        dmat = jnp.where(causal, b_c - b_r + li_r, -jnp.inf)
        m_t = jnp.maximum(a_c, jnp.max(dmat, axis=-1, keepdims=True))
        w_inter = jnp.exp(a_c - m_t)
        q_b = q_h.astype(BF16)
        k_b = k_h.astype(BF16)
        sc = (lax.dot_general(q_b, k_b, _NT, preferred_element_type=F32)
              * jnp.exp(dmat - m_t))
        num = (w_inter * jnp.dot(q_b, c_prev.astype(BF16), preferred_element_type=F32)
               + jnp.dot(sc.astype(BF16), v_h, preferred_element_type=F32))
        den = (w_inter * jnp.sum(q_h * n_prev, axis=-1, keepdims=True)
               + jnp.sum(sc, axis=-1, keepdims=True))
        hh = num / jnp.maximum(jnp.abs(den), jnp.exp(-m_t))

        b_last = b_c[L - 1:L, :]
        g_c = b_last - b_c + li_c
        m_new = jnp.maximum(b_last + m_prev, jnp.max(g_c, axis=0, keepdims=True))
        decay = jnp.exp(b_last + m_prev - m_new)
        wk = jnp.exp(g_c - m_new) * k_h
        c_s[h] = decay * c_prev + lax.dot_general(wk.astype(BF16), v_h, (((0,), (0,)), ((), ())),
                                                  preferred_element_type=F32)
        n_s[h:h + 1, :] = decay * n_prev + jnp.sum(wk, axis=0, keepdims=True)
        m_s[h:h + 1, :] = jnp.broadcast_to(m_new, (1, LANES))

        mu = jnp.mean(hh, axis=-1, keepdims=True)
        dlt = hh - mu
        var = jnp.mean(dlt * dlt, axis=-1, keepdims=True)
        y = dlt * lax.rsqrt(var + EPS) * gn_ref[:, h * dv:(h + 1) * dv]
        y = (y * jax.nn.sigmoid(ob_ref[:, h * dv:(h + 1) * dv].astype(F32))
             * _silu(zb_ref[:, h * dv:(h + 1) * dv].astype(F32)))
        o_ref[:, h * dv:(h + 1) * dv] = y.astype(o_ref.dtype)


def mlstm_branch(slab, conv_w, conv_b, i_bias, f_bias, out_norm):
    bsz, s, _ = slab.shape
    L = MLSTM_CHUNK
    assert s % L == 0
    gate_bias = jnp.zeros((1, LANES), F32).at[0, :B_HEADS].set(i_bias).at[0, B_HEADS:2 * B_HEADS].set(f_bias)
    w2 = 2 * B_QK_W
    full = lambda shape: pl.BlockSpec(shape, lambda b, c: (0,) * len(shape))
    return pl.pallas_call(
        _mlstm_body,
        grid=(bsz, s // L),
        in_specs=[pl.BlockSpec((None, L, w2), lambda b, c: (b, c, OFF_QKB // w2)),
                  pl.BlockSpec((None, L, B_V_W), lambda b, c: (b, c, OFF_VB // B_V_W)),
                  pl.BlockSpec((None, L, B_V_W), lambda b, c: (b, c, OFF_OB // B_V_W)),
                  pl.BlockSpec((None, L, B_V_W), lambda b, c: (b, c, OFF_ZB // B_V_W)),
                  pl.BlockSpec((None, L, LANES), lambda b, c: (b, c, OFF_IF // LANES)),
                  full((B_CONV, w2)), full((1, w2)), full((1, LANES)), full((1, B_V_W))],
        out_specs=pl.BlockSpec((None, L, B_V_W), lambda b, c: (b, c, 0)),
        out_shape=jax.ShapeDtypeStruct((bsz, s, B_V_W), BF16),
        scratch_shapes=[pltpu.VMEM((L + CONV_HALO, w2), F32),
                        pltpu.VMEM((B_HEADS, B_QK_DIM, B_V_DIM), F32),
                        pltpu.VMEM((8, B_QK_DIM), F32),
                        pltpu.VMEM((8, LANES), F32)],
        compiler_params=_cparams(2),
        name="mlstm",
    )(slab, slab, slab, slab, slab, conv_w, conv_b.reshape(1, w2), gate_bias, out_norm.reshape(1, B_V_W))


def _mla_prep_body(cq_ref, ckv_ref, kr_ref, qg_ref, kg_ref, wqt_ref, wqst_ref, wkn_ref, wvt_ref,
                   cos_ref, sin_ref, cost_ref, sint_ref, qt_ref, kf_ref, vt_ref):
    def normed(ref, g_ref):
        t = ref[...].astype(F32)
        y = t * lax.rsqrt(jnp.mean(t * t, axis=-1, keepdims=True) + EPS)
        return (y * g_ref[...]).astype(BF16)

    cqn = normed(cq_ref, qg_ref)
    ckvn = normed(ckv_ref, kg_ref)

    qt_main = lax.dot_general(wqt_ref[...], cqn, _NT, preferred_element_type=F32)
    qt_swap = lax.dot_general(wqst_ref[...], cqn, _NT, preferred_element_type=F32)
    cos_t = cost_ref[...]
    sin_t = sint_ref[...]
    for h in range(C_HEADS):
        lo = h * ATTN_QK_PAD
        qt_ref[lo:lo + LANES, :] = qt_main[lo:lo + LANES, :].astype(qt_ref.dtype)
        qt_ref[lo + LANES:lo + 2 * LANES, :] = (qt_main[lo + LANES:lo + 2 * LANES, :] * cos_t
                                                + qt_swap[h * LANES:(h + 1) * LANES, :] * sin_t).astype(qt_ref.dtype)

    k_nope = jnp.dot(ckvn, wkn_ref[...], preferred_element_type=F32)
    kr = kr_ref[...].astype(F32)
    k_rot = (kr * cos_ref[...] + pltpu.roll(kr, LANES // 2, axis=1) * sin_ref[...]).astype(kf_ref.dtype)
    for h in range(C_HEADS):
        lo = h * ATTN_QK_PAD
        kf_ref[:, lo:lo + LANES] = k_nope[:, h * LANES:(h + 1) * LANES].astype(kf_ref.dtype)
        kf_ref[:, lo + LANES:lo + 2 * LANES] = k_rot

    vt_ref[...] = lax.dot_general(wvt_ref[...], ckvn, _NT, preferred_element_type=F32).astype(vt_ref.dtype)


def mla_prep(slab, q_norm, kv_norm, w_uq, w_ukv):
    bsz, s, _ = slab.shape
    tm = ATTN_TILE
    half = C_ROPE // 2
    scale = (C_NOPE + C_ROPE) ** -0.5 * LOG2E
    wq = (w_uq * scale).reshape(C_Q_RANK, C_HEADS, C_NOPE + C_ROPE)
    pad = jnp.zeros((C_Q_RANK, C_HEADS, LANES - C_ROPE), F32)
    x1, x2 = wq[..., C_NOPE:C_NOPE + half], wq[..., C_NOPE + half:]
    wqt = jnp.concatenate([wq, pad], axis=-1).reshape(C_Q_RANK, C_HEADS * ATTN_QK_PAD).T.astype(BF16)
    wqst = jnp.concatenate([-x2, x1, pad], axis=-1).reshape(C_Q_RANK, C_HEADS * LANES).T.astype(BF16)
    wkv = w_ukv.reshape(C_KV_RANK, C_HEADS, C_NOPE + C_V_DIM)
    wkn = wkv[..., :C_NOPE].reshape(C_KV_RANK, C_HEADS * C_NOPE).astype(BF16)
    wvt = wkv[..., C_NOPE:].reshape(C_KV_RANK, C_W).T.astype(BF16)

    pos = jnp.arange(s, dtype=jnp.int32)
    inv = ROPE_THETA ** (-jnp.arange(half, dtype=F32) / half)
    ang = pos.astype(F32)[:, None] * inv[None, :]
    zpad = jnp.zeros((s, LANES - C_ROPE), F32)
    cos_tab = jnp.concatenate([jnp.cos(ang), jnp.cos(ang), zpad], axis=-1)
    sin_tab = jnp.concatenate([jnp.sin(ang), jnp.sin(ang), zpad], axis=-1)

    nt = s // tm
    full = lambda shape: pl.BlockSpec(shape, lambda b, i: (0,) * len(shape))
    return pl.pallas_call(
        _mla_prep_body,
        grid=(bsz, nt),
        in_specs=[pl.BlockSpec((None, tm, C_Q_RANK), lambda b, i: (b, i, OFF_CQ // C_Q_RANK)),
                  pl.BlockSpec((None, tm, C_KV_RANK), lambda b, i: (b, i, OFF_CKV // C_KV_RANK)),
                  pl.BlockSpec((None, tm, LANES), lambda b, i: (b, i, OFF_KR // LANES)),
                  full((1, C_Q_RANK)), full((1, C_KV_RANK)),
                  full(wqt.shape), full(wqst.shape), full(wkn.shape), full(wvt.shape),
                  pl.BlockSpec((tm, LANES), lambda b, i: (i, 0)),
                  pl.BlockSpec((tm, LANES), lambda b, i: (i, 0)),
                  pl.BlockSpec((LANES, tm), lambda b, i: (0, i)),
                  pl.BlockSpec((LANES, tm), lambda b, i: (0, i))],
        out_specs=[pl.BlockSpec((None, None, C_HEADS * ATTN_QK_PAD, tm), lambda b, i: (b, i, 0, 0)),
                   pl.BlockSpec((None, tm, C_HEADS * ATTN_QK_PAD), lambda b, i: (b, i, 0)),
                   pl.BlockSpec((None, None, C_W, tm), lambda b, i: (b, i, 0, 0))],
        out_shape=[jax.ShapeDtypeStruct((bsz, nt, C_HEADS * ATTN_QK_PAD, tm), BF16),
                   jax.ShapeDtypeStruct((bsz, s, C_HEADS * ATTN_QK_PAD), BF16),
                   jax.ShapeDtypeStruct((bsz, nt, C_W, tm), BF16)],
        compiler_params=_cparams(2),
        name="mla_prep",
    )(slab, slab, slab, q_norm.reshape(1, C_Q_RANK), kv_norm.reshape(1, C_KV_RANK),
      wqt, wqst, wkn, wvt, cos_tab, sin_tab, cos_tab.T, sin_tab.T)


def _mla_attn_body(qt_ref, k_ref, vt_ref, z_ref, o_ref, m_s, l_s, acc_s, *s_refs, g_heads):
    i = pl.program_id(2)
    t = ATTN_TILE
    dq, dv = ATTN_QK_PAD, C_V_DIM
    _softmax_init(m_s, l_s, acc_s)

    def raw_scores(g, j):
        start = pl.multiple_of(j * t, t)
        return jnp.dot(k_ref[pl.ds(start, t), g * dq:(g + 1) * dq], qt_ref[g * dq:(g + 1) * dq, :],
                       preferred_element_type=F32)

    def diagonal_scores(slot):
        key = lax.broadcasted_iota(jnp.int32, (t, t), 0)
        qry = lax.broadcasted_iota(jnp.int32, (t, t), 1)
        for g in range(g_heads):
            s_refs[g][slot] = jnp.where(key <= qry, raw_scores(g, i), NEG)

    def past_scores(j, slot):
        for g in range(g_heads):
            s_refs[g][slot] = raw_scores(g, j)

    def consume(j, slot):
        for g in range(g_heads):
            _softmax_step_t(s_refs[g][slot], vt_ref[j, g * dv:(g + 1) * dv, :], g, m_s, l_s, acc_s)

    _skewed_blocks([(diagonal_scores, i)], i, past_scores, consume)
    _attn_finish(z_ref, o_ref, l_s, acc_s, g_heads, dv)


def mla_attention(qt, kf, vt, slab, g_heads=ATTN_HEADS_PER_STEP):
    bsz, nt, _, t = qt.shape
    s = nt * t
    dq, dv = ATTN_QK_PAD, C_V_DIM
    assert t == ATTN_TILE and C_HEADS % g_heads == 0 and OFF_ZC % (g_heads * dv) == 0
    cz = OFF_ZC // (g_heads * dv)
    return pl.pallas_call(
        functools.partial(_mla_attn_body, g_heads=g_heads),
        grid=(bsz, C_HEADS // g_heads, nt),
        in_specs=[pl.BlockSpec((None, None, g_heads * dq, t), lambda b, h, i: (b, i, h, 0)),
                  pl.BlockSpec((None, s, g_heads * dq), lambda b, h, i: (b, 0, h)),
                  pl.BlockSpec((None, nt, g_heads * dv, t), lambda b, h, i: (b, 0, h, 0)),
                  pl.BlockSpec((None, t, g_heads * dv), lambda b, h, i: (b, i, cz + h))],
        out_specs=pl.BlockSpec((None, t, g_heads * dv), lambda b, h, i: (b, i, h)),
        out_shape=jax.ShapeDtypeStruct((bsz, s, C_W), BF16),
        scratch_shapes=_attn_scratch(g_heads, t, dv),
        compiler_params=_cparams(3),
        name="mla_attn",
    )(qt, kf, vt, slab)


def _merge_body(ya_ref, yb_ref, yc_ref, wa_ref, wb_ref, wc_ref, ga_ref, gb_ref, gc_ref, o_ref):
    acc = None
    for y_ref, w_ref, g_ref in ((ya_ref, wa_ref, ga_ref), (yb_ref, wb_ref, gb_ref), (yc_ref, wc_ref, gc_ref)):
        term = (jax.nn.sigmoid(g_ref[...].astype(F32))
                * jnp.dot(y_ref[...], w_ref[...], preferred_element_type=F32))
        acc = term if acc is None else acc + term
    o_ref[...] = acc.astype(o_ref.dtype)


def branch_merge(ya, yb, yc, w_branch, slab2d, tm=1024, tn=1024):
    m, w = ya.shape
    d = w_branch.shape[-1]
    assert OFF_GT % tn == 0
    g0 = OFF_GT // tn
    gper = d // tn
    y_spec = pl.BlockSpec((tm, w), lambda j, i: (i, 0))
    w_specs = [pl.BlockSpec((None, w, tn), functools.partial(lambda j, i, n: (n, 0, j), n=n)) for n in range(N_BRANCH)]
    g_specs = [pl.BlockSpec((tm, tn), functools.partial(lambda j, i, n: (i, g0 + n * gper + j), n=n))
               for n in range(N_BRANCH)]
    return pl.pallas_call(
        _merge_body,
        grid=(d // tn, m // tm),
        in_specs=[y_spec, y_spec, y_spec] + w_specs + g_specs,
        out_specs=pl.BlockSpec((tm, tn), lambda j, i: (i, j)),
        out_shape=jax.ShapeDtypeStruct((m, d), BF16),
        compiler_params=_cparams(2),
        name="branch_merge",
    )(ya, yb, yc, w_branch, w_branch, w_branch, slab2d, slab2d, slab2d)


def _out_body(mg_ref, w_ref, x_ref, g_ref, *out_refs, last):
    x_new = x_ref[...] + jnp.dot(mg_ref[...], w_ref[...], preferred_element_type=F32)
    y = x_new * lax.rsqrt(jnp.mean(x_new * x_new, axis=-1, keepdims=True) + EPS) * g_ref[...]
    if last:
        out_refs[0][...] = y
    else:
        out_refs[0][...] = x_new
        out_refs[1][...] = y.astype(out_refs[1].dtype)


def out_projection(merged, w_out, x2d, gain, last, tm=512):
    m, d = x2d.shape
    row = pl.BlockSpec((tm, d), lambda i: (i, 0))
    if last:
        out_specs, out_shape = row, jax.ShapeDtypeStruct((m, d), F32)
    else:
        out_specs = [row, row]
        out_shape = [jax.ShapeDtypeStruct((m, d), F32), jax.ShapeDtypeStruct((m, d), BF16)]
    return pl.pallas_call(
        functools.partial(_out_body, last=last),
        grid=(m // tm,),
        in_specs=[row, pl.BlockSpec((d, d), lambda i: (0, 0)), row, pl.BlockSpec((1, d), lambda i: (0, 0))],
        out_specs=out_specs,
        out_shape=out_shape,
        compiler_params=_cparams(1),
        name="out_proj",
    )(merged, w_out, x2d, gain.reshape(1, d))


def _input_weights(w_in_l):
    sizes = (A_W, A_W, A_W, A_W, B_QK_W, B_QK_W, B_V_W, B_HEADS, B_HEADS, B_V_W, B_V_W,
             C_Q_RANK, C_KV_RANK, C_ROPE, C_W, N_BRANCH * D_MODEL)
    points = np.cumsum(sizes)[:-1].tolist()
    (qa, ka, va, za, qb, kb, vb, ib, fb, ob, zb, cq, ckv, kr, zc, gt) = jnp.split(w_in_l, points, axis=1)
    d = w_in_l.shape[0]
    half = C_ROPE // 2
    if_group = jnp.concatenate([ib, fb, jnp.zeros((d, LANES - 2 * B_HEADS), F32)], axis=1)
    kr_group = jnp.concatenate([kr, -kr[:, half:], kr[:, :half]], axis=1)
    cols = [ka, za, qb, kb, vb, ob, zb, zc, cq, ckv, if_group, kr_group, gt]
    qa_scale = A_HEAD_DIM ** -0.5 * LOG2E
    wt_qv = jnp.concatenate([qa * qa_scale, va], axis=1).T.astype(BF16)
    return jnp.concatenate(cols, axis=1).astype(BF16), wt_qv


def kernel(x, norm_gain, w_in, t5_table, mlstm_conv_w, mlstm_conv_b, mlstm_i_bias, mlstm_f_bias,
           mlstm_out_norm, mla_q_norm, mla_kv_norm, mla_w_uq, mla_w_ukv, w_branch, w_out, final_norm):
    bsz, s, d = x.shape
    m = bsz * s
    x2d = x.reshape(m, d)
    h = rmsnorm_rows(x2d, norm_gain[0])
    out = None
    for l in range(DEPTH):
        w_slab, wt_qv = _input_weights(w_in[l])
        slab2d = input_projection(h, w_slab)
        slab = slab2d.reshape(bsz, s, D_SLAB)
        qt_a, vt_a = moba_qv_projection(h, wt_qv, bsz)
        ya = moba_branch(qt_a, vt_a, slab, t5_table)
        yb = mlstm_branch(slab, mlstm_conv_w[l], mlstm_conv_b[l], mlstm_i_bias[l], mlstm_f_bias[l],
                          mlstm_out_norm[l])
        qt_c, k_c, vt_c = mla_prep(slab, mla_q_norm[l], mla_kv_norm[l], mla_w_uq[l], mla_w_ukv[l])
        yc = mla_attention(qt_c, k_c, vt_c, slab)
        merged = branch_merge(ya.reshape(m, A_W), yb.reshape(m, B_V_W), yc.reshape(m, C_W),
                              w_branch[l].astype(BF16), slab2d)
        last = l == DEPTH - 1
        gain = final_norm if last else norm_gain[l + 1]
        res = out_projection(merged, w_out[l].astype(BF16), x2d, gain, last)
        if last:
            out = res
        else:
            x2d, h = res
    return out.reshape(bsz, s, d)
```

```python
import functools
import math

import jax
import jax.numpy as jnp
import numpy as np
from jax import lax
from jax.experimental import pallas as pl
from jax.experimental.pallas import tpu as pltpu

F32 = jnp.float32
BF16 = jnp.bfloat16

D_MODEL = 2048
DEPTH = 2
EPS = 1e-6
NEG = -1e30
LOG2E = math.log2(math.e)

A_HEADS = 8
A_HEAD_DIM = 128
A_BLOCK = 256
A_TOPK = 3
T5_BUCKETS = 32
T5_MAX_DIST = 128
B_HEADS = 4
B_QK_DIM = 128
B_V_DIM = 256
B_CONV = 4
C_HEADS = 8
C_Q_RANK = 512
C_KV_RANK = 256
C_NOPE = 128
C_ROPE = 64
C_V_DIM = 128
ROPE_THETA = 10000.0
N_BRANCH = 3

A_W = A_HEADS * A_HEAD_DIM
B_QK_W = B_HEADS * B_QK_DIM
B_V_W = B_HEADS * B_V_DIM
C_W = C_HEADS * C_V_DIM

LANES = 128
SUBLANES = 8
VMEM_LIMIT_BYTES = 56 * 1024 * 1024

OFF_KA = 0
OFF_ZA = OFF_KA + A_W
OFF_QKB = OFF_ZA + A_W
OFF_VB = OFF_QKB + 2 * B_QK_W
OFF_OB = OFF_VB + B_V_W
OFF_ZB = OFF_OB + B_V_W
OFF_ZC = OFF_ZB + B_V_W
OFF_CQ = OFF_ZC + C_W
OFF_CKV = OFF_CQ + C_Q_RANK
OFF_IF = OFF_CKV + C_KV_RANK
OFF_KR = OFF_IF + LANES
OFF_GT = OFF_KR + LANES
D_SLAB = OFF_GT + N_BRANCH * D_MODEL
assert D_SLAB == 14336

ATTN_TILE = 256
ATTN_QK_PAD = 2 * LANES
ATTN_HEADS_PER_STEP = 8
assert ATTN_TILE == A_BLOCK


def _cparams(n_axes):
    return pltpu.CompilerParams(dimension_semantics=("arbitrary",) * n_axes,
                                vmem_limit_bytes=VMEM_LIMIT_BYTES)


def _silu(t):
    return t * jax.nn.sigmoid(t)


_NT = (((1,), (1,)), ((), ()))


def _rmsnorm_body(x_ref, g_ref, o_ref):
    xf = x_ref[...]
    y = xf * lax.rsqrt(jnp.mean(xf * xf, axis=-1, keepdims=True) + EPS)
    o_ref[...] = (y * g_ref[...]).astype(o_ref.dtype)


def rmsnorm_rows(x2d, gain, tm=512):
    m, d = x2d.shape
    return pl.pallas_call(
        _rmsnorm_body,
        grid=(m // tm,),
        in_specs=[pl.BlockSpec((tm, d), lambda i: (i, 0)),
                  pl.BlockSpec((1, d), lambda i: (0, 0))],
        out_specs=pl.BlockSpec((tm, d), lambda i: (i, 0)),
        out_shape=jax.ShapeDtypeStruct((m, d), BF16),
        compiler_params=_cparams(1),
        name="rmsnorm",
    )(x2d, gain.reshape(1, d))


def _proj_body(h_ref, w_ref, o_ref):
    o_ref[...] = jnp.dot(h_ref[...], w_ref[...], preferred_element_type=F32).astype(o_ref.dtype)


def input_projection(h2d, w_slab, tm=1024, tn=1024):
    m, d = h2d.shape
    n = w_slab.shape[1]
    return pl.pallas_call(
        _proj_body,
        grid=(n // tn, m // tm),
        in_specs=[pl.BlockSpec((tm, d), lambda j, i: (i, 0)),
                  pl.BlockSpec((d, tn), lambda j, i: (0, j))],
        out_specs=pl.BlockSpec((tm, tn), lambda j, i: (i, j)),
        out_shape=jax.ShapeDtypeStruct((m, n), BF16),
        compiler_params=_cparams(2),
        name="input_proj",
    )(h2d, w_slab)


def _proj_t_body(h_ref, wt_ref, qt_ref, vt_ref, *, n_sub):
    t = ATTN_TILE
    hd = A_HEAD_DIM
    hv = hd + V_AUG_ROWS
    res = lax.dot_general(wt_ref[...], h_ref[...], _NT, preferred_element_type=F32)
    aug = _v_aug_rows(t, vt_ref.dtype)
    for c in range(n_sub):
        qt_ref[c] = res[:A_W, c * t:(c + 1) * t].astype(qt_ref.dtype)
        for h in range(A_HEADS):
            vt_ref[c, h * hv:h * hv + hd, :] = res[A_W + h * hd:A_W + (h + 1) * hd, c * t:(c + 1) * t
                                                   ].astype(vt_ref.dtype)
            vt_ref[c, h * hv + hd:(h + 1) * hv, :] = aug


def moba_qv_projection(h2d, wt_qv, bsz, tm=512):
    m, d = h2d.shape
    t = ATTN_TILE
    tm = min(tm, m // bsz)
    n_sub = tm // t
    nt = m // bsz // t
    steps = m // bsz // tm
    vt_rows = A_HEADS * (A_HEAD_DIM + V_AUG_ROWS)
    return pl.pallas_call(
        functools.partial(_proj_t_body, n_sub=n_sub),
        grid=(bsz, steps),
        in_specs=[pl.BlockSpec((tm, d), lambda b, i: (b * steps + i, 0)),
                  pl.BlockSpec((2 * A_W, d), lambda b, i: (0, 0))],
        out_specs=[pl.BlockSpec((None, n_sub, A_W, t), lambda b, i: (b, i, 0, 0)),
                   pl.BlockSpec((None, n_sub, vt_rows, t), lambda b, i: (b, i, 0, 0))],
        out_shape=[jax.ShapeDtypeStruct((bsz, nt, A_W, t), BF16),
                   jax.ShapeDtypeStruct((bsz, nt, vt_rows, t), BF16)],
        compiler_params=_cparams(2),
        name="moba_qv_proj",
    )(h2d, wt_qv)


V_AUG_ROWS = 16


def _v_aug_rows(t, dtype):
    return jnp.where(lax.broadcasted_iota(jnp.int32, (V_AUG_ROWS, t), 0) == 0, 1.0, 0.0).astype(dtype)


def _softmax_init(m_s, acc_s):
    m_s[...] = jnp.full(m_s.shape, -jnp.inf, F32)
    acc_s[...] = jnp.zeros(acc_s.shape, F32)


def _softmax_step_t(s, vt_blk, g, m_s, acc_s):
    m_prev = m_s[g]
    m_new = jnp.maximum(m_prev, jnp.max(s, axis=0, keepdims=True))
    alpha = jnp.exp2(m_prev - m_new)
    p = jnp.exp2(s - m_new)
    acc_s[g] = alpha * acc_s[g] + jnp.dot(vt_blk, p.astype(BF16), preferred_element_type=F32)
    m_s[g] = m_new


def _attn_scratch(g_heads, t, dv):
    state = [pltpu.VMEM((g_heads, 1, t), F32), pltpu.VMEM((g_heads, dv + V_AUG_ROWS, t), F32)]
    return state + [pltpu.VMEM((2, t, t), F32)] * g_heads


def _skewed_blocks(lead, n_uniform, uniform_scores, consume):
    lead[0][0](0)
    pend_blk, pend_slot = lead[0][1], 0
    for fn, blk in lead[1:]:
        fn(1 - pend_slot)
        consume(pend_blk, pend_slot)
        pend_blk, pend_slot = blk, 1 - pend_slot
    a, b = pend_slot, 1 - pend_slot

    def two_blocks(jj, pend):
        uniform_scores(2 * jj, b)
        consume(pend, a)
        uniform_scores(2 * jj + 1, a)
        consume(2 * jj, b)
        return 2 * jj + 1

    pend = lax.fori_loop(0, n_uniform // 2, two_blocks, pend_blk)

    @pl.when(n_uniform % 2 == 1)
    def _odd_tail():
        uniform_scores(n_uniform - 1, b)
        consume(pend, a)
        consume(n_uniform - 1, b)

    @pl.when(n_uniform % 2 == 0)
    def _even_tail():
        consume(pend, a)


def _attn_finish(z_ref, o_ref, acc_s, g_heads, dv):
    for g in range(g_heads):
        out_t = acc_s[g, :dv, :] / acc_s[g, dv:dv + 1, :]
        o_ref[:, g * dv:(g + 1) * dv] = (out_t.T * _silu(z_ref[:, g * dv:(g + 1) * dv].astype(F32))
                                         ).astype(o_ref.dtype)


def _t5_thresholds():
    max_exact = T5_BUCKETS // 2
    d = np.arange(0, 4 * T5_MAX_DIST, dtype=np.int64)
    dd = np.maximum(d, 1).astype(np.float32)
    large = max_exact + (np.log(dd / np.float32(max_exact)) / np.float32(math.log(T5_MAX_DIST / max_exact))
                         * np.float32(T5_BUCKETS - max_exact)).astype(np.int32)
    large = np.minimum(large, T5_BUCKETS - 1)
    bucket = np.where(d < max_exact, d, large)
    assert np.all(np.diff(bucket) >= 0) and bucket[-1] == T5_BUCKETS - 1
    return [int(np.argmax(bucket >= k)) for k in range(T5_BUCKETS)]


_T5_THRESH = _t5_thresholds()

MASK_BIG = 1e30


def _moba_body(t5_ref, qt_ref, k_ref, vt_ref, z_ref, o_ref,
               kaug_s, km_s, bias_own_s, bias_prev_s, qa_s, m_s, acc_s, *s_refs, n_blocks, g_heads):
    hg = pl.program_id(0)
    b = pl.program_id(1)
    i = pl.program_id(2)
    t, hd = A_BLOCK, A_HEAD_DIM
    hv = hd + V_AUG_ROWS
    seq = n_blocks * t
    nb_pad = -(-n_blocks // SUBLANES) * SUBLANES

    @pl.when((hg == 0) & (b == 0) & (i == 0))
    def _mask_columns():
        key_blk = lax.broadcasted_iota(jnp.int32, (seq, LANES), 0) // t
        lane = lax.broadcasted_iota(jnp.int32, (seq, LANES), 1)
        pattern = jnp.where(lane == key_blk, -MASK_BIG, 0.0).astype(kaug_s.dtype)
        for g in range(g_heads):
            kaug_s[g, :, hd:] = pattern

    @pl.when((b == 0) & (i == 0))
    def _build_bias():
        key = lax.broadcasted_iota(jnp.int32, (t, t), 0)
        qry = lax.broadcasted_iota(jnp.int32, (t, t), 1)
        for g in range(g_heads):
            head = hg * g_heads + g
            far = t5_ref[T5_BUCKETS - 1, head]
            for dist, dst in ((qry - key, bias_own_s), (t + qry - key, bias_prev_s)):
                bias = jnp.full((t, t), t5_ref[0, head], F32)
                for kk in range(1, T5_BUCKETS):
                    bias = jnp.where(dist >= _T5_THRESH[kk], t5_ref[kk, head], bias)
                dst[g] = (bias - far) * LOG2E

    @pl.when(i == 0)
    def _new_sequence():
        for g in range(g_heads):
            kaug_s[g, :, :hd] = k_ref[:, g * hd:(g + 1) * hd]
            km_s[g] = jnp.zeros(km_s.shape[1:], km_s.dtype)
            for nb in range(n_blocks):
                mean = jnp.mean(k_ref[nb * t:(nb + 1) * t, g * hd:(g + 1) * hd].astype(F32), axis=0, keepdims=True)
                hi = mean.astype(BF16)
                km_s[g, nb:nb + 1, :] = hi
                km_s[g, LANES + nb:LANES + nb + 1, :] = (mean - hi.astype(F32)).astype(BF16)

    blk_row = lax.broadcasted_iota(jnp.int32, (nb_pad, t), 0)
    past = blk_row < i
    for g in range(g_heads):
        qt = qt_ref[g * hd:(g + 1) * hd, :]
        g2 = jnp.dot(km_s[g], qt, preferred_element_type=F32)
        gate = g2[:nb_pad, :] + g2[LANES:LANES + nb_pad, :]
        cur = jnp.where(past, gate, NEG)
        keep = blk_row == i
        for _ in range(A_TOPK):
            mx = jnp.max(cur, axis=0, keepdims=True)
            idx = jnp.min(jnp.where(cur == mx, blk_row, nb_pad), axis=0, keepdims=True)
            pick = blk_row == idx
            keep = keep | (pick & past)
            cur = jnp.where(pick, -jnp.inf, cur)
        unsel = jnp.where(keep, 0.0, 1.0)
        qa_s[g, :hd, :] = qt
        qa_s[g, hd:, :] = jnp.concatenate([unsel, jnp.zeros((LANES - nb_pad, t), F32)], axis=0).astype(qa_s.dtype)

    _softmax_init(m_s, acc_s)

    def raw_scores(g, j):
        start = pl.multiple_of(j * t, t)
        return jnp.dot(kaug_s[g, pl.ds(start, t), :], qa_s[g], preferred_element_type=F32)

    def own_scores(slot):
        key = lax.broadcasted_iota(jnp.int32, (t, t), 0)
        qry = lax.broadcasted_iota(jnp.int32, (t, t), 1)
        for g in range(g_heads):
            s_refs[g][slot] = jnp.where(key <= qry, raw_scores(g, i) + bias_own_s[g], NEG)

    j_prev = jnp.maximum(i - 1, 0)
    no_prev = jnp.where(i == 0, -MASK_BIG, 0.0)

    def prev_scores(slot):
        for g in range(g_heads):
            s_refs[g][slot] = raw_scores(g, j_prev) + (bias_prev_s[g] + no_prev)

    def far_scores(j, slot):
        for g in range(g_heads):
            s_refs[g][slot] = raw_scores(g, j)

    def consume(j, slot):
        for g in range(g_heads):
            _softmax_step_t(s_refs[g][slot], vt_ref[j, g * hv:(g + 1) * hv, :], g, m_s, acc_s)

    _skewed_blocks([(own_scores, i), (prev_scores, j_prev)], j_prev, far_scores, consume)
    _attn_finish(z_ref, o_ref, acc_s, g_heads, hd)


def moba_branch(qt, vt, slab, t5_table, g_heads=ATTN_HEADS_PER_STEP):
    bsz, nb, _, t = qt.shape
    s = nb * t
    hd = A_HEAD_DIM
    gw = g_heads * hd
    assert t == A_BLOCK and nb <= LANES and A_HEADS % g_heads == 0
    assert OFF_KA % gw == 0 and OFF_ZA % gw == 0
    assert _T5_THRESH[T5_BUCKETS - 1] <= t + 1
    ck, cz = OFF_KA // gw, OFF_ZA // gw
    return pl.pallas_call(
        functools.partial(_moba_body, n_blocks=nb, g_heads=g_heads),
        grid=(A_HEADS // g_heads, bsz, nb),
        in_specs=[pl.BlockSpec(memory_space=pltpu.SMEM),
                  pl.BlockSpec((None, None, gw, t), lambda h, b, i: (b, i, h, 0)),
                  pl.BlockSpec((None, s, gw), lambda h, b, i: (b, 0, ck + h), pipeline_mode=pl.Buffered(1)),
                  pl.BlockSpec((None, nb, g_heads * (hd + V_AUG_ROWS), t), lambda h, b, i: (b, 0, h, 0),
                               pipeline_mode=pl.Buffered(1)),
                  pl.BlockSpec((None, t, gw), lambda h, b, i: (b, i, cz + h))],
        out_specs=pl.BlockSpec((None, t, gw), lambda h, b, i: (b, i, h)),
        out_shape=jax.ShapeDtypeStruct((bsz, s, A_W), BF16),
        scratch_shapes=[pltpu.VMEM((g_heads, s, ATTN_QK_PAD), BF16),
                        pltpu.VMEM((g_heads, 2 * LANES, hd), BF16),
                        pltpu.VMEM((g_heads, t, t), F32),
                        pltpu.VMEM((g_heads, t, t), F32),
                        pltpu.VMEM((g_heads, ATTN_QK_PAD, t), BF16)] + _attn_scratch(g_heads, t, hd),
        compiler_params=_cparams(3),
        name="moba",
    )(t5_table, qt, slab, vt, slab)


MLSTM_CHUNK = 256
CONV_HALO = 8


def _log_sigmoid(t):
    return jnp.minimum(t, 0.0) - jnp.log1p(jnp.exp(-jnp.abs(t)))


def _mlstm_body(qk_ref, v_ref, ob_ref, zb_ref, if_ref, cw_ref, cb_ref, gb_ref, gn_ref, o_ref,
                xe_s, c_s, n_s, m_s):
    c = pl.program_id(1)
    L = MLSTM_CHUNK
    dk, dv = B_QK_DIM, B_V_DIM

    @pl.when(c == 0)
    def _reset():
        xe_s[0:CONV_HALO, :] = jnp.zeros((CONV_HALO, 2 * B_QK_W), F32)
        c_s[...] = jnp.zeros_like(c_s)
        n_s[...] = jnp.zeros_like(n_s)
        m_s[...] = jnp.zeros_like(m_s)

    xe_s[CONV_HALO:CONV_HALO + L, :] = qk_ref[...].astype(F32)
    conv = cb_ref[...]
    for j in range(B_CONV):
        conv = conv + cw_ref[j:j + 1, :] * xe_s[pl.ds(CONV_HALO - (B_CONV - 1) + j, L), :]
    xe_s[0:CONV_HALO, :] = xe_s[L:L + CONV_HALO, :]
    qk = _silu(conv)

    gi = if_ref[...].astype(F32) + gb_ref[...]
    lf = _log_sigmoid(gi)
    row = lax.broadcasted_iota(jnp.int32, (L, L), 0)
    col = lax.broadcasted_iota(jnp.int32, (L, L), 1)
    causal = col <= row
    tri = jnp.where(causal, 1.0, 0.0).astype(F32)
    b_cols = jnp.dot(tri, lf, precision=lax.Precision.HIGHEST, preferred_element_type=F32)
    b_rows = b_cols.T
    li_rows = gi.T

    for h in range(B_HEADS):
        q_h = qk[:, h * dk:(h + 1) * dk]
        k_h = qk[:, B_QK_W + h * dk:B_QK_W + (h + 1) * dk] * (dk ** -0.5)
        v_h = v_ref[:, h * dv:(h + 1) * dv]
        b_c = b_cols[:, B_HEADS + h:B_HEADS + h + 1]
        b_r = b_rows[B_HEADS + h:B_HEADS + h + 1, :]
        li_c = gi[:, h:h + 1]
        li_r = li_rows[h:h + 1, :]
        m_prev = m_s[h:h + 1, 0:1]
        c_prev = c_s[h]
        n_prev = n_s[h:h + 1, :]

        a_c = b_c + m_prev
        dmat = jnp.where(causal, b_c - b_r + li_r, -jnp.inf)
        m_t = jnp.maximum(a_c, jnp.max(dmat, axis=-1, keepdims=True))
        w_inter = jnp.exp(a_c - m_t)
        q_b = q_h.astype(BF16)
        k_b = k_h.astype(BF16)
        sc = (lax.dot_general(q_b, k_b, _NT, preferred_element_type=F32)
              * jnp.exp(dmat - m_t))
        num = (w_inter * jnp.dot(q_b, c_prev.astype(BF16), preferred_element_type=F32)
               + jnp.dot(sc.astype(BF16), v_h, preferred_element_type=F32))
        den = (w_inter * jnp.sum(q_h * n_prev, axis=-1, keepdims=True)
               + jnp.sum(sc, axis=-1, keepdims=True))
        hh = num / jnp.maximum(jnp.abs(den), jnp.exp(-m_t))

        b_last = b_c[L - 1:L, :]
        g_c = b_last - b_c + li_c
        m_new = jnp.maximum(b_last + m_prev, jnp.max(g_c, axis=0, keepdims=True))
        decay = jnp.exp(b_last + m_prev - m_new)
        wk = jnp.exp(g_c - m_new) * k_h
        c_s[h] = decay * c_prev + lax.dot_general(wk.astype(BF16), v_h, (((0,), (0,)), ((), ())),
                                                  preferred_element_type=F32)
        n_s[h:h + 1, :] = decay * n_prev + jnp.sum(wk, axis=0, keepdims=True)
        m_s[h:h + 1, :] = jnp.broadcast_to(m_new, (1, LANES))

        mu = jnp.mean(hh, axis=-1, keepdims=True)
        dlt = hh - mu
        var = jnp.mean(dlt * dlt, axis=-1, keepdims=True)
        y = dlt * lax.rsqrt(var + EPS) * gn_ref[:, h * dv:(h + 1) * dv]
        y = (y * jax.nn.sigmoid(ob_ref[:, h * dv:(h + 1) * dv].astype(F32))
             * _silu(zb_ref[:, h * dv:(h + 1) * dv].astype(F32)))
        o_ref[:, h * dv:(h + 1) * dv] = y.astype(o_ref.dtype)


def mlstm_branch(slab, conv_w, conv_b, i_bias, f_bias, out_norm):
    bsz, s, _ = slab.shape
    L = MLSTM_CHUNK
    assert s % L == 0
    gate_bias = jnp.zeros((1, LANES), F32).at[0, :B_HEADS].set(i_bias).at[0, B_HEADS:2 * B_HEADS].set(f_bias)
    w2 = 2 * B_QK_W
    full = lambda shape: pl.BlockSpec(shape, lambda b, c: (0,) * len(shape))
    return pl.pallas_call(
        _mlstm_body,
        grid=(bsz, s // L),
        in_specs=[pl.BlockSpec((None, L, w2), lambda b, c: (b, c, OFF_QKB // w2)),
                  pl.BlockSpec((None, L, B_V_W), lambda b, c: (b, c, OFF_VB // B_V_W)),
                  pl.BlockSpec((None, L, B_V_W), lambda b, c: (b, c, OFF_OB // B_V_W)),
                  pl.BlockSpec((None, L, B_V_W), lambda b, c: (b, c, OFF_ZB // B_V_W)),
                  pl.BlockSpec((None, L, LANES), lambda b, c: (b, c, OFF_IF // LANES)),
                  full((B_CONV, w2)), full((1, w2)), full((1, LANES)), full((1, B_V_W))],
        out_specs=pl.BlockSpec((None, L, B_V_W), lambda b, c: (b, c, 0)),
        out_shape=jax.ShapeDtypeStruct((bsz, s, B_V_W), BF16),
        scratch_shapes=[pltpu.VMEM((L + CONV_HALO, w2), F32),
                        pltpu.VMEM((B_HEADS, B_QK_DIM, B_V_DIM), F32),
                        pltpu.VMEM((8, B_QK_DIM), F32),
                        pltpu.VMEM((8, LANES), F32)],
        compiler_params=_cparams(2),
        name="mlstm",
    )(slab, slab, slab, slab, slab, conv_w, conv_b.reshape(1, w2), gate_bias, out_norm.reshape(1, B_V_W))


def _mla_prep_body(cq_ref, ckv_ref, kr_ref, qg_ref, kg_ref, wqt_ref, wqst_ref, wkn_ref, wvt_ref,
                   cos_ref, sin_ref, cost_ref, sint_ref, qt_ref, kf_ref, vt_ref):
    def normed(ref, g_ref):
        t = ref[...].astype(F32)
        y = t * lax.rsqrt(jnp.mean(t * t, axis=-1, keepdims=True) + EPS)
        return (y * g_ref[...]).astype(BF16)

    cqn = normed(cq_ref, qg_ref)
    ckvn = normed(ckv_ref, kg_ref)

    qt_main = lax.dot_general(wqt_ref[...], cqn, _NT, preferred_element_type=F32)
    qt_swap = lax.dot_general(wqst_ref[...], cqn, _NT, preferred_element_type=F32)
    cos_t = cost_ref[...]
    sin_t = sint_ref[...]
    for h in range(C_HEADS):
        lo = h * ATTN_QK_PAD
        qt_ref[lo:lo + LANES, :] = qt_main[lo:lo + LANES, :].astype(qt_ref.dtype)
        qt_ref[lo + LANES:lo + 2 * LANES, :] = (qt_main[lo + LANES:lo + 2 * LANES, :] * cos_t
                                                + qt_swap[h * LANES:(h + 1) * LANES, :] * sin_t).astype(qt_ref.dtype)

    k_nope = jnp.dot(ckvn, wkn_ref[...], preferred_element_type=F32)
    kr = kr_ref[...].astype(F32)
    k_rot = (kr * cos_ref[...] + pltpu.roll(kr, LANES // 2, axis=1) * sin_ref[...]).astype(kf_ref.dtype)
    for h in range(C_HEADS):
        lo = h * ATTN_QK_PAD
        kf_ref[:, lo:lo + LANES] = k_nope[:, h * LANES:(h + 1) * LANES].astype(kf_ref.dtype)
        kf_ref[:, lo + LANES:lo + 2 * LANES] = k_rot

    v_t = lax.dot_general(wvt_ref[...], ckvn, _NT, preferred_element_type=F32)
    hv = C_V_DIM + V_AUG_ROWS
    aug = _v_aug_rows(v_t.shape[1], vt_ref.dtype)
    for h in range(C_HEADS):
        vt_ref[h * hv:h * hv + C_V_DIM, :] = v_t[h * C_V_DIM:(h + 1) * C_V_DIM, :].astype(vt_ref.dtype)
        vt_ref[h * hv + C_V_DIM:(h + 1) * hv, :] = aug


def mla_prep(slab, q_norm, kv_norm, w_uq, w_ukv):
    bsz, s, _ = slab.shape
    tm = ATTN_TILE
    vt_rows = C_HEADS * (C_V_DIM + V_AUG_ROWS)
    half = C_ROPE // 2
    scale = (C_NOPE + C_ROPE) ** -0.5 * LOG2E
    wq = (w_uq * scale).reshape(C_Q_RANK, C_HEADS, C_NOPE + C_ROPE)
    pad = jnp.zeros((C_Q_RANK, C_HEADS, LANES - C_ROPE), F32)
    x1, x2 = wq[..., C_NOPE:C_NOPE + half], wq[..., C_NOPE + half:]
    wqt = jnp.concatenate([wq, pad], axis=-1).reshape(C_Q_RANK, C_HEADS * ATTN_QK_PAD).T.astype(BF16)
    wqst = jnp.concatenate([-x2, x1, pad], axis=-1).reshape(C_Q_RANK, C_HEADS * LANES).T.astype(BF16)
    wkv = w_ukv.reshape(C_KV_RANK, C_HEADS, C_NOPE + C_V_DIM)
    wkn = wkv[..., :C_NOPE].reshape(C_KV_RANK, C_HEADS * C_NOPE).astype(BF16)
    wvt = wkv[..., C_NOPE:].reshape(C_KV_RANK, C_W).T.astype(BF16)

    pos = jnp.arange(s, dtype=jnp.int32)
    inv = ROPE_THETA ** (-jnp.arange(half, dtype=F32) / half)
    ang = pos.astype(F32)[:, None] * inv[None, :]
    zpad = jnp.zeros((s, LANES - C_ROPE), F32)
    cos_tab = jnp.concatenate([jnp.cos(ang), jnp.cos(ang), zpad], axis=-1)
    sin_tab = jnp.concatenate([jnp.sin(ang), jnp.sin(ang), zpad], axis=-1)

    nt = s // tm
    full = lambda shape: pl.BlockSpec(shape, lambda b, i: (0,) * len(shape))
    return pl.pallas_call(
        _mla_prep_body,
        grid=(bsz, nt),
        in_specs=[pl.BlockSpec((None, tm, C_Q_RANK), lambda b, i: (b, i, OFF_CQ // C_Q_RANK)),
                  pl.BlockSpec((None, tm, C_KV_RANK), lambda b, i: (b, i, OFF_CKV // C_KV_RANK)),
                  pl.BlockSpec((None, tm, LANES), lambda b, i: (b, i, OFF_KR // LANES)),
                  full((1, C_Q_RANK)), full((1, C_KV_RANK)),
                  full(wqt.shape), full(wqst.shape), full(wkn.shape), full(wvt.shape),
                  pl.BlockSpec((tm, LANES), lambda b, i: (i, 0)),
                  pl.BlockSpec((tm, LANES), lambda b, i: (i, 0)),
                  pl.BlockSpec((LANES, tm), lambda b, i: (0, i)),
                  pl.BlockSpec((LANES, tm), lambda b, i: (0, i))],
        out_specs=[pl.BlockSpec((None, None, C_HEADS * ATTN_QK_PAD, tm), lambda b, i: (b, i, 0, 0)),
                   pl.BlockSpec((None, tm, C_HEADS * ATTN_QK_PAD), lambda b, i: (b, i, 0)),
                   pl.BlockSpec((None, None, vt_rows, tm), lambda b, i: (b, i, 0, 0))],
        out_shape=[jax.ShapeDtypeStruct((bsz, nt, C_HEADS * ATTN_QK_PAD, tm), BF16),
                   jax.ShapeDtypeStruct((bsz, s, C_HEADS * ATTN_QK_PAD), BF16),
                   jax.ShapeDtypeStruct((bsz, nt, vt_rows, tm), BF16)],
        compiler_params=_cparams(2),
        name="mla_prep",
    )(slab, slab, slab, q_norm.reshape(1, C_Q_RANK), kv_norm.reshape(1, C_KV_RANK),
      wqt, wqst, wkn, wvt, cos_tab, sin_tab, cos_tab.T, sin_tab.T)


def _mla_attn_body(qt_ref, k_ref, vt_ref, z_ref, o_ref, m_s, acc_s, *s_refs, g_heads):
    i = pl.program_id(2)
    t = ATTN_TILE
    dq, dv = ATTN_QK_PAD, C_V_DIM
    hv = dv + V_AUG_ROWS
    _softmax_init(m_s, acc_s)

    def raw_scores(g, j):
        start = pl.multiple_of(j * t, t)
        return jnp.dot(k_ref[pl.ds(start, t), g * dq:(g + 1) * dq], qt_ref[g * dq:(g + 1) * dq, :],
                       preferred_element_type=F32)

    def diagonal_scores(slot):
        key = lax.broadcasted_iota(jnp.int32, (t, t), 0)
        qry = lax.broadcasted_iota(jnp.int32, (t, t), 1)
        for g in range(g_heads):
            s_refs[g][slot] = jnp.where(key <= qry, raw_scores(g, i), NEG)

    def past_scores(j, slot):
        for g in range(g_heads):
            s_refs[g][slot] = raw_scores(g, j)

    def consume(j, slot):
        for g in range(g_heads):
            _softmax_step_t(s_refs[g][slot], vt_ref[j, g * hv:(g + 1) * hv, :], g, m_s, acc_s)

    _skewed_blocks([(diagonal_scores, i)], i, past_scores, consume)
    _attn_finish(z_ref, o_ref, acc_s, g_heads, dv)


def mla_attention(qt, kf, vt, slab, g_heads=ATTN_HEADS_PER_STEP):
    bsz, nt, _, t = qt.shape
    s = nt * t
    dq, dv = ATTN_QK_PAD, C_V_DIM
    assert t == ATTN_TILE and C_HEADS % g_heads == 0 and OFF_ZC % (g_heads * dv) == 0
    cz = OFF_ZC // (g_heads * dv)
    return pl.pallas_call(
        functools.partial(_mla_attn_body, g_heads=g_heads),
        grid=(bsz, C_HEADS // g_heads, nt),
        in_specs=[pl.BlockSpec((None, None, g_heads * dq, t), lambda b, h, i: (b, i, h, 0)),
                  pl.BlockSpec((None, s, g_heads * dq), lambda b, h, i: (b, 0, h),
                               pipeline_mode=pl.Buffered(1)),
                  pl.BlockSpec((None, nt, g_heads * (dv + V_AUG_ROWS), t), lambda b, h, i: (b, 0, h, 0),
                               pipeline_mode=pl.Buffered(1)),
                  pl.BlockSpec((None, t, g_heads * dv), lambda b, h, i: (b, i, cz + h))],
        out_specs=pl.BlockSpec((None, t, g_heads * dv), lambda b, h, i: (b, i, h)),
        out_shape=jax.ShapeDtypeStruct((bsz, s, C_W), BF16),
        scratch_shapes=_attn_scratch(g_heads, t, dv),
        compiler_params=_cparams(3),
        name="mla_attn",
    )(qt, kf, vt, slab)


def _merge_body(ya_ref, yb_ref, yc_ref, wa_ref, wb_ref, wc_ref, ga_ref, gb_ref, gc_ref, o_ref):
    acc = None
    for y_ref, w_ref, g_ref in ((ya_ref, wa_ref, ga_ref), (yb_ref, wb_ref, gb_ref), (yc_ref, wc_ref, gc_ref)):
        term = (jax.nn.sigmoid(g_ref[...].astype(F32))
                * jnp.dot(y_ref[...], w_ref[...], preferred_element_type=F32))
        acc = term if acc is None else acc + term
    o_ref[...] = acc.astype(o_ref.dtype)


def branch_merge(ya, yb, yc, w_branch, slab2d, tm=1024, tn=1024):
    m, w = ya.shape
    d = w_branch.shape[-1]
    assert OFF_GT % tn == 0
    g0 = OFF_GT // tn
    gper = d // tn
    y_spec = pl.BlockSpec((tm, w), lambda j, i: (i, 0))
    w_specs = [pl.BlockSpec((None, w, tn), functools.partial(lambda j, i, n: (n, 0, j), n=n)) for n in range(N_BRANCH)]
    g_specs = [pl.BlockSpec((tm, tn), functools.partial(lambda j, i, n: (i, g0 + n * gper + j), n=n))
               for n in range(N_BRANCH)]
    return pl.pallas_call(
        _merge_body,
        grid=(d // tn, m // tm),
        in_specs=[y_spec, y_spec, y_spec] + w_specs + g_specs,
        out_specs=pl.BlockSpec((tm, tn), lambda j, i: (i, j)),
        out_shape=jax.ShapeDtypeStruct((m, d), BF16),
        compiler_params=_cparams(2),
        name="branch_merge",
    )(ya, yb, yc, w_branch, w_branch, w_branch, slab2d, slab2d, slab2d)


def _out_body(mg_ref, w_ref, x_ref, g_ref, *out_refs, last):
    x_new = x_ref[...] + jnp.dot(mg_ref[...], w_ref[...], preferred_element_type=F32)
    y = x_new * lax.rsqrt(jnp.mean(x_new * x_new, axis=-1, keepdims=True) + EPS) * g_ref[...]
    if last:
        out_refs[0][...] = y
    else:
        out_refs[0][...] = x_new
        out_refs[1][...] = y.astype(out_refs[1].dtype)


def out_projection(merged, w_out, x2d, gain, last, tm=512):
    m, d = x2d.shape
    row = pl.BlockSpec((tm, d), lambda i: (i, 0))
    if last:
        out_specs, out_shape = row, jax.ShapeDtypeStruct((m, d), F32)
    else:
        out_specs = [row, row]
        out_shape = [jax.ShapeDtypeStruct((m, d), F32), jax.ShapeDtypeStruct((m, d), BF16)]
    return pl.pallas_call(
        functools.partial(_out_body, last=last),
        grid=(m // tm,),
        in_specs=[row, pl.BlockSpec((d, d), lambda i: (0, 0)), row, pl.BlockSpec((1, d), lambda i: (0, 0))],
        out_specs=out_specs,
        out_shape=out_shape,
        compiler_params=_cparams(1),
        name="out_proj",
    )(merged, w_out, x2d, gain.reshape(1, d))


def _input_weights(w_in_l):
    sizes = (A_W, A_W, A_W, A_W, B_QK_W, B_QK_W, B_V_W, B_HEADS, B_HEADS, B_V_W, B_V_W,
             C_Q_RANK, C_KV_RANK, C_ROPE, C_W, N_BRANCH * D_MODEL)
    points = np.cumsum(sizes)[:-1].tolist()
    (qa, ka, va, za, qb, kb, vb, ib, fb, ob, zb, cq, ckv, kr, zc, gt) = jnp.split(w_in_l, points, axis=1)
    d = w_in_l.shape[0]
    half = C_ROPE // 2
    if_group = jnp.concatenate([ib, fb, jnp.zeros((d, LANES - 2 * B_HEADS), F32)], axis=1)
    kr_group = jnp.concatenate([kr, -kr[:, half:], kr[:, :half]], axis=1)
    cols = [ka, za, qb, kb, vb, ob, zb, zc, cq, ckv, if_group, kr_group, gt]
    qa_scale = A_HEAD_DIM ** -0.5 * LOG2E
    wt_qv = jnp.concatenate([qa * qa_scale, va], axis=1).T.astype(BF16)
    return jnp.concatenate(cols, axis=1).astype(BF16), wt_qv


def kernel(x, norm_gain, w_in, t5_table, mlstm_conv_w, mlstm_conv_b, mlstm_i_bias, mlstm_f_bias,
           mlstm_out_norm, mla_q_norm, mla_kv_norm, mla_w_uq, mla_w_ukv, w_branch, w_out, final_norm):
    bsz, s, d = x.shape
    m = bsz * s
    x2d = x.reshape(m, d)
    h = rmsnorm_rows(x2d, norm_gain[0])
    out = None
    for l in range(DEPTH):
        w_slab, wt_qv = _input_weights(w_in[l])
        slab2d = input_projection(h, w_slab)
        slab = slab2d.reshape(bsz, s, D_SLAB)
        qt_a, vt_a = moba_qv_projection(h, wt_qv, bsz)
        ya = moba_branch(qt_a, vt_a, slab, t5_table)
        yb = mlstm_branch(slab, mlstm_conv_w[l], mlstm_conv_b[l], mlstm_i_bias[l], mlstm_f_bias[l],
                          mlstm_out_norm[l])
        qt_c, k_c, vt_c = mla_prep(slab, mla_q_norm[l], mla_kv_norm[l], mla_w_uq[l], mla_w_ukv[l])
        yc = mla_attention(qt_c, k_c, vt_c, slab)
        merged = branch_merge(ya.reshape(m, A_W), yb.reshape(m, B_V_W), yc.reshape(m, C_W),
                              w_branch[l].astype(BF16), slab2d)
        last = l == DEPTH - 1
        gain = final_norm if last else norm_gain[l + 1]
        res = out_projection(merged, w_out[l].astype(BF16), x2d, gain, last)
        if last:
            out = res
        else:
            x2d, h = res
    return out.reshape(bsz, s, d)
```

```python
import functools
import math

import jax
import jax.numpy as jnp
import numpy as np
from jax import lax
from jax.experimental import pallas as pl
from jax.experimental.pallas import tpu as pltpu

F32 = jnp.float32
BF16 = jnp.bfloat16

D_MODEL = 2048
DEPTH = 2
EPS = 1e-6
NEG = -1e30
LOG2E = math.log2(math.e)

A_HEADS = 8
A_HEAD_DIM = 128
A_BLOCK = 256
A_TOPK = 3
T5_BUCKETS = 32
T5_MAX_DIST = 128
B_HEADS = 4
B_QK_DIM = 128
B_V_DIM = 256
B_CONV = 4
C_HEADS = 8
C_Q_RANK = 512
C_KV_RANK = 256
C_NOPE = 128
C_ROPE = 64
C_V_DIM = 128
ROPE_THETA = 10000.0
N_BRANCH = 3

A_W = A_HEADS * A_HEAD_DIM
B_QK_W = B_HEADS * B_QK_DIM
B_V_W = B_HEADS * B_V_DIM
C_W = C_HEADS * C_V_DIM

LANES = 128
SUBLANES = 8
VMEM_LIMIT_BYTES = 56 * 1024 * 1024

SPLIT_SIZES = (A_W, A_W, A_W, A_W, B_QK_W, B_QK_W, B_V_W, B_HEADS, B_HEADS, B_V_W, B_V_W,
               C_Q_RANK, C_KV_RANK, C_ROPE, C_W, N_BRANCH * D_MODEL)
(SRC_QA, SRC_KA, SRC_VA, SRC_ZA, SRC_QB, SRC_KB, SRC_VB, SRC_IB, SRC_FB, SRC_OB, SRC_ZB,
 SRC_CQ, SRC_CKV, SRC_KR, SRC_ZC, SRC_GT) = (int(v) for v in np.cumsum((0,) + SPLIT_SIZES[:-1]))
D_IN = int(sum(SPLIT_SIZES))

W_BLOCK = 1024
SLAB_A_SRC = (SRC_KA, SRC_ZA, SRC_QB, SRC_VB)
assert all(off % W_BLOCK == 0 for off in SLAB_A_SRC) and SRC_KB == SRC_QB + B_QK_W
OFF_KA = 0
OFF_ZA = OFF_KA + A_W
OFF_QKB = OFF_ZA + A_W
OFF_VB = OFF_QKB + 2 * B_QK_W
D_SLAB_A = OFF_VB + B_V_W
assert D_SLAB_A == len(SLAB_A_SRC) * W_BLOCK
OFF_OB = 0
OFF_ZB = OFF_OB + B_V_W
OFF_ZC = OFF_ZB + B_V_W
OFF_CQ = OFF_ZC + C_W
OFF_CKV = OFF_CQ + C_Q_RANK
OFF_IF = OFF_CKV + C_KV_RANK
OFF_KR = OFF_IF + LANES
OFF_GT = OFF_KR + LANES
D_SLAB_B = OFF_GT + N_BRANCH * D_MODEL
assert D_SLAB_B == 10 * W_BLOCK

ATTN_TILE = 256
ATTN_QK_PAD = 2 * LANES
ATTN_HEADS_PER_STEP = 8
assert ATTN_TILE == A_BLOCK


def _cparams(n_axes):
    return pltpu.CompilerParams(dimension_semantics=("arbitrary",) * n_axes,
                                vmem_limit_bytes=VMEM_LIMIT_BYTES)


def _silu(t):
    return t * jax.nn.sigmoid(t)


_NT = (((1,), (1,)), ((), ()))


def _rmsnorm_body(x_ref, g_ref, o_ref):
    xf = x_ref[...]
    y = xf * lax.rsqrt(jnp.mean(xf * xf, axis=-1, keepdims=True) + EPS)
    o_ref[...] = (y * g_ref[...]).astype(o_ref.dtype)


def rmsnorm_rows(x2d, gain, tm=512):
    m, d = x2d.shape
    return pl.pallas_call(
        _rmsnorm_body,
        grid=(m // tm,),
        in_specs=[pl.BlockSpec((tm, d), lambda i: (i, 0)),
                  pl.BlockSpec((1, d), lambda i: (0, 0))],
        out_specs=pl.BlockSpec((tm, d), lambda i: (i, 0)),
        out_shape=jax.ShapeDtypeStruct((m, d), BF16),
        compiler_params=_cparams(1),
        name="rmsnorm",
    )(x2d, gain.reshape(1, d))


def _proj_body(h_ref, w_ref, o_ref):
    o_ref[...] = jnp.dot(h_ref[...], w_ref[...], preferred_element_type=F32).astype(o_ref.dtype)


def input_projection(h2d, w_slab, tm=1024, tn=1024):
    m, d = h2d.shape
    n = w_slab.shape[1]
    return pl.pallas_call(
        _proj_body,
        grid=(n // tn, m // tm),
        in_specs=[pl.BlockSpec((tm, d), lambda j, i: (i, 0)),
                  pl.BlockSpec((d, tn), lambda j, i: (0, j))],
        out_specs=pl.BlockSpec((tm, tn), lambda j, i: (i, j)),
        out_shape=jax.ShapeDtypeStruct((m, n), BF16),
        compiler_params=_cparams(2),
        name="input_proj",
    )(h2d, w_slab)


def _proj_f32w_body(h_ref, w_ref, o_ref, wb_s):
    @pl.when(pl.program_id(1) == 0)
    def _cast_weights():
        wb_s[...] = w_ref[...].astype(wb_s.dtype)

    o_ref[...] = jnp.dot(h_ref[...], wb_s[...], preferred_element_type=F32).astype(o_ref.dtype)


def input_projection_in_place(h2d, w_in, layer, src_offsets, tm=1024):
    m, d = h2d.shape
    tn = W_BLOCK
    n_blk = len(src_offsets)
    src_blocks = [off // tn for off in src_offsets]
    steps = [b - a for a, b in zip(src_blocks, src_blocks[1:])]
    assert all(st in (1, 2) for st in steps)

    def src_block(j):
        blk = src_blocks[0] + j
        for pos, st in enumerate(steps):
            if st == 2:
                blk = blk + jnp.minimum(jnp.maximum(j - pos, 0), 1)
        return blk

    return pl.pallas_call(
        _proj_f32w_body,
        grid=(n_blk, m // tm),
        in_specs=[pl.BlockSpec((tm, d), lambda j, i: (i, 0)),
                  pl.BlockSpec((None, d, tn), lambda j, i: (layer, 0, src_block(j)))],
        out_specs=pl.BlockSpec((tm, tn), lambda j, i: (i, j)),
        out_shape=jax.ShapeDtypeStruct((m, n_blk * tn), BF16),
        scratch_shapes=[pltpu.VMEM((d, tn), BF16)],
        compiler_params=_cparams(2),
        name="input_proj_a",
    )(h2d, w_in)


def _qv_weights_t_body(wq_ref, wv_ref, o_ref):
    q_scale = A_HEAD_DIM ** -0.5 * LOG2E
    o_ref[0] = (wq_ref[...] * q_scale).T.astype(o_ref.dtype)
    o_ref[1] = wv_ref[...].T.astype(o_ref.dtype)


def moba_qv_weights_t(w_in, layer, tc=256):
    d = w_in.shape[1]
    nq, nv = SRC_QA // tc, SRC_VA // tc
    assert SRC_QA % tc == 0 and SRC_VA % tc == 0 and A_W % tc == 0
    out = pl.pallas_call(
        _qv_weights_t_body,
        grid=(A_W // tc,),
        in_specs=[pl.BlockSpec((None, d, tc), lambda c: (layer, 0, nq + c)),
                  pl.BlockSpec((None, d, tc), lambda c: (layer, 0, nv + c))],
        out_specs=pl.BlockSpec((2, tc, d), lambda c: (0, c, 0)),
        out_shape=jax.ShapeDtypeStruct((2, A_W, d), BF16),
        compiler_params=_cparams(1),
        name="moba_qv_weights_t",
    )(w_in, w_in)
    return out.reshape(2 * A_W, d)


def _proj_t_body(h_ref, wt_ref, qt_ref, vt_ref, *, n_sub):
    t = ATTN_TILE
    hd = A_HEAD_DIM
    hv = hd + V_AUG_ROWS
    res = lax.dot_general(wt_ref[...], h_ref[...], _NT, preferred_element_type=F32)
    aug = _v_aug_rows(t, vt_ref.dtype)
    for c in range(n_sub):
        qt_ref[c] = res[:A_W, c * t:(c + 1) * t].astype(qt_ref.dtype)
        for h in range(A_HEADS):
            vt_ref[c, h * hv:h * hv + hd, :] = res[A_W + h * hd:A_W + (h + 1) * hd, c * t:(c + 1) * t
                                                   ].astype(vt_ref.dtype)
            vt_ref[c, h * hv + hd:(h + 1) * hv, :] = aug


def moba_qv_projection(h2d, wt_qv, bsz, tm=512):
    m, d = h2d.shape
    t = ATTN_TILE
    tm = min(tm, m // bsz)
    n_sub = tm // t
    nt = m // bsz // t
    steps = m // bsz // tm
    vt_rows = A_HEADS * (A_HEAD_DIM + V_AUG_ROWS)
    return pl.pallas_call(
        functools.partial(_proj_t_body, n_sub=n_sub),
        grid=(bsz, steps),
        in_specs=[pl.BlockSpec((tm, d), lambda b, i: (b * steps + i, 0)),
                  pl.BlockSpec((2 * A_W, d), lambda b, i: (0, 0))],
        out_specs=[pl.BlockSpec((None, n_sub, A_W, t), lambda b, i: (b, i, 0, 0)),
                   pl.BlockSpec((None, n_sub, vt_rows, t), lambda b, i: (b, i, 0, 0))],
        out_shape=[jax.ShapeDtypeStruct((bsz, nt, A_W, t), BF16),
                   jax.ShapeDtypeStruct((bsz, nt, vt_rows, t), BF16)],
        compiler_params=_cparams(2),
        name="moba_qv_proj",
    )(h2d, wt_qv)


V_AUG_ROWS = 16


def _v_aug_rows(t, dtype):
    return jnp.where(lax.broadcasted_iota(jnp.int32, (V_AUG_ROWS, t), 0) == 0, 1.0, 0.0).astype(dtype)


def _softmax_init(m_s, acc_s):
    m_s[...] = jnp.full(m_s.shape, -jnp.inf, F32)
    acc_s[...] = jnp.zeros(acc_s.shape, F32)


def _softmax_step_t(s, vt_blk, g, m_s, acc_s):
    m_prev = m_s[g]
    m_new = jnp.maximum(m_prev, jnp.max(s, axis=0, keepdims=True))
    alpha = jnp.exp2(m_prev - m_new)
    p = jnp.exp2(s - m_new)
    acc_s[g] = alpha * acc_s[g] + jnp.dot(vt_blk, p.astype(BF16), preferred_element_type=F32)
    m_s[g] = m_new


def _attn_scratch(g_heads, t, dv):
    state = [pltpu.VMEM((g_heads, 1, t), F32), pltpu.VMEM((g_heads, dv + V_AUG_ROWS, t), F32)]
    return state + [pltpu.VMEM((2, t, t), F32)] * g_heads


def _skewed_blocks(lead, n_uniform, uniform_scores, consume):
    lead[0][0](0)
    pend_blk, pend_slot = lead[0][1], 0
    for fn, blk in lead[1:]:
        fn(1 - pend_slot)
        consume(pend_blk, pend_slot)
        pend_blk, pend_slot = blk, 1 - pend_slot
    a, b = pend_slot, 1 - pend_slot

    def two_blocks(jj, pend):
        uniform_scores(2 * jj, b)
        consume(pend, a)
        uniform_scores(2 * jj + 1, a)
        consume(2 * jj, b)
        return 2 * jj + 1

    pend = lax.fori_loop(0, n_uniform // 2, two_blocks, pend_blk)

    @pl.when(n_uniform % 2 == 1)
    def _odd_tail():
        uniform_scores(n_uniform - 1, b)
        consume(pend, a)
        consume(n_uniform - 1, b)

    @pl.when(n_uniform % 2 == 0)
    def _even_tail():
        consume(pend, a)


def _attn_finish(z_ref, o_ref, acc_s, g_heads, dv):
    for g in range(g_heads):
        out_t = acc_s[g, :dv, :] / acc_s[g, dv:dv + 1, :]
        o_ref[:, g * dv:(g + 1) * dv] = (out_t.T * _silu(z_ref[:, g * dv:(g + 1) * dv].astype(F32))
                                         ).astype(o_ref.dtype)


def _t5_thresholds():
    max_exact = T5_BUCKETS // 2
    d = np.arange(0, 4 * T5_MAX_DIST, dtype=np.int64)
    dd = np.maximum(d, 1).astype(np.float32)
    large = max_exact + (np.log(dd / np.float32(max_exact)) / np.float32(math.log(T5_MAX_DIST / max_exact))
                         * np.float32(T5_BUCKETS - max_exact)).astype(np.int32)
    large = np.minimum(large, T5_BUCKETS - 1)
    bucket = np.where(d < max_exact, d, large)
    assert np.all(np.diff(bucket) >= 0) and bucket[-1] == T5_BUCKETS - 1
    return [int(np.argmax(bucket >= k)) for k in range(T5_BUCKETS)]


_T5_THRESH = _t5_thresholds()

MASK_BIG = 1e30


def _moba_body(t5_ref, qt_ref, k_ref, vt_ref, z_ref, o_ref,
               kaug_s, km_s, bias_own_s, bias_prev_s, qa_s, m_s, acc_s, *s_refs, n_blocks, g_heads):
    hg = pl.program_id(0)
    b = pl.program_id(1)
    i = pl.program_id(2)
    t, hd = A_BLOCK, A_HEAD_DIM
    hv = hd + V_AUG_ROWS
    seq = n_blocks * t
    nb_pad = -(-n_blocks // SUBLANES) * SUBLANES

    @pl.when((hg == 0) & (b == 0) & (i == 0))
    def _mask_columns():
        key_blk = lax.broadcasted_iota(jnp.int32, (seq, LANES), 0) // t
        lane = lax.broadcasted_iota(jnp.int32, (seq, LANES), 1)
        pattern = jnp.where(lane == key_blk, -MASK_BIG, 0.0).astype(kaug_s.dtype)
        for g in range(g_heads):
            kaug_s[g, :, hd:] = pattern

    @pl.when((b == 0) & (i == 0))
    def _build_bias():
        key = lax.broadcasted_iota(jnp.int32, (t, t), 0)
        qry = lax.broadcasted_iota(jnp.int32, (t, t), 1)
        for g in range(g_heads):
            head = hg * g_heads + g
            far = t5_ref[T5_BUCKETS - 1, head]
            for dist, dst in ((qry - key, bias_own_s), (t + qry - key, bias_prev_s)):
                bias = jnp.full((t, t), t5_ref[0, head], F32)
                for kk in range(1, T5_BUCKETS):
                    bias = jnp.where(dist >= _T5_THRESH[kk], t5_ref[kk, head], bias)
                dst[g] = (bias - far) * LOG2E

    @pl.when(i == 0)
    def _new_sequence():
        for g in range(g_heads):
            kaug_s[g, :, :hd] = k_ref[:, g * hd:(g + 1) * hd]
            km_s[g] = jnp.zeros(km_s.shape[1:], km_s.dtype)
            for nb in range(n_blocks):
                mean = jnp.mean(k_ref[nb * t:(nb + 1) * t, g * hd:(g + 1) * hd].astype(F32), axis=0, keepdims=True)
                hi = mean.astype(BF16)
                km_s[g, nb:nb + 1, :] = hi
                km_s[g, LANES + nb:LANES + nb + 1, :] = (mean - hi.astype(F32)).astype(BF16)

    blk_row = lax.broadcasted_iota(jnp.int32, (nb_pad, t), 0)
    past = blk_row < i
    for g in range(g_heads):
        qt = qt_ref[g * hd:(g + 1) * hd, :]
        g2 = jnp.dot(km_s[g], qt, preferred_element_type=F32)
        gate = g2[:nb_pad, :] + g2[LANES:LANES + nb_pad, :]
        cur = jnp.where(past, gate, NEG)
        keep = blk_row == i
        for _ in range(A_TOPK):
            mx = jnp.max(cur, axis=0, keepdims=True)
            idx = jnp.min(jnp.where(cur == mx, blk_row, nb_pad), axis=0, keepdims=True)
            pick = blk_row == idx
            keep = keep | (pick & past)
            cur = jnp.where(pick, -jnp.inf, cur)
        unsel = jnp.where(keep, 0.0, 1.0)
        qa_s[g, :hd, :] = qt
        qa_s[g, hd:, :] = jnp.concatenate([unsel, jnp.zeros((LANES - nb_pad, t), F32)], axis=0).astype(qa_s.dtype)

    _softmax_init(m_s, acc_s)

    def raw_scores(g, j):
        start = pl.multiple_of(j * t, t)
        return jnp.dot(kaug_s[g, pl.ds(start, t), :], qa_s[g], preferred_element_type=F32)

    def own_scores(slot):
        key = lax.broadcasted_iota(jnp.int32, (t, t), 0)
        qry = lax.broadcasted_iota(jnp.int32, (t, t), 1)
        for g in range(g_heads):
            s_refs[g][slot] = jnp.where(key <= qry, raw_scores(g, i) + bias_own_s[g], NEG)

    j_prev = jnp.maximum(i - 1, 0)
    no_prev = jnp.where(i == 0, -MASK_BIG, 0.0)

    def prev_scores(slot):
        for g in range(g_heads):
            s_refs[g][slot] = raw_scores(g, j_prev) + (bias_prev_s[g] + no_prev)

    def far_scores(j, slot):
        for g in range(g_heads):
            s_refs[g][slot] = raw_scores(g, j)

    def consume(j, slot):
        for g in range(g_heads):
            _softmax_step_t(s_refs[g][slot], vt_ref[j, g * hv:(g + 1) * hv, :], g, m_s, acc_s)

    _skewed_blocks([(own_scores, i), (prev_scores, j_prev)], j_prev, far_scores, consume)
    _attn_finish(z_ref, o_ref, acc_s, g_heads, hd)


def moba_branch(qt, vt, slab, t5_table, g_heads=ATTN_HEADS_PER_STEP):
    bsz, nb, _, t = qt.shape
    s = nb * t
    hd = A_HEAD_DIM
    gw = g_heads * hd
    assert t == A_BLOCK and nb <= LANES and A_HEADS % g_heads == 0
    assert OFF_KA % gw == 0 and OFF_ZA % gw == 0
    assert _T5_THRESH[T5_BUCKETS - 1] <= t + 1
    ck, cz = OFF_KA // gw, OFF_ZA // gw
    return pl.pallas_call(
        functools.partial(_moba_body, n_blocks=nb, g_heads=g_heads),
        grid=(A_HEADS // g_heads, bsz, nb),
        in_specs=[pl.BlockSpec(memory_space=pltpu.SMEM),
                  pl.BlockSpec((None, None, gw, t), lambda h, b, i: (b, i, h, 0)),
                  pl.BlockSpec((None, s, gw), lambda h, b, i: (b, 0, ck + h), pipeline_mode=pl.Buffered(1)),
                  pl.BlockSpec((None, nb, g_heads * (hd + V_AUG_ROWS), t), lambda h, b, i: (b, 0, h, 0),
                               pipeline_mode=pl.Buffered(1)),
                  pl.BlockSpec((None, t, gw), lambda h, b, i: (b, i, cz + h))],
        out_specs=pl.BlockSpec((None, t, gw), lambda h, b, i: (b, i, h)),
        out_shape=jax.ShapeDtypeStruct((bsz, s, A_W), BF16),
        scratch_shapes=[pltpu.VMEM((g_heads, s, ATTN_QK_PAD), BF16),
                        pltpu.VMEM((g_heads, 2 * LANES, hd), BF16),
                        pltpu.VMEM((g_heads, t, t), F32),
                        pltpu.VMEM((g_heads, t, t), F32),
                        pltpu.VMEM((g_heads, ATTN_QK_PAD, t), BF16)] + _attn_scratch(g_heads, t, hd),
        compiler_params=_cparams(3),
        name="moba",
    )(t5_table, qt, slab, vt, slab)


MLSTM_CHUNK = 256
CONV_HALO = 8


def _log_sigmoid(t):
    return jnp.minimum(t, 0.0) - jnp.log1p(jnp.exp(-jnp.abs(t)))


def _mlstm_body(qk_ref, v_ref, ob_ref, zb_ref, if_ref, cw_ref, cb_ref, gb_ref, gn_ref, o_ref,
                xe_s, c_s, n_s, m_s):
    c = pl.program_id(1)
    L = MLSTM_CHUNK
    dk, dv = B_QK_DIM, B_V_DIM

    @pl.when(c == 0)
    def _reset():
        xe_s[0:CONV_HALO, :] = jnp.zeros((CONV_HALO, 2 * B_QK_W), F32)
        c_s[...] = jnp.zeros_like(c_s)
        n_s[...] = jnp.zeros_like(n_s)
        m_s[...] = jnp.zeros_like(m_s)

    xe_s[CONV_HALO:CONV_HALO + L, :] = qk_ref[...].astype(F32)
    conv = cb_ref[...]
    for j in range(B_CONV):
        conv = conv + cw_ref[j:j + 1, :] * xe_s[pl.ds(CONV_HALO - (B_CONV - 1) + j, L), :]
    xe_s[0:CONV_HALO, :] = xe_s[L:L + CONV_HALO, :]
    qk = _silu(conv)

    gi = if_ref[...].astype(F32) + gb_ref[...]
    lf = _log_sigmoid(gi)
    row = lax.broadcasted_iota(jnp.int32, (L, L), 0)
    col = lax.broadcasted_iota(jnp.int32, (L, L), 1)
    causal = col <= row
    tri = jnp.where(causal, 1.0, 0.0).astype(F32)
    b_cols = jnp.dot(tri, lf, precision=lax.Precision.HIGHEST, preferred_element_type=F32)
    b_rows = b_cols.T
    li_rows = gi.T

    for h in range(B_HEADS):
        q_h = qk[:, h * dk:(h + 1) * dk]
        k_h = qk[:, B_QK_W + h * dk:B_QK_W + (h + 1) * dk] * (dk ** -0.5)
        v_h = v_ref[:, h * dv:(h + 1) * dv]
        b_c = b_cols[:, B_HEADS + h:B_HEADS + h + 1]
        b_r = b_rows[B_HEADS + h:B_HEADS + h + 1, :]
        li_c = gi[:, h:h + 1]
        li_r = li_rows[h:h + 1, :]
        m_prev = m_s[h:h + 1, 0:1]
        c_prev = c_s[h]
        n_prev = n_s[h:h + 1, :]

        a_c = b_c + m_prev
        dmat = jnp.where(causal, b_c - b_r + li_r, -jnp.inf)
        m_t = jnp.maximum(a_c, jnp.max(dmat, axis=-1, keepdims=True))
        w_inter = jnp.exp(a_c - m_t)
        q_b = q_h.astype(BF16)
        k_b = k_h.astype(BF16)
        sc = (lax.dot_general(q_b, k_b, _NT, preferred_element_type=F32)
              * jnp.exp(dmat - m_t))
        num = (w_inter * jnp.dot(q_b, c_prev.astype(BF16), preferred_element_type=F32)
               + jnp.dot(sc.astype(BF16), v_h, preferred_element_type=F32))
        den = (w_inter * jnp.sum(q_h * n_prev, axis=-1, keepdims=True)
               + jnp.sum(sc, axis=-1, keepdims=True))
        hh = num / jnp.maximum(jnp.abs(den), jnp.exp(-m_t))

        b_last = b_c[L - 1:L, :]
        g_c = b_last - b_c + li_c
        m_new = jnp.maximum(b_last + m_prev, jnp.max(g_c, axis=0, keepdims=True))
        decay = jnp.exp(b_last + m_prev - m_new)
        wk = jnp.exp(g_c - m_new) * k_h
        c_s[h] = decay * c_prev + lax.dot_general(wk.astype(BF16), v_h, (((0,), (0,)), ((), ())),
                                                  preferred_element_type=F32)
        n_s[h:h + 1, :] = decay * n_prev + jnp.sum(wk, axis=0, keepdims=True)
        m_s[h:h + 1, :] = jnp.broadcast_to(m_new, (1, LANES))

        mu = jnp.mean(hh, axis=-1, keepdims=True)
        dlt = hh - mu
        var = jnp.mean(dlt * dlt, axis=-1, keepdims=True)
        y = dlt * lax.rsqrt(var + EPS) * gn_ref[:, h * dv:(h + 1) * dv]
        y = (y * jax.nn.sigmoid(ob_ref[:, h * dv:(h + 1) * dv].astype(F32))
             * _silu(zb_ref[:, h * dv:(h + 1) * dv].astype(F32)))
        o_ref[:, h * dv:(h + 1) * dv] = y.astype(o_ref.dtype)


def mlstm_branch(slab_a, slab_b, conv_w, conv_b, i_bias, f_bias, out_norm):
    bsz, s, _ = slab_a.shape
    L = MLSTM_CHUNK
    assert s % L == 0
    gate_bias = jnp.zeros((1, LANES), F32).at[0, :B_HEADS].set(i_bias).at[0, B_HEADS:2 * B_HEADS].set(f_bias)
    w2 = 2 * B_QK_W
    full = lambda shape: pl.BlockSpec(shape, lambda b, c: (0,) * len(shape))
    return pl.pallas_call(
        _mlstm_body,
        grid=(bsz, s // L),
        in_specs=[pl.BlockSpec((None, L, w2), lambda b, c: (b, c, OFF_QKB // w2)),
                  pl.BlockSpec((None, L, B_V_W), lambda b, c: (b, c, OFF_VB // B_V_W)),
                  pl.BlockSpec((None, L, B_V_W), lambda b, c: (b, c, OFF_OB // B_V_W)),
                  pl.BlockSpec((None, L, B_V_W), lambda b, c: (b, c, OFF_ZB // B_V_W)),
                  pl.BlockSpec((None, L, LANES), lambda b, c: (b, c, OFF_IF // LANES)),
                  full((B_CONV, w2)), full((1, w2)), full((1, LANES)), full((1, B_V_W))],
        out_specs=pl.BlockSpec((None, L, B_V_W), lambda b, c: (b, c, 0)),
        out_shape=jax.ShapeDtypeStruct((bsz, s, B_V_W), BF16),
        scratch_shapes=[pltpu.VMEM((L + CONV_HALO, w2), F32),
                        pltpu.VMEM((B_HEADS, B_QK_DIM, B_V_DIM), F32),
                        pltpu.VMEM((8, B_QK_DIM), F32),
                        pltpu.VMEM((8, LANES), F32)],
        compiler_params=_cparams(2),
        name="mlstm",
    )(slab_a, slab_a, slab_b, slab_b, slab_b, conv_w, conv_b.reshape(1, w2), gate_bias,
      out_norm.reshape(1, B_V_W))


def _mla_prep_body(cq_ref, ckv_ref, kr_ref, qg_ref, kg_ref, wqt_ref, wqst_ref, wkn_ref, wvt_ref,
                   cos_ref, sin_ref, cost_ref, sint_ref, qt_ref, kf_ref, vt_ref):
    def normed(ref, g_ref):
        t = ref[...].astype(F32)
        y = t * lax.rsqrt(jnp.mean(t * t, axis=-1, keepdims=True) + EPS)
        return (y * g_ref[...]).astype(BF16)

    cqn = normed(cq_ref, qg_ref)
    ckvn = normed(ckv_ref, kg_ref)

    qt_main = lax.dot_general(wqt_ref[...], cqn, _NT, preferred_element_type=F32)
    qt_swap = lax.dot_general(wqst_ref[...], cqn, _NT, preferred_element_type=F32)
    cos_t = cost_ref[...]
    sin_t = sint_ref[...]
    for h in range(C_HEADS):
        lo = h * ATTN_QK_PAD
        qt_ref[lo:lo + LANES, :] = qt_main[lo:lo + LANES, :].astype(qt_ref.dtype)
        qt_ref[lo + LANES:lo + 2 * LANES, :] = (qt_main[lo + LANES:lo + 2 * LANES, :] * cos_t
                                                + qt_swap[h * LANES:(h + 1) * LANES, :] * sin_t).astype(qt_ref.dtype)

    k_nope = jnp.dot(ckvn, wkn_ref[...], preferred_element_type=F32)
    kr = kr_ref[...].astype(F32)
    k_rot = (kr * cos_ref[...] + pltpu.roll(kr, LANES // 2, axis=1) * sin_ref[...]).astype(kf_ref.dtype)
    for h in range(C_HEADS):
        lo = h * ATTN_QK_PAD
        kf_ref[:, lo:lo + LANES] = k_nope[:, h * LANES:(h + 1) * LANES].astype(kf_ref.dtype)
        kf_ref[:, lo + LANES:lo + 2 * LANES] = k_rot

    v_t = lax.dot_general(wvt_ref[...], ckvn, _NT, preferred_element_type=F32)
    hv = C_V_DIM + V_AUG_ROWS
    aug = _v_aug_rows(v_t.shape[1], vt_ref.dtype)
    for h in range(C_HEADS):
        vt_ref[h * hv:h * hv + C_V_DIM, :] = v_t[h * C_V_DIM:(h + 1) * C_V_DIM, :].astype(vt_ref.dtype)
        vt_ref[h * hv + C_V_DIM:(h + 1) * hv, :] = aug


def mla_prep(slab, q_norm, kv_norm, w_uq, w_ukv):
    bsz, s, _ = slab.shape
    tm = ATTN_TILE
    vt_rows = C_HEADS * (C_V_DIM + V_AUG_ROWS)
    half = C_ROPE // 2
    scale = (C_NOPE + C_ROPE) ** -0.5 * LOG2E
    wq = (w_uq * scale).reshape(C_Q_RANK, C_HEADS, C_NOPE + C_ROPE)
    pad = jnp.zeros((C_Q_RANK, C_HEADS, LANES - C_ROPE), F32)
    x1, x2 = wq[..., C_NOPE:C_NOPE + half], wq[..., C_NOPE + half:]
    wqt = jnp.concatenate([wq, pad], axis=-1).reshape(C_Q_RANK, C_HEADS * ATTN_QK_PAD).T.astype(BF16)
    wqst = jnp.concatenate([-x2, x1, pad], axis=-1).reshape(C_Q_RANK, C_HEADS * LANES).T.astype(BF16)
    wkv = w_ukv.reshape(C_KV_RANK, C_HEADS, C_NOPE + C_V_DIM)
    wkn = wkv[..., :C_NOPE].reshape(C_KV_RANK, C_HEADS * C_NOPE).astype(BF16)
    wvt = wkv[..., C_NOPE:].reshape(C_KV_RANK, C_W).T.astype(BF16)

    pos = jnp.arange(s, dtype=jnp.int32)
    inv = ROPE_THETA ** (-jnp.arange(half, dtype=F32) / half)
    ang = pos.astype(F32)[:, None] * inv[None, :]
    zpad = jnp.zeros((s, LANES - C_ROPE), F32)
    cos_tab = jnp.concatenate([jnp.cos(ang), jnp.cos(ang), zpad], axis=-1)
    sin_tab = jnp.concatenate([jnp.sin(ang), jnp.sin(ang), zpad], axis=-1)

    nt = s // tm
    full = lambda shape: pl.BlockSpec(shape, lambda b, i: (0,) * len(shape))
    return pl.pallas_call(
        _mla_prep_body,
        grid=(bsz, nt),
        in_specs=[pl.BlockSpec((None, tm, C_Q_RANK), lambda b, i: (b, i, OFF_CQ // C_Q_RANK)),
                  pl.BlockSpec((None, tm, C_KV_RANK), lambda b, i: (b, i, OFF_CKV // C_KV_RANK)),
                  pl.BlockSpec((None, tm, LANES), lambda b, i: (b, i, OFF_KR // LANES)),
                  full((1, C_Q_RANK)), full((1, C_KV_RANK)),
                  full(wqt.shape), full(wqst.shape), full(wkn.shape), full(wvt.shape),
                  pl.BlockSpec((tm, LANES), lambda b, i: (i, 0)),
                  pl.BlockSpec((tm, LANES), lambda b, i: (i, 0)),
                  pl.BlockSpec((LANES, tm), lambda b, i: (0, i)),
                  pl.BlockSpec((LANES, tm), lambda b, i: (0, i))],
        out_specs=[pl.BlockSpec((None, None, C_HEADS * ATTN_QK_PAD, tm), lambda b, i: (b, i, 0, 0)),
                   pl.BlockSpec((None, tm, C_HEADS * ATTN_QK_PAD), lambda b, i: (b, i, 0)),
                   pl.BlockSpec((None, None, vt_rows, tm), lambda b, i: (b, i, 0, 0))],
        out_shape=[jax.ShapeDtypeStruct((bsz, nt, C_HEADS * ATTN_QK_PAD, tm), BF16),
                   jax.ShapeDtypeStruct((bsz, s, C_HEADS * ATTN_QK_PAD), BF16),
                   jax.ShapeDtypeStruct((bsz, nt, vt_rows, tm), BF16)],
        compiler_params=_cparams(2),
        name="mla_prep",
    )(slab, slab, slab, q_norm.reshape(1, C_Q_RANK), kv_norm.reshape(1, C_KV_RANK),
      wqt, wqst, wkn, wvt, cos_tab, sin_tab, cos_tab.T, sin_tab.T)


def _mla_attn_body(qt_ref, k_ref, vt_ref, z_ref, o_ref, m_s, acc_s, *s_refs, g_heads):
    i = pl.program_id(2)
    t = ATTN_TILE
    dq, dv = ATTN_QK_PAD, C_V_DIM
    hv = dv + V_AUG_ROWS
    _softmax_init(m_s, acc_s)

    def raw_scores(g, j):
        start = pl.multiple_of(j * t, t)
        return jnp.dot(k_ref[pl.ds(start, t), g * dq:(g + 1) * dq], qt_ref[g * dq:(g + 1) * dq, :],
                       preferred_element_type=F32)

    def diagonal_scores(slot):
        key = lax.broadcasted_iota(jnp.int32, (t, t), 0)
        qry = lax.broadcasted_iota(jnp.int32, (t, t), 1)
        for g in range(g_heads):
            s_refs[g][slot] = jnp.where(key <= qry, raw_scores(g, i), NEG)

    def past_scores(j, slot):
        for g in range(g_heads):
            s_refs[g][slot] = raw_scores(g, j)

    def consume(j, slot):
        for g in range(g_heads):
            _softmax_step_t(s_refs[g][slot], vt_ref[j, g * hv:(g + 1) * hv, :], g, m_s, acc_s)

    _skewed_blocks([(diagonal_scores, i)], i, past_scores, consume)
    _attn_finish(z_ref, o_ref, acc_s, g_heads, dv)


def mla_attention(qt, kf, vt, slab, g_heads=ATTN_HEADS_PER_STEP):
    bsz, nt, _, t = qt.shape
    s = nt * t
    dq, dv = ATTN_QK_PAD, C_V_DIM
    assert t == ATTN_TILE and C_HEADS % g_heads == 0 and OFF_ZC % (g_heads * dv) == 0
    cz = OFF_ZC // (g_heads * dv)
    return pl.pallas_call(
        functools.partial(_mla_attn_body, g_heads=g_heads),
        grid=(bsz, C_HEADS // g_heads, nt),
        in_specs=[pl.BlockSpec((None, None, g_heads * dq, t), lambda b, h, i: (b, i, h, 0)),
                  pl.BlockSpec((None, s, g_heads * dq), lambda b, h, i: (b, 0, h),
                               pipeline_mode=pl.Buffered(1)),
                  pl.BlockSpec((None, nt, g_heads * (dv + V_AUG_ROWS), t), lambda b, h, i: (b, 0, h, 0),
                               pipeline_mode=pl.Buffered(1)),
                  pl.BlockSpec((None, t, g_heads * dv), lambda b, h, i: (b, i, cz + h))],
        out_specs=pl.BlockSpec((None, t, g_heads * dv), lambda b, h, i: (b, i, h)),
        out_shape=jax.ShapeDtypeStruct((bsz, s, C_W), BF16),
        scratch_shapes=_attn_scratch(g_heads, t, dv),
        compiler_params=_cparams(3),
        name="mla_attn",
    )(qt, kf, vt, slab)


def _merge_body(ya_ref, yb_ref, yc_ref, wa_ref, wb_ref, wc_ref, ga_ref, gb_ref, gc_ref, o_ref, w_s):
    @pl.when(pl.program_id(1) == 0)
    def _cast_weights():
        for n, w_ref in enumerate((wa_ref, wb_ref, wc_ref)):
            w_s[n] = w_ref[...].astype(w_s.dtype)

    acc = None
    for n, (y_ref, g_ref) in enumerate(((ya_ref, ga_ref), (yb_ref, gb_ref), (yc_ref, gc_ref))):
        term = (jax.nn.sigmoid(g_ref[...].astype(F32))
                * jnp.dot(y_ref[...], w_s[n], preferred_element_type=F32))
        acc = term if acc is None else acc + term
    o_ref[...] = acc.astype(o_ref.dtype)


def branch_merge(ya, yb, yc, w_branch, layer, slab_b, tm=512, tn=1024):
    m, w = ya.shape
    d = w_branch.shape[-1]
    assert OFF_GT % tn == 0
    g0 = OFF_GT // tn
    gper = d // tn
    y_spec = pl.BlockSpec((tm, w), lambda j, i: (i, 0))
    w_specs = [pl.BlockSpec((None, None, w, tn), functools.partial(lambda j, i, n: (layer, n, 0, j), n=n))
               for n in range(N_BRANCH)]
    g_specs = [pl.BlockSpec((tm, tn), functools.partial(lambda j, i, n: (i, g0 + n * gper + j), n=n))
               for n in range(N_BRANCH)]
    return pl.pallas_call(
        _merge_body,
        grid=(d // tn, m // tm),
        in_specs=[y_spec, y_spec, y_spec] + w_specs + g_specs,
        out_specs=pl.BlockSpec((tm, tn), lambda j, i: (i, j)),
        out_shape=jax.ShapeDtypeStruct((m, d), BF16),
        scratch_shapes=[pltpu.VMEM((N_BRANCH, w, tn), BF16)],
        compiler_params=_cparams(2),
        name="branch_merge",
    )(ya, yb, yc, w_branch, w_branch, w_branch, slab_b, slab_b, slab_b)


def _out_body(mg_ref, w_ref, x_ref, g_ref, *refs, last):
    out_refs, w_s = refs[:-1], refs[-1]

    @pl.when(pl.program_id(0) == 0)
    def _cast_weights():
        w_s[...] = w_ref[...].astype(w_s.dtype)

    x_new = x_ref[...] + jnp.dot(mg_ref[...], w_s[...], preferred_element_type=F32)
    y = x_new * lax.rsqrt(jnp.mean(x_new * x_new, axis=-1, keepdims=True) + EPS) * g_ref[...]
    if last:
        out_refs[0][...] = y
    else:
        out_refs[0][...] = x_new
        out_refs[1][...] = y.astype(out_refs[1].dtype)


def out_projection(merged, w_out, layer, x2d, gain, last, tm=256):
    m, d = x2d.shape
    row = pl.BlockSpec((tm, d), lambda i: (i, 0))
    if last:
        out_specs, out_shape = row, jax.ShapeDtypeStruct((m, d), F32)
    else:
        out_specs = [row, row]
        out_shape = [jax.ShapeDtypeStruct((m, d), F32), jax.ShapeDtypeStruct((m, d), BF16)]
    return pl.pallas_call(
        functools.partial(_out_body, last=last),
        grid=(m // tm,),
        in_specs=[row,
                  pl.BlockSpec((None, d, d), lambda i: (layer, 0, 0), pipeline_mode=pl.Buffered(1)),
                  row, pl.BlockSpec((1, d), lambda i: (0, 0))],
        out_specs=out_specs,
        out_shape=out_shape,
        scratch_shapes=[pltpu.VMEM((d, d), BF16)],
        compiler_params=_cparams(1),
        name="out_proj",
    )(merged, w_out, x2d, gain.reshape(1, d))


def _slab_b_weights(w_in, layer):
    def cols(start, width):
        return w_in[layer, :, start:start + width].astype(BF16)

    d = w_in.shape[1]
    half = C_ROPE // 2
    if_group = jnp.concatenate([cols(SRC_IB, 2 * B_HEADS), jnp.zeros((d, LANES - 2 * B_HEADS), BF16)], axis=1)
    kr_group = jnp.concatenate([cols(SRC_KR, C_ROPE), -cols(SRC_KR + half, half), cols(SRC_KR, half)], axis=1)
    pieces = [cols(SRC_OB, B_V_W), cols(SRC_ZB, B_V_W), cols(SRC_ZC, C_W), cols(SRC_CQ, C_Q_RANK),
              cols(SRC_CKV, C_KV_RANK), if_group, kr_group, cols(SRC_GT, N_BRANCH * D_MODEL)]
    return jnp.concatenate(pieces, axis=1)


def kernel(x, norm_gain, w_in, t5_table, mlstm_conv_w, mlstm_conv_b, mlstm_i_bias, mlstm_f_bias,
           mlstm_out_norm, mla_q_norm, mla_kv_norm, mla_w_uq, mla_w_ukv, w_branch, w_out, final_norm):
    bsz, s, d = x.shape
    m = bsz * s
    x2d = x.reshape(m, d)
    h = rmsnorm_rows(x2d, norm_gain[0])
    out = None
    for l in range(DEPTH):
        slab_a2d = input_projection_in_place(h, w_in, l, SLAB_A_SRC)
        slab_b2d = input_projection(h, _slab_b_weights(w_in, l))
        slab_a = slab_a2d.reshape(bsz, s, D_SLAB_A)
        slab_b = slab_b2d.reshape(bsz, s, D_SLAB_B)
        qt_a, vt_a = moba_qv_projection(h, moba_qv_weights_t(w_in, l), bsz)
        ya = moba_branch(qt_a, vt_a, slab_a, t5_table)
        yb = mlstm_branch(slab_a, slab_b, mlstm_conv_w[l], mlstm_conv_b[l], mlstm_i_bias[l], mlstm_f_bias[l],
                          mlstm_out_norm[l])
        qt_c, k_c, vt_c = mla_prep(slab_b, mla_q_norm[l], mla_kv_norm[l], mla_w_uq[l], mla_w_ukv[l])
        yc = mla_attention(qt_c, k_c, vt_c, slab_b)
        merged = branch_merge(ya.reshape(m, A_W), yb.reshape(m, B_V_W), yc.reshape(m, C_W),
                              w_branch, l, slab_b2d)
        last = l == DEPTH - 1
        gain = final_norm if last else norm_gain[l + 1]
        res = out_projection(merged, w_out, l, x2d, gain, last)
        if last:
            out = res
        else:
            x2d, h = res
    return out.reshape(bsz, s, d)
```

```python
import functools
import math

import jax
import jax.numpy as jnp
import numpy as np
from jax import lax
from jax.experimental import pallas as pl
from jax.experimental.pallas import tpu as pltpu

F32 = jnp.float32
BF16 = jnp.bfloat16

D_MODEL = 2048
DEPTH = 2
EPS = 1e-6
NEG = -1e30
LOG2E = math.log2(math.e)

A_HEADS = 8
A_HEAD_DIM = 128
A_BLOCK = 256
A_TOPK = 3
T5_BUCKETS = 32
T5_MAX_DIST = 128
B_HEADS = 4
B_QK_DIM = 128
B_V_DIM = 256
B_CONV = 4
C_HEADS = 8
C_Q_RANK = 512
C_KV_RANK = 256
C_NOPE = 128
C_ROPE = 64
C_V_DIM = 128
ROPE_THETA = 10000.0
N_BRANCH = 3

A_W = A_HEADS * A_HEAD_DIM
B_QK_W = B_HEADS * B_QK_DIM
B_V_W = B_HEADS * B_V_DIM
C_W = C_HEADS * C_V_DIM

LANES = 128
SUBLANES = 8
VMEM_LIMIT_BYTES = 56 * 1024 * 1024

SPLIT_SIZES = (A_W, A_W, A_W, A_W, B_QK_W, B_QK_W, B_V_W, B_HEADS, B_HEADS, B_V_W, B_V_W,
               C_Q_RANK, C_KV_RANK, C_ROPE, C_W, N_BRANCH * D_MODEL)
(SRC_QA, SRC_KA, SRC_VA, SRC_ZA, SRC_QB, SRC_KB, SRC_VB, SRC_IB, SRC_FB, SRC_OB, SRC_ZB,
 SRC_CQ, SRC_CKV, SRC_KR, SRC_ZC, SRC_GT) = (int(v) for v in np.cumsum((0,) + SPLIT_SIZES[:-1]))
D_IN = int(sum(SPLIT_SIZES))

W_BLOCK = 1024
SLAB_SRC_ROWS = ((SRC_KA, SRC_ZA, SRC_QB, SRC_VB, SRC_OB, SRC_ZB, SRC_ZC)
                 + tuple(SRC_GT + k * W_BLOCK for k in range(N_BRANCH * D_MODEL // W_BLOCK)))
assert SRC_KB == SRC_QB + B_QK_W and all(r % SUBLANES == 0 for r in SLAB_SRC_ROWS)
OFF_KA = 0
OFF_ZA = OFF_KA + W_BLOCK
OFF_QKB = OFF_ZA + W_BLOCK
OFF_VB = OFF_QKB + W_BLOCK
OFF_OB = OFF_VB + W_BLOCK
OFF_ZB = OFF_OB + W_BLOCK
OFF_ZC = OFF_ZB + W_BLOCK
OFF_GT = OFF_ZC + W_BLOCK
D_SLAB = len(SLAB_SRC_ROWS) * W_BLOCK
SMALL_WIN1_ROWS = C_Q_RANK + C_KV_RANK + LANES
SMALL_WIN2_ROWS = LANES
assert SRC_CKV == SRC_CQ + C_Q_RANK and SRC_KR == SRC_CKV + C_KV_RANK and SRC_FB == SRC_IB + B_HEADS
assert SRC_CQ % SUBLANES == 0 and SRC_IB % SUBLANES == 0
assert SRC_CQ + SMALL_WIN1_ROWS <= D_IN and SRC_IB + SMALL_WIN2_ROWS <= D_IN
OFF_CQ = 0
OFF_CKV = OFF_CQ + C_Q_RANK
OFF_KR = OFF_CKV + C_KV_RANK
OFF_IF = OFF_KR + LANES
D_SLAB_SMALL = SMALL_WIN1_ROWS + SMALL_WIN2_ROWS
assert D_SLAB_SMALL == W_BLOCK

ATTN_TILE = 256
ATTN_QK_PAD = 2 * LANES
ATTN_HEADS_PER_STEP = 8
assert ATTN_TILE == A_BLOCK


def _cparams(n_axes):
    return pltpu.CompilerParams(dimension_semantics=("arbitrary",) * n_axes,
                                vmem_limit_bytes=VMEM_LIMIT_BYTES)


def _silu(t):
    return t * jax.nn.sigmoid(t)


_NT = (((1,), (1,)), ((), ()))


def _rmsnorm_body(x_ref, g_ref, o_ref):
    xf = x_ref[...]
    y = xf * lax.rsqrt(jnp.mean(xf * xf, axis=-1, keepdims=True) + EPS)
    o_ref[...] = (y * g_ref[...]).astype(o_ref.dtype)


def rmsnorm_rows(x2d, gain, tm=512):
    m, d = x2d.shape
    return pl.pallas_call(
        _rmsnorm_body,
        grid=(m // tm,),
        in_specs=[pl.BlockSpec((tm, d), lambda i: (i, 0)),
                  pl.BlockSpec((1, d), lambda i: (0, 0))],
        out_specs=pl.BlockSpec((tm, d), lambda i: (i, 0)),
        out_shape=jax.ShapeDtypeStruct((m, d), BF16),
        compiler_params=_cparams(1),
        name="rmsnorm",
    )(x2d, gain.reshape(1, d))


def _row_window(rows, d):
    return (pl.Squeezed(), pl.Element(rows), pl.Element(d))


def _proj_body(rows_ref, h_ref, w_ref, o_ref, wb_s):
    del rows_ref
    @pl.when(pl.program_id(1) == 0)
    def _cast_weights():
        wb_s[...] = w_ref[...].astype(wb_s.dtype)

    o_ref[...] = lax.dot_general(h_ref[...], wb_s[...], _NT, preferred_element_type=F32).astype(o_ref.dtype)


def input_projection(h2d, w_in_t, layer, src_rows, tm=1024):
    m, d = h2d.shape
    tn = W_BLOCK
    grid_spec = pltpu.PrefetchScalarGridSpec(
        num_scalar_prefetch=1,
        grid=(len(src_rows), m // tm),
        in_specs=[pl.BlockSpec((tm, d), lambda j, i, rows: (i, 0)),
                  pl.BlockSpec(_row_window(tn, d),
                               lambda j, i, rows: (layer, pl.multiple_of(rows[j], SUBLANES), 0))],
        out_specs=pl.BlockSpec((tm, tn), lambda j, i, rows: (i, j)),
        scratch_shapes=[pltpu.VMEM((tn, d), BF16)])
    return pl.pallas_call(
        _proj_body,
        grid_spec=grid_spec,
        out_shape=jax.ShapeDtypeStruct((m, len(src_rows) * tn), BF16),
        compiler_params=_cparams(2),
        name="input_proj",
    )(jnp.asarray(src_rows, jnp.int32), h2d, w_in_t)


def _proj_small_body(h_ref, w1_ref, w2_ref, o_ref, wb_s):
    @pl.when(pl.program_id(0) == 0)
    def _cast_weights():
        wb_s[:SMALL_WIN1_ROWS, :] = w1_ref[...].astype(wb_s.dtype)
        wb_s[SMALL_WIN1_ROWS:, :] = w2_ref[...].astype(wb_s.dtype)

    o_ref[...] = lax.dot_general(h_ref[...], wb_s[...], _NT, preferred_element_type=F32).astype(o_ref.dtype)


def input_projection_small(h2d, w_in_t, layer, tm=1024):
    m, d = h2d.shape
    return pl.pallas_call(
        _proj_small_body,
        grid=(m // tm,),
        in_specs=[pl.BlockSpec((tm, d), lambda i: (i, 0)),
                  pl.BlockSpec(_row_window(SMALL_WIN1_ROWS, d), lambda i: (layer, SRC_CQ, 0)),
                  pl.BlockSpec(_row_window(SMALL_WIN2_ROWS, d), lambda i: (layer, SRC_IB, 0))],
        out_specs=pl.BlockSpec((tm, D_SLAB_SMALL), lambda i: (i, 0)),
        out_shape=jax.ShapeDtypeStruct((m, D_SLAB_SMALL), BF16),
        scratch_shapes=[pltpu.VMEM((D_SLAB_SMALL, d), BF16)],
        compiler_params=_cparams(1),
        name="input_proj_small",
    )(h2d, w_in_t, w_in_t)


def _proj_t_body(h_ref, wq_ref, wv_ref, qt_ref, vt_ref, wt_s, *, n_sub):
    t = ATTN_TILE
    hd = A_HEAD_DIM
    hv = hd + V_AUG_ROWS

    @pl.when((pl.program_id(0) == 0) & (pl.program_id(1) == 0))
    def _cast_weights():
        q_scale = A_HEAD_DIM ** -0.5 * LOG2E
        wt_s[:A_W, :] = (wq_ref[...] * q_scale).astype(wt_s.dtype)
        wt_s[A_W:, :] = wv_ref[...].astype(wt_s.dtype)

    res = lax.dot_general(wt_s[...], h_ref[...], _NT, preferred_element_type=F32)
    aug = _v_aug_rows(t, vt_ref.dtype)
    for c in range(n_sub):
        qt_ref[c] = res[:A_W, c * t:(c + 1) * t].astype(qt_ref.dtype)
        for h in range(A_HEADS):
            vt_ref[c, h * hv:h * hv + hd, :] = res[A_W + h * hd:A_W + (h + 1) * hd, c * t:(c + 1) * t
                                                   ].astype(vt_ref.dtype)
            vt_ref[c, h * hv + hd:(h + 1) * hv, :] = aug


def moba_qv_projection(h2d, w_in_t, layer, bsz, tm=512):
    m, d = h2d.shape
    assert SRC_QA % A_W == 0 and SRC_VA % A_W == 0
    t = ATTN_TILE
    tm = min(tm, m // bsz)
    n_sub = tm // t
    nt = m // bsz // t
    steps = m // bsz // tm
    vt_rows = A_HEADS * (A_HEAD_DIM + V_AUG_ROWS)
    return pl.pallas_call(
        functools.partial(_proj_t_body, n_sub=n_sub),
        grid=(bsz, steps),
        in_specs=[pl.BlockSpec((tm, d), lambda b, i: (b * steps + i, 0)),
                  pl.BlockSpec((None, A_W, d), lambda b, i: (layer, SRC_QA // A_W, 0),
                               pipeline_mode=pl.Buffered(1)),
                  pl.BlockSpec((None, A_W, d), lambda b, i: (layer, SRC_VA // A_W, 0),
                               pipeline_mode=pl.Buffered(1))],
        out_specs=[pl.BlockSpec((None, n_sub, A_W, t), lambda b, i: (b, i, 0, 0)),
                   pl.BlockSpec((None, n_sub, vt_rows, t), lambda b, i: (b, i, 0, 0))],
        out_shape=[jax.ShapeDtypeStruct((bsz, nt, A_W, t), BF16),
                   jax.ShapeDtypeStruct((bsz, nt, vt_rows, t), BF16)],
        scratch_shapes=[pltpu.VMEM((2 * A_W, d), BF16)],
        compiler_params=_cparams(2),
        name="moba_qv_proj",
    )(h2d, w_in_t, w_in_t)


V_AUG_ROWS = 16


def _v_aug_rows(t, dtype):
    return jnp.where(lax.broadcasted_iota(jnp.int32, (V_AUG_ROWS, t), 0) == 0, 1.0, 0.0).astype(dtype)


def _softmax_init(m_s, acc_s):
    m_s[...] = jnp.full(m_s.shape, -jnp.inf, F32)
    acc_s[...] = jnp.zeros(acc_s.shape, F32)


def _softmax_step_t(s, vt_blk, g, m_s, acc_s):
    m_prev = m_s[g]
    m_new = jnp.maximum(m_prev, jnp.max(s, axis=0, keepdims=True))
    alpha = jnp.exp2(m_prev - m_new)
    p = jnp.exp2(s - m_new)
    acc_s[g] = alpha * acc_s[g] + jnp.dot(vt_blk, p.astype(BF16), preferred_element_type=F32)
    m_s[g] = m_new


def _attn_scratch(g_heads, t, dv):
    state = [pltpu.VMEM((g_heads, 1, t), F32), pltpu.VMEM((g_heads, dv + V_AUG_ROWS, t), F32)]
    return state + [pltpu.VMEM((2, t, t), F32)] * g_heads


def _skewed_blocks(lead, n_uniform, uniform_scores, consume):
    lead[0][0](0)
    pend_blk, pend_slot = lead[0][1], 0
    for fn, blk in lead[1:]:
        fn(1 - pend_slot)
        consume(pend_blk, pend_slot)
        pend_blk, pend_slot = blk, 1 - pend_slot
    a, b = pend_slot, 1 - pend_slot

    def two_blocks(jj, pend):
        uniform_scores(2 * jj, b)
        consume(pend, a)
        uniform_scores(2 * jj + 1, a)
        consume(2 * jj, b)
        return 2 * jj + 1

    pend = lax.fori_loop(0, n_uniform // 2, two_blocks, pend_blk)

    @pl.when(n_uniform % 2 == 1)
    def _odd_tail():
        uniform_scores(n_uniform - 1, b)
        consume(pend, a)
        consume(n_uniform - 1, b)

    @pl.when(n_uniform % 2 == 0)
    def _even_tail():
        consume(pend, a)


def _attn_finish(z_ref, o_ref, acc_s, g_heads, dv):
    for g in range(g_heads):
        out_t = acc_s[g, :dv, :] / acc_s[g, dv:dv + 1, :]
        o_ref[:, g * dv:(g + 1) * dv] = (out_t.T * _silu(z_ref[:, g * dv:(g + 1) * dv].astype(F32))
                                         ).astype(o_ref.dtype)


def _t5_thresholds():
    max_exact = T5_BUCKETS // 2
    d = np.arange(0, 4 * T5_MAX_DIST, dtype=np.int64)
    dd = np.maximum(d, 1).astype(np.float32)
    large = max_exact + (np.log(dd / np.float32(max_exact)) / np.float32(math.log(T5_MAX_DIST / max_exact))
                         * np.float32(T5_BUCKETS - max_exact)).astype(np.int32)
    large = np.minimum(large, T5_BUCKETS - 1)
    bucket = np.where(d < max_exact, d, large)
    assert np.all(np.diff(bucket) >= 0) and bucket[-1] == T5_BUCKETS - 1
    return [int(np.argmax(bucket >= k)) for k in range(T5_BUCKETS)]


_T5_THRESH = _t5_thresholds()

MASK_BIG = 1e30


def _moba_body(t5_ref, qt_ref, k_ref, vt_ref, z_ref, o_ref,
               kaug_s, km_s, bias_own_s, bias_prev_s, qa_s, m_s, acc_s, *s_refs, n_blocks, g_heads):
    hg = pl.program_id(0)
    b = pl.program_id(1)
    i = pl.program_id(2)
    t, hd = A_BLOCK, A_HEAD_DIM
    hv = hd + V_AUG_ROWS
    seq = n_blocks * t
    nb_pad = -(-n_blocks // SUBLANES) * SUBLANES

    @pl.when((hg == 0) & (b == 0) & (i == 0))
    def _mask_columns():
        key_blk = lax.broadcasted_iota(jnp.int32, (seq, LANES), 0) // t
        lane = lax.broadcasted_iota(jnp.int32, (seq, LANES), 1)
        pattern = jnp.where(lane == key_blk, -MASK_BIG, 0.0).astype(kaug_s.dtype)
        for g in range(g_heads):
            kaug_s[g, :, hd:] = pattern

    @pl.when((b == 0) & (i == 0))
    def _build_bias():
        key = lax.broadcasted_iota(jnp.int32, (t, t), 0)
        qry = lax.broadcasted_iota(jnp.int32, (t, t), 1)
        for g in range(g_heads):
            head = hg * g_heads + g
            far = t5_ref[T5_BUCKETS - 1, head]
            for dist, dst in ((qry - key, bias_own_s), (t + qry - key, bias_prev_s)):
                bias = jnp.full((t, t), t5_ref[0, head], F32)
                for kk in range(1, T5_BUCKETS):
                    bias = jnp.where(dist >= _T5_THRESH[kk], t5_ref[kk, head], bias)
                dst[g] = (bias - far) * LOG2E

    @pl.when(i == 0)
    def _new_sequence():
        for g in range(g_heads):
            kaug_s[g, :, :hd] = k_ref[:, g * hd:(g + 1) * hd]
            km_s[g] = jnp.zeros(km_s.shape[1:], km_s.dtype)
            for nb in range(n_blocks):
                mean = jnp.mean(k_ref[nb * t:(nb + 1) * t, g * hd:(g + 1) * hd].astype(F32), axis=0, keepdims=True)
                hi = mean.astype(BF16)
                km_s[g, nb:nb + 1, :] = hi
                km_s[g, LANES + nb:LANES + nb + 1, :] = (mean - hi.astype(F32)).astype(BF16)

    blk_row = lax.broadcasted_iota(jnp.int32, (nb_pad, t), 0)
    past = blk_row < i
    for g in range(g_heads):
        qt = qt_ref[g * hd:(g + 1) * hd, :]
        g2 = jnp.dot(km_s[g], qt, preferred_element_type=F32)
        gate = g2[:nb_pad, :] + g2[LANES:LANES + nb_pad, :]
        cur = jnp.where(past, gate, NEG)
        keep = blk_row == i
        for _ in range(A_TOPK):
            mx = jnp.max(cur, axis=0, keepdims=True)
            idx = jnp.min(jnp.where(cur == mx, blk_row, nb_pad), axis=0, keepdims=True)
            pick = blk_row == idx
            keep = keep | (pick & past)
            cur = jnp.where(pick, -jnp.inf, cur)
        unsel = jnp.where(keep, 0.0, 1.0)
        qa_s[g, :hd, :] = qt
        qa_s[g, hd:, :] = jnp.concatenate([unsel, jnp.zeros((LANES - nb_pad, t), F32)], axis=0).astype(qa_s.dtype)

    _softmax_init(m_s, acc_s)

    def raw_scores(g, j):
        start = pl.multiple_of(j * t, t)
        return jnp.dot(kaug_s[g, pl.ds(start, t), :], qa_s[g], preferred_element_type=F32)

    def own_scores(slot):
        key = lax.broadcasted_iota(jnp.int32, (t, t), 0)
        qry = lax.broadcasted_iota(jnp.int32, (t, t), 1)
        for g in range(g_heads):
            s_refs[g][slot] = jnp.where(key <= qry, raw_scores(g, i) + bias_own_s[g], NEG)

    j_prev = jnp.maximum(i - 1, 0)
    no_prev = jnp.where(i == 0, -MASK_BIG, 0.0)

    def prev_scores(slot):
        for g in range(g_heads):
            s_refs[g][slot] = raw_scores(g, j_prev) + (bias_prev_s[g] + no_prev)

    def far_scores(j, slot):
        for g in range(g_heads):
            s_refs[g][slot] = raw_scores(g, j)

    def consume(j, slot):
        for g in range(g_heads):
            _softmax_step_t(s_refs[g][slot], vt_ref[j, g * hv:(g + 1) * hv, :], g, m_s, acc_s)

    _skewed_blocks([(own_scores, i), (prev_scores, j_prev)], j_prev, far_scores, consume)
    _attn_finish(z_ref, o_ref, acc_s, g_heads, hd)


def moba_branch(qt, vt, slab, t5_table, g_heads=ATTN_HEADS_PER_STEP):
    bsz, nb, _, t = qt.shape
    s = nb * t
    hd = A_HEAD_DIM
    gw = g_heads * hd
    assert t == A_BLOCK and nb <= LANES and A_HEADS % g_heads == 0
    assert OFF_KA % gw == 0 and OFF_ZA % gw == 0
    assert _T5_THRESH[T5_BUCKETS - 1] <= t + 1
    ck, cz = OFF_KA // gw, OFF_ZA // gw
    return pl.pallas_call(
        functools.partial(_moba_body, n_blocks=nb, g_heads=g_heads),
        grid=(A_HEADS // g_heads, bsz, nb),
        in_specs=[pl.BlockSpec(memory_space=pltpu.SMEM),
                  pl.BlockSpec((None, None, gw, t), lambda h, b, i: (b, i, h, 0)),
                  pl.BlockSpec((None, s, gw), lambda h, b, i: (b, 0, ck + h), pipeline_mode=pl.Buffered(1)),
                  pl.BlockSpec((None, nb, g_heads * (hd + V_AUG_ROWS), t), lambda h, b, i: (b, 0, h, 0),
                               pipeline_mode=pl.Buffered(1)),
                  pl.BlockSpec((None, t, gw), lambda h, b, i: (b, i, cz + h))],
        out_specs=pl.BlockSpec((None, t, gw), lambda h, b, i: (b, i, h)),
        out_shape=jax.ShapeDtypeStruct((bsz, s, A_W), BF16),
        scratch_shapes=[pltpu.VMEM((g_heads, s, ATTN_QK_PAD), BF16),
                        pltpu.VMEM((g_heads, 2 * LANES, hd), BF16),
                        pltpu.VMEM((g_heads, t, t), F32),
                        pltpu.VMEM((g_heads, t, t), F32),
                        pltpu.VMEM((g_heads, ATTN_QK_PAD, t), BF16)] + _attn_scratch(g_heads, t, hd),
        compiler_params=_cparams(3),
        name="moba",
    )(t5_table, qt, slab, vt, slab)


MLSTM_CHUNK = 256
CONV_HALO = 8


def _log_sigmoid(t):
    return jnp.minimum(t, 0.0) - jnp.log1p(jnp.exp(-jnp.abs(t)))


def _mlstm_body(qk_ref, v_ref, ob_ref, zb_ref, if_ref, cw_ref, cb_ref, gb_ref, gn_ref, o_ref,
                xe_s, c_s, n_s, m_s):
    c = pl.program_id(1)
    L = MLSTM_CHUNK
    dk, dv = B_QK_DIM, B_V_DIM

    @pl.when(c == 0)
    def _reset():
        xe_s[0:CONV_HALO, :] = jnp.zeros((CONV_HALO, 2 * B_QK_W), F32)
        c_s[...] = jnp.zeros_like(c_s)
        n_s[...] = jnp.zeros_like(n_s)
        m_s[...] = jnp.zeros_like(m_s)

    xe_s[CONV_HALO:CONV_HALO + L, :] = qk_ref[...].astype(F32)
    conv = cb_ref[...]
    for j in range(B_CONV):
        conv = conv + cw_ref[j:j + 1, :] * xe_s[pl.ds(CONV_HALO - (B_CONV - 1) + j, L), :]
    xe_s[0:CONV_HALO, :] = xe_s[L:L + CONV_HALO, :]
    qk = _silu(conv)

    gi = if_ref[...].astype(F32) + gb_ref[...]
    lf = _log_sigmoid(gi)
    row = lax.broadcasted_iota(jnp.int32, (L, L), 0)
    col = lax.broadcasted_iota(jnp.int32, (L, L), 1)
    causal = col <= row
    tri = jnp.where(causal, 1.0, 0.0).astype(F32)
    b_cols = jnp.dot(tri, lf, precision=lax.Precision.HIGHEST, preferred_element_type=F32)
    b_rows = b_cols.T
    li_rows = gi.T

    for h in range(B_HEADS):
        q_h = qk[:, h * dk:(h + 1) * dk]
        k_h = qk[:, B_QK_W + h * dk:B_QK_W + (h + 1) * dk] * (dk ** -0.5)
        v_h = v_ref[:, h * dv:(h + 1) * dv]
        b_c = b_cols[:, B_HEADS + h:B_HEADS + h + 1]
        b_r = b_rows[B_HEADS + h:B_HEADS + h + 1, :]
        li_c = gi[:, h:h + 1]
        li_r = li_rows[h:h + 1, :]
        m_prev = m_s[h:h + 1, 0:1]
        c_prev = c_s[h]
        n_prev = n_s[h:h + 1, :]

        a_c = b_c + m_prev
        dmat = jnp.where(causal, b_c - b_r + li_r, -jnp.inf)
        m_t = jnp.maximum(a_c, jnp.max(dmat, axis=-1, keepdims=True))
        w_inter = jnp.exp(a_c - m_t)
        q_b = q_h.astype(BF16)
        k_b = k_h.astype(BF16)
        sc = (lax.dot_general(q_b, k_b, _NT, preferred_element_type=F32)
              * jnp.exp(dmat - m_t))
        num = (w_inter * jnp.dot(q_b, c_prev.astype(BF16), preferred_element_type=F32)
               + jnp.dot(sc.astype(BF16), v_h, preferred_element_type=F32))
        den = (w_inter * jnp.sum(q_h * n_prev, axis=-1, keepdims=True)
               + jnp.sum(sc, axis=-1, keepdims=True))
        hh = num / jnp.maximum(jnp.abs(den), jnp.exp(-m_t))

        b_last = b_c[L - 1:L, :]
        g_c = b_last - b_c + li_c
        m_new = jnp.maximum(b_last + m_prev, jnp.max(g_c, axis=0, keepdims=True))
        decay = jnp.exp(b_last + m_prev - m_new)
        wk = jnp.exp(g_c - m_new) * k_h
        c_s[h] = decay * c_prev + lax.dot_general(wk.astype(BF16), v_h, (((0,), (0,)), ((), ())),
                                                  preferred_element_type=F32)
        n_s[h:h + 1, :] = decay * n_prev + jnp.sum(wk, axis=0, keepdims=True)
        m_s[h:h + 1, :] = jnp.broadcast_to(m_new, (1, LANES))

        mu = jnp.mean(hh, axis=-1, keepdims=True)
        dlt = hh - mu
        var = jnp.mean(dlt * dlt, axis=-1, keepdims=True)
        y = dlt * lax.rsqrt(var + EPS) * gn_ref[:, h * dv:(h + 1) * dv]
        y = (y * jax.nn.sigmoid(ob_ref[:, h * dv:(h + 1) * dv].astype(F32))
             * _silu(zb_ref[:, h * dv:(h + 1) * dv].astype(F32)))
        o_ref[:, h * dv:(h + 1) * dv] = y.astype(o_ref.dtype)


def mlstm_branch(slab, slab_small, conv_w, conv_b, i_bias, f_bias, out_norm):
    bsz, s, _ = slab.shape
    L = MLSTM_CHUNK
    assert s % L == 0
    gate_bias = jnp.zeros((1, LANES), F32).at[0, :B_HEADS].set(i_bias).at[0, B_HEADS:2 * B_HEADS].set(f_bias)
    w2 = 2 * B_QK_W
    full = lambda shape: pl.BlockSpec(shape, lambda b, c: (0,) * len(shape))
    return pl.pallas_call(
        _mlstm_body,
        grid=(bsz, s // L),
        in_specs=[pl.BlockSpec((None, L, w2), lambda b, c: (b, c, OFF_QKB // w2)),
                  pl.BlockSpec((None, L, B_V_W), lambda b, c: (b, c, OFF_VB // B_V_W)),
                  pl.BlockSpec((None, L, B_V_W), lambda b, c: (b, c, OFF_OB // B_V_W)),
                  pl.BlockSpec((None, L, B_V_W), lambda b, c: (b, c, OFF_ZB // B_V_W)),
                  pl.BlockSpec((None, L, LANES), lambda b, c: (b, c, OFF_IF // LANES)),
                  full((B_CONV, w2)), full((1, w2)), full((1, LANES)), full((1, B_V_W))],
        out_specs=pl.BlockSpec((None, L, B_V_W), lambda b, c: (b, c, 0)),
        out_shape=jax.ShapeDtypeStruct((bsz, s, B_V_W), BF16),
        scratch_shapes=[pltpu.VMEM((L + CONV_HALO, w2), F32),
                        pltpu.VMEM((B_HEADS, B_QK_DIM, B_V_DIM), F32),
                        pltpu.VMEM((8, B_QK_DIM), F32),
                        pltpu.VMEM((8, LANES), F32)],
        compiler_params=_cparams(2),
        name="mlstm",
    )(slab, slab, slab, slab, slab_small, conv_w, conv_b.reshape(1, w2), gate_bias,
      out_norm.reshape(1, B_V_W))


def _mla_prep_body(cq_ref, ckv_ref, kr_ref, qg_ref, kg_ref, wqt_ref, wqst_ref, wkn_ref, wvt_ref,
                   cos_ref, sin_ref, cost_ref, sint_ref, qt_ref, kf_ref, vt_ref):
    def normed(ref, g_ref):
        t = ref[...].astype(F32)
        y = t * lax.rsqrt(jnp.mean(t * t, axis=-1, keepdims=True) + EPS)
        return (y * g_ref[...]).astype(BF16)

    cqn = normed(cq_ref, qg_ref)
    ckvn = normed(ckv_ref, kg_ref)

    qt_main = lax.dot_general(wqt_ref[...], cqn, _NT, preferred_element_type=F32)
    qt_swap = lax.dot_general(wqst_ref[...], cqn, _NT, preferred_element_type=F32)
    cos_t = cost_ref[...]
    sin_t = sint_ref[...]
    for h in range(C_HEADS):
        lo = h * ATTN_QK_PAD
        qt_ref[lo:lo + LANES, :] = qt_main[lo:lo + LANES, :].astype(qt_ref.dtype)
        qt_ref[lo + LANES:lo + 2 * LANES, :] = (qt_main[lo + LANES:lo + 2 * LANES, :] * cos_t
                                                + qt_swap[h * LANES:(h + 1) * LANES, :] * sin_t).astype(qt_ref.dtype)

    k_nope = jnp.dot(ckvn, wkn_ref[...], preferred_element_type=F32)
    kr = kr_ref[...].astype(F32)
    half = C_ROPE // 2
    lane = lax.broadcasted_iota(jnp.int32, kr.shape, 1)
    swapped = jnp.where(lane < half, -pltpu.roll(kr, LANES - half, axis=1), pltpu.roll(kr, half, axis=1))
    k_rot = (kr * cos_ref[...] + swapped * sin_ref[...]).astype(kf_ref.dtype)
    for h in range(C_HEADS):
        lo = h * ATTN_QK_PAD
        kf_ref[:, lo:lo + LANES] = k_nope[:, h * LANES:(h + 1) * LANES].astype(kf_ref.dtype)
        kf_ref[:, lo + LANES:lo + 2 * LANES] = k_rot

    v_t = lax.dot_general(wvt_ref[...], ckvn, _NT, preferred_element_type=F32)
    hv = C_V_DIM + V_AUG_ROWS
    aug = _v_aug_rows(v_t.shape[1], vt_ref.dtype)
    for h in range(C_HEADS):
        vt_ref[h * hv:h * hv + C_V_DIM, :] = v_t[h * C_V_DIM:(h + 1) * C_V_DIM, :].astype(vt_ref.dtype)
        vt_ref[h * hv + C_V_DIM:(h + 1) * hv, :] = aug


def mla_prep(slab, q_norm, kv_norm, w_uq, w_ukv):
    bsz, s, _ = slab.shape
    tm = ATTN_TILE
    vt_rows = C_HEADS * (C_V_DIM + V_AUG_ROWS)
    half = C_ROPE // 2
    scale = (C_NOPE + C_ROPE) ** -0.5 * LOG2E
    wq = (w_uq * scale).reshape(C_Q_RANK, C_HEADS, C_NOPE + C_ROPE)
    pad = jnp.zeros((C_Q_RANK, C_HEADS, LANES - C_ROPE), F32)
    x1, x2 = wq[..., C_NOPE:C_NOPE + half], wq[..., C_NOPE + half:]
    wqt = jnp.concatenate([wq, pad], axis=-1).reshape(C_Q_RANK, C_HEADS * ATTN_QK_PAD).T.astype(BF16)
    wqst = jnp.concatenate([-x2, x1, pad], axis=-1).reshape(C_Q_RANK, C_HEADS * LANES).T.astype(BF16)
    wkv = w_ukv.reshape(C_KV_RANK, C_HEADS, C_NOPE + C_V_DIM)
    wkn = wkv[..., :C_NOPE].reshape(C_KV_RANK, C_HEADS * C_NOPE).astype(BF16)
    wvt = wkv[..., C_NOPE:].reshape(C_KV_RANK, C_W).T.astype(BF16)

    pos = jnp.arange(s, dtype=jnp.int32)
    inv = ROPE_THETA ** (-jnp.arange(half, dtype=F32) / half)
    ang = pos.astype(F32)[:, None] * inv[None, :]
    zpad = jnp.zeros((s, LANES - C_ROPE), F32)
    cos_tab = jnp.concatenate([jnp.cos(ang), jnp.cos(ang), zpad], axis=-1)
    sin_tab = jnp.concatenate([jnp.sin(ang), jnp.sin(ang), zpad], axis=-1)

    nt = s // tm
    full = lambda shape: pl.BlockSpec(shape, lambda b, i: (0,) * len(shape))
    return pl.pallas_call(
        _mla_prep_body,
        grid=(bsz, nt),
        in_specs=[pl.BlockSpec((None, tm, C_Q_RANK), lambda b, i: (b, i, OFF_CQ // C_Q_RANK)),
                  pl.BlockSpec((None, tm, C_KV_RANK), lambda b, i: (b, i, OFF_CKV // C_KV_RANK)),
                  pl.BlockSpec((None, tm, LANES), lambda b, i: (b, i, OFF_KR // LANES)),
                  full((1, C_Q_RANK)), full((1, C_KV_RANK)),
                  full(wqt.shape), full(wqst.shape), full(wkn.shape), full(wvt.shape),
                  pl.BlockSpec((tm, LANES), lambda b, i: (i, 0)),
                  pl.BlockSpec((tm, LANES), lambda b, i: (i, 0)),
                  pl.BlockSpec((LANES, tm), lambda b, i: (0, i)),
                  pl.BlockSpec((LANES, tm), lambda b, i: (0, i))],
        out_specs=[pl.BlockSpec((None, None, C_HEADS * ATTN_QK_PAD, tm), lambda b, i: (b, i, 0, 0)),
                   pl.BlockSpec((None, tm, C_HEADS * ATTN_QK_PAD), lambda b, i: (b, i, 0)),
                   pl.BlockSpec((None, None, vt_rows, tm), lambda b, i: (b, i, 0, 0))],
        out_shape=[jax.ShapeDtypeStruct((bsz, nt, C_HEADS * ATTN_QK_PAD, tm), BF16),
                   jax.ShapeDtypeStruct((bsz, s, C_HEADS * ATTN_QK_PAD), BF16),
                   jax.ShapeDtypeStruct((bsz, nt, vt_rows, tm), BF16)],
        compiler_params=_cparams(2),
        name="mla_prep",
    )(slab, slab, slab, q_norm.reshape(1, C_Q_RANK), kv_norm.reshape(1, C_KV_RANK),
      wqt, wqst, wkn, wvt, cos_tab, sin_tab, cos_tab.T, sin_tab.T)


def _mla_attn_body(qt_ref, k_ref, vt_ref, z_ref, o_ref, m_s, acc_s, *s_refs, g_heads):
    i = pl.program_id(2)
    t = ATTN_TILE
    dq, dv = ATTN_QK_PAD, C_V_DIM
    hv = dv + V_AUG_ROWS
    _softmax_init(m_s, acc_s)

    def raw_scores(g, j):
        start = pl.multiple_of(j * t, t)
        return jnp.dot(k_ref[pl.ds(start, t), g * dq:(g + 1) * dq], qt_ref[g * dq:(g + 1) * dq, :],
                       preferred_element_type=F32)

    def diagonal_scores(slot):
        key = lax.broadcasted_iota(jnp.int32, (t, t), 0)
        qry = lax.broadcasted_iota(jnp.int32, (t, t), 1)
        for g in range(g_heads):
            s_refs[g][slot] = jnp.where(key <= qry, raw_scores(g, i), NEG)

    def past_scores(j, slot):
        for g in range(g_heads):
            s_refs[g][slot] = raw_scores(g, j)

    def consume(j, slot):
        for g in range(g_heads):
            _softmax_step_t(s_refs[g][slot], vt_ref[j, g * hv:(g + 1) * hv, :], g, m_s, acc_s)

    _skewed_blocks([(diagonal_scores, i)], i, past_scores, consume)
    _attn_finish(z_ref, o_ref, acc_s, g_heads, dv)


def mla_attention(qt, kf, vt, slab, g_heads=ATTN_HEADS_PER_STEP):
    bsz, nt, _, t = qt.shape
    s = nt * t
    dq, dv = ATTN_QK_PAD, C_V_DIM
    assert t == ATTN_TILE and C_HEADS % g_heads == 0 and OFF_ZC % (g_heads * dv) == 0
    cz = OFF_ZC // (g_heads * dv)
    return pl.pallas_call(
        functools.partial(_mla_attn_body, g_heads=g_heads),
        grid=(bsz, C_HEADS // g_heads, nt),
        in_specs=[pl.BlockSpec((None, None, g_heads * dq, t), lambda b, h, i: (b, i, h, 0)),
                  pl.BlockSpec((None, s, g_heads * dq), lambda b, h, i: (b, 0, h),
                               pipeline_mode=pl.Buffered(1)),
                  pl.BlockSpec((None, nt, g_heads * (dv + V_AUG_ROWS), t), lambda b, h, i: (b, 0, h, 0),
                               pipeline_mode=pl.Buffered(1)),
                  pl.BlockSpec((None, t, g_heads * dv), lambda b, h, i: (b, i, cz + h))],
        out_specs=pl.BlockSpec((None, t, g_heads * dv), lambda b, h, i: (b, i, h)),
        out_shape=jax.ShapeDtypeStruct((bsz, s, C_W), BF16),
        scratch_shapes=_attn_scratch(g_heads, t, dv),
        compiler_params=_cparams(3),
        name="mla_attn",
    )(qt, kf, vt, slab)


def _merge_body(ya_ref, yb_ref, yc_ref, wa_ref, wb_ref, wc_ref, ga_ref, gb_ref, gc_ref, o_ref, w_s):
    @pl.when(pl.program_id(1) == 0)
    def _cast_weights():
        for n, w_ref in enumerate((wa_ref, wb_ref, wc_ref)):
            w_s[n] = w_ref[...].astype(w_s.dtype)

    acc = None
    for n, (y_ref, g_ref) in enumerate(((ya_ref, ga_ref), (yb_ref, gb_ref), (yc_ref, gc_ref))):
        term = (jax.nn.sigmoid(g_ref[...].astype(F32))
                * jnp.dot(y_ref[...], w_s[n], preferred_element_type=F32))
        acc = term if acc is None else acc + term
    o_ref[...] = acc.astype(o_ref.dtype)


def branch_merge(ya, yb, yc, w_branch, layer, slab_b, tm=512, tn=1024):
    m, w = ya.shape
    d = w_branch.shape[-1]
    assert OFF_GT % tn == 0
    g0 = OFF_GT // tn
    gper = d // tn
    y_spec = pl.BlockSpec((tm, w), lambda j, i: (i, 0))
    w_specs = [pl.BlockSpec((None, None, w, tn), functools.partial(lambda j, i, n: (layer, n, 0, j), n=n))
               for n in range(N_BRANCH)]
    g_specs = [pl.BlockSpec((tm, tn), functools.partial(lambda j, i, n: (i, g0 + n * gper + j), n=n))
               for n in range(N_BRANCH)]
    return pl.pallas_call(
        _merge_body,
        grid=(d // tn, m // tm),
        in_specs=[y_spec, y_spec, y_spec] + w_specs + g_specs,
        out_specs=pl.BlockSpec((tm, tn), lambda j, i: (i, j)),
        out_shape=jax.ShapeDtypeStruct((m, d), BF16),
        scratch_shapes=[pltpu.VMEM((N_BRANCH, w, tn), BF16)],
        compiler_params=_cparams(2),
        name="branch_merge",
    )(ya, yb, yc, w_branch, w_branch, w_branch, slab_b, slab_b, slab_b)


def _out_body(mg_ref, w_ref, x_ref, g_ref, *refs, last):
    out_refs, w_s = refs[:-1], refs[-1]

    @pl.when(pl.program_id(0) == 0)
    def _cast_weights():
        w_s[...] = w_ref[...].astype(w_s.dtype)

    x_new = x_ref[...] + jnp.dot(mg_ref[...], w_s[...], preferred_element_type=F32)
    y = x_new * lax.rsqrt(jnp.mean(x_new * x_new, axis=-1, keepdims=True) + EPS) * g_ref[...]
    if last:
        out_refs[0][...] = y
    else:
        out_refs[0][...] = x_new
        out_refs[1][...] = y.astype(out_refs[1].dtype)


def out_projection(merged, w_out, layer, x2d, gain, last, tm=256):
    m, d = x2d.shape
    row = pl.BlockSpec((tm, d), lambda i: (i, 0))
    if last:
        out_specs, out_shape = row, jax.ShapeDtypeStruct((m, d), F32)
    else:
        out_specs = [row, row]
        out_shape = [jax.ShapeDtypeStruct((m, d), F32), jax.ShapeDtypeStruct((m, d), BF16)]
    return pl.pallas_call(
        functools.partial(_out_body, last=last),
        grid=(m // tm,),
        in_specs=[row,
                  pl.BlockSpec((None, d, d), lambda i: (layer, 0, 0), pipeline_mode=pl.Buffered(1)),
                  row, pl.BlockSpec((1, d), lambda i: (0, 0))],
        out_specs=out_specs,
        out_shape=out_shape,
        scratch_shapes=[pltpu.VMEM((d, d), BF16)],
        compiler_params=_cparams(1),
        name="out_proj",
    )(merged, w_out, x2d, gain.reshape(1, d))


def kernel(x, norm_gain, w_in, t5_table, mlstm_conv_w, mlstm_conv_b, mlstm_i_bias, mlstm_f_bias,
           mlstm_out_norm, mla_q_norm, mla_kv_norm, mla_w_uq, mla_w_ukv, w_branch, w_out, final_norm):
    bsz, s, d = x.shape
    m = bsz * s
    x2d = x.reshape(m, d)
    h = rmsnorm_rows(x2d, norm_gain[0])
    w_in_t = jnp.swapaxes(w_in, 1, 2)
    out = None
    for l in range(DEPTH):
        slab2d = input_projection(h, w_in_t, l, SLAB_SRC_ROWS)
        slab = slab2d.reshape(bsz, s, D_SLAB)
        slab_small = input_projection_small(h, w_in_t, l).reshape(bsz, s, D_SLAB_SMALL)
        qt_a, vt_a = moba_qv_projection(h, w_in_t, l, bsz)
        ya = moba_branch(qt_a, vt_a, slab, t5_table)
        yb = mlstm_branch(slab, slab_small, mlstm_conv_w[l], mlstm_conv_b[l], mlstm_i_bias[l], mlstm_f_bias[l],
                          mlstm_out_norm[l])
        qt_c, k_c, vt_c = mla_prep(slab_small, mla_q_norm[l], mla_kv_norm[l], mla_w_uq[l], mla_w_ukv[l])
        yc = mla_attention(qt_c, k_c, vt_c, slab)
        merged = branch_merge(ya.reshape(m, A_W), yb.reshape(m, B_V_W), yc.reshape(m, C_W),
                              w_branch, l, slab2d)
        last = l == DEPTH - 1
        gain = final_norm if last else norm_gain[l + 1]
        res = out_projection(merged, w_out, l, x2d, gain, last)
        if last:
            out = res
        else:
            x2d, h = res
    return out.reshape(bsz, s, d)
```

```python
import functools
import math

import jax
import jax.numpy as jnp
import numpy as np
from jax import lax
from jax.experimental import pallas as pl
from jax.experimental.pallas import tpu as pltpu

F32 = jnp.float32
BF16 = jnp.bfloat16

D_MODEL = 2048
DEPTH = 2
EPS = 1e-6
NEG = -1e30
LOG2E = math.log2(math.e)

A_HEADS = 8
A_HEAD_DIM = 128
A_BLOCK = 256
A_TOPK = 3
T5_BUCKETS = 32
T5_MAX_DIST = 128
B_HEADS = 4
B_QK_DIM = 128
B_V_DIM = 256
B_CONV = 4
C_HEADS = 8
C_Q_RANK = 512
C_KV_RANK = 256
C_NOPE = 128
C_ROPE = 64
C_V_DIM = 128
ROPE_THETA = 10000.0
N_BRANCH = 3

A_W = A_HEADS * A_HEAD_DIM
B_QK_W = B_HEADS * B_QK_DIM
B_V_W = B_HEADS * B_V_DIM
C_W = C_HEADS * C_V_DIM

LANES = 128
SUBLANES = 8
VMEM_LIMIT_BYTES = 56 * 1024 * 1024

SPLIT_SIZES = (A_W, A_W, A_W, A_W, B_QK_W, B_QK_W, B_V_W, B_HEADS, B_HEADS, B_V_W, B_V_W,
               C_Q_RANK, C_KV_RANK, C_ROPE, C_W, N_BRANCH * D_MODEL)
(SRC_QA, SRC_KA, SRC_VA, SRC_ZA, SRC_QB, SRC_KB, SRC_VB, SRC_IB, SRC_FB, SRC_OB, SRC_ZB,
 SRC_CQ, SRC_CKV, SRC_KR, SRC_ZC, SRC_GT) = (int(v) for v in np.cumsum((0,) + SPLIT_SIZES[:-1]))
D_IN = int(sum(SPLIT_SIZES))

W_BLOCK = 1024
SLAB_SRC_ROWS = ((SRC_KA, SRC_ZA, SRC_QB, SRC_VB, SRC_OB, SRC_ZB, SRC_ZC)
                 + tuple(SRC_GT + k * W_BLOCK for k in range(N_BRANCH * D_MODEL // W_BLOCK)))
assert SRC_KB == SRC_QB + B_QK_W and all(r % SUBLANES == 0 for r in SLAB_SRC_ROWS)
OFF_KA = 0
OFF_ZA = OFF_KA + W_BLOCK
OFF_QKB = OFF_ZA + W_BLOCK
OFF_VB = OFF_QKB + W_BLOCK
OFF_OB = OFF_VB + W_BLOCK
OFF_ZB = OFF_OB + W_BLOCK
OFF_ZC = OFF_ZB + W_BLOCK
OFF_GT = OFF_ZC + W_BLOCK
D_SLAB = len(SLAB_SRC_ROWS) * W_BLOCK
SMALL_WIN1_ROWS = C_Q_RANK + C_KV_RANK + LANES
SMALL_WIN2_ROWS = LANES
assert SRC_CKV == SRC_CQ + C_Q_RANK and SRC_KR == SRC_CKV + C_KV_RANK and SRC_FB == SRC_IB + B_HEADS
assert SRC_CQ % SUBLANES == 0 and SRC_IB % SUBLANES == 0
assert SRC_CQ + SMALL_WIN1_ROWS <= D_IN and SRC_IB + SMALL_WIN2_ROWS <= D_IN
OFF_CQ = 0
OFF_CKV = OFF_CQ + C_Q_RANK
OFF_KR = OFF_CKV + C_KV_RANK
OFF_IF = OFF_KR + LANES
D_SLAB_SMALL = SMALL_WIN1_ROWS + SMALL_WIN2_ROWS
assert D_SLAB_SMALL == W_BLOCK

ATTN_TILE = 256
ATTN_QK_PAD = 2 * LANES
ATTN_HEADS_PER_STEP = 8
assert ATTN_TILE == A_BLOCK


def _cparams(n_axes):
    return pltpu.CompilerParams(dimension_semantics=("arbitrary",) * n_axes,
                                vmem_limit_bytes=VMEM_LIMIT_BYTES)


def _silu(t):
    return t * jax.nn.sigmoid(t)


_NT = (((1,), (1,)), ((), ()))


def _rmsnorm_body(x_ref, g_ref, o_ref):
    xf = x_ref[...]
    y = xf * lax.rsqrt(jnp.mean(xf * xf, axis=-1, keepdims=True) + EPS)
    o_ref[...] = (y * g_ref[...]).astype(o_ref.dtype)


def rmsnorm_rows(x2d, gain, tm=512):
    m, d = x2d.shape
    return pl.pallas_call(
        _rmsnorm_body,
        grid=(m // tm,),
        in_specs=[pl.BlockSpec((tm, d), lambda i: (i, 0)),
                  pl.BlockSpec((1, d), lambda i: (0, 0))],
        out_specs=pl.BlockSpec((tm, d), lambda i: (i, 0)),
        out_shape=jax.ShapeDtypeStruct((m, d), BF16),
        compiler_params=_cparams(1),
        name="rmsnorm",
    )(x2d, gain.reshape(1, d))


def _row_window(rows, d):
    return (pl.Squeezed(), pl.Element(rows), pl.Element(d))


def _proj_body(rows_ref, h_ref, w_ref, o_ref, wb_s):
    del rows_ref
    @pl.when(pl.program_id(1) == 0)
    def _cast_weights():
        for c in range(0, w_ref.shape[0], LANES):
            wb_s[:, c:c + LANES] = w_ref[c:c + LANES, :].T.astype(wb_s.dtype)

    o_ref[...] = jnp.dot(h_ref[...], wb_s[...], preferred_element_type=F32).astype(o_ref.dtype)


def input_projection(h2d, w_in_t, layer, src_rows, tm=1024):
    m, d = h2d.shape
    tn = W_BLOCK
    grid_spec = pltpu.PrefetchScalarGridSpec(
        num_scalar_prefetch=1,
        grid=(len(src_rows), m // tm),
        in_specs=[pl.BlockSpec((tm, d), lambda j, i, rows: (i, 0)),
                  pl.BlockSpec(_row_window(tn, d),
                               lambda j, i, rows: (layer, pl.multiple_of(rows[j], SUBLANES), 0))],
        out_specs=pl.BlockSpec((tm, tn), lambda j, i, rows: (i, j)),
        scratch_shapes=[pltpu.VMEM((d, tn), BF16)])
    return pl.pallas_call(
        _proj_body,
        grid_spec=grid_spec,
        out_shape=jax.ShapeDtypeStruct((m, len(src_rows) * tn), BF16),
        compiler_params=_cparams(2),
        name="input_proj",
    )(jnp.asarray(src_rows, jnp.int32), h2d, w_in_t)


def _proj_small_body(h_ref, w1_ref, w2_ref, o_ref, wb_s):
    @pl.when(pl.program_id(0) == 0)
    def _cast_weights():
        wb_s[:SMALL_WIN1_ROWS, :] = w1_ref[...].astype(wb_s.dtype)
        wb_s[SMALL_WIN1_ROWS:, :] = w2_ref[...].astype(wb_s.dtype)

    o_ref[...] = lax.dot_general(h_ref[...], wb_s[...], _NT, preferred_element_type=F32).astype(o_ref.dtype)


def input_projection_small(h2d, w_in_t, layer, tm=1024):
    m, d = h2d.shape
    return pl.pallas_call(
        _proj_small_body,
        grid=(m // tm,),
        in_specs=[pl.BlockSpec((tm, d), lambda i: (i, 0)),
                  pl.BlockSpec(_row_window(SMALL_WIN1_ROWS, d), lambda i: (layer, SRC_CQ, 0)),
                  pl.BlockSpec(_row_window(SMALL_WIN2_ROWS, d), lambda i: (layer, SRC_IB, 0))],
        out_specs=pl.BlockSpec((tm, D_SLAB_SMALL), lambda i: (i, 0)),
        out_shape=jax.ShapeDtypeStruct((m, D_SLAB_SMALL), BF16),
        scratch_shapes=[pltpu.VMEM((D_SLAB_SMALL, d), BF16)],
        compiler_params=_cparams(1),
        name="input_proj_small",
    )(h2d, w_in_t, w_in_t)


def _proj_t_body(h_ref, wq_ref, wv_ref, qt_ref, vt_ref, wt_s, *, n_sub):
    t = ATTN_TILE
    hd = A_HEAD_DIM
    hv = hd + V_AUG_ROWS

    @pl.when((pl.program_id(0) == 0) & (pl.program_id(1) == 0))
    def _cast_weights():
        q_scale = A_HEAD_DIM ** -0.5 * LOG2E
        wt_s[:A_W, :] = (wq_ref[...] * q_scale).astype(wt_s.dtype)
        wt_s[A_W:, :] = wv_ref[...].astype(wt_s.dtype)

    res = lax.dot_general(wt_s[...], h_ref[...], _NT, preferred_element_type=F32)
    aug = _v_aug_rows(t, vt_ref.dtype)
    for c in range(n_sub):
        qt_ref[c] = res[:A_W, c * t:(c + 1) * t].astype(qt_ref.dtype)
        for h in range(A_HEADS):
            vt_ref[c, h * hv:h * hv + hd, :] = res[A_W + h * hd:A_W + (h + 1) * hd, c * t:(c + 1) * t
                                                   ].astype(vt_ref.dtype)
            vt_ref[c, h * hv + hd:(h + 1) * hv, :] = aug


def moba_qv_projection(h2d, w_in_t, layer, bsz, tm=512):
    m, d = h2d.shape
    assert SRC_QA % A_W == 0 and SRC_VA % A_W == 0
    t = ATTN_TILE
    tm = min(tm, m // bsz)
    n_sub = tm // t
    nt = m // bsz // t
    steps = m // bsz // tm
    vt_rows = A_HEADS * (A_HEAD_DIM + V_AUG_ROWS)
    return pl.pallas_call(
        functools.partial(_proj_t_body, n_sub=n_sub),
        grid=(bsz, steps),
        in_specs=[pl.BlockSpec((tm, d), lambda b, i: (b * steps + i, 0)),
                  pl.BlockSpec((None, A_W, d), lambda b, i: (layer, SRC_QA // A_W, 0),
                               pipeline_mode=pl.Buffered(1)),
                  pl.BlockSpec((None, A_W, d), lambda b, i: (layer, SRC_VA // A_W, 0),
                               pipeline_mode=pl.Buffered(1))],
        out_specs=[pl.BlockSpec((None, n_sub, A_W, t), lambda b, i: (b, i, 0, 0)),
                   pl.BlockSpec((None, n_sub, vt_rows, t), lambda b, i: (b, i, 0, 0))],
        out_shape=[jax.ShapeDtypeStruct((bsz, nt, A_W, t), BF16),
                   jax.ShapeDtypeStruct((bsz, nt, vt_rows, t), BF16)],
        scratch_shapes=[pltpu.VMEM((2 * A_W, d), BF16)],
        compiler_params=_cparams(2),
        name="moba_qv_proj",
    )(h2d, w_in_t, w_in_t)


V_AUG_ROWS = 16


def _v_aug_rows(t, dtype):
    return jnp.where(lax.broadcasted_iota(jnp.int32, (V_AUG_ROWS, t), 0) == 0, 1.0, 0.0).astype(dtype)


def _softmax_init(m_s, acc_s):
    m_s[...] = jnp.full(m_s.shape, -jnp.inf, F32)
    acc_s[...] = jnp.zeros(acc_s.shape, F32)


def _softmax_step_t(s, vt_blk, g, m_s, acc_s):
    m_prev = m_s[g]
    m_new = jnp.maximum(m_prev, jnp.max(s, axis=0, keepdims=True))
    alpha = jnp.exp2(m_prev - m_new)
    p = jnp.exp2(s - m_new)
    acc_s[g] = alpha * acc_s[g] + jnp.dot(vt_blk, p.astype(BF16), preferred_element_type=F32)
    m_s[g] = m_new


def _attn_scratch(g_heads, t, dv):
    state = [pltpu.VMEM((g_heads, 1, t), F32), pltpu.VMEM((g_heads, dv + V_AUG_ROWS, t), F32)]
    return state + [pltpu.VMEM((2, t, t), F32)] * g_heads


def _skewed_blocks(lead, n_uniform, uniform_scores, consume):
    lead[0][0](0)
    pend_blk, pend_slot = lead[0][1], 0
    for fn, blk in lead[1:]:
        fn(1 - pend_slot)
        consume(pend_blk, pend_slot)
        pend_blk, pend_slot = blk, 1 - pend_slot
    a, b = pend_slot, 1 - pend_slot

    def two_blocks(jj, pend):
        uniform_scores(2 * jj, b)
        consume(pend, a)
        uniform_scores(2 * jj + 1, a)
        consume(2 * jj, b)
        return 2 * jj + 1

    pend = lax.fori_loop(0, n_uniform // 2, two_blocks, pend_blk)

    @pl.when(n_uniform % 2 == 1)
    def _odd_tail():
        uniform_scores(n_uniform - 1, b)
        consume(pend, a)
        consume(n_uniform - 1, b)

    @pl.when(n_uniform % 2 == 0)
    def _even_tail():
        consume(pend, a)


def _attn_finish(z_ref, o_ref, acc_s, g_heads, dv):
    for g in range(g_heads):
        out_t = acc_s[g, :dv, :] / acc_s[g, dv:dv + 1, :]
        o_ref[:, g * dv:(g + 1) * dv] = (out_t.T * _silu(z_ref[:, g * dv:(g + 1) * dv].astype(F32))
                                         ).astype(o_ref.dtype)


def _t5_thresholds():
    max_exact = T5_BUCKETS // 2
    d = np.arange(0, 4 * T5_MAX_DIST, dtype=np.int64)
    dd = np.maximum(d, 1).astype(np.float32)
    large = max_exact + (np.log(dd / np.float32(max_exact)) / np.float32(math.log(T5_MAX_DIST / max_exact))
                         * np.float32(T5_BUCKETS - max_exact)).astype(np.int32)
    large = np.minimum(large, T5_BUCKETS - 1)
    bucket = np.where(d < max_exact, d, large)
    assert np.all(np.diff(bucket) >= 0) and bucket[-1] == T5_BUCKETS - 1
    return [int(np.argmax(bucket >= k)) for k in range(T5_BUCKETS)]


_T5_THRESH = _t5_thresholds()

MASK_BIG = 1e30


def _moba_body(t5_ref, qt_ref, k_ref, vt_ref, z_ref, o_ref,
               kaug_s, km_s, bias_own_s, bias_prev_s, qa_s, m_s, acc_s, *s_refs, n_blocks, g_heads):
    hg = pl.program_id(0)
    b = pl.program_id(1)
    i = pl.program_id(2)
    t, hd = A_BLOCK, A_HEAD_DIM
    hv = hd + V_AUG_ROWS
    seq = n_blocks * t
    nb_pad = -(-n_blocks // SUBLANES) * SUBLANES

    @pl.when((hg == 0) & (b == 0) & (i == 0))
    def _mask_columns():
        key_blk = lax.broadcasted_iota(jnp.int32, (seq, LANES), 0) // t
        lane = lax.broadcasted_iota(jnp.int32, (seq, LANES), 1)
        pattern = jnp.where(lane == key_blk, -MASK_BIG, 0.0).astype(kaug_s.dtype)
        for g in range(g_heads):
            kaug_s[g, :, hd:] = pattern

    @pl.when((b == 0) & (i == 0))
    def _build_bias():
        key = lax.broadcasted_iota(jnp.int32, (t, t), 0)
        qry = lax.broadcasted_iota(jnp.int32, (t, t), 1)
        for g in range(g_heads):
            head = hg * g_heads + g
            far = t5_ref[T5_BUCKETS - 1, head]
            for dist, dst in ((qry - key, bias_own_s), (t + qry - key, bias_prev_s)):
                bias = jnp.full((t, t), t5_ref[0, head], F32)
                for kk in range(1, T5_BUCKETS):
                    bias = jnp.where(dist >= _T5_THRESH[kk], t5_ref[kk, head], bias)
                dst[g] = (bias - far) * LOG2E

    @pl.when(i == 0)
    def _new_sequence():
        for g in range(g_heads):
            kaug_s[g, :, :hd] = k_ref[:, g * hd:(g + 1) * hd]
            km_s[g] = jnp.zeros(km_s.shape[1:], km_s.dtype)
            for nb in range(n_blocks):
                mean = jnp.mean(k_ref[nb * t:(nb + 1) * t, g * hd:(g + 1) * hd].astype(F32), axis=0, keepdims=True)
                hi = mean.astype(BF16)
                km_s[g, nb:nb + 1, :] = hi
                km_s[g, LANES + nb:LANES + nb + 1, :] = (mean - hi.astype(F32)).astype(BF16)

    blk_row = lax.broadcasted_iota(jnp.int32, (nb_pad, t), 0)
    past = blk_row < i
    for g in range(g_heads):
        qt = qt_ref[g * hd:(g + 1) * hd, :]
        g2 = jnp.dot(km_s[g], qt, preferred_element_type=F32)
        gate = g2[:nb_pad, :] + g2[LANES:LANES + nb_pad, :]
        cur = jnp.where(past, gate, NEG)
        keep = blk_row == i
        for _ in range(A_TOPK):
            mx = jnp.max(cur, axis=0, keepdims=True)
            idx = jnp.min(jnp.where(cur == mx, blk_row, nb_pad), axis=0, keepdims=True)
            pick = blk_row == idx
            keep = keep | (pick & past)
            cur = jnp.where(pick, -jnp.inf, cur)
        unsel = jnp.where(keep, 0.0, 1.0)
        qa_s[g, :hd, :] = qt
        qa_s[g, hd:, :] = jnp.concatenate([unsel, jnp.zeros((LANES - nb_pad, t), F32)], axis=0).astype(qa_s.dtype)

    _softmax_init(m_s, acc_s)

    def raw_scores(g, j):
        start = pl.multiple_of(j * t, t)
        return jnp.dot(kaug_s[g, pl.ds(start, t), :], qa_s[g], preferred_element_type=F32)

    def own_scores(slot):
        key = lax.broadcasted_iota(jnp.int32, (t, t), 0)
        qry = lax.broadcasted_iota(jnp.int32, (t, t), 1)
        for g in range(g_heads):
            s_refs[g][slot] = jnp.where(key <= qry, raw_scores(g, i) + bias_own_s[g], NEG)

    j_prev = jnp.maximum(i - 1, 0)
    no_prev = jnp.where(i == 0, -MASK_BIG, 0.0)

    def prev_scores(slot):
        for g in range(g_heads):
            s_refs[g][slot] = raw_scores(g, j_prev) + (bias_prev_s[g] + no_prev)

    def far_scores(j, slot):
        for g in range(g_heads):
            s_refs[g][slot] = raw_scores(g, j)

    def consume(j, slot):
        for g in range(g_heads):
            _softmax_step_t(s_refs[g][slot], vt_ref[j, g * hv:(g + 1) * hv, :], g, m_s, acc_s)

    _skewed_blocks([(own_scores, i), (prev_scores, j_prev)], j_prev, far_scores, consume)
    _attn_finish(z_ref, o_ref, acc_s, g_heads, hd)


def moba_branch(qt, vt, slab, t5_table, g_heads=ATTN_HEADS_PER_STEP):
    bsz, nb, _, t = qt.shape
    s = nb * t
    hd = A_HEAD_DIM
    gw = g_heads * hd
    assert t == A_BLOCK and nb <= LANES and A_HEADS % g_heads == 0
    assert OFF_KA % gw == 0 and OFF_ZA % gw == 0
    assert _T5_THRESH[T5_BUCKETS - 1] <= t + 1
    ck, cz = OFF_KA // gw, OFF_ZA // gw
    return pl.pallas_call(
        functools.partial(_moba_body, n_blocks=nb, g_heads=g_heads),
        grid=(A_HEADS // g_heads, bsz, nb),
        in_specs=[pl.BlockSpec(memory_space=pltpu.SMEM),
                  pl.BlockSpec((None, None, gw, t), lambda h, b, i: (b, i, h, 0)),
                  pl.BlockSpec((None, s, gw), lambda h, b, i: (b, 0, ck + h), pipeline_mode=pl.Buffered(1)),
                  pl.BlockSpec((None, nb, g_heads * (hd + V_AUG_ROWS), t), lambda h, b, i: (b, 0, h, 0),
                               pipeline_mode=pl.Buffered(1)),
                  pl.BlockSpec((None, t, gw), lambda h, b, i: (b, i, cz + h))],
        out_specs=pl.BlockSpec((None, t, gw), lambda h, b, i: (b, i, h)),
        out_shape=jax.ShapeDtypeStruct((bsz, s, A_W), BF16),
        scratch_shapes=[pltpu.VMEM((g_heads, s, ATTN_QK_PAD), BF16),
                        pltpu.VMEM((g_heads, 2 * LANES, hd), BF16),
                        pltpu.VMEM((g_heads, t, t), F32),
                        pltpu.VMEM((g_heads, t, t), F32),
                        pltpu.VMEM((g_heads, ATTN_QK_PAD, t), BF16)] + _attn_scratch(g_heads, t, hd),
        compiler_params=_cparams(3),
        name="moba",
    )(t5_table, qt, slab, vt, slab)


MLSTM_CHUNK = 256
CONV_HALO = 8


def _log_sigmoid(t):
    return jnp.minimum(t, 0.0) - jnp.log1p(jnp.exp(-jnp.abs(t)))


def _mlstm_body(qk_ref, v_ref, ob_ref, zb_ref, if_ref, cw_ref, cb_ref, gb_ref, gn_ref, o_ref,
                xe_s, c_s, n_s, m_s):
    c = pl.program_id(1)
    L = MLSTM_CHUNK
    dk, dv = B_QK_DIM, B_V_DIM

    @pl.when(c == 0)
    def _reset():
        xe_s[0:CONV_HALO, :] = jnp.zeros((CONV_HALO, 2 * B_QK_W), F32)
        c_s[...] = jnp.zeros_like(c_s)
        n_s[...] = jnp.zeros_like(n_s)
        m_s[...] = jnp.zeros_like(m_s)

    xe_s[CONV_HALO:CONV_HALO + L, :] = qk_ref[...].astype(F32)
    conv = cb_ref[...]
    for j in range(B_CONV):
        conv = conv + cw_ref[j:j + 1, :] * xe_s[pl.ds(CONV_HALO - (B_CONV - 1) + j, L), :]
    xe_s[0:CONV_HALO, :] = xe_s[L:L + CONV_HALO, :]
    qk = _silu(conv)

    gi = if_ref[...].astype(F32) + gb_ref[...]
    lf = _log_sigmoid(gi)
    row = lax.broadcasted_iota(jnp.int32, (L, L), 0)
    col = lax.broadcasted_iota(jnp.int32, (L, L), 1)
    causal = col <= row
    tri = jnp.where(causal, 1.0, 0.0).astype(F32)
    b_cols = jnp.dot(tri, lf, precision=lax.Precision.HIGHEST, preferred_element_type=F32)
    b_rows = b_cols.T
    li_rows = gi.T

    for h in range(B_HEADS):
        q_h = qk[:, h * dk:(h + 1) * dk]
        k_h = qk[:, B_QK_W + h * dk:B_QK_W + (h + 1) * dk] * (dk ** -0.5)
        v_h = v_ref[:, h * dv:(h + 1) * dv]
        b_c = b_cols[:, B_HEADS + h:B_HEADS + h + 1]
        b_r = b_rows[B_HEADS + h:B_HEADS + h + 1, :]
        li_c = gi[:, h:h + 1]
        li_r = li_rows[h:h + 1, :]
        m_prev = m_s[h:h + 1, 0:1]
        c_prev = c_s[h]
        n_prev = n_s[h:h + 1, :]

        a_c = b_c + m_prev
        dmat = jnp.where(causal, b_c - b_r + li_r, -jnp.inf)
        m_t = jnp.maximum(a_c, jnp.max(dmat, axis=-1, keepdims=True))
        w_inter = jnp.exp(a_c - m_t)
        q_b = q_h.astype(BF16)
        k_b = k_h.astype(BF16)
        sc = (lax.dot_general(q_b, k_b, _NT, preferred_element_type=F32)
              * jnp.exp(dmat - m_t))
        num = (w_inter * jnp.dot(q_b, c_prev.astype(BF16), preferred_element_type=F32)
               + jnp.dot(sc.astype(BF16), v_h, preferred_element_type=F32))
        den = (w_inter * jnp.sum(q_h * n_prev, axis=-1, keepdims=True)
               + jnp.sum(sc, axis=-1, keepdims=True))
        hh = num / jnp.maximum(jnp.abs(den), jnp.exp(-m_t))

        b_last = b_c[L - 1:L, :]
        g_c = b_last - b_c + li_c
        m_new = jnp.maximum(b_last + m_prev, jnp.max(g_c, axis=0, keepdims=True))
        decay = jnp.exp(b_last + m_prev - m_new)
        wk = jnp.exp(g_c - m_new) * k_h
        c_s[h] = decay * c_prev + lax.dot_general(wk.astype(BF16), v_h, (((0,), (0,)), ((), ())),
                                                  preferred_element_type=F32)
        n_s[h:h + 1, :] = decay * n_prev + jnp.sum(wk, axis=0, keepdims=True)
        m_s[h:h + 1, :] = jnp.broadcast_to(m_new, (1, LANES))

        mu = jnp.mean(hh, axis=-1, keepdims=True)
        dlt = hh - mu
        var = jnp.mean(dlt * dlt, axis=-1, keepdims=True)
        y = dlt * lax.rsqrt(var + EPS) * gn_ref[:, h * dv:(h + 1) * dv]
        y = (y * jax.nn.sigmoid(ob_ref[:, h * dv:(h + 1) * dv].astype(F32))
             * _silu(zb_ref[:, h * dv:(h + 1) * dv].astype(F32)))
        o_ref[:, h * dv:(h + 1) * dv] = y.astype(o_ref.dtype)


def mlstm_branch(slab, slab_small, conv_w, conv_b, i_bias, f_bias, out_norm):
    bsz, s, _ = slab.shape
    L = MLSTM_CHUNK
    assert s % L == 0
    gate_bias = jnp.zeros((1, LANES), F32).at[0, :B_HEADS].set(i_bias).at[0, B_HEADS:2 * B_HEADS].set(f_bias)
    w2 = 2 * B_QK_W
    full = lambda shape: pl.BlockSpec(shape, lambda b, c: (0,) * len(shape))
    return pl.pallas_call(
        _mlstm_body,
        grid=(bsz, s // L),
        in_specs=[pl.BlockSpec((None, L, w2), lambda b, c: (b, c, OFF_QKB // w2)),
                  pl.BlockSpec((None, L, B_V_W), lambda b, c: (b, c, OFF_VB // B_V_W)),
                  pl.BlockSpec((None, L, B_V_W), lambda b, c: (b, c, OFF_OB // B_V_W)),
                  pl.BlockSpec((None, L, B_V_W), lambda b, c: (b, c, OFF_ZB // B_V_W)),
                  pl.BlockSpec((None, L, LANES), lambda b, c: (b, c, OFF_IF // LANES)),
                  full((B_CONV, w2)), full((1, w2)), full((1, LANES)), full((1, B_V_W))],
        out_specs=pl.BlockSpec((None, L, B_V_W), lambda b, c: (b, c, 0)),
        out_shape=jax.ShapeDtypeStruct((bsz, s, B_V_W), BF16),
        scratch_shapes=[pltpu.VMEM((L + CONV_HALO, w2), F32),
                        pltpu.VMEM((B_HEADS, B_QK_DIM, B_V_DIM), F32),
                        pltpu.VMEM((8, B_QK_DIM), F32),
                        pltpu.VMEM((8, LANES), F32)],
        compiler_params=_cparams(2),
        name="mlstm",
    )(slab, slab, slab, slab, slab_small, conv_w, conv_b.reshape(1, w2), gate_bias,
      out_norm.reshape(1, B_V_W))


def _mla_prep_body(cq_ref, ckv_ref, kr_ref, qg_ref, kg_ref, wqt_ref, wqst_ref, wkn_ref, wvt_ref,
                   cos_ref, sin_ref, cost_ref, sint_ref, qt_ref, kf_ref, vt_ref):
    def normed(ref, g_ref):
        t = ref[...].astype(F32)
        y = t * lax.rsqrt(jnp.mean(t * t, axis=-1, keepdims=True) + EPS)
        return (y * g_ref[...]).astype(BF16)

    cqn = normed(cq_ref, qg_ref)
    ckvn = normed(ckv_ref, kg_ref)

    qt_main = lax.dot_general(wqt_ref[...], cqn, _NT, preferred_element_type=F32)
    qt_swap = lax.dot_general(wqst_ref[...], cqn, _NT, preferred_element_type=F32)
    cos_t = cost_ref[...]
    sin_t = sint_ref[...]
    t = ATTN_TILE
    n_sub = cos_t.shape[1] // t
    for h in range(C_HEADS):
        lo = h * ATTN_QK_PAD
        nope = qt_main[lo:lo + LANES, :].astype(qt_ref.dtype)
        rope = (qt_main[lo + LANES:lo + 2 * LANES, :] * cos_t
                + qt_swap[h * LANES:(h + 1) * LANES, :] * sin_t).astype(qt_ref.dtype)
        for c in range(n_sub):
            qt_ref[c, lo:lo + LANES, :] = nope[:, c * t:(c + 1) * t]
            qt_ref[c, lo + LANES:lo + 2 * LANES, :] = rope[:, c * t:(c + 1) * t]

    k_nope = jnp.dot(ckvn, wkn_ref[...], preferred_element_type=F32)
    kr = kr_ref[...].astype(F32)
    half = C_ROPE // 2
    lane = lax.broadcasted_iota(jnp.int32, kr.shape, 1)
    swapped = jnp.where(lane < half, -pltpu.roll(kr, LANES - half, axis=1), pltpu.roll(kr, half, axis=1))
    k_rot = (kr * cos_ref[...] + swapped * sin_ref[...]).astype(kf_ref.dtype)
    for h in range(C_HEADS):
        lo = h * ATTN_QK_PAD
        kf_ref[:, lo:lo + LANES] = k_nope[:, h * LANES:(h + 1) * LANES].astype(kf_ref.dtype)
        kf_ref[:, lo + LANES:lo + 2 * LANES] = k_rot

    v_t = lax.dot_general(wvt_ref[...], ckvn, _NT, preferred_element_type=F32)
    hv = C_V_DIM + V_AUG_ROWS
    aug = _v_aug_rows(t, vt_ref.dtype)
    for h in range(C_HEADS):
        v_h = v_t[h * C_V_DIM:(h + 1) * C_V_DIM, :].astype(vt_ref.dtype)
        for c in range(n_sub):
            vt_ref[c, h * hv:h * hv + C_V_DIM, :] = v_h[:, c * t:(c + 1) * t]
            vt_ref[c, h * hv + C_V_DIM:(h + 1) * hv, :] = aug


def mla_prep(slab, q_norm, kv_norm, w_uq, w_ukv, tm=512):
    bsz, s, _ = slab.shape
    t = ATTN_TILE
    n_sub = tm // t
    vt_rows = C_HEADS * (C_V_DIM + V_AUG_ROWS)
    half = C_ROPE // 2
    scale = (C_NOPE + C_ROPE) ** -0.5 * LOG2E
    wq = (w_uq * scale).reshape(C_Q_RANK, C_HEADS, C_NOPE + C_ROPE)
    pad = jnp.zeros((C_Q_RANK, C_HEADS, LANES - C_ROPE), F32)
    x1, x2 = wq[..., C_NOPE:C_NOPE + half], wq[..., C_NOPE + half:]
    wqt = jnp.concatenate([wq, pad], axis=-1).reshape(C_Q_RANK, C_HEADS * ATTN_QK_PAD).T.astype(BF16)
    wqst = jnp.concatenate([-x2, x1, pad], axis=-1).reshape(C_Q_RANK, C_HEADS * LANES).T.astype(BF16)
    wkv = w_ukv.reshape(C_KV_RANK, C_HEADS, C_NOPE + C_V_DIM)
    wkn = wkv[..., :C_NOPE].reshape(C_KV_RANK, C_HEADS * C_NOPE).astype(BF16)
    wvt = wkv[..., C_NOPE:].reshape(C_KV_RANK, C_W).T.astype(BF16)

    pos = jnp.arange(s, dtype=jnp.int32)
    inv = ROPE_THETA ** (-jnp.arange(half, dtype=F32) / half)
    ang = pos.astype(F32)[:, None] * inv[None, :]
    zpad = jnp.zeros((s, LANES - C_ROPE), F32)
    cos_tab = jnp.concatenate([jnp.cos(ang), jnp.cos(ang), zpad], axis=-1)
    sin_tab = jnp.concatenate([jnp.sin(ang), jnp.sin(ang), zpad], axis=-1)

    nt = s // t
    full = lambda shape: pl.BlockSpec(shape, lambda b, i: (0,) * len(shape))
    return pl.pallas_call(
        _mla_prep_body,
        grid=(bsz, s // tm),
        in_specs=[pl.BlockSpec((None, tm, C_Q_RANK), lambda b, i: (b, i, OFF_CQ // C_Q_RANK)),
                  pl.BlockSpec((None, tm, C_KV_RANK), lambda b, i: (b, i, OFF_CKV // C_KV_RANK)),
                  pl.BlockSpec((None, tm, LANES), lambda b, i: (b, i, OFF_KR // LANES)),
                  full((1, C_Q_RANK)), full((1, C_KV_RANK)),
                  full(wqt.shape), full(wqst.shape), full(wkn.shape), full(wvt.shape),
                  pl.BlockSpec((tm, LANES), lambda b, i: (i, 0)),
                  pl.BlockSpec((tm, LANES), lambda b, i: (i, 0)),
                  pl.BlockSpec((LANES, tm), lambda b, i: (0, i)),
                  pl.BlockSpec((LANES, tm), lambda b, i: (0, i))],
        out_specs=[pl.BlockSpec((None, n_sub, C_HEADS * ATTN_QK_PAD, t), lambda b, i: (b, i, 0, 0)),
                   pl.BlockSpec((None, tm, C_HEADS * ATTN_QK_PAD), lambda b, i: (b, i, 0)),
                   pl.BlockSpec((None, n_sub, vt_rows, t), lambda b, i: (b, i, 0, 0))],
        out_shape=[jax.ShapeDtypeStruct((bsz, nt, C_HEADS * ATTN_QK_PAD, t), BF16),
                   jax.ShapeDtypeStruct((bsz, s, C_HEADS * ATTN_QK_PAD), BF16),
                   jax.ShapeDtypeStruct((bsz, nt, vt_rows, t), BF16)],
        compiler_params=_cparams(2),
        name="mla_prep",
    )(slab, slab, slab, q_norm.reshape(1, C_Q_RANK), kv_norm.reshape(1, C_KV_RANK),
      wqt, wqst, wkn, wvt, cos_tab, sin_tab, cos_tab.T, sin_tab.T)


def _mla_attn_body(qt_ref, k_ref, vt_ref, z_ref, o_ref, m_s, acc_s, *s_refs, g_heads):
    i = pl.program_id(2)
    t = ATTN_TILE
    dq, dv = ATTN_QK_PAD, C_V_DIM
    hv = dv + V_AUG_ROWS
    _softmax_init(m_s, acc_s)

    def raw_scores(g, j):
        start = pl.multiple_of(j * t, t)
        return jnp.dot(k_ref[pl.ds(start, t), g * dq:(g + 1) * dq], qt_ref[g * dq:(g + 1) * dq, :],
                       preferred_element_type=F32)

    def diagonal_scores(slot):
        key = lax.broadcasted_iota(jnp.int32, (t, t), 0)
        qry = lax.broadcasted_iota(jnp.int32, (t, t), 1)
        for g in range(g_heads):
            s_refs[g][slot] = jnp.where(key <= qry, raw_scores(g, i), NEG)

    def past_scores(j, slot):
        for g in range(g_heads):
            s_refs[g][slot] = raw_scores(g, j)

    def consume(j, slot):
        for g in range(g_heads):
            _softmax_step_t(s_refs[g][slot], vt_ref[j, g * hv:(g + 1) * hv, :], g, m_s, acc_s)

    _skewed_blocks([(diagonal_scores, i)], i, past_scores, consume)
    _attn_finish(z_ref, o_ref, acc_s, g_heads, dv)


def mla_attention(qt, kf, vt, slab, g_heads=ATTN_HEADS_PER_STEP):
    bsz, nt, _, t = qt.shape
    s = nt * t
    dq, dv = ATTN_QK_PAD, C_V_DIM
    assert t == ATTN_TILE and C_HEADS % g_heads == 0 and OFF_ZC % (g_heads * dv) == 0
    cz = OFF_ZC // (g_heads * dv)
    return pl.pallas_call(
        functools.partial(_mla_attn_body, g_heads=g_heads),
        grid=(bsz, C_HEADS // g_heads, nt),
        in_specs=[pl.BlockSpec((None, None, g_heads * dq, t), lambda b, h, i: (b, i, h, 0)),
                  pl.BlockSpec((None, s, g_heads * dq), lambda b, h, i: (b, 0, h),
                               pipeline_mode=pl.Buffered(1)),
                  pl.BlockSpec((None, nt, g_heads * (dv + V_AUG_ROWS), t), lambda b, h, i: (b, 0, h, 0),
                               pipeline_mode=pl.Buffered(1)),
                  pl.BlockSpec((None, t, g_heads * dv), lambda b, h, i: (b, i, cz + h))],
        out_specs=pl.BlockSpec((None, t, g_heads * dv), lambda b, h, i: (b, i, h)),
        out_shape=jax.ShapeDtypeStruct((bsz, s, C_W), BF16),
        scratch_shapes=_attn_scratch(g_heads, t, dv),
        compiler_params=_cparams(3),
        name="mla_attn",
    )(qt, kf, vt, slab)


def _merge_body(ya_ref, yb_ref, yc_ref, wa_ref, wb_ref, wc_ref, ga_ref, gb_ref, gc_ref, o_ref, w_s):
    @pl.when(pl.program_id(1) == 0)
    def _cast_weights():
        for n, w_ref in enumerate((wa_ref, wb_ref, wc_ref)):
            w_s[n] = w_ref[...].astype(w_s.dtype)

    acc = None
    for n, (y_ref, g_ref) in enumerate(((ya_ref, ga_ref), (yb_ref, gb_ref), (yc_ref, gc_ref))):
        term = (jax.nn.sigmoid(g_ref[...].astype(F32))
                * jnp.dot(y_ref[...], w_s[n], preferred_element_type=F32))
        acc = term if acc is None else acc + term
    o_ref[...] = acc.astype(o_ref.dtype)


def branch_merge(ya, yb, yc, w_branch, layer, slab_b, tm=512, tn=1024):
    m, w = ya.shape
    d = w_branch.shape[-1]
    assert OFF_GT % tn == 0
    g0 = OFF_GT // tn
    gper = d // tn
    y_spec = pl.BlockSpec((tm, w), lambda j, i: (i, 0))
    w_specs = [pl.BlockSpec((None, None, w, tn), functools.partial(lambda j, i, n: (layer, n, 0, j), n=n))
               for n in range(N_BRANCH)]
    g_specs = [pl.BlockSpec((tm, tn), functools.partial(lambda j, i, n: (i, g0 + n * gper + j), n=n))
               for n in range(N_BRANCH)]
    return pl.pallas_call(
        _merge_body,
        grid=(d // tn, m // tm),
        in_specs=[y_spec, y_spec, y_spec] + w_specs + g_specs,
        out_specs=pl.BlockSpec((tm, tn), lambda j, i: (i, j)),
        out_shape=jax.ShapeDtypeStruct((m, d), BF16),
        scratch_shapes=[pltpu.VMEM((N_BRANCH, w, tn), BF16)],
        compiler_params=_cparams(2),
        name="branch_merge",
    )(ya, yb, yc, w_branch, w_branch, w_branch, slab_b, slab_b, slab_b)


def _out_body(mg_ref, w_ref, x_ref, g_ref, *refs, last):
    out_refs, w_s = refs[:-1], refs[-1]

    @pl.when(pl.program_id(0) == 0)
    def _cast_weights():
        w_s[...] = w_ref[...].astype(w_s.dtype)

    x_new = x_ref[...] + jnp.dot(mg_ref[...], w_s[...], preferred_element_type=F32)
    y = x_new * lax.rsqrt(jnp.mean(x_new * x_new, axis=-1, keepdims=True) + EPS) * g_ref[...]
    if last:
        out_refs[0][...] = y
    else:
        out_refs[0][...] = x_new
        out_refs[1][...] = y.astype(out_refs[1].dtype)


def out_projection(merged, w_out, layer, x2d, gain, last, tm=512):
    m, d = x2d.shape
    row = pl.BlockSpec((tm, d), lambda i: (i, 0))
    if last:
        out_specs, out_shape = row, jax.ShapeDtypeStruct((m, d), F32)
    else:
        out_specs = [row, row]
        out_shape = [jax.ShapeDtypeStruct((m, d), F32), jax.ShapeDtypeStruct((m, d), BF16)]
    return pl.pallas_call(
        functools.partial(_out_body, last=last),
        grid=(m // tm,),
        in_specs=[row,
                  pl.BlockSpec((None, d, d), lambda i: (layer, 0, 0), pipeline_mode=pl.Buffered(1)),
                  row, pl.BlockSpec((1, d), lambda i: (0, 0))],
        out_specs=out_specs,
        out_shape=out_shape,
        scratch_shapes=[pltpu.VMEM((d, d), BF16)],
        compiler_params=_cparams(1),
        name="out_proj",
    )(merged, w_out, x2d, gain.reshape(1, d))


def kernel(x, norm_gain, w_in, t5_table, mlstm_conv_w, mlstm_conv_b, mlstm_i_bias, mlstm_f_bias,
           mlstm_out_norm, mla_q_norm, mla_kv_norm, mla_w_uq, mla_w_ukv, w_branch, w_out, final_norm):
    bsz, s, d = x.shape
    m = bsz * s
    x2d = x.reshape(m, d)
    h = rmsnorm_rows(x2d, norm_gain[0])
    w_in_t = jnp.swapaxes(w_in, 1, 2)
    out = None
    for l in range(DEPTH):
        slab2d = input_projection(h, w_in_t, l, SLAB_SRC_ROWS)
        slab = slab2d.reshape(bsz, s, D_SLAB)
        slab_small = input_projection_small(h, w_in_t, l).reshape(bsz, s, D_SLAB_SMALL)
        qt_a, vt_a = moba_qv_projection(h, w_in_t, l, bsz)
        ya = moba_branch(qt_a, vt_a, slab, t5_table)
        yb = mlstm_branch(slab, slab_small, mlstm_conv_w[l], mlstm_conv_b[l], mlstm_i_bias[l], mlstm_f_bias[l],
                          mlstm_out_norm[l])
        qt_c, k_c, vt_c = mla_prep(slab_small, mla_q_norm[l], mla_kv_norm[l], mla_w_uq[l], mla_w_ukv[l])
        yc = mla_attention(qt_c, k_c, vt_c, slab)
        merged = branch_merge(ya.reshape(m, A_W), yb.reshape(m, B_V_W), yc.reshape(m, C_W),
                              w_branch, l, slab2d)
        last = l == DEPTH - 1
        gain = final_norm if last else norm_gain[l + 1]
        res = out_projection(merged, w_out, l, x2d, gain, last)
        if last:
            out = res
        else:
            x2d, h = res
    return out.reshape(bsz, s, d)
```

```python
import functools
import math

import jax
import jax.numpy as jnp
import numpy as np
from jax import lax
from jax.experimental import pallas as pl
from jax.experimental.pallas import tpu as pltpu

F32 = jnp.float32
BF16 = jnp.bfloat16

D_MODEL = 2048
DEPTH = 2
EPS = 1e-6
NEG = -1e30
LOG2E = math.log2(math.e)

A_HEADS = 8
A_HEAD_DIM = 128
A_BLOCK = 256
A_TOPK = 3
T5_BUCKETS = 32
T5_MAX_DIST = 128
B_HEADS = 4
B_QK_DIM = 128
B_V_DIM = 256
B_CONV = 4
C_HEADS = 8
C_Q_RANK = 512
C_KV_RANK = 256
C_NOPE = 128
C_ROPE = 64
C_V_DIM = 128
ROPE_THETA = 10000.0
N_BRANCH = 3

A_W = A_HEADS * A_HEAD_DIM
B_QK_W = B_HEADS * B_QK_DIM
B_V_W = B_HEADS * B_V_DIM
C_W = C_HEADS * C_V_DIM

LANES = 128
SUBLANES = 8
VMEM_LIMIT_BYTES = 56 * 1024 * 1024

SPLIT_SIZES = (A_W, A_W, A_W, A_W, B_QK_W, B_QK_W, B_V_W, B_HEADS, B_HEADS, B_V_W, B_V_W,
               C_Q_RANK, C_KV_RANK, C_ROPE, C_W, N_BRANCH * D_MODEL)
(SRC_QA, SRC_KA, SRC_VA, SRC_ZA, SRC_QB, SRC_KB, SRC_VB, SRC_IB, SRC_FB, SRC_OB, SRC_ZB,
 SRC_CQ, SRC_CKV, SRC_KR, SRC_ZC, SRC_GT) = (int(v) for v in np.cumsum((0,) + SPLIT_SIZES[:-1]))
D_IN = int(sum(SPLIT_SIZES))

W_BLOCK = 1024
SLAB_SRC_ROWS = ((SRC_KA, SRC_ZA, SRC_QB, SRC_VB, SRC_OB, SRC_ZB, SRC_ZC)
                 + tuple(SRC_GT + k * W_BLOCK for k in range(N_BRANCH * D_MODEL // W_BLOCK)))
assert SRC_KB == SRC_QB + B_QK_W and all(r % SUBLANES == 0 for r in SLAB_SRC_ROWS)
OFF_KA = 0
OFF_ZA = OFF_KA + W_BLOCK
OFF_QKB = OFF_ZA + W_BLOCK
OFF_VB = OFF_QKB + W_BLOCK
OFF_OB = OFF_VB + W_BLOCK
OFF_ZB = OFF_OB + W_BLOCK
OFF_ZC = OFF_ZB + W_BLOCK
OFF_GT = OFF_ZC + W_BLOCK
D_SLAB = len(SLAB_SRC_ROWS) * W_BLOCK
SMALL_WIN1_ROWS = C_Q_RANK + C_KV_RANK + LANES
SMALL_WIN2_ROWS = LANES
assert SRC_CKV == SRC_CQ + C_Q_RANK and SRC_KR == SRC_CKV + C_KV_RANK and SRC_FB == SRC_IB + B_HEADS
assert SRC_CQ % SUBLANES == 0 and SRC_IB % SUBLANES == 0
assert SRC_CQ + SMALL_WIN1_ROWS <= D_IN and SRC_IB + SMALL_WIN2_ROWS <= D_IN
OFF_CQ = 0
OFF_CKV = OFF_CQ + C_Q_RANK
OFF_KR = OFF_CKV + C_KV_RANK
OFF_IF = OFF_KR + LANES
D_SLAB_SMALL = SMALL_WIN1_ROWS + SMALL_WIN2_ROWS
assert D_SLAB_SMALL == W_BLOCK

ATTN_TILE = 256
ATTN_QK_PAD = 2 * LANES
ATTN_HEADS_PER_STEP = 8
assert ATTN_TILE == A_BLOCK


def _cparams(n_axes):
    return pltpu.CompilerParams(dimension_semantics=("arbitrary",) * n_axes,
                                vmem_limit_bytes=VMEM_LIMIT_BYTES)


def _silu(t):
    return t * jax.nn.sigmoid(t)


_NT = (((1,), (1,)), ((), ()))


def _rmsnorm_body(x_ref, g_ref, o_ref):
    xf = x_ref[...]
    y = xf * lax.rsqrt(jnp.mean(xf * xf, axis=-1, keepdims=True) + EPS)
    o_ref[...] = (y * g_ref[...]).astype(o_ref.dtype)


def rmsnorm_rows(x2d, gain, tm=512):
    m, d = x2d.shape
    return pl.pallas_call(
        _rmsnorm_body,
        grid=(m // tm,),
        in_specs=[pl.BlockSpec((tm, d), lambda i: (i, 0)),
                  pl.BlockSpec((1, d), lambda i: (0, 0))],
        out_specs=pl.BlockSpec((tm, d), lambda i: (i, 0)),
        out_shape=jax.ShapeDtypeStruct((m, d), BF16),
        compiler_params=_cparams(1),
        name="rmsnorm",
    )(x2d, gain.reshape(1, d))


def _row_window(rows, d):
    return (pl.Squeezed(), pl.Element(rows), pl.Element(d))


def _proj_body(rows_ref, h_ref, w_ref, o_ref, wb_s):
    del rows_ref
    @pl.when(pl.program_id(1) == 0)
    def _cast_weights():
        for c in range(0, w_ref.shape[0], LANES):
            wb_s[:, c:c + LANES] = w_ref[c:c + LANES, :].T.astype(wb_s.dtype)

    o_ref[...] = jnp.dot(h_ref[...], wb_s[...], preferred_element_type=F32).astype(o_ref.dtype)


def input_projection(h2d, w_in_t, layer, src_rows, tm=1024):
    m, d = h2d.shape
    tn = W_BLOCK
    grid_spec = pltpu.PrefetchScalarGridSpec(
        num_scalar_prefetch=1,
        grid=(len(src_rows), m // tm),
        in_specs=[pl.BlockSpec((tm, d), lambda j, i, rows: (i, 0)),
                  pl.BlockSpec(_row_window(tn, d),
                               lambda j, i, rows: (layer, pl.multiple_of(rows[j], SUBLANES), 0))],
        out_specs=pl.BlockSpec((tm, tn), lambda j, i, rows: (i, j)),
        scratch_shapes=[pltpu.VMEM((d, tn), BF16)])
    return pl.pallas_call(
        _proj_body,
        grid_spec=grid_spec,
        out_shape=jax.ShapeDtypeStruct((m, len(src_rows) * tn), BF16),
        compiler_params=_cparams(2),
        name="input_proj",
    )(jnp.asarray(src_rows, jnp.int32), h2d, w_in_t)


def _proj_small_body(h_ref, w1_ref, w2_ref, o_ref, wb_s):
    @pl.when(pl.program_id(0) == 0)
    def _cast_weights():
        wb_s[:SMALL_WIN1_ROWS, :] = w1_ref[...].astype(wb_s.dtype)
        wb_s[SMALL_WIN1_ROWS:, :] = w2_ref[...].astype(wb_s.dtype)

    o_ref[...] = lax.dot_general(h_ref[...], wb_s[...], _NT, preferred_element_type=F32).astype(o_ref.dtype)


def input_projection_small(h2d, w_in_t, layer, tm=1024):
    m, d = h2d.shape
    return pl.pallas_call(
        _proj_small_body,
        grid=(m // tm,),
        in_specs=[pl.BlockSpec((tm, d), lambda i: (i, 0)),
                  pl.BlockSpec(_row_window(SMALL_WIN1_ROWS, d), lambda i: (layer, SRC_CQ, 0)),
                  pl.BlockSpec(_row_window(SMALL_WIN2_ROWS, d), lambda i: (layer, SRC_IB, 0))],
        out_specs=pl.BlockSpec((tm, D_SLAB_SMALL), lambda i: (i, 0)),
        out_shape=jax.ShapeDtypeStruct((m, D_SLAB_SMALL), BF16),
        scratch_shapes=[pltpu.VMEM((D_SLAB_SMALL, d), BF16)],
        compiler_params=_cparams(1),
        name="input_proj_small",
    )(h2d, w_in_t, w_in_t)


def _proj_t_body(h_ref, wq_ref, wv_ref, qt_ref, vt_ref, wt_s, *, n_sub):
    t = ATTN_TILE
    hd = A_HEAD_DIM
    hv = hd + V_AUG_ROWS

    @pl.when((pl.program_id(0) == 0) & (pl.program_id(1) == 0))
    def _cast_weights():
        q_scale = A_HEAD_DIM ** -0.5 * LOG2E
        wt_s[:A_W, :] = (wq_ref[...] * q_scale).astype(wt_s.dtype)
        wt_s[A_W:, :] = wv_ref[...].astype(wt_s.dtype)

    res = lax.dot_general(wt_s[...], h_ref[...], _NT, preferred_element_type=F32)
    aug = _v_aug_rows(t, vt_ref.dtype)
    for c in range(n_sub):
        qt_ref[c] = res[:A_W, c * t:(c + 1) * t].astype(qt_ref.dtype)
        for h in range(A_HEADS):
            vt_ref[c, h * hv:h * hv + hd, :] = res[A_W + h * hd:A_W + (h + 1) * hd, c * t:(c + 1) * t
                                                   ].astype(vt_ref.dtype)
            vt_ref[c, h * hv + hd:(h + 1) * hv, :] = aug


def moba_qv_projection(h2d, w_in_t, layer, bsz, tm=512):
    m, d = h2d.shape
    assert SRC_QA % A_W == 0 and SRC_VA % A_W == 0
    t = ATTN_TILE
    tm = min(tm, m // bsz)
    n_sub = tm // t
    nt = m // bsz // t
    steps = m // bsz // tm
    vt_rows = A_HEADS * (A_HEAD_DIM + V_AUG_ROWS)
    return pl.pallas_call(
        functools.partial(_proj_t_body, n_sub=n_sub),
        grid=(bsz, steps),
        in_specs=[pl.BlockSpec((tm, d), lambda b, i: (b * steps + i, 0)),
                  pl.BlockSpec((None, A_W, d), lambda b, i: (layer, SRC_QA // A_W, 0),
                               pipeline_mode=pl.Buffered(1)),
                  pl.BlockSpec((None, A_W, d), lambda b, i: (layer, SRC_VA // A_W, 0),
                               pipeline_mode=pl.Buffered(1))],
        out_specs=[pl.BlockSpec((None, n_sub, A_W, t), lambda b, i: (b, i, 0, 0)),
                   pl.BlockSpec((None, n_sub, vt_rows, t), lambda b, i: (b, i, 0, 0))],
        out_shape=[jax.ShapeDtypeStruct((bsz, nt, A_W, t), BF16),
                   jax.ShapeDtypeStruct((bsz, nt, vt_rows, t), BF16)],
        scratch_shapes=[pltpu.VMEM((2 * A_W, d), BF16)],
        compiler_params=_cparams(2),
        name="moba_qv_proj",
    )(h2d, w_in_t, w_in_t)


V_AUG_ROWS = 16


def _v_aug_rows(t, dtype):
    return jnp.where(lax.broadcasted_iota(jnp.int32, (V_AUG_ROWS, t), 0) == 0, 1.0, 0.0).astype(dtype)


def _softmax_init(m_s, acc_s):
    m_s[...] = jnp.full(m_s.shape, -jnp.inf, F32)
    acc_s[...] = jnp.zeros(acc_s.shape, F32)


def _softmax_step_t(s, vt_blk, g, m_s, acc_s):
    m_prev = m_s[g]
    m_new = jnp.maximum(m_prev, jnp.max(s, axis=0, keepdims=True))
    alpha = jnp.exp2(m_prev - m_new)
    p = jnp.exp2(s - m_new)
    acc_s[g] = alpha * acc_s[g] + jnp.dot(vt_blk, p.astype(BF16), preferred_element_type=F32)
    m_s[g] = m_new


def _attn_scratch(g_heads, t, dv):
    state = [pltpu.VMEM((g_heads, 1, t), F32), pltpu.VMEM((g_heads, dv + V_AUG_ROWS, t), F32)]
    return state + [pltpu.VMEM((2, t, t), F32)] * g_heads


def _skewed_blocks(lead, n_uniform, uniform_scores, consume):
    lead[0][0](0)
    pend_blk, pend_slot = lead[0][1], 0
    for fn, blk in lead[1:]:
        fn(1 - pend_slot)
        consume(pend_blk, pend_slot)
        pend_blk, pend_slot = blk, 1 - pend_slot
    a, b = pend_slot, 1 - pend_slot

    def pair(first, pend):
        uniform_scores(first, b)
        consume(pend, a)
        uniform_scores(first + 1, a)
        consume(first, b)
        return first + 1

    def two_pairs(jj, pend):
        return pair(4 * jj + 2, pair(4 * jj, pend))

    n_quads = n_uniform // 4
    pend = lax.fori_loop(0, n_quads, two_pairs, pend_blk)
    rest = n_uniform - 4 * n_quads
    has_pair = rest >= 2
    first = 4 * n_quads

    @pl.when(has_pair)
    def _last_pair():
        pair(first, pend)

    pend = jnp.where(has_pair, first + 1, pend)
    last = jnp.where(has_pair, first + 2, first)

    @pl.when(rest % 2 == 1)
    def _odd_tail():
        uniform_scores(last, b)
        consume(pend, a)
        consume(last, b)

    @pl.when(rest % 2 == 0)
    def _even_tail():
        consume(pend, a)


def _attn_finish(z_ref, o_ref, acc_s, g_heads, dv):
    for g in range(g_heads):
        out_t = acc_s[g, :dv, :] / acc_s[g, dv:dv + 1, :]
        o_ref[:, g * dv:(g + 1) * dv] = (out_t.T * _silu(z_ref[:, g * dv:(g + 1) * dv].astype(F32))
                                         ).astype(o_ref.dtype)


def _t5_thresholds():
    max_exact = T5_BUCKETS // 2
    d = np.arange(0, 4 * T5_MAX_DIST, dtype=np.int64)
    dd = np.maximum(d, 1).astype(np.float32)
    large = max_exact + (np.log(dd / np.float32(max_exact)) / np.float32(math.log(T5_MAX_DIST / max_exact))
                         * np.float32(T5_BUCKETS - max_exact)).astype(np.int32)
    large = np.minimum(large, T5_BUCKETS - 1)
    bucket = np.where(d < max_exact, d, large)
    assert np.all(np.diff(bucket) >= 0) and bucket[-1] == T5_BUCKETS - 1
    return [int(np.argmax(bucket >= k)) for k in range(T5_BUCKETS)]


_T5_THRESH = _t5_thresholds()

MASK_BIG = 1e30


def _moba_body(t5_ref, qt_ref, k_ref, vt_ref, z_ref, o_ref,
               kaug_s, km_s, bias_own_s, bias_prev_s, qa_s, m_s, acc_s, *s_refs, n_blocks, g_heads):
    hg = pl.program_id(0)
    b = pl.program_id(1)
    i = pl.program_id(2)
    t, hd = A_BLOCK, A_HEAD_DIM
    hv = hd + V_AUG_ROWS
    seq = n_blocks * t
    nb_pad = -(-n_blocks // SUBLANES) * SUBLANES

    @pl.when((hg == 0) & (b == 0) & (i == 0))
    def _mask_columns():
        key_blk = lax.broadcasted_iota(jnp.int32, (seq, LANES), 0) // t
        lane = lax.broadcasted_iota(jnp.int32, (seq, LANES), 1)
        pattern = jnp.where(lane == key_blk, -MASK_BIG, 0.0).astype(kaug_s.dtype)
        for g in range(g_heads):
            kaug_s[g, :, hd:] = pattern

    @pl.when((b == 0) & (i == 0))
    def _build_bias():
        key = lax.broadcasted_iota(jnp.int32, (t, t), 0)
        qry = lax.broadcasted_iota(jnp.int32, (t, t), 1)
        for g in range(g_heads):
            head = hg * g_heads + g
            far = t5_ref[T5_BUCKETS - 1, head]
            for dist, dst in ((qry - key, bias_own_s), (t + qry - key, bias_prev_s)):
                bias = jnp.full((t, t), t5_ref[0, head], F32)
                for kk in range(1, T5_BUCKETS):
                    bias = jnp.where(dist >= _T5_THRESH[kk], t5_ref[kk, head], bias)
                dst[g] = (bias - far) * LOG2E

    @pl.when(i == 0)
    def _new_sequence():
        for g in range(g_heads):
            kaug_s[g, :, :hd] = k_ref[:, g * hd:(g + 1) * hd]
            km_s[g] = jnp.zeros(km_s.shape[1:], km_s.dtype)
            for nb in range(n_blocks):
                mean = jnp.mean(k_ref[nb * t:(nb + 1) * t, g * hd:(g + 1) * hd].astype(F32), axis=0, keepdims=True)
                hi = mean.astype(BF16)
                km_s[g, nb:nb + 1, :] = hi
                km_s[g, LANES + nb:LANES + nb + 1, :] = (mean - hi.astype(F32)).astype(BF16)

    blk_row = lax.broadcasted_iota(jnp.int32, (nb_pad, t), 0)
    past = blk_row < i
    for g in range(g_heads):
        qt = qt_ref[g * hd:(g + 1) * hd, :]
        g2 = jnp.dot(km_s[g], qt, preferred_element_type=F32)
        gate = g2[:nb_pad, :] + g2[LANES:LANES + nb_pad, :]
        cur = jnp.where(past, gate, NEG)
        keep = blk_row == i
        for _ in range(A_TOPK):
            mx = jnp.max(cur, axis=0, keepdims=True)
            idx = jnp.min(jnp.where(cur == mx, blk_row, nb_pad), axis=0, keepdims=True)
            pick = blk_row == idx
            keep = keep | (pick & past)
            cur = jnp.where(pick, -jnp.inf, cur)
        unsel = jnp.where(keep, 0.0, 1.0)
        qa_s[g, :hd, :] = qt
        qa_s[g, hd:, :] = jnp.concatenate([unsel, jnp.zeros((LANES - nb_pad, t), F32)], axis=0).astype(qa_s.dtype)

    _softmax_init(m_s, acc_s)

    def raw_scores(g, j):
        start = pl.multiple_of(j * t, t)
        return jnp.dot(kaug_s[g, pl.ds(start, t), :], qa_s[g], preferred_element_type=F32)

    def own_scores(slot):
        key = lax.broadcasted_iota(jnp.int32, (t, t), 0)
        qry = lax.broadcasted_iota(jnp.int32, (t, t), 1)
        for g in range(g_heads):
            s_refs[g][slot] = jnp.where(key <= qry, raw_scores(g, i) + bias_own_s[g], NEG)

    j_prev = jnp.maximum(i - 1, 0)
    no_prev = jnp.where(i == 0, -MASK_BIG, 0.0)

    def prev_scores(slot):
        for g in range(g_heads):
            s_refs[g][slot] = raw_scores(g, j_prev) + (bias_prev_s[g] + no_prev)

    def far_scores(j, slot):
        for g in range(g_heads):
            s_refs[g][slot] = raw_scores(g, j)

    def consume(j, slot):
        for g in range(g_heads):
            _softmax_step_t(s_refs[g][slot], vt_ref[j, g * hv:(g + 1) * hv, :], g, m_s, acc_s)

    _skewed_blocks([(own_scores, i), (prev_scores, j_prev)], j_prev, far_scores, consume)
    _attn_finish(z_ref, o_ref, acc_s, g_heads, hd)


def moba_branch(qt, vt, slab, t5_table, g_heads=ATTN_HEADS_PER_STEP):
    bsz, nb, _, t = qt.shape
    s = nb * t
    hd = A_HEAD_DIM
    gw = g_heads * hd
    assert t == A_BLOCK and nb <= LANES and A_HEADS % g_heads == 0
    assert OFF_KA % gw == 0 and OFF_ZA % gw == 0
    assert _T5_THRESH[T5_BUCKETS - 1] <= t + 1
    ck, cz = OFF_KA // gw, OFF_ZA // gw
    return pl.pallas_call(
        functools.partial(_moba_body, n_blocks=nb, g_heads=g_heads),
        grid=(A_HEADS // g_heads, bsz, nb),
        in_specs=[pl.BlockSpec(memory_space=pltpu.SMEM),
                  pl.BlockSpec((None, None, gw, t), lambda h, b, i: (b, i, h, 0)),
                  pl.BlockSpec((None, s, gw), lambda h, b, i: (b, 0, ck + h), pipeline_mode=pl.Buffered(1)),
                  pl.BlockSpec((None, nb, g_heads * (hd + V_AUG_ROWS), t), lambda h, b, i: (b, 0, h, 0),
                               pipeline_mode=pl.Buffered(1)),
                  pl.BlockSpec((None, t, gw), lambda h, b, i: (b, i, cz + h))],
        out_specs=pl.BlockSpec((None, t, gw), lambda h, b, i: (b, i, h)),
        out_shape=jax.ShapeDtypeStruct((bsz, s, A_W), BF16),
        scratch_shapes=[pltpu.VMEM((g_heads, s, ATTN_QK_PAD), BF16),
                        pltpu.VMEM((g_heads, 2 * LANES, hd), BF16),
                        pltpu.VMEM((g_heads, t, t), F32),
                        pltpu.VMEM((g_heads, t, t), F32),
                        pltpu.VMEM((g_heads, ATTN_QK_PAD, t), BF16)] + _attn_scratch(g_heads, t, hd),
        compiler_params=_cparams(3),
        name="moba",
    )(t5_table, qt, slab, vt, slab)


MLSTM_CHUNK = 256
CONV_HALO = 8


def _log_sigmoid(t):
    return jnp.minimum(t, 0.0) - jnp.log1p(jnp.exp(-jnp.abs(t)))


def _mlstm_body(qk_ref, v_ref, ob_ref, zb_ref, if_ref, cw_ref, cb_ref, gb_ref, gn_ref, o_ref,
                xe_s, c_s, n_s, m_s):
    c = pl.program_id(1)
    L = MLSTM_CHUNK
    dk, dv = B_QK_DIM, B_V_DIM

    @pl.when(c == 0)
    def _reset():
        xe_s[...] = jnp.zeros_like(xe_s)
        c_s[...] = jnp.zeros_like(c_s)
        n_s[...] = jnp.zeros_like(n_s)
        m_s[...] = jnp.zeros_like(m_s)

    x = qk_ref[...].astype(F32)
    xe = jnp.concatenate([xe_s[...], x], axis=0)
    conv = cb_ref[...] + cw_ref[B_CONV - 1:B_CONV, :] * x
    for j in range(B_CONV - 1):
        back = B_CONV - 1 - j
        conv = conv + cw_ref[j:j + 1, :] * pltpu.roll(xe, back, axis=0)[CONV_HALO:, :]
    xe_s[...] = x[L - CONV_HALO:, :]
    qk = _silu(conv)

    gi = if_ref[...].astype(F32) + gb_ref[...]
    lf = _log_sigmoid(gi)
    row = lax.broadcasted_iota(jnp.int32, (L, L), 0)
    col = lax.broadcasted_iota(jnp.int32, (L, L), 1)
    causal = col <= row
    tri = jnp.where(causal, 1.0, 0.0).astype(F32)
    b_cols = jnp.dot(tri, lf, precision=lax.Precision.HIGHEST, preferred_element_type=F32)
    b_rows = b_cols.T
    li_rows = gi.T

    for h in range(B_HEADS):
        q_h = qk[:, h * dk:(h + 1) * dk]
        k_h = qk[:, B_QK_W + h * dk:B_QK_W + (h + 1) * dk] * (dk ** -0.5)
        v_h = v_ref[:, h * dv:(h + 1) * dv]
        b_c = b_cols[:, B_HEADS + h:B_HEADS + h + 1]
        b_r = b_rows[B_HEADS + h:B_HEADS + h + 1, :]
        li_c = gi[:, h:h + 1]
        li_r = li_rows[h:h + 1, :]
        m_prev = m_s[h:h + 1, 0:1]
        c_prev = c_s[h]
        n_prev = n_s[h:h + 1, :]

        a_c = b_c + m_prev
        dmat = jnp.where(causal, b_c - b_r + li_r, -jnp.inf)
        m_t = jnp.maximum(a_c, jnp.max(dmat, axis=-1, keepdims=True))
        w_inter = jnp.exp(a_c - m_t)
        q_b = q_h.astype(BF16)
        k_b = k_h.astype(BF16)
        sc = (lax.dot_general(q_b, k_b, _NT, preferred_element_type=F32)
              * jnp.exp(dmat - m_t))
        num = (w_inter * jnp.dot(q_b, c_prev.astype(BF16), preferred_element_type=F32)
               + jnp.dot(sc.astype(BF16), v_h, preferred_element_type=F32))
        den = (w_inter * jnp.sum(q_h * n_prev, axis=-1, keepdims=True)
               + jnp.sum(sc, axis=-1, keepdims=True))
        hh = num / jnp.maximum(jnp.abs(den), jnp.exp(-m_t))

        b_last = b_c[L - 1:L, :]
        g_c = b_last - b_c + li_c
        m_new = jnp.maximum(b_last + m_prev, jnp.max(g_c, axis=0, keepdims=True))
        decay = jnp.exp(b_last + m_prev - m_new)
        wk = jnp.exp(g_c - m_new) * k_h
        c_s[h] = decay * c_prev + lax.dot_general(wk.astype(BF16), v_h, (((0,), (0,)), ((), ())),
                                                  preferred_element_type=F32)
        n_s[h:h + 1, :] = decay * n_prev + jnp.sum(wk, axis=0, keepdims=True)
        m_s[h:h + 1, :] = jnp.broadcast_to(m_new, (1, LANES))

        mu = jnp.mean(hh, axis=-1, keepdims=True)
        dlt = hh - mu
        var = jnp.mean(dlt * dlt, axis=-1, keepdims=True)
        y = dlt * lax.rsqrt(var + EPS) * gn_ref[:, h * dv:(h + 1) * dv]
        y = (y * jax.nn.sigmoid(ob_ref[:, h * dv:(h + 1) * dv].astype(F32))
             * _silu(zb_ref[:, h * dv:(h + 1) * dv].astype(F32)))
        o_ref[:, h * dv:(h + 1) * dv] = y.astype(o_ref.dtype)


def mlstm_branch(slab, slab_small, conv_w, conv_b, i_bias, f_bias, out_norm):
    bsz, s, _ = slab.shape
    L = MLSTM_CHUNK
    assert s % L == 0
    gate_bias = jnp.zeros((1, LANES), F32).at[0, :B_HEADS].set(i_bias).at[0, B_HEADS:2 * B_HEADS].set(f_bias)
    w2 = 2 * B_QK_W
    full = lambda shape: pl.BlockSpec(shape, lambda b, c: (0,) * len(shape))
    return pl.pallas_call(
        _mlstm_body,
        grid=(bsz, s // L),
        in_specs=[pl.BlockSpec((None, L, w2), lambda b, c: (b, c, OFF_QKB // w2)),
                  pl.BlockSpec((None, L, B_V_W), lambda b, c: (b, c, OFF_VB // B_V_W)),
                  pl.BlockSpec((None, L, B_V_W), lambda b, c: (b, c, OFF_OB // B_V_W)),
                  pl.BlockSpec((None, L, B_V_W), lambda b, c: (b, c, OFF_ZB // B_V_W)),
                  pl.BlockSpec((None, L, LANES), lambda b, c: (b, c, OFF_IF // LANES)),
                  full((B_CONV, w2)), full((1, w2)), full((1, LANES)), full((1, B_V_W))],
        out_specs=pl.BlockSpec((None, L, B_V_W), lambda b, c: (b, c, 0)),
        out_shape=jax.ShapeDtypeStruct((bsz, s, B_V_W), BF16),
        scratch_shapes=[pltpu.VMEM((CONV_HALO, w2), F32),
                        pltpu.VMEM((B_HEADS, B_QK_DIM, B_V_DIM), F32),
                        pltpu.VMEM((8, B_QK_DIM), F32),
                        pltpu.VMEM((8, LANES), F32)],
        compiler_params=_cparams(2),
        name="mlstm",
    )(slab, slab, slab, slab, slab_small, conv_w, conv_b.reshape(1, w2), gate_bias,
      out_norm.reshape(1, B_V_W))


def _mla_prep_body(cq_ref, ckv_ref, kr_ref, qg_ref, kg_ref, wqt_ref, wqst_ref, wkn_ref, wvt_ref,
                   cos_ref, sin_ref, cost_ref, sint_ref, qt_ref, kf_ref, vt_ref):
    def normed(ref, g_ref):
        t = ref[...].astype(F32)
        y = t * lax.rsqrt(jnp.mean(t * t, axis=-1, keepdims=True) + EPS)
        return (y * g_ref[...]).astype(BF16)

    cqn = normed(cq_ref, qg_ref)
    ckvn = normed(ckv_ref, kg_ref)

    qt_main = lax.dot_general(wqt_ref[...], cqn, _NT, preferred_element_type=F32)
    qt_swap = lax.dot_general(wqst_ref[...], cqn, _NT, preferred_element_type=F32)
    cos_t = cost_ref[...]
    sin_t = sint_ref[...]
    t = ATTN_TILE
    n_sub = cos_t.shape[1] // t
    for h in range(C_HEADS):
        lo = h * ATTN_QK_PAD
        nope = qt_main[lo:lo + LANES, :].astype(qt_ref.dtype)
        rope = (qt_main[lo + LANES:lo + 2 * LANES, :] * cos_t
                + qt_swap[h * LANES:(h + 1) * LANES, :] * sin_t).astype(qt_ref.dtype)
        for c in range(n_sub):
            qt_ref[c, lo:lo + LANES, :] = nope[:, c * t:(c + 1) * t]
            qt_ref[c, lo + LANES:lo + 2 * LANES, :] = rope[:, c * t:(c + 1) * t]

    k_nope = jnp.dot(ckvn, wkn_ref[...], preferred_element_type=F32)
    kr = kr_ref[...].astype(F32)
    half = C_ROPE // 2
    lane = lax.broadcasted_iota(jnp.int32, kr.shape, 1)
    swapped = jnp.where(lane < half, -pltpu.roll(kr, LANES - half, axis=1), pltpu.roll(kr, half, axis=1))
    k_rot = (kr * cos_ref[...] + swapped * sin_ref[...]).astype(kf_ref.dtype)
    for h in range(C_HEADS):
        lo = h * ATTN_QK_PAD
        kf_ref[:, lo:lo + LANES] = k_nope[:, h * LANES:(h + 1) * LANES].astype(kf_ref.dtype)
        kf_ref[:, lo + LANES:lo + 2 * LANES] = k_rot

    v_t = lax.dot_general(wvt_ref[...], ckvn, _NT, preferred_element_type=F32)
    hv = C_V_DIM + V_AUG_ROWS
    aug = _v_aug_rows(t, vt_ref.dtype)
    for h in range(C_HEADS):
        v_h = v_t[h * C_V_DIM:(h + 1) * C_V_DIM, :].astype(vt_ref.dtype)
        for c in range(n_sub):
            vt_ref[c, h * hv:h * hv + C_V_DIM, :] = v_h[:, c * t:(c + 1) * t]
            vt_ref[c, h * hv + C_V_DIM:(h + 1) * hv, :] = aug


def mla_prep(slab, q_norm, kv_norm, w_uq, w_ukv, tm=512):
    bsz, s, _ = slab.shape
    t = ATTN_TILE
    n_sub = tm // t
    vt_rows = C_HEADS * (C_V_DIM + V_AUG_ROWS)
    half = C_ROPE // 2
    scale = (C_NOPE + C_ROPE) ** -0.5 * LOG2E
    wq = (w_uq * scale).reshape(C_Q_RANK, C_HEADS, C_NOPE + C_ROPE)
    pad = jnp.zeros((C_Q_RANK, C_HEADS, LANES - C_ROPE), F32)
    x1, x2 = wq[..., C_NOPE:C_NOPE + half], wq[..., C_NOPE + half:]
    wqt = jnp.concatenate([wq, pad], axis=-1).reshape(C_Q_RANK, C_HEADS * ATTN_QK_PAD).T.astype(BF16)
    wqst = jnp.concatenate([-x2, x1, pad], axis=-1).reshape(C_Q_RANK, C_HEADS * LANES).T.astype(BF16)
    wkv = w_ukv.reshape(C_KV_RANK, C_HEADS, C_NOPE + C_V_DIM)
    wkn = wkv[..., :C_NOPE].reshape(C_KV_RANK, C_HEADS * C_NOPE).astype(BF16)
    wvt = wkv[..., C_NOPE:].reshape(C_KV_RANK, C_W).T.astype(BF16)

    pos = jnp.arange(s, dtype=jnp.int32)
    inv = ROPE_THETA ** (-jnp.arange(half, dtype=F32) / half)
    ang = pos.astype(F32)[:, None] * inv[None, :]
    zpad = jnp.zeros((s, LANES - C_ROPE), F32)
    cos_tab = jnp.concatenate([jnp.cos(ang), jnp.cos(ang), zpad], axis=-1)
    sin_tab = jnp.concatenate([jnp.sin(ang), jnp.sin(ang), zpad], axis=-1)

    nt = s // t
    full = lambda shape: pl.BlockSpec(shape, lambda b, i: (0,) * len(shape))
    return pl.pallas_call(
        _mla_prep_body,
        grid=(bsz, s // tm),
        in_specs=[pl.BlockSpec((None, tm, C_Q_RANK), lambda b, i: (b, i, OFF_CQ // C_Q_RANK)),
                  pl.BlockSpec((None, tm, C_KV_RANK), lambda b, i: (b, i, OFF_CKV // C_KV_RANK)),
                  pl.BlockSpec((None, tm, LANES), lambda b, i: (b, i, OFF_KR // LANES)),
                  full((1, C_Q_RANK)), full((1, C_KV_RANK)),
                  full(wqt.shape), full(wqst.shape), full(wkn.shape), full(wvt.shape),
                  pl.BlockSpec((tm, LANES), lambda b, i: (i, 0)),
                  pl.BlockSpec((tm, LANES), lambda b, i: (i, 0)),
                  pl.BlockSpec((LANES, tm), lambda b, i: (0, i)),
                  pl.BlockSpec((LANES, tm), lambda b, i: (0, i))],
        out_specs=[pl.BlockSpec((None, n_sub, C_HEADS * ATTN_QK_PAD, t), lambda b, i: (b, i, 0, 0)),
                   pl.BlockSpec((None, tm, C_HEADS * ATTN_QK_PAD), lambda b, i: (b, i, 0)),
                   pl.BlockSpec((None, n_sub, vt_rows, t), lambda b, i: (b, i, 0, 0))],
        out_shape=[jax.ShapeDtypeStruct((bsz, nt, C_HEADS * ATTN_QK_PAD, t), BF16),
                   jax.ShapeDtypeStruct((bsz, s, C_HEADS * ATTN_QK_PAD), BF16),
                   jax.ShapeDtypeStruct((bsz, nt, vt_rows, t), BF16)],
        compiler_params=_cparams(2),
        name="mla_prep",
    )(slab, slab, slab, q_norm.reshape(1, C_Q_RANK), kv_norm.reshape(1, C_KV_RANK),
      wqt, wqst, wkn, wvt, cos_tab, sin_tab, cos_tab.T, sin_tab.T)


def _mla_attn_body(qt_ref, k_ref, vt_ref, z_ref, o_ref, m_s, acc_s, *s_refs, g_heads):
    i = pl.program_id(2)
    t = ATTN_TILE
    dq, dv = ATTN_QK_PAD, C_V_DIM
    hv = dv + V_AUG_ROWS
    _softmax_init(m_s, acc_s)

    def raw_scores(g, j):
        start = pl.multiple_of(j * t, t)
        return jnp.dot(k_ref[pl.ds(start, t), g * dq:(g + 1) * dq], qt_ref[g * dq:(g + 1) * dq, :],
                       preferred_element_type=F32)

    def diagonal_scores(slot):
        key = lax.broadcasted_iota(jnp.int32, (t, t), 0)
        qry = lax.broadcasted_iota(jnp.int32, (t, t), 1)
        for g in range(g_heads):
            s_refs[g][slot] = jnp.where(key <= qry, raw_scores(g, i), NEG)

    def past_scores(j, slot):
        for g in range(g_heads):
            s_refs[g][slot] = raw_scores(g, j)

    def consume(j, slot):
        for g in range(g_heads):
            _softmax_step_t(s_refs[g][slot], vt_ref[j, g * hv:(g + 1) * hv, :], g, m_s, acc_s)

    _skewed_blocks([(diagonal_scores, i)], i, past_scores, consume)
    _attn_finish(z_ref, o_ref, acc_s, g_heads, dv)


def mla_attention(qt, kf, vt, slab, g_heads=ATTN_HEADS_PER_STEP):
    bsz, nt, _, t = qt.shape
    s = nt * t
    dq, dv = ATTN_QK_PAD, C_V_DIM
    assert t == ATTN_TILE and C_HEADS % g_heads == 0 and OFF_ZC % (g_heads * dv) == 0
    cz = OFF_ZC // (g_heads * dv)
    return pl.pallas_call(
        functools.partial(_mla_attn_body, g_heads=g_heads),
        grid=(bsz, C_HEADS // g_heads, nt),
        in_specs=[pl.BlockSpec((None, None, g_heads * dq, t), lambda b, h, i: (b, i, h, 0)),
                  pl.BlockSpec((None, s, g_heads * dq), lambda b, h, i: (b, 0, h),
                               pipeline_mode=pl.Buffered(1)),
                  pl.BlockSpec((None, nt, g_heads * (dv + V_AUG_ROWS), t), lambda b, h, i: (b, 0, h, 0),
                               pipeline_mode=pl.Buffered(1)),
                  pl.BlockSpec((None, t, g_heads * dv), lambda b, h, i: (b, i, cz + h))],
        out_specs=pl.BlockSpec((None, t, g_heads * dv), lambda b, h, i: (b, i, h)),
        out_shape=jax.ShapeDtypeStruct((bsz, s, C_W), BF16),
        scratch_shapes=_attn_scratch(g_heads, t, dv),
        compiler_params=_cparams(3),
        name="mla_attn",
    )(qt, kf, vt, slab)


def _merge_body(ya_ref, yb_ref, yc_ref, wa_ref, wb_ref, wc_ref, ga_ref, gb_ref, gc_ref, o_ref, w_s):
    @pl.when(pl.program_id(1) == 0)
    def _cast_weights():
        for n, w_ref in enumerate((wa_ref, wb_ref, wc_ref)):
            w_s[n] = w_ref[...].astype(w_s.dtype)

    acc = None
    for n, (y_ref, g_ref) in enumerate(((ya_ref, ga_ref), (yb_ref, gb_ref), (yc_ref, gc_ref))):
        term = (jax.nn.sigmoid(g_ref[...].astype(F32))
                * jnp.dot(y_ref[...], w_s[n], preferred_element_type=F32))
        acc = term if acc is None else acc + term
    o_ref[...] = acc.astype(o_ref.dtype)


def branch_merge(ya, yb, yc, w_branch, layer, slab_b, tm=512, tn=1024):
    m, w = ya.shape
    d = w_branch.shape[-1]
    assert OFF_GT % tn == 0
    g0 = OFF_GT // tn
    gper = d // tn
    y_spec = pl.BlockSpec((tm, w), lambda j, i: (i, 0))
    w_specs = [pl.BlockSpec((None, None, w, tn), functools.partial(lambda j, i, n: (layer, n, 0, j), n=n))
               for n in range(N_BRANCH)]
    g_specs = [pl.BlockSpec((tm, tn), functools.partial(lambda j, i, n: (i, g0 + n * gper + j), n=n))
               for n in range(N_BRANCH)]
    return pl.pallas_call(
        _merge_body,
        grid=(d // tn, m // tm),
        in_specs=[y_spec, y_spec, y_spec] + w_specs + g_specs,
        out_specs=pl.BlockSpec((tm, tn), lambda j, i: (i, j)),
        out_shape=jax.ShapeDtypeStruct((m, d), BF16),
        scratch_shapes=[pltpu.VMEM((N_BRANCH, w, tn), BF16)],
        compiler_params=_cparams(2),
        name="branch_merge",
    )(ya, yb, yc, w_branch, w_branch, w_branch, slab_b, slab_b, slab_b)


def _out_body(mg_ref, w_ref, x_ref, g_ref, *refs, last):
    out_refs, w_s = refs[:-1], refs[-1]

    @pl.when(pl.program_id(0) == 0)
    def _cast_weights():
        w_s[...] = w_ref[...].astype(w_s.dtype)

    x_new = x_ref[...] + jnp.dot(mg_ref[...], w_s[...], preferred_element_type=F32)
    y = x_new * lax.rsqrt(jnp.mean(x_new * x_new, axis=-1, keepdims=True) + EPS) * g_ref[...]
    if last:
        out_refs[0][...] = y
    else:
        out_refs[0][...] = x_new
        out_refs[1][...] = y.astype(out_refs[1].dtype)


def out_projection(merged, w_out, layer, x2d, gain, last, tm=512):
    m, d = x2d.shape
    row = pl.BlockSpec((tm, d), lambda i: (i, 0))
    if last:
        out_specs, out_shape = row, jax.ShapeDtypeStruct((m, d), F32)
    else:
        out_specs = [row, row]
        out_shape = [jax.ShapeDtypeStruct((m, d), F32), jax.ShapeDtypeStruct((m, d), BF16)]
    return pl.pallas_call(
        functools.partial(_out_body, last=last),
        grid=(m // tm,),
        in_specs=[row,
                  pl.BlockSpec((None, d, d), lambda i: (layer, 0, 0), pipeline_mode=pl.Buffered(1)),
                  row, pl.BlockSpec((1, d), lambda i: (0, 0))],
        out_specs=out_specs,
        out_shape=out_shape,
        scratch_shapes=[pltpu.VMEM((d, d), BF16)],
        compiler_params=_cparams(1),
        name="out_proj",
    )(merged, w_out, x2d, gain.reshape(1, d))


def kernel(x, norm_gain, w_in, t5_table, mlstm_conv_w, mlstm_conv_b, mlstm_i_bias, mlstm_f_bias,
           mlstm_out_norm, mla_q_norm, mla_kv_norm, mla_w_uq, mla_w_ukv, w_branch, w_out, final_norm):
    bsz, s, d = x.shape
    m = bsz * s
    x2d = x.reshape(m, d)
    h = rmsnorm_rows(x2d, norm_gain[0])
    w_in_t = jnp.swapaxes(w_in, 1, 2)
    out = None
    for l in range(DEPTH):
        slab2d = input_projection(h, w_in_t, l, SLAB_SRC_ROWS)
        slab = slab2d.reshape(bsz, s, D_SLAB)
        slab_small = input_projection_small(h, w_in_t, l).reshape(bsz, s, D_SLAB_SMALL)
        qt_a, vt_a = moba_qv_projection(h, w_in_t, l, bsz)
        ya = moba_branch(qt_a, vt_a, slab, t5_table)
        yb = mlstm_branch(slab, slab_small, mlstm_conv_w[l], mlstm_conv_b[l], mlstm_i_bias[l], mlstm_f_bias[l],
                          mlstm_out_norm[l])
        qt_c, k_c, vt_c = mla_prep(slab_small, mla_q_norm[l], mla_kv_norm[l], mla_w_uq[l], mla_w_ukv[l])
        yc = mla_attention(qt_c, k_c, vt_c, slab)
        merged = branch_merge(ya.reshape(m, A_W), yb.reshape(m, B_V_W), yc.reshape(m, C_W),
                              w_branch, l, slab2d)
        last = l == DEPTH - 1
        gain = final_norm if last else norm_gain[l + 1]
        res = out_projection(merged, w_out, l, x2d, gain, last)
        if last:
            out = res
        else:
            x2d, h = res
    return out.reshape(bsz, s, d)
```

```python
import functools
import math

import jax
import jax.numpy as jnp
import numpy as np
from jax import lax
from jax.experimental import pallas as pl
from jax.experimental.pallas import tpu as pltpu

F32 = jnp.float32
BF16 = jnp.bfloat16

D_MODEL = 2048
DEPTH = 2
EPS = 1e-6
NEG = -1e30
LOG2E = math.log2(math.e)

A_HEADS = 8
A_HEAD_DIM = 128
A_BLOCK = 256
A_TOPK = 3
T5_BUCKETS = 32
T5_MAX_DIST = 128
B_HEADS = 4
B_QK_DIM = 128
B_V_DIM = 256
B_CONV = 4
C_HEADS = 8
C_Q_RANK = 512
C_KV_RANK = 256
C_NOPE = 128
C_ROPE = 64
C_V_DIM = 128
ROPE_THETA = 10000.0
N_BRANCH = 3

A_W = A_HEADS * A_HEAD_DIM
B_QK_W = B_HEADS * B_QK_DIM
B_V_W = B_HEADS * B_V_DIM
C_W = C_HEADS * C_V_DIM

LANES = 128
SUBLANES = 8
VMEM_LIMIT_BYTES = 56 * 1024 * 1024

SPLIT_SIZES = (A_W, A_W, A_W, A_W, B_QK_W, B_QK_W, B_V_W, B_HEADS, B_HEADS, B_V_W, B_V_W,
               C_Q_RANK, C_KV_RANK, C_ROPE, C_W, N_BRANCH * D_MODEL)
(SRC_QA, SRC_KA, SRC_VA, SRC_ZA, SRC_QB, SRC_KB, SRC_VB, SRC_IB, SRC_FB, SRC_OB, SRC_ZB,
 SRC_CQ, SRC_CKV, SRC_KR, SRC_ZC, SRC_GT) = (int(v) for v in np.cumsum((0,) + SPLIT_SIZES[:-1]))
D_IN = int(sum(SPLIT_SIZES))

W_BLOCK = 1024
SLAB_SRC_ROWS = ((SRC_KA, SRC_ZA, SRC_QB, SRC_VB, SRC_OB, SRC_ZB, SRC_ZC)
                 + tuple(SRC_GT + k * W_BLOCK for k in range(N_BRANCH * D_MODEL // W_BLOCK)))
assert SRC_KB == SRC_QB + B_QK_W and all(r % SUBLANES == 0 for r in SLAB_SRC_ROWS)
OFF_KA = 0
OFF_ZA = OFF_KA + W_BLOCK
OFF_QKB = OFF_ZA + W_BLOCK
OFF_VB = OFF_QKB + W_BLOCK
OFF_OB = OFF_VB + W_BLOCK
OFF_ZB = OFF_OB + W_BLOCK
OFF_ZC = OFF_ZB + W_BLOCK
OFF_GT = OFF_ZC + W_BLOCK
D_SLAB = len(SLAB_SRC_ROWS) * W_BLOCK
SMALL_WIN1_ROWS = C_Q_RANK + C_KV_RANK + LANES
SMALL_WIN2_ROWS = LANES
assert SRC_CKV == SRC_CQ + C_Q_RANK and SRC_KR == SRC_CKV + C_KV_RANK and SRC_FB == SRC_IB + B_HEADS
assert SRC_CQ % SUBLANES == 0 and SRC_IB % SUBLANES == 0
assert SRC_CQ + SMALL_WIN1_ROWS <= D_IN and SRC_IB + SMALL_WIN2_ROWS <= D_IN
OFF_CQ = 0
OFF_CKV = OFF_CQ + C_Q_RANK
OFF_KR = OFF_CKV + C_KV_RANK
OFF_IF = OFF_KR + LANES
D_SLAB_SMALL = SMALL_WIN1_ROWS + SMALL_WIN2_ROWS
assert D_SLAB_SMALL == W_BLOCK

ATTN_TILE = 256
ATTN_QK_PAD = 2 * LANES
ATTN_HEADS_PER_STEP = 8
assert ATTN_TILE == A_BLOCK


def _cparams(n_axes):
    return pltpu.CompilerParams(dimension_semantics=("arbitrary",) * n_axes,
                                vmem_limit_bytes=VMEM_LIMIT_BYTES)


def _silu(t):
    return t * jax.nn.sigmoid(t)


_NT = (((1,), (1,)), ((), ()))


def _rmsnorm_body(x_ref, g_ref, o_ref):
    xf = x_ref[...]
    y = xf * lax.rsqrt(jnp.mean(xf * xf, axis=-1, keepdims=True) + EPS)
    o_ref[...] = (y * g_ref[...]).astype(o_ref.dtype)


def rmsnorm_rows(x2d, gain, tm=512):
    m, d = x2d.shape
    return pl.pallas_call(
        _rmsnorm_body,
        grid=(m // tm,),
        in_specs=[pl.BlockSpec((tm, d), lambda i: (i, 0)),
                  pl.BlockSpec((1, d), lambda i: (0, 0))],
        out_specs=pl.BlockSpec((tm, d), lambda i: (i, 0)),
        out_shape=jax.ShapeDtypeStruct((m, d), BF16),
        compiler_params=_cparams(1),
        name="rmsnorm",
    )(x2d, gain.reshape(1, d))


def _row_window(rows, d):
    return (pl.Squeezed(), pl.Element(rows), pl.Element(d))


def _proj_body(rows_ref, h_ref, w_ref, o_ref, wb_s):
    del rows_ref
    @pl.when(pl.program_id(1) == 0)
    def _cast_weights():
        for c in range(0, w_ref.shape[0], LANES):
            wb_s[:, c:c + LANES] = w_ref[c:c + LANES, :].T.astype(wb_s.dtype)

    o_ref[...] = jnp.dot(h_ref[...], wb_s[...], preferred_element_type=F32).astype(o_ref.dtype)


def input_projection(h2d, w_in_t, layer, src_rows, tm=2048):
    m, d = h2d.shape
    tn = W_BLOCK
    grid_spec = pltpu.PrefetchScalarGridSpec(
        num_scalar_prefetch=1,
        grid=(len(src_rows), m // tm),
        in_specs=[pl.BlockSpec((tm, d), lambda j, i, rows: (i, 0)),
                  pl.BlockSpec(_row_window(tn, d),
                               lambda j, i, rows: (layer, pl.multiple_of(rows[j], SUBLANES), 0))],
        out_specs=pl.BlockSpec((tm, tn), lambda j, i, rows: (i, j)),
        scratch_shapes=[pltpu.VMEM((d, tn), BF16)])
    return pl.pallas_call(
        _proj_body,
        grid_spec=grid_spec,
        out_shape=jax.ShapeDtypeStruct((m, len(src_rows) * tn), BF16),
        compiler_params=_cparams(2),
        name="input_proj",
    )(jnp.asarray(src_rows, jnp.int32), h2d, w_in_t)


def _proj_small_body(h_ref, w1_ref, w2_ref, o_ref, wb_s):
    @pl.when(pl.program_id(0) == 0)
    def _cast_weights():
        wb_s[:SMALL_WIN1_ROWS, :] = w1_ref[...].astype(wb_s.dtype)
        wb_s[SMALL_WIN1_ROWS:, :] = w2_ref[...].astype(wb_s.dtype)

    o_ref[...] = lax.dot_general(h_ref[...], wb_s[...], _NT, preferred_element_type=F32).astype(o_ref.dtype)


def input_projection_small(h2d, w_in_t, layer, tm=1024):
    m, d = h2d.shape
    return pl.pallas_call(
        _proj_small_body,
        grid=(m // tm,),
        in_specs=[pl.BlockSpec((tm, d), lambda i: (i, 0)),
                  pl.BlockSpec(_row_window(SMALL_WIN1_ROWS, d), lambda i: (layer, SRC_CQ, 0)),
                  pl.BlockSpec(_row_window(SMALL_WIN2_ROWS, d), lambda i: (layer, SRC_IB, 0))],
        out_specs=pl.BlockSpec((tm, D_SLAB_SMALL), lambda i: (i, 0)),
        out_shape=jax.ShapeDtypeStruct((m, D_SLAB_SMALL), BF16),
        scratch_shapes=[pltpu.VMEM((D_SLAB_SMALL, d), BF16)],
        compiler_params=_cparams(1),
        name="input_proj_small",
    )(h2d, w_in_t, w_in_t)


def _proj_t_body(h_ref, wq_ref, wv_ref, qt_ref, vt_ref, wt_s, *, n_sub):
    t = ATTN_TILE
    hd = A_HEAD_DIM
    hv = hd + V_AUG_ROWS

    @pl.when((pl.program_id(0) == 0) & (pl.program_id(1) == 0))
    def _cast_weights():
        q_scale = A_HEAD_DIM ** -0.5 * LOG2E
        wt_s[:A_W, :] = (wq_ref[...] * q_scale).astype(wt_s.dtype)
        wt_s[A_W:, :] = wv_ref[...].astype(wt_s.dtype)

    res = lax.dot_general(wt_s[...], h_ref[...], _NT, preferred_element_type=F32)
    aug = _v_aug_rows(t, vt_ref.dtype)
    for c in range(n_sub):
        qt_ref[c] = res[:A_W, c * t:(c + 1) * t].astype(qt_ref.dtype)
        for h in range(A_HEADS):
            vt_ref[c, h * hv:h * hv + hd, :] = res[A_W + h * hd:A_W + (h + 1) * hd, c * t:(c + 1) * t
                                                   ].astype(vt_ref.dtype)
            vt_ref[c, h * hv + hd:(h + 1) * hv, :] = aug


def moba_qv_projection(h2d, w_in_t, layer, bsz, tm=512):
    m, d = h2d.shape
    assert SRC_QA % A_W == 0 and SRC_VA % A_W == 0
    t = ATTN_TILE
    tm = min(tm, m // bsz)
    n_sub = tm // t
    nt = m // bsz // t
    steps = m // bsz // tm
    vt_rows = A_HEADS * (A_HEAD_DIM + V_AUG_ROWS)
    return pl.pallas_call(
        functools.partial(_proj_t_body, n_sub=n_sub),
        grid=(bsz, steps),
        in_specs=[pl.BlockSpec((tm, d), lambda b, i: (b * steps + i, 0)),
                  pl.BlockSpec((None, A_W, d), lambda b, i: (layer, SRC_QA // A_W, 0),
                               pipeline_mode=pl.Buffered(1)),
                  pl.BlockSpec((None, A_W, d), lambda b, i: (layer, SRC_VA // A_W, 0),
                               pipeline_mode=pl.Buffered(1))],
        out_specs=[pl.BlockSpec((None, n_sub, A_W, t), lambda b, i: (b, i, 0, 0)),
                   pl.BlockSpec((None, n_sub, vt_rows, t), lambda b, i: (b, i, 0, 0))],
        out_shape=[jax.ShapeDtypeStruct((bsz, nt, A_W, t), BF16),
                   jax.ShapeDtypeStruct((bsz, nt, vt_rows, t), BF16)],
        scratch_shapes=[pltpu.VMEM((2 * A_W, d), BF16)],
        compiler_params=_cparams(2),
        name="moba_qv_proj",
    )(h2d, w_in_t, w_in_t)


V_AUG_ROWS = 16


def _v_aug_rows(t, dtype):
    return jnp.where(lax.broadcasted_iota(jnp.int32, (V_AUG_ROWS, t), 0) == 0, 1.0, 0.0).astype(dtype)


def _softmax_init(m_s, acc_s):
    m_s[...] = jnp.full(m_s.shape, -jnp.inf, F32)
    acc_s[...] = jnp.zeros(acc_s.shape, F32)


def _softmax_step_t(s, vt_blk, g, m_s, acc_s):
    m_prev = m_s[g]
    m_new = jnp.maximum(m_prev, jnp.max(s, axis=0, keepdims=True))
    alpha = jnp.exp2(m_prev - m_new)
    p = jnp.exp2(s - m_new)
    acc_s[g] = alpha * acc_s[g] + jnp.dot(vt_blk, p.astype(BF16), preferred_element_type=F32)
    m_s[g] = m_new


def _attn_scratch(g_heads, t, dv):
    state = [pltpu.VMEM((g_heads, 1, t), F32), pltpu.VMEM((g_heads, dv + V_AUG_ROWS, t), F32)]
    return state + [pltpu.VMEM((2, t, t), F32)] * g_heads


def _skewed_blocks(lead, n_uniform, uniform_scores, consume, finish):
    lead[0][0](0)
    pend_blk, pend_slot = lead[0][1], 0
    for fn, blk in lead[1:]:
        fn(1 - pend_slot)
        consume(pend_blk, pend_slot)
        pend_blk, pend_slot = blk, 1 - pend_slot
    a, b = pend_slot, 1 - pend_slot

    def pair(first, pend):
        uniform_scores(first, b)
        consume(pend, a)
        uniform_scores(first + 1, a)
        consume(first, b)
        return first + 1

    def two_pairs(jj, pend):
        return pair(4 * jj + 2, pair(4 * jj, pend))

    n_quads = n_uniform // 4
    pend = lax.fori_loop(0, n_quads, two_pairs, pend_blk)
    rest = n_uniform - 4 * n_quads
    has_pair = rest >= 2
    first = 4 * n_quads

    @pl.when(has_pair)
    def _last_pair():
        pair(first, pend)

    pend = jnp.where(has_pair, first + 1, pend)
    last = jnp.where(has_pair, first + 2, first)

    @pl.when(rest % 2 == 1)
    def _odd_tail():
        uniform_scores(last, b)
        consume(pend, a)
        consume(last, b)
        finish()

    @pl.when(rest % 2 == 0)
    def _even_tail():
        consume(pend, a)
        finish()


def _attn_finish(z_ref, o_ref, acc_s, g_heads, dv):
    for g in range(g_heads):
        out_t = acc_s[g, :dv, :] / acc_s[g, dv:dv + 1, :]
        o_ref[:, g * dv:(g + 1) * dv] = (out_t.T * _silu(z_ref[:, g * dv:(g + 1) * dv].astype(F32))
                                         ).astype(o_ref.dtype)


def _t5_thresholds():
    max_exact = T5_BUCKETS // 2
    d = np.arange(0, 4 * T5_MAX_DIST, dtype=np.int64)
    dd = np.maximum(d, 1).astype(np.float32)
    large = max_exact + (np.log(dd / np.float32(max_exact)) / np.float32(math.log(T5_MAX_DIST / max_exact))
                         * np.float32(T5_BUCKETS - max_exact)).astype(np.int32)
    large = np.minimum(large, T5_BUCKETS - 1)
    bucket = np.where(d < max_exact, d, large)
    assert np.all(np.diff(bucket) >= 0) and bucket[-1] == T5_BUCKETS - 1
    return [int(np.argmax(bucket >= k)) for k in range(T5_BUCKETS)]


_T5_THRESH = _t5_thresholds()

MASK_BIG = 1e30


def _moba_body(t5_ref, qt_ref, k_ref, vt_ref, z_ref, o_ref,
               kaug_s, km_s, bias_own_s, bias_prev_s, qa_s, m_s, acc_s, *s_refs, n_blocks, g_heads):
    hg = pl.program_id(0)
    b = pl.program_id(1)
    i = pl.program_id(2)
    t, hd = A_BLOCK, A_HEAD_DIM
    hv = hd + V_AUG_ROWS
    seq = n_blocks * t
    nb_pad = -(-n_blocks // SUBLANES) * SUBLANES

    @pl.when((hg == 0) & (b == 0) & (i == 0))
    def _mask_columns():
        key_blk = lax.broadcasted_iota(jnp.int32, (seq, LANES), 0) // t
        lane = lax.broadcasted_iota(jnp.int32, (seq, LANES), 1)
        pattern = jnp.where(lane == key_blk, -MASK_BIG, 0.0).astype(kaug_s.dtype)
        for g in range(g_heads):
            kaug_s[g, :, hd:] = pattern

    @pl.when((b == 0) & (i == 0))
    def _build_bias():
        key = lax.broadcasted_iota(jnp.int32, (t, t), 0)
        qry = lax.broadcasted_iota(jnp.int32, (t, t), 1)
        for g in range(g_heads):
            head = hg * g_heads + g
            far = t5_ref[T5_BUCKETS - 1, head]
            for dist, dst in ((qry - key, bias_own_s), (t + qry - key, bias_prev_s)):
                bias = jnp.full((t, t), t5_ref[0, head], F32)
                for kk in range(1, T5_BUCKETS):
                    bias = jnp.where(dist >= _T5_THRESH[kk], t5_ref[kk, head], bias)
                dst[g] = (bias - far) * LOG2E

    @pl.when(i == 0)
    def _new_sequence():
        for g in range(g_heads):
            kaug_s[g, :, :hd] = k_ref[:, g * hd:(g + 1) * hd]
            km_s[g] = jnp.zeros(km_s.shape[1:], km_s.dtype)
            for nb in range(n_blocks):
                mean = jnp.mean(k_ref[nb * t:(nb + 1) * t, g * hd:(g + 1) * hd].astype(F32), axis=0, keepdims=True)
                hi = mean.astype(BF16)
                km_s[g, nb:nb + 1, :] = hi
                km_s[g, LANES + nb:LANES + nb + 1, :] = (mean - hi.astype(F32)).astype(BF16)

    blk_row = lax.broadcasted_iota(jnp.int32, (nb_pad, t), 0)
    past = blk_row < i
    for g in range(g_heads):
        qt = qt_ref[g * hd:(g + 1) * hd, :]
        g2 = jnp.dot(km_s[g], qt, preferred_element_type=F32)
        gate = g2[:nb_pad, :] + g2[LANES:LANES + nb_pad, :]
        cur = jnp.where(past, gate, NEG)
        keep = blk_row == i
        for _ in range(A_TOPK):
            mx = jnp.max(cur, axis=0, keepdims=True)
            idx = jnp.min(jnp.where(cur == mx, blk_row, nb_pad), axis=0, keepdims=True)
            pick = blk_row == idx
            keep = keep | (pick & past)
            cur = jnp.where(pick, -jnp.inf, cur)
        unsel = jnp.where(keep, 0.0, 1.0)
        qa_s[g, :hd, :] = qt
        qa_s[g, hd:, :] = jnp.concatenate([unsel, jnp.zeros((LANES - nb_pad, t), F32)], axis=0).astype(qa_s.dtype)

    _softmax_init(m_s, acc_s)

    def raw_scores(g, j):
        start = pl.multiple_of(j * t, t)
        return jnp.dot(kaug_s[g, pl.ds(start, t), :], qa_s[g], preferred_element_type=F32)

    def own_scores(slot):
        key = lax.broadcasted_iota(jnp.int32, (t, t), 0)
        qry = lax.broadcasted_iota(jnp.int32, (t, t), 1)
        for g in range(g_heads):
            s_refs[g][slot] = jnp.where(key <= qry, raw_scores(g, i) + bias_own_s[g], NEG)

    j_prev = jnp.maximum(i - 1, 0)
    no_prev = jnp.where(i == 0, -MASK_BIG, 0.0)

    def prev_scores(slot):
        for g in range(g_heads):
            s_refs[g][slot] = raw_scores(g, j_prev) + (bias_prev_s[g] + no_prev)

    def far_scores(j, slot):
        for g in range(g_heads):
            s_refs[g][slot] = raw_scores(g, j)

    def consume(j, slot):
        for g in range(g_heads):
            _softmax_step_t(s_refs[g][slot], vt_ref[j, g * hv:(g + 1) * hv, :], g, m_s, acc_s)

    _skewed_blocks([(own_scores, i), (prev_scores, j_prev)], j_prev, far_scores, consume,
                   functools.partial(_attn_finish, z_ref, o_ref, acc_s, g_heads, hd))


def moba_branch(qt, vt, slab, t5_table, g_heads=ATTN_HEADS_PER_STEP):
    bsz, nb, _, t = qt.shape
    s = nb * t
    hd = A_HEAD_DIM
    gw = g_heads * hd
    assert t == A_BLOCK and nb <= LANES and A_HEADS % g_heads == 0
    assert OFF_KA % gw == 0 and OFF_ZA % gw == 0
    assert _T5_THRESH[T5_BUCKETS - 1] <= t + 1
    ck, cz = OFF_KA // gw, OFF_ZA // gw
    return pl.pallas_call(
        functools.partial(_moba_body, n_blocks=nb, g_heads=g_heads),
        grid=(A_HEADS // g_heads, bsz, nb),
        in_specs=[pl.BlockSpec(memory_space=pltpu.SMEM),
                  pl.BlockSpec((None, None, gw, t), lambda h, b, i: (b, i, h, 0)),
                  pl.BlockSpec((None, s, gw), lambda h, b, i: (b, 0, ck + h), pipeline_mode=pl.Buffered(1)),
                  pl.BlockSpec((None, nb, g_heads * (hd + V_AUG_ROWS), t), lambda h, b, i: (b, 0, h, 0),
                               pipeline_mode=pl.Buffered(1)),
                  pl.BlockSpec((None, t, gw), lambda h, b, i: (b, i, cz + h))],
        out_specs=pl.BlockSpec((None, t, gw), lambda h, b, i: (b, i, h)),
        out_shape=jax.ShapeDtypeStruct((bsz, s, A_W), BF16),
        scratch_shapes=[pltpu.VMEM((g_heads, s, ATTN_QK_PAD), BF16),
                        pltpu.VMEM((g_heads, 2 * LANES, hd), BF16),
                        pltpu.VMEM((g_heads, t, t), F32),
                        pltpu.VMEM((g_heads, t, t), F32),
                        pltpu.VMEM((g_heads, ATTN_QK_PAD, t), BF16)] + _attn_scratch(g_heads, t, hd),
        compiler_params=_cparams(3),
        name="moba",
    )(t5_table, qt, slab, vt, slab)


MLSTM_CHUNK = 256
CONV_HALO = 8


def _log_sigmoid(t):
    return jnp.minimum(t, 0.0) - jnp.log1p(jnp.exp(-jnp.abs(t)))


def _mlstm_body(qk_ref, v_ref, ob_ref, zb_ref, if_ref, cw_ref, cb_ref, gb_ref, gn_ref, o_ref,
                xe_s, c_s, n_s, m_s):
    c = pl.program_id(1)
    L = MLSTM_CHUNK
    dk, dv = B_QK_DIM, B_V_DIM

    @pl.when(c == 0)
    def _reset():
        xe_s[...] = jnp.zeros_like(xe_s)
        c_s[...] = jnp.zeros_like(c_s)
        n_s[...] = jnp.zeros_like(n_s)
        m_s[...] = jnp.zeros_like(m_s)

    x = qk_ref[...].astype(F32)
    xe = jnp.concatenate([xe_s[...], x], axis=0)
    conv = cb_ref[...] + cw_ref[B_CONV - 1:B_CONV, :] * x
    for j in range(B_CONV - 1):
        back = B_CONV - 1 - j
        conv = conv + cw_ref[j:j + 1, :] * pltpu.roll(xe, back, axis=0)[CONV_HALO:, :]
    xe_s[...] = x[L - CONV_HALO:, :]
    qk = _silu(conv)

    gi = if_ref[...].astype(F32) + gb_ref[...]
    lf = _log_sigmoid(gi)
    row = lax.broadcasted_iota(jnp.int32, (L, L), 0)
    col = lax.broadcasted_iota(jnp.int32, (L, L), 1)
    causal = col <= row
    tri = jnp.where(causal, 1.0, 0.0).astype(F32)
    b_cols = jnp.dot(tri, lf, precision=lax.Precision.HIGHEST, preferred_element_type=F32)
    b_rows = b_cols.T
    li_rows = gi.T

    for h in range(B_HEADS):
        q_h = qk[:, h * dk:(h + 1) * dk]
        k_h = qk[:, B_QK_W + h * dk:B_QK_W + (h + 1) * dk] * (dk ** -0.5)
        v_h = v_ref[:, h * dv:(h + 1) * dv]
        b_c = b_cols[:, B_HEADS + h:B_HEADS + h + 1]
        b_r = b_rows[B_HEADS + h:B_HEADS + h + 1, :]
        li_c = gi[:, h:h + 1]
        li_r = li_rows[h:h + 1, :]
        m_prev = m_s[h:h + 1, 0:1]
        c_prev = c_s[h]
        n_prev = n_s[h:h + 1, :]

        a_c = b_c + m_prev
        dmat = jnp.where(causal, b_c - b_r + li_r, -jnp.inf)
        m_t = jnp.maximum(a_c, jnp.max(dmat, axis=-1, keepdims=True))
        w_inter = jnp.exp(a_c - m_t)
        q_b = q_h.astype(BF16)
        k_b = k_h.astype(BF16)
        sc = (lax.dot_general(q_b, k_b, _NT, preferred_element_type=F32)
              * jnp.exp(dmat - m_t))
        num = (w_inter * jnp.dot(q_b, c_prev.astype(BF16), preferred_element_type=F32)
               + jnp.dot(sc.astype(BF16), v_h, preferred_element_type=F32))
        den = (w_inter * jnp.sum(q_h * n_prev, axis=-1, keepdims=True)
               + jnp.sum(sc, axis=-1, keepdims=True))
        hh = num / jnp.maximum(jnp.abs(den), jnp.exp(-m_t))

        b_last = b_c[L - 1:L, :]
        g_c = b_last - b_c + li_c
        m_new = jnp.maximum(b_last + m_prev, jnp.max(g_c, axis=0, keepdims=True))
        decay = jnp.exp(b_last + m_prev - m_new)
        wk = jnp.exp(g_c - m_new) * k_h
        c_s[h] = decay * c_prev + lax.dot_general(wk.astype(BF16), v_h, (((0,), (0,)), ((), ())),
                                                  preferred_element_type=F32)
        n_s[h:h + 1, :] = decay * n_prev + jnp.sum(wk, axis=0, keepdims=True)
        m_s[h:h + 1, :] = jnp.broadcast_to(m_new, (1, LANES))

        mu = jnp.mean(hh, axis=-1, keepdims=True)
        dlt = hh - mu
        var = jnp.mean(dlt * dlt, axis=-1, keepdims=True)
        y = dlt * lax.rsqrt(var + EPS) * gn_ref[:, h * dv:(h + 1) * dv]
        y = (y * jax.nn.sigmoid(ob_ref[:, h * dv:(h + 1) * dv].astype(F32))
             * _silu(zb_ref[:, h * dv:(h + 1) * dv].astype(F32)))
        o_ref[:, h * dv:(h + 1) * dv] = y.astype(o_ref.dtype)


def mlstm_branch(slab, slab_small, conv_w, conv_b, i_bias, f_bias, out_norm):
    bsz, s, _ = slab.shape
    L = MLSTM_CHUNK
    assert s % L == 0
    gate_bias = jnp.zeros((1, LANES), F32).at[0, :B_HEADS].set(i_bias).at[0, B_HEADS:2 * B_HEADS].set(f_bias)
    w2 = 2 * B_QK_W
    full = lambda shape: pl.BlockSpec(shape, lambda b, c: (0,) * len(shape))
    return pl.pallas_call(
        _mlstm_body,
        grid=(bsz, s // L),
        in_specs=[pl.BlockSpec((None, L, w2), lambda b, c: (b, c, OFF_QKB // w2)),
                  pl.BlockSpec((None, L, B_V_W), lambda b, c: (b, c, OFF_VB // B_V_W)),
                  pl.BlockSpec((None, L, B_V_W), lambda b, c: (b, c, OFF_OB // B_V_W)),
                  pl.BlockSpec((None, L, B_V_W), lambda b, c: (b, c, OFF_ZB // B_V_W)),
                  pl.BlockSpec((None, L, LANES), lambda b, c: (b, c, OFF_IF // LANES)),
                  full((B_CONV, w2)), full((1, w2)), full((1, LANES)), full((1, B_V_W))],
        out_specs=pl.BlockSpec((None, L, B_V_W), lambda b, c: (b, c, 0)),
        out_shape=jax.ShapeDtypeStruct((bsz, s, B_V_W), BF16),
        scratch_shapes=[pltpu.VMEM((CONV_HALO, w2), F32),
                        pltpu.VMEM((B_HEADS, B_QK_DIM, B_V_DIM), F32),
                        pltpu.VMEM((8, B_QK_DIM), F32),
                        pltpu.VMEM((8, LANES), F32)],
        compiler_params=_cparams(2),
        name="mlstm",
    )(slab, slab, slab, slab, slab_small, conv_w, conv_b.reshape(1, w2), gate_bias,
      out_norm.reshape(1, B_V_W))


def _mla_prep_body(cq_ref, ckv_ref, kr_ref, qg_ref, kg_ref, wqt_ref, wqst_ref, wkn_ref, wvt_ref,
                   cos_ref, sin_ref, cost_ref, sint_ref, qt_ref, kf_ref, vt_ref):
    def normed(ref, g_ref):
        t = ref[...].astype(F32)
        y = t * lax.rsqrt(jnp.mean(t * t, axis=-1, keepdims=True) + EPS)
        return (y * g_ref[...]).astype(BF16)

    cqn = normed(cq_ref, qg_ref)
    ckvn = normed(ckv_ref, kg_ref)

    qt_main = lax.dot_general(wqt_ref[...], cqn, _NT, preferred_element_type=F32)
    qt_swap = lax.dot_general(wqst_ref[...], cqn, _NT, preferred_element_type=F32)
    cos_t = cost_ref[...]
    sin_t = sint_ref[...]
    t = ATTN_TILE
    n_sub = cos_t.shape[1] // t
    for h in range(C_HEADS):
        lo = h * ATTN_QK_PAD
        nope = qt_main[lo:lo + LANES, :].astype(qt_ref.dtype)
        rope = (qt_main[lo + LANES:lo + 2 * LANES, :] * cos_t
                + qt_swap[h * LANES:(h + 1) * LANES, :] * sin_t).astype(qt_ref.dtype)
        for c in range(n_sub):
            qt_ref[c, lo:lo + LANES, :] = nope[:, c * t:(c + 1) * t]
            qt_ref[c, lo + LANES:lo + 2 * LANES, :] = rope[:, c * t:(c + 1) * t]

    k_nope = jnp.dot(ckvn, wkn_ref[...], preferred_element_type=F32)
    kr = kr_ref[...].astype(F32)
    half = C_ROPE // 2
    lane = lax.broadcasted_iota(jnp.int32, kr.shape, 1)
    swapped = jnp.where(lane < half, -pltpu.roll(kr, LANES - half, axis=1), pltpu.roll(kr, half, axis=1))
    k_rot = (kr * cos_ref[...] + swapped * sin_ref[...]).astype(kf_ref.dtype)
    for h in range(C_HEADS):
        lo = h * ATTN_QK_PAD
        kf_ref[:, lo:lo + LANES] = k_nope[:, h * LANES:(h + 1) * LANES].astype(kf_ref.dtype)
        kf_ref[:, lo + LANES:lo + 2 * LANES] = k_rot

    v_t = lax.dot_general(wvt_ref[...], ckvn, _NT, preferred_element_type=F32)
    hv = C_V_DIM + V_AUG_ROWS
    aug = _v_aug_rows(t, vt_ref.dtype)
    for h in range(C_HEADS):
        v_h = v_t[h * C_V_DIM:(h + 1) * C_V_DIM, :].astype(vt_ref.dtype)
        for c in range(n_sub):
            vt_ref[c, h * hv:h * hv + C_V_DIM, :] = v_h[:, c * t:(c + 1) * t]
            vt_ref[c, h * hv + C_V_DIM:(h + 1) * hv, :] = aug


def mla_prep(slab, q_norm, kv_norm, w_uq, w_ukv, tm=512):
    bsz, s, _ = slab.shape
    t = ATTN_TILE
    n_sub = tm // t
    vt_rows = C_HEADS * (C_V_DIM + V_AUG_ROWS)
    half = C_ROPE // 2
    scale = (C_NOPE + C_ROPE) ** -0.5 * LOG2E
    wq = (w_uq * scale).reshape(C_Q_RANK, C_HEADS, C_NOPE + C_ROPE)
    pad = jnp.zeros((C_Q_RANK, C_HEADS, LANES - C_ROPE), F32)
    x1, x2 = wq[..., C_NOPE:C_NOPE + half], wq[..., C_NOPE + half:]
    wqt = jnp.concatenate([wq, pad], axis=-1).reshape(C_Q_RANK, C_HEADS * ATTN_QK_PAD).T.astype(BF16)
    wqst = jnp.concatenate([-x2, x1, pad], axis=-1).reshape(C_Q_RANK, C_HEADS * LANES).T.astype(BF16)
    wkv = w_ukv.reshape(C_KV_RANK, C_HEADS, C_NOPE + C_V_DIM)
    wkn = wkv[..., :C_NOPE].reshape(C_KV_RANK, C_HEADS * C_NOPE).astype(BF16)
    wvt = wkv[..., C_NOPE:].reshape(C_KV_RANK, C_W).T.astype(BF16)

    pos = jnp.arange(s, dtype=jnp.int32)
    inv = ROPE_THETA ** (-jnp.arange(half, dtype=F32) / half)
    ang = pos.astype(F32)[:, None] * inv[None, :]
    zpad = jnp.zeros((s, LANES - C_ROPE), F32)
    cos_tab = jnp.concatenate([jnp.cos(ang), jnp.cos(ang), zpad], axis=-1)
    sin_tab = jnp.concatenate([jnp.sin(ang), jnp.sin(ang), zpad], axis=-1)

    nt = s // t
    full = lambda shape: pl.BlockSpec(shape, lambda b, i: (0,) * len(shape))
    return pl.pallas_call(
        _mla_prep_body,
        grid=(bsz, s // tm),
        in_specs=[pl.BlockSpec((None, tm, C_Q_RANK), lambda b, i: (b, i, OFF_CQ // C_Q_RANK)),
                  pl.BlockSpec((None, tm, C_KV_RANK), lambda b, i: (b, i, OFF_CKV // C_KV_RANK)),
                  pl.BlockSpec((None, tm, LANES), lambda b, i: (b, i, OFF_KR // LANES)),
                  full((1, C_Q_RANK)), full((1, C_KV_RANK)),
                  full(wqt.shape), full(wqst.shape), full(wkn.shape), full(wvt.shape),
                  pl.BlockSpec((tm, LANES), lambda b, i: (i, 0)),
                  pl.BlockSpec((tm, LANES), lambda b, i: (i, 0)),
                  pl.BlockSpec((LANES, tm), lambda b, i: (0, i)),
                  pl.BlockSpec((LANES, tm), lambda b, i: (0, i))],
        out_specs=[pl.BlockSpec((None, n_sub, C_HEADS * ATTN_QK_PAD, t), lambda b, i: (b, i, 0, 0)),
                   pl.BlockSpec((None, tm, C_HEADS * ATTN_QK_PAD), lambda b, i: (b, i, 0)),
                   pl.BlockSpec((None, n_sub, vt_rows, t), lambda b, i: (b, i, 0, 0))],
        out_shape=[jax.ShapeDtypeStruct((bsz, nt, C_HEADS * ATTN_QK_PAD, t), BF16),
                   jax.ShapeDtypeStruct((bsz, s, C_HEADS * ATTN_QK_PAD), BF16),
                   jax.ShapeDtypeStruct((bsz, nt, vt_rows, t), BF16)],
        compiler_params=_cparams(2),
        name="mla_prep",
    )(slab, slab, slab, q_norm.reshape(1, C_Q_RANK), kv_norm.reshape(1, C_KV_RANK),
      wqt, wqst, wkn, wvt, cos_tab, sin_tab, cos_tab.T, sin_tab.T)


def _mla_attn_body(qt_ref, k_ref, vt_ref, z_ref, o_ref, m_s, acc_s, *s_refs, g_heads):
    i = pl.program_id(2)
    t = ATTN_TILE
    dq, dv = ATTN_QK_PAD, C_V_DIM
    hv = dv + V_AUG_ROWS
    _softmax_init(m_s, acc_s)

    def raw_scores(g, j):
        start = pl.multiple_of(j * t, t)
        return jnp.dot(k_ref[pl.ds(start, t), g * dq:(g + 1) * dq], qt_ref[g * dq:(g + 1) * dq, :],
                       preferred_element_type=F32)

    def diagonal_scores(slot):
        key = lax.broadcasted_iota(jnp.int32, (t, t), 0)
        qry = lax.broadcasted_iota(jnp.int32, (t, t), 1)
        for g in range(g_heads):
            s_refs[g][slot] = jnp.where(key <= qry, raw_scores(g, i), NEG)

    def past_scores(j, slot):
        for g in range(g_heads):
            s_refs[g][slot] = raw_scores(g, j)

    def consume(j, slot):
        for g in range(g_heads):
            _softmax_step_t(s_refs[g][slot], vt_ref[j, g * hv:(g + 1) * hv, :], g, m_s, acc_s)

    _skewed_blocks([(diagonal_scores, i)], i, past_scores, consume,
                   functools.partial(_attn_finish, z_ref, o_ref, acc_s, g_heads, dv))


def mla_attention(qt, kf, vt, slab, g_heads=ATTN_HEADS_PER_STEP):
    bsz, nt, _, t = qt.shape
    s = nt * t
    dq, dv = ATTN_QK_PAD, C_V_DIM
    assert t == ATTN_TILE and C_HEADS % g_heads == 0 and OFF_ZC % (g_heads * dv) == 0
    cz = OFF_ZC // (g_heads * dv)
    return pl.pallas_call(
        functools.partial(_mla_attn_body, g_heads=g_heads),
        grid=(bsz, C_HEADS // g_heads, nt),
        in_specs=[pl.BlockSpec((None, None, g_heads * dq, t), lambda b, h, i: (b, i, h, 0)),
                  pl.BlockSpec((None, s, g_heads * dq), lambda b, h, i: (b, 0, h),
                               pipeline_mode=pl.Buffered(1)),
                  pl.BlockSpec((None, nt, g_heads * (dv + V_AUG_ROWS), t), lambda b, h, i: (b, 0, h, 0),
                               pipeline_mode=pl.Buffered(1)),
                  pl.BlockSpec((None, t, g_heads * dv), lambda b, h, i: (b, i, cz + h))],
        out_specs=pl.BlockSpec((None, t, g_heads * dv), lambda b, h, i: (b, i, h)),
        out_shape=jax.ShapeDtypeStruct((bsz, s, C_W), BF16),
        scratch_shapes=_attn_scratch(g_heads, t, dv),
        compiler_params=_cparams(3),
        name="mla_attn",
    )(qt, kf, vt, slab)


def _merge_body(ya_ref, yb_ref, yc_ref, wa_ref, wb_ref, wc_ref, ga_ref, gb_ref, gc_ref, o_ref, w_s):
    @pl.when(pl.program_id(1) == 0)
    def _cast_weights():
        for n, w_ref in enumerate((wa_ref, wb_ref, wc_ref)):
            w_s[n] = w_ref[...].astype(w_s.dtype)

    acc = None
    for n, (y_ref, g_ref) in enumerate(((ya_ref, ga_ref), (yb_ref, gb_ref), (yc_ref, gc_ref))):
        term = (jax.nn.sigmoid(g_ref[...].astype(F32))
                * jnp.dot(y_ref[...], w_s[n], preferred_element_type=F32))
        acc = term if acc is None else acc + term
    o_ref[...] = acc.astype(o_ref.dtype)


def branch_merge(ya, yb, yc, w_branch, layer, slab_b, tm=512, tn=1024):
    m, w = ya.shape
    d = w_branch.shape[-1]
    assert OFF_GT % tn == 0
    g0 = OFF_GT // tn
    gper = d // tn
    y_spec = pl.BlockSpec((tm, w), lambda j, i: (i, 0))
    w_specs = [pl.BlockSpec((None, None, w, tn), functools.partial(lambda j, i, n: (layer, n, 0, j), n=n))
               for n in range(N_BRANCH)]
    g_specs = [pl.BlockSpec((tm, tn), functools.partial(lambda j, i, n: (i, g0 + n * gper + j), n=n))
               for n in range(N_BRANCH)]
    return pl.pallas_call(
        _merge_body,
        grid=(d // tn, m // tm),
        in_specs=[y_spec, y_spec, y_spec] + w_specs + g_specs,
        out_specs=pl.BlockSpec((tm, tn), lambda j, i: (i, j)),
        out_shape=jax.ShapeDtypeStruct((m, d), BF16),
        scratch_shapes=[pltpu.VMEM((N_BRANCH, w, tn), BF16)],
        compiler_params=_cparams(2),
        name="branch_merge",
    )(ya, yb, yc, w_branch, w_branch, w_branch, slab_b, slab_b, slab_b)


def _out_body(mg_ref, w_ref, x_ref, g_ref, *refs, last):
    out_refs, w_s = refs[:-1], refs[-1]

    @pl.when(pl.program_id(0) == 0)
    def _cast_weights():
        w_s[...] = w_ref[...].astype(w_s.dtype)

    x_new = x_ref[...] + jnp.dot(mg_ref[...], w_s[...], preferred_element_type=F32)
    y = x_new * lax.rsqrt(jnp.mean(x_new * x_new, axis=-1, keepdims=True) + EPS) * g_ref[...]
    if last:
        out_refs[0][...] = y
    else:
        out_refs[0][...] = x_new
        out_refs[1][...] = y.astype(out_refs[1].dtype)


def out_projection(merged, w_out, layer, x2d, gain, last, tm=512):
    m, d = x2d.shape
    row = pl.BlockSpec((tm, d), lambda i: (i, 0))
    if last:
        out_specs, out_shape = row, jax.ShapeDtypeStruct((m, d), F32)
    else:
        out_specs = [row, row]
        out_shape = [jax.ShapeDtypeStruct((m, d), F32), jax.ShapeDtypeStruct((m, d), BF16)]
    return pl.pallas_call(
        functools.partial(_out_body, last=last),
        grid=(m // tm,),
        in_specs=[row,
                  pl.BlockSpec((None, d, d), lambda i: (layer, 0, 0), pipeline_mode=pl.Buffered(1)),
                  row, pl.BlockSpec((1, d), lambda i: (0, 0))],
        out_specs=out_specs,
        out_shape=out_shape,
        scratch_shapes=[pltpu.VMEM((d, d), BF16)],
        compiler_params=_cparams(1),
        name="out_proj",
    )(merged, w_out, x2d, gain.reshape(1, d))


def kernel(x, norm_gain, w_in, t5_table, mlstm_conv_w, mlstm_conv_b, mlstm_i_bias, mlstm_f_bias,
           mlstm_out_norm, mla_q_norm, mla_kv_norm, mla_w_uq, mla_w_ukv, w_branch, w_out, final_norm):
    bsz, s, d = x.shape
    m = bsz * s
    x2d = x.reshape(m, d)
    h = rmsnorm_rows(x2d, norm_gain[0])
    w_in_t = jnp.swapaxes(w_in, 1, 2)
    out = None
    for l in range(DEPTH):
        slab2d = input_projection(h, w_in_t, l, SLAB_SRC_ROWS)
        slab = slab2d.reshape(bsz, s, D_SLAB)
        slab_small = input_projection_small(h, w_in_t, l).reshape(bsz, s, D_SLAB_SMALL)
        qt_a, vt_a = moba_qv_projection(h, w_in_t, l, bsz)
        ya = moba_branch(qt_a, vt_a, slab, t5_table)
        yb = mlstm_branch(slab, slab_small, mlstm_conv_w[l], mlstm_conv_b[l], mlstm_i_bias[l], mlstm_f_bias[l],
                          mlstm_out_norm[l])
        qt_c, k_c, vt_c = mla_prep(slab_small, mla_q_norm[l], mla_kv_norm[l], mla_w_uq[l], mla_w_ukv[l])
        yc = mla_attention(qt_c, k_c, vt_c, slab)
        merged = branch_merge(ya.reshape(m, A_W), yb.reshape(m, B_V_W), yc.reshape(m, C_W),
                              w_branch, l, slab2d)
        last = l == DEPTH - 1
        gain = final_norm if last else norm_gain[l + 1]
        res = out_projection(merged, w_out, l, x2d, gain, last)
        if last:
            out = res
        else:
            x2d, h = res
    return out.reshape(bsz, s, d)
```

```python
import functools
import math

import jax
import jax.numpy as jnp
import numpy as np
from jax import lax
from jax.experimental import pallas as pl
from jax.experimental.pallas import tpu as pltpu

F32 = jnp.float32
BF16 = jnp.bfloat16

D_MODEL = 2048
DEPTH = 2
EPS = 1e-6
NEG = -1e30
LOG2E = math.log2(math.e)

A_HEADS = 8
A_HEAD_DIM = 128
A_BLOCK = 256
A_TOPK = 3
T5_BUCKETS = 32
T5_MAX_DIST = 128
B_HEADS = 4
B_QK_DIM = 128
B_V_DIM = 256
B_CONV = 4
C_HEADS = 8
C_Q_RANK = 512
C_KV_RANK = 256
C_NOPE = 128
C_ROPE = 64
C_V_DIM = 128
ROPE_THETA = 10000.0
N_BRANCH = 3

A_W = A_HEADS * A_HEAD_DIM
B_QK_W = B_HEADS * B_QK_DIM
B_V_W = B_HEADS * B_V_DIM
C_W = C_HEADS * C_V_DIM

LANES = 128
SUBLANES = 8
VMEM_LIMIT_BYTES = 56 * 1024 * 1024

SPLIT_SIZES = (A_W, A_W, A_W, A_W, B_QK_W, B_QK_W, B_V_W, B_HEADS, B_HEADS, B_V_W, B_V_W,
               C_Q_RANK, C_KV_RANK, C_ROPE, C_W, N_BRANCH * D_MODEL)
(SRC_QA, SRC_KA, SRC_VA, SRC_ZA, SRC_QB, SRC_KB, SRC_VB, SRC_IB, SRC_FB, SRC_OB, SRC_ZB,
 SRC_CQ, SRC_CKV, SRC_KR, SRC_ZC, SRC_GT) = (int(v) for v in np.cumsum((0,) + SPLIT_SIZES[:-1]))
D_IN = int(sum(SPLIT_SIZES))

W_BLOCK = 1024
SLAB_SRC_ROWS = ((SRC_KA, SRC_ZA, SRC_QB, SRC_VB, SRC_OB, SRC_ZB, SRC_ZC)
                 + tuple(SRC_GT + k * W_BLOCK for k in range(N_BRANCH * D_MODEL // W_BLOCK)))
assert SRC_KB == SRC_QB + B_QK_W and all(r % SUBLANES == 0 for r in SLAB_SRC_ROWS)
OFF_KA = 0
OFF_ZA = OFF_KA + W_BLOCK
OFF_QKB = OFF_ZA + W_BLOCK
OFF_VB = OFF_QKB + W_BLOCK
OFF_OB = OFF_VB + W_BLOCK
OFF_ZB = OFF_OB + W_BLOCK
OFF_ZC = OFF_ZB + W_BLOCK
OFF_GT = OFF_ZC + W_BLOCK
D_SLAB = len(SLAB_SRC_ROWS) * W_BLOCK
SMALL_WIN1_ROWS = C_Q_RANK + C_KV_RANK + LANES
SMALL_WIN2_ROWS = LANES
assert SRC_CKV == SRC_CQ + C_Q_RANK and SRC_KR == SRC_CKV + C_KV_RANK and SRC_FB == SRC_IB + B_HEADS
assert SRC_CQ % SUBLANES == 0 and SRC_IB % SUBLANES == 0
assert SRC_CQ + SMALL_WIN1_ROWS <= D_IN and SRC_IB + SMALL_WIN2_ROWS <= D_IN
OFF_CQ = 0
OFF_CKV = OFF_CQ + C_Q_RANK
OFF_KR = OFF_CKV + C_KV_RANK
OFF_IF = OFF_KR + LANES
D_SLAB_SMALL = SMALL_WIN1_ROWS + SMALL_WIN2_ROWS
assert D_SLAB_SMALL == W_BLOCK

ATTN_TILE = 256
ATTN_QK_PAD = 2 * LANES
ATTN_HEADS_PER_STEP = 8
assert ATTN_TILE == A_BLOCK


def _cparams(n_axes):
    return pltpu.CompilerParams(dimension_semantics=("arbitrary",) * n_axes,
                                vmem_limit_bytes=VMEM_LIMIT_BYTES)


def _silu(t):
    return t * jax.nn.sigmoid(t)


_NT = (((1,), (1,)), ((), ()))


def _rmsnorm_body(x_ref, g_ref, o_ref):
    xf = x_ref[...]
    y = xf * lax.rsqrt(jnp.mean(xf * xf, axis=-1, keepdims=True) + EPS)
    o_ref[...] = (y * g_ref[...]).astype(o_ref.dtype)


def rmsnorm_rows(x2d, gain, tm=512):
    m, d = x2d.shape
    return pl.pallas_call(
        _rmsnorm_body,
        grid=(m // tm,),
        in_specs=[pl.BlockSpec((tm, d), lambda i: (i, 0)),
                  pl.BlockSpec((1, d), lambda i: (0, 0))],
        out_specs=pl.BlockSpec((tm, d), lambda i: (i, 0)),
        out_shape=jax.ShapeDtypeStruct((m, d), BF16),
        compiler_params=_cparams(1),
        name="rmsnorm",
    )(x2d, gain.reshape(1, d))


def _row_window(rows, d):
    return (pl.Squeezed(), pl.Element(rows), pl.Element(d))


def _proj_body(rows_ref, h_ref, w_ref, o_ref, wb_s):
    del rows_ref
    @pl.when(pl.program_id(1) == 0)
    def _cast_weights():
        for c in range(0, w_ref.shape[0], LANES):
            wb_s[:, c:c + LANES] = w_ref[c:c + LANES, :].T.astype(wb_s.dtype)

    o_ref[...] = jnp.dot(h_ref[...], wb_s[...], preferred_element_type=F32).astype(o_ref.dtype)


def input_projection(h2d, w_in_t, layer, src_rows, tm=2048):
    m, d = h2d.shape
    tn = W_BLOCK
    grid_spec = pltpu.PrefetchScalarGridSpec(
        num_scalar_prefetch=1,
        grid=(len(src_rows), m // tm),
        in_specs=[pl.BlockSpec((tm, d), lambda j, i, rows: (i, 0)),
                  pl.BlockSpec(_row_window(tn, d),
                               lambda j, i, rows: (layer, pl.multiple_of(rows[j], SUBLANES), 0))],
        out_specs=pl.BlockSpec((tm, tn), lambda j, i, rows: (i, j)),
        scratch_shapes=[pltpu.VMEM((d, tn), BF16)])
    return pl.pallas_call(
        _proj_body,
        grid_spec=grid_spec,
        out_shape=jax.ShapeDtypeStruct((m, len(src_rows) * tn), BF16),
        compiler_params=_cparams(2),
        name="input_proj",
    )(jnp.asarray(src_rows, jnp.int32), h2d, w_in_t)


def _proj_small_body(h_ref, w1_ref, w2_ref, o_ref, wb_s):
    @pl.when(pl.program_id(0) == 0)
    def _cast_weights():
        wb_s[:SMALL_WIN1_ROWS, :] = w1_ref[...].astype(wb_s.dtype)
        wb_s[SMALL_WIN1_ROWS:, :] = w2_ref[...].astype(wb_s.dtype)

    o_ref[...] = lax.dot_general(h_ref[...], wb_s[...], _NT, preferred_element_type=F32).astype(o_ref.dtype)


def input_projection_small(h2d, w_in_t, layer, tm=1024):
    m, d = h2d.shape
    return pl.pallas_call(
        _proj_small_body,
        grid=(m // tm,),
        in_specs=[pl.BlockSpec((tm, d), lambda i: (i, 0)),
                  pl.BlockSpec(_row_window(SMALL_WIN1_ROWS, d), lambda i: (layer, SRC_CQ, 0)),
                  pl.BlockSpec(_row_window(SMALL_WIN2_ROWS, d), lambda i: (layer, SRC_IB, 0))],
        out_specs=pl.BlockSpec((tm, D_SLAB_SMALL), lambda i: (i, 0)),
        out_shape=jax.ShapeDtypeStruct((m, D_SLAB_SMALL), BF16),
        scratch_shapes=[pltpu.VMEM((D_SLAB_SMALL, d), BF16)],
        compiler_params=_cparams(1),
        name="input_proj_small",
    )(h2d, w_in_t, w_in_t)


def _proj_t_body(h_ref, wq_ref, wv_ref, qt_ref, vt_ref, wt_s, *, n_sub):
    t = ATTN_TILE
    hd = A_HEAD_DIM
    hv = hd + V_AUG_ROWS

    @pl.when((pl.program_id(0) == 0) & (pl.program_id(1) == 0))
    def _cast_weights():
        q_scale = A_HEAD_DIM ** -0.5 * LOG2E
        wt_s[:A_W, :] = (wq_ref[...] * q_scale).astype(wt_s.dtype)
        wt_s[A_W:, :] = wv_ref[...].astype(wt_s.dtype)

    res = lax.dot_general(wt_s[...], h_ref[...], _NT, preferred_element_type=F32)
    aug = _v_aug_rows(t, vt_ref.dtype)
    for c in range(n_sub):
        qt_ref[c] = res[:A_W, c * t:(c + 1) * t].astype(qt_ref.dtype)
        for h in range(A_HEADS):
            vt_ref[c, h * hv:h * hv + hd, :] = res[A_W + h * hd:A_W + (h + 1) * hd, c * t:(c + 1) * t
                                                   ].astype(vt_ref.dtype)
            vt_ref[c, h * hv + hd:(h + 1) * hv, :] = aug


def moba_qv_projection(h2d, w_in_t, layer, bsz, tm=512):
    m, d = h2d.shape
    assert SRC_QA % A_W == 0 and SRC_VA % A_W == 0
    t = ATTN_TILE
    tm = min(tm, m // bsz)
    n_sub = tm // t
    nt = m // bsz // t
    steps = m // bsz // tm
    vt_rows = A_HEADS * (A_HEAD_DIM + V_AUG_ROWS)
    return pl.pallas_call(
        functools.partial(_proj_t_body, n_sub=n_sub),
        grid=(bsz, steps),
        in_specs=[pl.BlockSpec((tm, d), lambda b, i: (b * steps + i, 0)),
                  pl.BlockSpec((None, A_W, d), lambda b, i: (layer, SRC_QA // A_W, 0),
                               pipeline_mode=pl.Buffered(1)),
                  pl.BlockSpec((None, A_W, d), lambda b, i: (layer, SRC_VA // A_W, 0),
                               pipeline_mode=pl.Buffered(1))],
        out_specs=[pl.BlockSpec((None, n_sub, A_W, t), lambda b, i: (b, i, 0, 0)),
                   pl.BlockSpec((None, n_sub, vt_rows, t), lambda b, i: (b, i, 0, 0))],
        out_shape=[jax.ShapeDtypeStruct((bsz, nt, A_W, t), BF16),
                   jax.ShapeDtypeStruct((bsz, nt, vt_rows, t), BF16)],
        scratch_shapes=[pltpu.VMEM((2 * A_W, d), BF16)],
        compiler_params=_cparams(2),
        name="moba_qv_proj",
    )(h2d, w_in_t, w_in_t)


V_AUG_ROWS = 16


def _v_aug_rows(t, dtype):
    return jnp.where(lax.broadcasted_iota(jnp.int32, (V_AUG_ROWS, t), 0) == 0, 1.0, 0.0).astype(dtype)


def _softmax_init(m_s, acc_s):
    m_s[...] = jnp.full(m_s.shape, -jnp.inf, F32)
    acc_s[...] = jnp.zeros(acc_s.shape, F32)


def _softmax_step_t(s, vt_blk, g, m_s, acc_s):
    m_prev = m_s[g]
    m_new = jnp.maximum(m_prev, jnp.max(s, axis=0, keepdims=True))
    alpha = jnp.exp2(m_prev - m_new)
    p = jnp.exp2(s - m_new)
    acc_s[g] = alpha * acc_s[g] + jnp.dot(vt_blk, p.astype(BF16), preferred_element_type=F32)
    m_s[g] = m_new


TILES_PER_STEP = 2


def _attn_scratch(g_heads, t, dv):
    n_state = TILES_PER_STEP * g_heads
    state = [pltpu.VMEM((n_state, 1, t), F32), pltpu.VMEM((n_state, dv + V_AUG_ROWS, t), F32)]
    return state + [pltpu.VMEM((2, t, t), F32)] * g_heads


def _tile_pair_specs(nt, block_shape, index_of_tile):
    even = pl.BlockSpec(block_shape, lambda x, y, p: index_of_tile(x, y, 2 * p))
    odd = pl.BlockSpec(block_shape, lambda x, y, p: index_of_tile(x, y, nt - 1 - 2 * p))
    return even, odd


def _skewed_blocks(lead, n_uniform, uniform_scores, consume, finish, carry_in=None, drain=True):
    if carry_in is None:
        lead[0][0](0)
        pend_blk, pend_slot = lead[0][1], 0
    else:
        prev_blk, prev_slot, prev_consume, prev_finish = carry_in
        lead[0][0](1 - prev_slot)
        prev_consume(prev_blk, prev_slot)
        prev_finish()
        pend_blk, pend_slot = lead[0][1], 1 - prev_slot
    for fn, blk in lead[1:]:
        fn(1 - pend_slot)
        consume(pend_blk, pend_slot)
        pend_blk, pend_slot = blk, 1 - pend_slot
    a, b = pend_slot, 1 - pend_slot

    def pair(first, pend):
        uniform_scores(first, b)
        consume(pend, a)
        uniform_scores(first + 1, a)
        consume(first, b)
        return first + 1

    def two_pairs(jj, pend):
        return pair(4 * jj + 2, pair(4 * jj, pend))

    n_quads = n_uniform // 4
    pend = lax.fori_loop(0, n_quads, two_pairs, pend_blk)
    rest = n_uniform - 4 * n_quads
    has_pair = rest >= 2
    first = 4 * n_quads

    @pl.when(has_pair)
    def _last_pair():
        pair(first, pend)

    pend = jnp.where(has_pair, first + 1, pend)
    if not drain:
        return pend, a, consume, finish
    last = jnp.where(has_pair, first + 2, first)

    @pl.when(rest % 2 == 1)
    def _odd_tail():
        uniform_scores(last, b)
        consume(pend, a)
        consume(last, b)
        finish()

    @pl.when(rest % 2 == 0)
    def _even_tail():
        consume(pend, a)
        finish()


def _attn_finish(z_ref, o_ref, acc_s, g_heads, dv, state0):
    for g in range(g_heads):
        out_t = acc_s[state0 + g, :dv, :] / acc_s[state0 + g, dv:dv + 1, :]
        o_ref[:, g * dv:(g + 1) * dv] = (out_t.T * _silu(z_ref[:, g * dv:(g + 1) * dv].astype(F32))
                                         ).astype(o_ref.dtype)


def _t5_thresholds():
    max_exact = T5_BUCKETS // 2
    d = np.arange(0, 4 * T5_MAX_DIST, dtype=np.int64)
    dd = np.maximum(d, 1).astype(np.float32)
    large = max_exact + (np.log(dd / np.float32(max_exact)) / np.float32(math.log(T5_MAX_DIST / max_exact))
                         * np.float32(T5_BUCKETS - max_exact)).astype(np.int32)
    large = np.minimum(large, T5_BUCKETS - 1)
    bucket = np.where(d < max_exact, d, large)
    assert np.all(np.diff(bucket) >= 0) and bucket[-1] == T5_BUCKETS - 1
    return [int(np.argmax(bucket >= k)) for k in range(T5_BUCKETS)]


_T5_THRESH = _t5_thresholds()

MASK_BIG = 1e30


def _moba_body(t5_ref, qt_even_ref, qt_odd_ref, k_ref, vt_ref, z_even_ref, z_odd_ref, o_even_ref, o_odd_ref,
               kaug_s, km_s, bias_own_s, bias_prev_s, qa_s, m_s, acc_s, *s_refs, n_blocks, g_heads):
    hg = pl.program_id(0)
    b = pl.program_id(1)
    i = pl.program_id(2)
    t, hd = A_BLOCK, A_HEAD_DIM
    hv = hd + V_AUG_ROWS
    seq = n_blocks * t
    nb_pad = -(-n_blocks // SUBLANES) * SUBLANES

    @pl.when((hg == 0) & (b == 0) & (i == 0))
    def _mask_columns():
        key_blk = lax.broadcasted_iota(jnp.int32, (seq, LANES), 0) // t
        lane = lax.broadcasted_iota(jnp.int32, (seq, LANES), 1)
        pattern = jnp.where(lane == key_blk, -MASK_BIG, 0.0).astype(kaug_s.dtype)
        for g in range(g_heads):
            kaug_s[g, :, hd:] = pattern

    @pl.when((b == 0) & (i == 0))
    def _build_bias():
        key = lax.broadcasted_iota(jnp.int32, (t, t), 0)
        qry = lax.broadcasted_iota(jnp.int32, (t, t), 1)
        for g in range(g_heads):
            head = hg * g_heads + g
            far = t5_ref[T5_BUCKETS - 1, head]
            for dist, dst in ((qry - key, bias_own_s), (t + qry - key, bias_prev_s)):
                bias = jnp.full((t, t), t5_ref[0, head], F32)
                for kk in range(1, T5_BUCKETS):
                    bias = jnp.where(dist >= _T5_THRESH[kk], t5_ref[kk, head], bias)
                dst[g] = (bias - far) * LOG2E

    @pl.when(i == 0)
    def _new_sequence():
        for g in range(g_heads):
            kaug_s[g, :, :hd] = k_ref[:, g * hd:(g + 1) * hd]
            km_s[g] = jnp.zeros(km_s.shape[1:], km_s.dtype)
            for nb in range(n_blocks):
                mean = jnp.mean(k_ref[nb * t:(nb + 1) * t, g * hd:(g + 1) * hd].astype(F32), axis=0, keepdims=True)
                hi = mean.astype(BF16)
                km_s[g, nb:nb + 1, :] = hi
                km_s[g, LANES + nb:LANES + nb + 1, :] = (mean - hi.astype(F32)).astype(BF16)

    _softmax_init(m_s, acc_s)
    blk_row = lax.broadcasted_iota(jnp.int32, (nb_pad, t), 0)

    def query_tile(tile, qt_ref, z_ref, o_ref, which):
        state0 = which * g_heads

        past = blk_row < tile
        for g in range(g_heads):
            qt = qt_ref[g * hd:(g + 1) * hd, :]
            g2 = jnp.dot(km_s[g], qt, preferred_element_type=F32)
            gate = g2[:nb_pad, :] + g2[LANES:LANES + nb_pad, :]
            cur = jnp.where(past, gate, NEG)
            keep = blk_row == tile
            for _ in range(A_TOPK):
                mx = jnp.max(cur, axis=0, keepdims=True)
                idx = jnp.min(jnp.where(cur == mx, blk_row, nb_pad), axis=0, keepdims=True)
                pick = blk_row == idx
                keep = keep | (pick & past)
                cur = jnp.where(pick, -jnp.inf, cur)
            unsel = jnp.where(keep, 0.0, 1.0)
            qa_s[state0 + g, :hd, :] = qt
            qa_s[state0 + g, hd:, :] = jnp.concatenate([unsel, jnp.zeros((LANES - nb_pad, t), F32)],
                                                       axis=0).astype(qa_s.dtype)

        def raw_scores(g, j):
            start = pl.multiple_of(j * t, t)
            return jnp.dot(kaug_s[g, pl.ds(start, t), :], qa_s[state0 + g],
                           preferred_element_type=F32)

        def own_scores(slot):
            key = lax.broadcasted_iota(jnp.int32, (t, t), 0)
            qry = lax.broadcasted_iota(jnp.int32, (t, t), 1)
            for g in range(g_heads):
                s_refs[g][slot] = jnp.where(key <= qry, raw_scores(g, tile) + bias_own_s[g], NEG)

        j_prev = jnp.maximum(tile - 1, 0)
        no_prev = jnp.where(tile == 0, -MASK_BIG, 0.0)

        def prev_scores(slot):
            for g in range(g_heads):
                s_refs[g][slot] = raw_scores(g, j_prev) + (bias_prev_s[g] + no_prev)

        def far_scores(j, slot):
            for g in range(g_heads):
                s_refs[g][slot] = raw_scores(g, j)

        def consume(j, slot):
            for g in range(g_heads):
                _softmax_step_t(s_refs[g][slot], vt_ref[j, g * hv:(g + 1) * hv, :], state0 + g, m_s, acc_s)

        finish = functools.partial(_attn_finish, z_ref, o_ref, acc_s, g_heads, hd, state0)
        return [(own_scores, tile), (prev_scores, j_prev)], j_prev, far_scores, consume, finish

    carry = _skewed_blocks(*query_tile(n_blocks - 1 - 2 * i, qt_odd_ref, z_odd_ref, o_odd_ref, 1), drain=False)
    _skewed_blocks(*query_tile(2 * i, qt_even_ref, z_even_ref, o_even_ref, 0), carry_in=carry)


def moba_branch(qt, vt, slab, t5_table, g_heads=ATTN_HEADS_PER_STEP):
    bsz, nb, _, t = qt.shape
    s = nb * t
    hd = A_HEAD_DIM
    gw = g_heads * hd
    assert t == A_BLOCK and nb <= LANES and nb % TILES_PER_STEP == 0 and A_HEADS % g_heads == 0
    assert OFF_KA % gw == 0 and OFF_ZA % gw == 0
    assert _T5_THRESH[T5_BUCKETS - 1] <= t + 1
    ck, cz = OFF_KA // gw, OFF_ZA // gw
    qt_even, qt_odd = _tile_pair_specs(nb, (None, None, gw, t), lambda h, b, tile: (b, tile, h, 0))
    z_even, z_odd = _tile_pair_specs(nb, (None, t, gw), lambda h, b, tile: (b, tile, cz + h))
    out = jax.ShapeDtypeStruct((bsz, nb // 2, t, A_W), BF16)
    return pl.pallas_call(
        functools.partial(_moba_body, n_blocks=nb, g_heads=g_heads),
        grid=(A_HEADS // g_heads, bsz, nb // 2),
        in_specs=[pl.BlockSpec(memory_space=pltpu.SMEM),
                  qt_even, qt_odd,
                  pl.BlockSpec((None, s, gw), lambda h, b, i: (b, 0, ck + h), pipeline_mode=pl.Buffered(1)),
                  pl.BlockSpec((None, nb, g_heads * (hd + V_AUG_ROWS), t), lambda h, b, i: (b, 0, h, 0),
                               pipeline_mode=pl.Buffered(1)),
                  z_even, z_odd],
        out_specs=[pl.BlockSpec((None, None, t, gw), lambda h, b, i: (b, i, 0, h)),
                   pl.BlockSpec((None, None, t, gw), lambda h, b, i: (b, nb // 2 - 1 - i, 0, h))],
        out_shape=[out, out],
        scratch_shapes=[pltpu.VMEM((g_heads, s, ATTN_QK_PAD), BF16),
                        pltpu.VMEM((g_heads, 2 * LANES, hd), BF16),
                        pltpu.VMEM((g_heads, t, t), F32),
                        pltpu.VMEM((g_heads, t, t), F32),
                        pltpu.VMEM((TILES_PER_STEP * g_heads, ATTN_QK_PAD, t), BF16)]
        + _attn_scratch(g_heads, t, hd),
        compiler_params=_cparams(3),
        name="moba",
    )(t5_table, qt, qt, slab, vt, slab, slab)


MLSTM_CHUNK = 256
CONV_HALO = 8


def _log_sigmoid(t):
    return jnp.minimum(t, 0.0) - jnp.log1p(jnp.exp(-jnp.abs(t)))


def _mlstm_body(qk_ref, v_ref, ob_ref, zb_ref, if_ref, cw_ref, cb_ref, gb_ref, gn_ref, o_ref,
                xe_s, c_s, n_s, m_s):
    c = pl.program_id(1)
    L = MLSTM_CHUNK
    dk, dv = B_QK_DIM, B_V_DIM

    @pl.when(c == 0)
    def _reset():
        xe_s[...] = jnp.zeros_like(xe_s)
        c_s[...] = jnp.zeros_like(c_s)
        n_s[...] = jnp.zeros_like(n_s)
        m_s[...] = jnp.zeros_like(m_s)

    x = qk_ref[...].astype(F32)
    xe = jnp.concatenate([xe_s[...], x], axis=0)
    conv = cb_ref[...] + cw_ref[B_CONV - 1:B_CONV, :] * x
    for j in range(B_CONV - 1):
        back = B_CONV - 1 - j
        conv = conv + cw_ref[j:j + 1, :] * pltpu.roll(xe, back, axis=0)[CONV_HALO:, :]
    xe_s[...] = x[L - CONV_HALO:, :]
    qk = _silu(conv)

    gi = if_ref[...].astype(F32) + gb_ref[...]
    lf = _log_sigmoid(gi)
    row = lax.broadcasted_iota(jnp.int32, (L, L), 0)
    col = lax.broadcasted_iota(jnp.int32, (L, L), 1)
    causal = col <= row
    tri = jnp.where(causal, 1.0, 0.0).astype(F32)
    b_cols = jnp.dot(tri, lf, precision=lax.Precision.HIGHEST, preferred_element_type=F32)
    b_rows = b_cols.T
    li_rows = gi.T

    for h in range(B_HEADS):
        q_h = qk[:, h * dk:(h + 1) * dk]
        k_h = qk[:, B_QK_W + h * dk:B_QK_W + (h + 1) * dk] * (dk ** -0.5)
        v_h = v_ref[:, h * dv:(h + 1) * dv]
        b_c = b_cols[:, B_HEADS + h:B_HEADS + h + 1]
        b_r = b_rows[B_HEADS + h:B_HEADS + h + 1, :]
        li_c = gi[:, h:h + 1]
        li_r = li_rows[h:h + 1, :]
        m_prev = m_s[h:h + 1, 0:1]
        c_prev = c_s[h]
        n_prev = n_s[h:h + 1, :]

        a_c = b_c + m_prev
        dmat = jnp.where(causal, b_c - b_r + li_r, -jnp.inf)
        m_t = jnp.maximum(a_c, jnp.max(dmat, axis=-1, keepdims=True))
        w_inter = jnp.exp(a_c - m_t)
        q_b = q_h.astype(BF16)
        k_b = k_h.astype(BF16)
        sc = (lax.dot_general(q_b, k_b, _NT, preferred_element_type=F32)
              * jnp.exp(dmat - m_t))
        num = (w_inter * jnp.dot(q_b, c_prev.astype(BF16), preferred_element_type=F32)
               + jnp.dot(sc.astype(BF16), v_h, preferred_element_type=F32))
        den = (w_inter * jnp.sum(q_h * n_prev, axis=-1, keepdims=True)
               + jnp.sum(sc, axis=-1, keepdims=True))
        hh = num / jnp.maximum(jnp.abs(den), jnp.exp(-m_t))

        b_last = b_c[L - 1:L, :]
        g_c = b_last - b_c + li_c
        m_new = jnp.maximum(b_last + m_prev, jnp.max(g_c, axis=0, keepdims=True))
        decay = jnp.exp(b_last + m_prev - m_new)
        wk = jnp.exp(g_c - m_new) * k_h
        c_s[h] = decay * c_prev + lax.dot_general(wk.astype(BF16), v_h, (((0,), (0,)), ((), ())),
                                                  preferred_element_type=F32)
        n_s[h:h + 1, :] = decay * n_prev + jnp.sum(wk, axis=0, keepdims=True)
        m_s[h:h + 1, :] = jnp.broadcast_to(m_new, (1, LANES))

        mu = jnp.mean(hh, axis=-1, keepdims=True)
        dlt = hh - mu
        var = jnp.mean(dlt * dlt, axis=-1, keepdims=True)
        y = dlt * lax.rsqrt(var + EPS) * gn_ref[:, h * dv:(h + 1) * dv]
        y = (y * jax.nn.sigmoid(ob_ref[:, h * dv:(h + 1) * dv].astype(F32))
             * _silu(zb_ref[:, h * dv:(h + 1) * dv].astype(F32)))
        o_ref[:, h * dv:(h + 1) * dv] = y.astype(o_ref.dtype)


def mlstm_branch(slab, slab_small, conv_w, conv_b, i_bias, f_bias, out_norm):
    bsz, s, _ = slab.shape
    L = MLSTM_CHUNK
    assert s % L == 0
    gate_bias = jnp.zeros((1, LANES), F32).at[0, :B_HEADS].set(i_bias).at[0, B_HEADS:2 * B_HEADS].set(f_bias)
    w2 = 2 * B_QK_W
    full = lambda shape: pl.BlockSpec(shape, lambda b, c: (0,) * len(shape))
    return pl.pallas_call(
        _mlstm_body,
        grid=(bsz, s // L),
        in_specs=[pl.BlockSpec((None, L, w2), lambda b, c: (b, c, OFF_QKB // w2)),
                  pl.BlockSpec((None, L, B_V_W), lambda b, c: (b, c, OFF_VB // B_V_W)),
                  pl.BlockSpec((None, L, B_V_W), lambda b, c: (b, c, OFF_OB // B_V_W)),
                  pl.BlockSpec((None, L, B_V_W), lambda b, c: (b, c, OFF_ZB // B_V_W)),
                  pl.BlockSpec((None, L, LANES), lambda b, c: (b, c, OFF_IF // LANES)),
                  full((B_CONV, w2)), full((1, w2)), full((1, LANES)), full((1, B_V_W))],
        out_specs=pl.BlockSpec((None, L, B_V_W), lambda b, c: (b, c, 0)),
        out_shape=jax.ShapeDtypeStruct((bsz, s, B_V_W), BF16),
        scratch_shapes=[pltpu.VMEM((CONV_HALO, w2), F32),
                        pltpu.VMEM((B_HEADS, B_QK_DIM, B_V_DIM), F32),
                        pltpu.VMEM((8, B_QK_DIM), F32),
                        pltpu.VMEM((8, LANES), F32)],
        compiler_params=_cparams(2),
        name="mlstm",
    )(slab, slab, slab, slab, slab_small, conv_w, conv_b.reshape(1, w2), gate_bias,
      out_norm.reshape(1, B_V_W))


def _mla_prep_body(cq_ref, ckv_ref, kr_ref, qg_ref, kg_ref, wqt_ref, wqst_ref, wkn_ref, wvt_ref,
                   cos_ref, sin_ref, cost_ref, sint_ref, qt_ref, kf_ref, vt_ref):
    def normed(ref, g_ref):
        t = ref[...].astype(F32)
        y = t * lax.rsqrt(jnp.mean(t * t, axis=-1, keepdims=True) + EPS)
        return (y * g_ref[...]).astype(BF16)

    cqn = normed(cq_ref, qg_ref)
    ckvn = normed(ckv_ref, kg_ref)

    qt_main = lax.dot_general(wqt_ref[...], cqn, _NT, preferred_element_type=F32)
    qt_swap = lax.dot_general(wqst_ref[...], cqn, _NT, preferred_element_type=F32)
    cos_t = cost_ref[...]
    sin_t = sint_ref[...]
    t = ATTN_TILE
    n_sub = cos_t.shape[1] // t
    for h in range(C_HEADS):
        lo = h * ATTN_QK_PAD
        nope = qt_main[lo:lo + LANES, :].astype(qt_ref.dtype)
        rope = (qt_main[lo + LANES:lo + 2 * LANES, :] * cos_t
                + qt_swap[h * LANES:(h + 1) * LANES, :] * sin_t).astype(qt_ref.dtype)
        for c in range(n_sub):
            qt_ref[c, lo:lo + LANES, :] = nope[:, c * t:(c + 1) * t]
            qt_ref[c, lo + LANES:lo + 2 * LANES, :] = rope[:, c * t:(c + 1) * t]

    k_nope = jnp.dot(ckvn, wkn_ref[...], preferred_element_type=F32)
    kr = kr_ref[...].astype(F32)
    half = C_ROPE // 2
    lane = lax.broadcasted_iota(jnp.int32, kr.shape, 1)
    swapped = jnp.where(lane < half, -pltpu.roll(kr, LANES - half, axis=1), pltpu.roll(kr, half, axis=1))
    k_rot = (kr * cos_ref[...] + swapped * sin_ref[...]).astype(kf_ref.dtype)
    for h in range(C_HEADS):
        lo = h * ATTN_QK_PAD
        kf_ref[:, lo:lo + LANES] = k_nope[:, h * LANES:(h + 1) * LANES].astype(kf_ref.dtype)
        kf_ref[:, lo + LANES:lo + 2 * LANES] = k_rot

    v_t = lax.dot_general(wvt_ref[...], ckvn, _NT, preferred_element_type=F32)
    hv = C_V_DIM + V_AUG_ROWS
    aug = _v_aug_rows(t, vt_ref.dtype)
    for h in range(C_HEADS):
        v_h = v_t[h * C_V_DIM:(h + 1) * C_V_DIM, :].astype(vt_ref.dtype)
        for c in range(n_sub):
            vt_ref[c, h * hv:h * hv + C_V_DIM, :] = v_h[:, c * t:(c + 1) * t]
            vt_ref[c, h * hv + C_V_DIM:(h + 1) * hv, :] = aug


def mla_prep(slab, q_norm, kv_norm, w_uq, w_ukv, tm=512):
    bsz, s, _ = slab.shape
    t = ATTN_TILE
    n_sub = tm // t
    vt_rows = C_HEADS * (C_V_DIM + V_AUG_ROWS)
    half = C_ROPE // 2
    scale = (C_NOPE + C_ROPE) ** -0.5 * LOG2E
    wq = (w_uq * scale).reshape(C_Q_RANK, C_HEADS, C_NOPE + C_ROPE)
    pad = jnp.zeros((C_Q_RANK, C_HEADS, LANES - C_ROPE), F32)
    x1, x2 = wq[..., C_NOPE:C_NOPE + half], wq[..., C_NOPE + half:]
    wqt = jnp.concatenate([wq, pad], axis=-1).reshape(C_Q_RANK, C_HEADS * ATTN_QK_PAD).T.astype(BF16)
    wqst = jnp.concatenate([-x2, x1, pad], axis=-1).reshape(C_Q_RANK, C_HEADS * LANES).T.astype(BF16)
    wkv = w_ukv.reshape(C_KV_RANK, C_HEADS, C_NOPE + C_V_DIM)
    wkn = wkv[..., :C_NOPE].reshape(C_KV_RANK, C_HEADS * C_NOPE).astype(BF16)
    wvt = wkv[..., C_NOPE:].reshape(C_KV_RANK, C_W).T.astype(BF16)

    pos = jnp.arange(s, dtype=jnp.int32)
    inv = ROPE_THETA ** (-jnp.arange(half, dtype=F32) / half)
    ang = pos.astype(F32)[:, None] * inv[None, :]
    zpad = jnp.zeros((s, LANES - C_ROPE), F32)
    cos_tab = jnp.concatenate([jnp.cos(ang), jnp.cos(ang), zpad], axis=-1)
    sin_tab = jnp.concatenate([jnp.sin(ang), jnp.sin(ang), zpad], axis=-1)

    nt = s // t
    full = lambda shape: pl.BlockSpec(shape, lambda b, i: (0,) * len(shape))
    return pl.pallas_call(
        _mla_prep_body,
        grid=(bsz, s // tm),
        in_specs=[pl.BlockSpec((None, tm, C_Q_RANK), lambda b, i: (b, i, OFF_CQ // C_Q_RANK)),
                  pl.BlockSpec((None, tm, C_KV_RANK), lambda b, i: (b, i, OFF_CKV // C_KV_RANK)),
                  pl.BlockSpec((None, tm, LANES), lambda b, i: (b, i, OFF_KR // LANES)),
                  full((1, C_Q_RANK)), full((1, C_KV_RANK)),
                  full(wqt.shape), full(wqst.shape), full(wkn.shape), full(wvt.shape),
                  pl.BlockSpec((tm, LANES), lambda b, i: (i, 0)),
                  pl.BlockSpec((tm, LANES), lambda b, i: (i, 0)),
                  pl.BlockSpec((LANES, tm), lambda b, i: (0, i)),
                  pl.BlockSpec((LANES, tm), lambda b, i: (0, i))],
        out_specs=[pl.BlockSpec((None, n_sub, C_HEADS * ATTN_QK_PAD, t), lambda b, i: (b, i, 0, 0)),
                   pl.BlockSpec((None, tm, C_HEADS * ATTN_QK_PAD), lambda b, i: (b, i, 0)),
                   pl.BlockSpec((None, n_sub, vt_rows, t), lambda b, i: (b, i, 0, 0))],
        out_shape=[jax.ShapeDtypeStruct((bsz, nt, C_HEADS * ATTN_QK_PAD, t), BF16),
                   jax.ShapeDtypeStruct((bsz, s, C_HEADS * ATTN_QK_PAD), BF16),
                   jax.ShapeDtypeStruct((bsz, nt, vt_rows, t), BF16)],
        compiler_params=_cparams(2),
        name="mla_prep",
    )(slab, slab, slab, q_norm.reshape(1, C_Q_RANK), kv_norm.reshape(1, C_KV_RANK),
      wqt, wqst, wkn, wvt, cos_tab, sin_tab, cos_tab.T, sin_tab.T)


def _mla_attn_body(qt_even_ref, qt_odd_ref, k_ref, vt_ref, z_even_ref, z_odd_ref, o_even_ref, o_odd_ref,
                   m_s, acc_s, *s_refs, g_heads, n_tiles):
    i = pl.program_id(2)
    t = ATTN_TILE
    dq, dv = ATTN_QK_PAD, C_V_DIM
    hv = dv + V_AUG_ROWS
    _softmax_init(m_s, acc_s)

    def query_tile(tile, qt_ref, z_ref, o_ref, which):
        state0 = which * g_heads

        def raw_scores(g, j):
            start = pl.multiple_of(j * t, t)
            return jnp.dot(k_ref[pl.ds(start, t), g * dq:(g + 1) * dq], qt_ref[g * dq:(g + 1) * dq, :],
                           preferred_element_type=F32)

        def diagonal_scores(slot):
            key = lax.broadcasted_iota(jnp.int32, (t, t), 0)
            qry = lax.broadcasted_iota(jnp.int32, (t, t), 1)
            for g in range(g_heads):
                s_refs[g][slot] = jnp.where(key <= qry, raw_scores(g, tile), NEG)

        def past_scores(j, slot):
            for g in range(g_heads):
                s_refs[g][slot] = raw_scores(g, j)

        def consume(j, slot):
            for g in range(g_heads):
                _softmax_step_t(s_refs[g][slot], vt_ref[j, g * hv:(g + 1) * hv, :], state0 + g, m_s, acc_s)

        finish = functools.partial(_attn_finish, z_ref, o_ref, acc_s, g_heads, dv, state0)
        return [(diagonal_scores, tile)], tile, past_scores, consume, finish

    carry = _skewed_blocks(*query_tile(2 * i, qt_even_ref, z_even_ref, o_even_ref, 0), drain=False)
    _skewed_blocks(*query_tile(n_tiles - 1 - 2 * i, qt_odd_ref, z_odd_ref, o_odd_ref, 1), carry_in=carry)


def mla_attention(qt, kf, vt, slab, g_heads=ATTN_HEADS_PER_STEP):
    bsz, nt, _, t = qt.shape
    s = nt * t
    dq, dv = ATTN_QK_PAD, C_V_DIM
    assert t == ATTN_TILE and nt % TILES_PER_STEP == 0
    assert C_HEADS % g_heads == 0 and OFF_ZC % (g_heads * dv) == 0
    cz = OFF_ZC // (g_heads * dv)
    qt_even, qt_odd = _tile_pair_specs(nt, (None, None, g_heads * dq, t), lambda b, h, tile: (b, tile, h, 0))
    z_even, z_odd = _tile_pair_specs(nt, (None, t, g_heads * dv), lambda b, h, tile: (b, tile, cz + h))
    out = jax.ShapeDtypeStruct((bsz, nt // 2, t, C_W), BF16)
    return pl.pallas_call(
        functools.partial(_mla_attn_body, g_heads=g_heads, n_tiles=nt),
        grid=(bsz, C_HEADS // g_heads, nt // 2),
        in_specs=[qt_even, qt_odd,
                  pl.BlockSpec((None, s, g_heads * dq), lambda b, h, i: (b, 0, h),
                               pipeline_mode=pl.Buffered(1)),
                  pl.BlockSpec((None, nt, g_heads * (dv + V_AUG_ROWS), t), lambda b, h, i: (b, 0, h, 0),
                               pipeline_mode=pl.Buffered(1)),
                  z_even, z_odd],
        out_specs=[pl.BlockSpec((None, None, t, g_heads * dv), lambda b, h, i: (b, i, 0, h)),
                   pl.BlockSpec((None, None, t, g_heads * dv), lambda b, h, i: (b, nt // 2 - 1 - i, 0, h))],
        out_shape=[out, out],
        scratch_shapes=_attn_scratch(g_heads, t, dv),
        compiler_params=_cparams(3),
        name="mla_attn",
    )(qt, qt, kf, vt, slab, slab)


def _merge_body(ya_even_ref, ya_odd_ref, yb_ref, yc_even_ref, yc_odd_ref, wa_ref, wb_ref, wc_ref,
                ga_ref, gb_ref, gc_ref, o_ref, w_s):
    @pl.when(pl.program_id(1) == 0)
    def _cast_weights():
        for n, w_ref in enumerate((wa_ref, wb_ref, wc_ref)):
            w_s[n] = w_ref[...].astype(w_s.dtype)

    ya = jnp.concatenate([ya_even_ref[...], ya_odd_ref[...]], axis=0)
    yc = jnp.concatenate([yc_even_ref[...], yc_odd_ref[...]], axis=0)
    acc = None
    for n, (y, g_ref) in enumerate(((ya, ga_ref), (yb_ref[...], gb_ref), (yc, gc_ref))):
        term = jax.nn.sigmoid(g_ref[...].astype(F32)) * jnp.dot(y, w_s[n], preferred_element_type=F32)
        acc = term if acc is None else acc + term
    o_ref[...] = acc.astype(o_ref.dtype)


def branch_merge(ya_tiles, yb, yc_tiles, w_branch, layer, slab2d, tn=1024):
    m, w = yb.shape
    d = w_branch.shape[-1]
    t = ATTN_TILE
    tm = TILES_PER_STEP * t
    assert OFF_GT % tn == 0
    g0 = OFF_GT // tn
    gper = d // tn
    halves = [y.reshape(m // tm, t, w) for y in (*ya_tiles, *yc_tiles)]
    half_spec = pl.BlockSpec((None, t, w), lambda j, i: (i, 0, 0))
    w_specs = [pl.BlockSpec((None, None, w, tn), functools.partial(lambda j, i, n: (layer, n, 0, j), n=n))
               for n in range(N_BRANCH)]
    g_specs = [pl.BlockSpec((tm, tn), functools.partial(lambda j, i, n: (i, g0 + n * gper + j), n=n))
               for n in range(N_BRANCH)]
    return pl.pallas_call(
        _merge_body,
        grid=(d // tn, m // tm),
        in_specs=[half_spec, half_spec, pl.BlockSpec((tm, w), lambda j, i: (i, 0)), half_spec, half_spec]
        + w_specs + g_specs,
        out_specs=pl.BlockSpec((tm, tn), lambda j, i: (i, j)),
        out_shape=jax.ShapeDtypeStruct((m, d), BF16),
        scratch_shapes=[pltpu.VMEM((N_BRANCH, w, tn), BF16)],
        compiler_params=_cparams(2),
        name="branch_merge",
    )(halves[0], halves[1], yb, halves[2], halves[3], w_branch, w_branch, w_branch, slab2d, slab2d, slab2d)


def _out_body(mg_ref, w_ref, x_ref, g_ref, *refs, last):
    out_refs, w_s = refs[:-1], refs[-1]

    @pl.when(pl.program_id(0) == 0)
    def _cast_weights():
        w_s[...] = w_ref[...].astype(w_s.dtype)

    x_new = x_ref[...] + jnp.dot(mg_ref[...], w_s[...], preferred_element_type=F32)
    y = x_new * lax.rsqrt(jnp.mean(x_new * x_new, axis=-1, keepdims=True) + EPS) * g_ref[...]
    if last:
        out_refs[0][...] = y
    else:
        out_refs[0][...] = x_new
        out_refs[1][...] = y.astype(out_refs[1].dtype)


def out_projection(merged, w_out, layer, x2d, gain, last, tm=512):
    m, d = x2d.shape
    row = pl.BlockSpec((tm, d), lambda i: (i, 0))
    if last:
        out_specs, out_shape = row, jax.ShapeDtypeStruct((m, d), F32)
    else:
        out_specs = [row, row]
        out_shape = [jax.ShapeDtypeStruct((m, d), F32), jax.ShapeDtypeStruct((m, d), BF16)]
    return pl.pallas_call(
        functools.partial(_out_body, last=last),
        grid=(m // tm,),
        in_specs=[row,
                  pl.BlockSpec((None, d, d), lambda i: (layer, 0, 0), pipeline_mode=pl.Buffered(1)),
                  row, pl.BlockSpec((1, d), lambda i: (0, 0))],
        out_specs=out_specs,
        out_shape=out_shape,
        scratch_shapes=[pltpu.VMEM((d, d), BF16)],
        compiler_params=_cparams(1),
        name="out_proj",
    )(merged, w_out, x2d, gain.reshape(1, d))


def kernel(x, norm_gain, w_in, t5_table, mlstm_conv_w, mlstm_conv_b, mlstm_i_bias, mlstm_f_bias,
           mlstm_out_norm, mla_q_norm, mla_kv_norm, mla_w_uq, mla_w_ukv, w_branch, w_out, final_norm):
    bsz, s, d = x.shape
    m = bsz * s
    x2d = x.reshape(m, d)
    h = rmsnorm_rows(x2d, norm_gain[0])
    w_in_t = jnp.swapaxes(w_in, 1, 2)
    out = None
    for l in range(DEPTH):
        slab2d = input_projection(h, w_in_t, l, SLAB_SRC_ROWS)
        slab = slab2d.reshape(bsz, s, D_SLAB)
        slab_small = input_projection_small(h, w_in_t, l).reshape(bsz, s, D_SLAB_SMALL)
        qt_a, vt_a = moba_qv_projection(h, w_in_t, l, bsz)
        ya = moba_branch(qt_a, vt_a, slab, t5_table)
        yb = mlstm_branch(slab, slab_small, mlstm_conv_w[l], mlstm_conv_b[l], mlstm_i_bias[l], mlstm_f_bias[l],
                          mlstm_out_norm[l])
        qt_c, k_c, vt_c = mla_prep(slab_small, mla_q_norm[l], mla_kv_norm[l], mla_w_uq[l], mla_w_ukv[l])
        yc = mla_attention(qt_c, k_c, vt_c, slab)
        merged = branch_merge(ya, yb.reshape(m, B_V_W), yc, w_branch, l, slab2d)
        last = l == DEPTH - 1
        gain = final_norm if last else norm_gain[l + 1]
        res = out_projection(merged, w_out, l, x2d, gain, last)
        if last:
            out = res
        else:
            x2d, h = res
    return out.reshape(bsz, s, d)
```

```python
import functools
import math

import jax
import jax.numpy as jnp
import numpy as np
from jax import lax
from jax.experimental import pallas as pl
from jax.experimental.pallas import tpu as pltpu

F32 = jnp.float32
BF16 = jnp.bfloat16

D_MODEL = 2048
DEPTH = 2
EPS = 1e-6
NEG = -1e30
LOG2E = math.log2(math.e)

A_HEADS = 8
A_HEAD_DIM = 128
A_BLOCK = 256
A_TOPK = 3
T5_BUCKETS = 32
T5_MAX_DIST = 128
B_HEADS = 4
B_QK_DIM = 128
B_V_DIM = 256
B_CONV = 4
C_HEADS = 8
C_Q_RANK = 512
C_KV_RANK = 256
C_NOPE = 128
C_ROPE = 64
C_V_DIM = 128
ROPE_THETA = 10000.0
N_BRANCH = 3

A_W = A_HEADS * A_HEAD_DIM
B_QK_W = B_HEADS * B_QK_DIM
B_V_W = B_HEADS * B_V_DIM
C_W = C_HEADS * C_V_DIM

LANES = 128
SUBLANES = 8
VMEM_LIMIT_BYTES = 58 * 1024 * 1024

SPLIT_SIZES = (A_W, A_W, A_W, A_W, B_QK_W, B_QK_W, B_V_W, B_HEADS, B_HEADS, B_V_W, B_V_W,
               C_Q_RANK, C_KV_RANK, C_ROPE, C_W, N_BRANCH * D_MODEL)
(SRC_QA, SRC_KA, SRC_VA, SRC_ZA, SRC_QB, SRC_KB, SRC_VB, SRC_IB, SRC_FB, SRC_OB, SRC_ZB,
 SRC_CQ, SRC_CKV, SRC_KR, SRC_ZC, SRC_GT) = (int(v) for v in np.cumsum((0,) + SPLIT_SIZES[:-1]))
D_IN = int(sum(SPLIT_SIZES))

W_BLOCK = 1024
SLAB_SRC_ROWS = ((SRC_KA, SRC_ZA, SRC_QB, SRC_VB, SRC_OB, SRC_ZB, SRC_ZC)
                 + tuple(SRC_GT + k * W_BLOCK for k in range(N_BRANCH * D_MODEL // W_BLOCK)))
assert SRC_KB == SRC_QB + B_QK_W and all(r % SUBLANES == 0 for r in SLAB_SRC_ROWS)
OFF_KA = 0
OFF_ZA = OFF_KA + W_BLOCK
OFF_QKB = OFF_ZA + W_BLOCK
OFF_VB = OFF_QKB + W_BLOCK
OFF_OB = OFF_VB + W_BLOCK
OFF_ZB = OFF_OB + W_BLOCK
OFF_ZC = OFF_ZB + W_BLOCK
OFF_GT = OFF_ZC + W_BLOCK
D_SLAB = len(SLAB_SRC_ROWS) * W_BLOCK
SMALL_WIN1_ROWS = C_Q_RANK + C_KV_RANK + LANES
SMALL_WIN2_ROWS = LANES
assert SRC_CKV == SRC_CQ + C_Q_RANK and SRC_KR == SRC_CKV + C_KV_RANK and SRC_FB == SRC_IB + B_HEADS
assert SRC_CQ % SUBLANES == 0 and SRC_IB % SUBLANES == 0
assert SRC_CQ + SMALL_WIN1_ROWS <= D_IN and SRC_IB + SMALL_WIN2_ROWS <= D_IN
OFF_CQ = 0
OFF_CKV = OFF_CQ + C_Q_RANK
OFF_KR = OFF_CKV + C_KV_RANK
OFF_IF = OFF_KR + LANES
D_SLAB_SMALL = SMALL_WIN1_ROWS + SMALL_WIN2_ROWS
assert D_SLAB_SMALL == W_BLOCK

ATTN_TILE = 256
ATTN_QK_PAD = 2 * LANES
ATTN_HEADS_PER_STEP = 8
assert ATTN_TILE == A_BLOCK


def _cparams(n_axes):
    return pltpu.CompilerParams(dimension_semantics=("arbitrary",) * n_axes,
                                vmem_limit_bytes=VMEM_LIMIT_BYTES)


def _silu(t):
    return t * jax.nn.sigmoid(t)


_NT = (((1,), (1,)), ((), ()))


def _rmsnorm_body(x_ref, g_ref, o_ref):
    xf = x_ref[...]
    y = xf * lax.rsqrt(jnp.mean(xf * xf, axis=-1, keepdims=True) + EPS)
    o_ref[...] = (y * g_ref[...]).astype(o_ref.dtype)


def rmsnorm_rows(x2d, gain, tm=512):
    m, d = x2d.shape
    return pl.pallas_call(
        _rmsnorm_body,
        grid=(m // tm,),
        in_specs=[pl.BlockSpec((tm, d), lambda i: (i, 0)),
                  pl.BlockSpec((1, d), lambda i: (0, 0))],
        out_specs=pl.BlockSpec((tm, d), lambda i: (i, 0)),
        out_shape=jax.ShapeDtypeStruct((m, d), BF16),
        compiler_params=_cparams(1),
        name="rmsnorm",
    )(x2d, gain.reshape(1, d))


def _row_window(rows, d):
    return (pl.Squeezed(), pl.Element(rows), pl.Element(d))


def _proj_body(rows_ref, h_ref, w_ref, o_ref, wb_s):
    del rows_ref
    @pl.when(pl.program_id(1) == 0)
    def _cast_weights():
        for c in range(0, w_ref.shape[0], LANES):
            wb_s[:, c:c + LANES] = w_ref[c:c + LANES, :].T.astype(wb_s.dtype)

    o_ref[...] = jnp.dot(h_ref[...], wb_s[...], preferred_element_type=F32).astype(o_ref.dtype)


def input_projection(h2d, w_in_t, layer, src_rows, tm=2048):
    m, d = h2d.shape
    tn = W_BLOCK
    grid_spec = pltpu.PrefetchScalarGridSpec(
        num_scalar_prefetch=1,
        grid=(len(src_rows), m // tm),
        in_specs=[pl.BlockSpec((tm, d), lambda j, i, rows: (i, 0)),
                  pl.BlockSpec(_row_window(tn, d),
                               lambda j, i, rows: (layer, pl.multiple_of(rows[j], SUBLANES), 0))],
        out_specs=pl.BlockSpec((tm, tn), lambda j, i, rows: (i, j)),
        scratch_shapes=[pltpu.VMEM((d, tn), BF16)])
    return pl.pallas_call(
        _proj_body,
        grid_spec=grid_spec,
        out_shape=jax.ShapeDtypeStruct((m, len(src_rows) * tn), BF16),
        compiler_params=_cparams(2),
        name="input_proj",
    )(jnp.asarray(src_rows, jnp.int32), h2d, w_in_t)


def _proj_small_body(h_ref, w1_ref, w2_ref, o_ref, wb_s):
    @pl.when(pl.program_id(0) == 0)
    def _cast_weights():
        wb_s[:SMALL_WIN1_ROWS, :] = w1_ref[...].astype(wb_s.dtype)
        wb_s[SMALL_WIN1_ROWS:, :] = w2_ref[...].astype(wb_s.dtype)

    o_ref[...] = lax.dot_general(h_ref[...], wb_s[...], _NT, preferred_element_type=F32).astype(o_ref.dtype)


def input_projection_small(h2d, w_in_t, layer, tm=1024):
    m, d = h2d.shape
    return pl.pallas_call(
        _proj_small_body,
        grid=(m // tm,),
        in_specs=[pl.BlockSpec((tm, d), lambda i: (i, 0)),
                  pl.BlockSpec(_row_window(SMALL_WIN1_ROWS, d), lambda i: (layer, SRC_CQ, 0)),
                  pl.BlockSpec(_row_window(SMALL_WIN2_ROWS, d), lambda i: (layer, SRC_IB, 0))],
        out_specs=pl.BlockSpec((tm, D_SLAB_SMALL), lambda i: (i, 0)),
        out_shape=jax.ShapeDtypeStruct((m, D_SLAB_SMALL), BF16),
        scratch_shapes=[pltpu.VMEM((D_SLAB_SMALL, d), BF16)],
        compiler_params=_cparams(1),
        name="input_proj_small",
    )(h2d, w_in_t, w_in_t)


def _proj_t_body(h_ref, wq_ref, wv_ref, qt_ref, vt_ref, wt_s, *, n_sub):
    t = ATTN_TILE
    hd = A_HEAD_DIM
    hv = hd + V_AUG_ROWS

    @pl.when((pl.program_id(0) == 0) & (pl.program_id(1) == 0))
    def _cast_weights():
        q_scale = A_HEAD_DIM ** -0.5 * LOG2E
        wt_s[:A_W, :] = (wq_ref[...] * q_scale).astype(wt_s.dtype)
        wt_s[A_W:, :] = wv_ref[...].astype(wt_s.dtype)

    res = lax.dot_general(wt_s[...], h_ref[...], _NT, preferred_element_type=F32)
    aug = _v_aug_rows(t, vt_ref.dtype)
    for c in range(n_sub):
        qt_ref[c] = res[:A_W, c * t:(c + 1) * t].astype(qt_ref.dtype)
        for h in range(A_HEADS):
            vt_ref[c, h * hv:h * hv + hd, :] = res[A_W + h * hd:A_W + (h + 1) * hd, c * t:(c + 1) * t
                                                   ].astype(vt_ref.dtype)
            vt_ref[c, h * hv + hd:(h + 1) * hv, :] = aug


def moba_qv_projection(h2d, w_in_t, layer, bsz, tm=512):
    m, d = h2d.shape
    assert SRC_QA % A_W == 0 and SRC_VA % A_W == 0
    t = ATTN_TILE
    tm = min(tm, m // bsz)
    n_sub = tm // t
    nt = m // bsz // t
    steps = m // bsz // tm
    vt_rows = A_HEADS * (A_HEAD_DIM + V_AUG_ROWS)
    return pl.pallas_call(
        functools.partial(_proj_t_body, n_sub=n_sub),
        grid=(bsz, steps),
        in_specs=[pl.BlockSpec((tm, d), lambda b, i: (b * steps + i, 0)),
                  pl.BlockSpec((None, A_W, d), lambda b, i: (layer, SRC_QA // A_W, 0),
                               pipeline_mode=pl.Buffered(1)),
                  pl.BlockSpec((None, A_W, d), lambda b, i: (layer, SRC_VA // A_W, 0),
                               pipeline_mode=pl.Buffered(1))],
        out_specs=[pl.BlockSpec((None, n_sub, A_W, t), lambda b, i: (b, i, 0, 0)),
                   pl.BlockSpec((None, n_sub, vt_rows, t), lambda b, i: (b, i, 0, 0))],
        out_shape=[jax.ShapeDtypeStruct((bsz, nt, A_W, t), BF16),
                   jax.ShapeDtypeStruct((bsz, nt, vt_rows, t), BF16)],
        scratch_shapes=[pltpu.VMEM((2 * A_W, d), BF16)],
        compiler_params=_cparams(2),
        name="moba_qv_proj",
    )(h2d, w_in_t, w_in_t)


V_AUG_ROWS = 16


def _v_aug_rows(t, dtype):
    return jnp.where(lax.broadcasted_iota(jnp.int32, (V_AUG_ROWS, t), 0) == 0, 1.0, 0.0).astype(dtype)


def _softmax_init(m_s, acc_s):
    m_s[...] = jnp.full(m_s.shape, -jnp.inf, F32)
    acc_s[...] = jnp.zeros(acc_s.shape, F32)


def _softmax_step_t(s, vt_blk, g, m_s, acc_s):
    m_prev = m_s[g]
    m_new = jnp.maximum(m_prev, jnp.max(s, axis=0, keepdims=True))
    alpha = jnp.exp2(m_prev - m_new)
    p = jnp.exp2(s - m_new)
    acc_s[g] = alpha * acc_s[g] + jnp.dot(vt_blk, p.astype(BF16), preferred_element_type=F32)
    m_s[g] = m_new


TILES_PER_STEP = 2


def _attn_scratch(g_heads, t, dv):
    n_state = TILES_PER_STEP * g_heads
    state = [pltpu.VMEM((n_state, 1, t), F32), pltpu.VMEM((n_state, dv + V_AUG_ROWS, t), F32)]
    return state + [pltpu.VMEM((2, t, t), F32)] * g_heads


def _tile_pair_specs(nt, block_shape, index_of_tile):
    even = pl.BlockSpec(block_shape, lambda x, y, p: index_of_tile(x, y, 2 * p))
    odd = pl.BlockSpec(block_shape, lambda x, y, p: index_of_tile(x, y, nt - 1 - 2 * p))
    return even, odd


def _skewed_blocks(lead, n_uniform, uniform_scores, consume, finish, carry_in=None, drain=True):
    if carry_in is None:
        lead[0][0](0)
        pend_blk, pend_slot = lead[0][1], 0
    else:
        prev_blk, prev_slot, prev_consume, prev_finish = carry_in
        lead[0][0](1 - prev_slot)
        prev_consume(prev_blk, prev_slot)
        prev_finish()
        pend_blk, pend_slot = lead[0][1], 1 - prev_slot
    for fn, blk in lead[1:]:
        fn(1 - pend_slot)
        consume(pend_blk, pend_slot)
        pend_blk, pend_slot = blk, 1 - pend_slot
    a, b = pend_slot, 1 - pend_slot

    def pair(first, pend):
        uniform_scores(first, b)
        consume(pend, a)
        uniform_scores(first + 1, a)
        consume(first, b)
        return first + 1

    def two_pairs(jj, pend):
        return pair(4 * jj + 2, pair(4 * jj, pend))

    n_quads = n_uniform // 4
    pend = lax.fori_loop(0, n_quads, two_pairs, pend_blk)
    rest = n_uniform - 4 * n_quads
    has_pair = rest >= 2
    first = 4 * n_quads

    @pl.when(has_pair)
    def _last_pair():
        pair(first, pend)

    pend = jnp.where(has_pair, first + 1, pend)
    if not drain:
        return pend, a, consume, finish
    last = jnp.where(has_pair, first + 2, first)

    @pl.when(rest % 2 == 1)
    def _odd_tail():
        uniform_scores(last, b)
        consume(pend, a)
        consume(last, b)
        finish()

    @pl.when(rest % 2 == 0)
    def _even_tail():
        consume(pend, a)
        finish()


def _attn_finish(z_ref, o_ref, acc_s, g_heads, dv, state0):
    for g in range(g_heads):
        out_t = acc_s[state0 + g, :dv, :] / acc_s[state0 + g, dv:dv + 1, :]
        o_ref[:, g * dv:(g + 1) * dv] = (out_t.T * _silu(z_ref[:, g * dv:(g + 1) * dv].astype(F32))
                                         ).astype(o_ref.dtype)


def _t5_thresholds():
    max_exact = T5_BUCKETS // 2
    d = np.arange(0, 4 * T5_MAX_DIST, dtype=np.int64)
    dd = np.maximum(d, 1).astype(np.float32)
    large = max_exact + (np.log(dd / np.float32(max_exact)) / np.float32(math.log(T5_MAX_DIST / max_exact))
                         * np.float32(T5_BUCKETS - max_exact)).astype(np.int32)
    large = np.minimum(large, T5_BUCKETS - 1)
    bucket = np.where(d < max_exact, d, large)
    assert np.all(np.diff(bucket) >= 0) and bucket[-1] == T5_BUCKETS - 1
    return [int(np.argmax(bucket >= k)) for k in range(T5_BUCKETS)]


_T5_THRESH = _t5_thresholds()

MASK_BIG = 1e30


def _moba_body(t5_ref, qt_even_ref, qt_odd_ref, k_ref, vt_ref, z_even_ref, z_odd_ref, o_even_ref, o_odd_ref,
               kmask_s, km_s, bias_own_s, bias_prev_s, qa_s, m_s, acc_s, *s_refs, n_blocks, g_heads):
    hg = pl.program_id(0)
    b = pl.program_id(1)
    i = pl.program_id(2)
    t, hd = A_BLOCK, A_HEAD_DIM
    hv = hd + V_AUG_ROWS
    seq = n_blocks * t
    nb_pad = -(-n_blocks // SUBLANES) * SUBLANES

    @pl.when((hg == 0) & (b == 0) & (i == 0))
    def _mask_columns():
        key_blk = lax.broadcasted_iota(jnp.int32, (seq, LANES), 0) // t
        lane = lax.broadcasted_iota(jnp.int32, (seq, LANES), 1)
        kmask_s[...] = jnp.where(lane == key_blk, -MASK_BIG, 0.0).astype(kmask_s.dtype)

    @pl.when((b == 0) & (i == 0))
    def _build_bias():
        key = lax.broadcasted_iota(jnp.int32, (t, t), 0)
        qry = lax.broadcasted_iota(jnp.int32, (t, t), 1)
        for g in range(g_heads):
            head = hg * g_heads + g
            far = t5_ref[T5_BUCKETS - 1, head]
            for dist, dst in ((qry - key, bias_own_s), (t + qry - key, bias_prev_s)):
                bias = jnp.full((t, t), t5_ref[0, head], F32)
                for kk in range(1, T5_BUCKETS):
                    bias = jnp.where(dist >= _T5_THRESH[kk], t5_ref[kk, head], bias)
                dst[g] = (bias - far) * LOG2E

    @pl.when(i == 0)
    def _new_sequence():
        for g in range(g_heads):
            km_s[g] = jnp.zeros(km_s.shape[1:], km_s.dtype)
            for nb in range(n_blocks):
                mean = jnp.mean(k_ref[nb * t:(nb + 1) * t, g * hd:(g + 1) * hd].astype(F32), axis=0, keepdims=True)
                hi = mean.astype(BF16)
                km_s[g, nb:nb + 1, :] = hi
                km_s[g, LANES + nb:LANES + nb + 1, :] = (mean - hi.astype(F32)).astype(BF16)

    _softmax_init(m_s, acc_s)
    blk_row = lax.broadcasted_iota(jnp.int32, (nb_pad, t), 0)

    def query_tile(tile, qt_ref, z_ref, o_ref, which):
        state0 = which * g_heads

        past = blk_row < tile
        for g in range(g_heads):
            qt = qt_ref[g * hd:(g + 1) * hd, :]
            g2 = jnp.dot(km_s[g], qt, preferred_element_type=F32)
            gate = g2[:nb_pad, :] + g2[LANES:LANES + nb_pad, :]
            cur = jnp.where(past, gate, NEG)
            keep = blk_row == tile
            for _ in range(A_TOPK):
                mx = jnp.max(cur, axis=0, keepdims=True)
                idx = jnp.min(jnp.where(cur == mx, blk_row, nb_pad), axis=0, keepdims=True)
                pick = blk_row == idx
                keep = keep | (pick & past)
                cur = jnp.where(pick, -jnp.inf, cur)
            unsel = jnp.where(keep, 0.0, 1.0)
            qa_s[state0 + g, :hd, :] = qt
            qa_s[state0 + g, hd:, :] = jnp.concatenate([unsel, jnp.zeros((LANES - nb_pad, t), F32)],
                                                       axis=0).astype(qa_s.dtype)

        def raw_scores(g, j):
            start = pl.multiple_of(j * t, t)
            keys = jnp.concatenate([k_ref[pl.ds(start, t), g * hd:(g + 1) * hd], kmask_s[pl.ds(start, t), :]],
                                   axis=1)
            return jnp.dot(keys, qa_s[state0 + g], preferred_element_type=F32)

        def own_scores(slot):
            key = lax.broadcasted_iota(jnp.int32, (t, t), 0)
            qry = lax.broadcasted_iota(jnp.int32, (t, t), 1)
            for g in range(g_heads):
                s_refs[g][slot] = jnp.where(key <= qry, raw_scores(g, tile) + bias_own_s[g], NEG)

        j_prev = jnp.maximum(tile - 1, 0)
        no_prev = jnp.where(tile == 0, -MASK_BIG, 0.0)

        def prev_scores(slot):
            for g in range(g_heads):
                s_refs[g][slot] = raw_scores(g, j_prev) + (bias_prev_s[g] + no_prev)

        def far_scores(j, slot):
            for g in range(g_heads):
                s_refs[g][slot] = raw_scores(g, j)

        def consume(j, slot):
            for g in range(g_heads):
                _softmax_step_t(s_refs[g][slot], vt_ref[j, g * hv:(g + 1) * hv, :], state0 + g, m_s, acc_s)

        finish = functools.partial(_attn_finish, z_ref, o_ref, acc_s, g_heads, hd, state0)
        return [(own_scores, tile), (prev_scores, j_prev)], j_prev, far_scores, consume, finish

    carry = _skewed_blocks(*query_tile(n_blocks - 1 - 2 * i, qt_odd_ref, z_odd_ref, o_odd_ref, 1), drain=False)
    _skewed_blocks(*query_tile(2 * i, qt_even_ref, z_even_ref, o_even_ref, 0), carry_in=carry)


def moba_branch(qt, vt, slab, t5_table, g_heads=ATTN_HEADS_PER_STEP):
    bsz, nb, _, t = qt.shape
    s = nb * t
    hd = A_HEAD_DIM
    gw = g_heads * hd
    assert t == A_BLOCK and nb <= LANES and nb % TILES_PER_STEP == 0 and A_HEADS % g_heads == 0
    assert OFF_KA % gw == 0 and OFF_ZA % gw == 0
    assert _T5_THRESH[T5_BUCKETS - 1] <= t + 1
    ck, cz = OFF_KA // gw, OFF_ZA // gw
    qt_even, qt_odd = _tile_pair_specs(nb, (None, None, gw, t), lambda h, b, tile: (b, tile, h, 0))
    z_even, z_odd = _tile_pair_specs(nb, (None, t, gw), lambda h, b, tile: (b, tile, cz + h))
    out = jax.ShapeDtypeStruct((bsz, nb // 2, t, A_W), BF16)
    return pl.pallas_call(
        functools.partial(_moba_body, n_blocks=nb, g_heads=g_heads),
        grid=(A_HEADS // g_heads, bsz, nb // 2),
        in_specs=[pl.BlockSpec(memory_space=pltpu.SMEM),
                  qt_even, qt_odd,
                  pl.BlockSpec((None, s, gw), lambda h, b, i: (b, 0, ck + h)),
                  pl.BlockSpec((None, nb, g_heads * (hd + V_AUG_ROWS), t), lambda h, b, i: (b, 0, h, 0)),
                  z_even, z_odd],
        out_specs=[pl.BlockSpec((None, None, t, gw), lambda h, b, i: (b, i, 0, h)),
                   pl.BlockSpec((None, None, t, gw), lambda h, b, i: (b, nb // 2 - 1 - i, 0, h))],
        out_shape=[out, out],
        scratch_shapes=[pltpu.VMEM((s, LANES), BF16),
                        pltpu.VMEM((g_heads, 2 * LANES, hd), BF16),
                        pltpu.VMEM((g_heads, t, t), F32),
                        pltpu.VMEM((g_heads, t, t), F32),
                        pltpu.VMEM((TILES_PER_STEP * g_heads, ATTN_QK_PAD, t), BF16)]
        + _attn_scratch(g_heads, t, hd),
        compiler_params=_cparams(3),
        name="moba",
    )(t5_table, qt, qt, slab, vt, slab, slab)


MLSTM_CHUNK = 256
CONV_HALO = 8


def _log_sigmoid(t):
    return jnp.minimum(t, 0.0) - jnp.log1p(jnp.exp(-jnp.abs(t)))


def _mlstm_body(qk_ref, v_ref, ob_ref, zb_ref, if_ref, cw_ref, cb_ref, gb_ref, gn_ref, o_ref,
                xe_s, c_s, n_s, m_s):
    c = pl.program_id(1)
    L = MLSTM_CHUNK
    dk, dv = B_QK_DIM, B_V_DIM

    @pl.when(c == 0)
    def _reset():
        xe_s[...] = jnp.zeros_like(xe_s)
        c_s[...] = jnp.zeros_like(c_s)
        n_s[...] = jnp.zeros_like(n_s)
        m_s[...] = jnp.zeros_like(m_s)

    x = qk_ref[...].astype(F32)
    xe = jnp.concatenate([xe_s[...], x], axis=0)
    conv = cb_ref[...] + cw_ref[B_CONV - 1:B_CONV, :] * x
    for j in range(B_CONV - 1):
        back = B_CONV - 1 - j
        conv = conv + cw_ref[j:j + 1, :] * pltpu.roll(xe, back, axis=0)[CONV_HALO:, :]
    xe_s[...] = x[L - CONV_HALO:, :]
    qk = _silu(conv)

    gi = if_ref[...].astype(F32) + gb_ref[...]
    lf = _log_sigmoid(gi)
    row = lax.broadcasted_iota(jnp.int32, (L, L), 0)
    col = lax.broadcasted_iota(jnp.int32, (L, L), 1)
    causal = col <= row
    tri = jnp.where(causal, 1.0, 0.0).astype(F32)
    b_cols = jnp.dot(tri, lf, precision=lax.Precision.HIGHEST, preferred_element_type=F32)
    b_rows = b_cols.T
    li_rows = gi.T

    for h in range(B_HEADS):
        q_h = qk[:, h * dk:(h + 1) * dk]
        k_h = qk[:, B_QK_W + h * dk:B_QK_W + (h + 1) * dk] * (dk ** -0.5)
        v_h = v_ref[:, h * dv:(h + 1) * dv]
        b_c = b_cols[:, B_HEADS + h:B_HEADS + h + 1]
        b_r = b_rows[B_HEADS + h:B_HEADS + h + 1, :]
        li_c = gi[:, h:h + 1]
        li_r = li_rows[h:h + 1, :]
        m_prev = m_s[h:h + 1, 0:1]
        c_prev = c_s[h]
        n_prev = n_s[h:h + 1, :]

        a_c = b_c + m_prev
        dmat = jnp.where(causal, b_c - b_r + li_r, -jnp.inf)
        m_t = jnp.maximum(a_c, jnp.max(dmat, axis=-1, keepdims=True))
        w_inter = jnp.exp(a_c - m_t)
        q_b = q_h.astype(BF16)
        k_b = k_h.astype(BF16)
        sc = (lax.dot_general(q_b, k_b, _NT, preferred_element_type=F32)
              * jnp.exp(dmat - m_t))
        num = (w_inter * jnp.dot(q_b, c_prev.astype(BF16), preferred_element_type=F32)
               + jnp.dot(sc.astype(BF16), v_h, preferred_element_type=F32))
        den = (w_inter * jnp.sum(q_h * n_prev, axis=-1, keepdims=True)
               + jnp.sum(sc, axis=-1, keepdims=True))
        hh = num / jnp.maximum(jnp.abs(den), jnp.exp(-m_t))

        b_last = b_c[L - 1:L, :]
        g_c = b_last - b_c + li_c
        m_new = jnp.maximum(b_last + m_prev, jnp.max(g_c, axis=0, keepdims=True))
        decay = jnp.exp(b_last + m_prev - m_new)
        wk = jnp.exp(g_c - m_new) * k_h
        c_s[h] = decay * c_prev + lax.dot_general(wk.astype(BF16), v_h, (((0,), (0,)), ((), ())),
                                                  preferred_element_type=F32)
        n_s[h:h + 1, :] = decay * n_prev + jnp.sum(wk, axis=0, keepdims=True)
        m_s[h:h + 1, :] = jnp.broadcast_to(m_new, (1, LANES))

        mu = jnp.mean(hh, axis=-1, keepdims=True)
        dlt = hh - mu
        var = jnp.mean(dlt * dlt, axis=-1, keepdims=True)
        y = dlt * lax.rsqrt(var + EPS) * gn_ref[:, h * dv:(h + 1) * dv]
        y = (y * jax.nn.sigmoid(ob_ref[:, h * dv:(h + 1) * dv].astype(F32))
             * _silu(zb_ref[:, h * dv:(h + 1) * dv].astype(F32)))
        o_ref[:, h * dv:(h + 1) * dv] = y.astype(o_ref.dtype)


def mlstm_branch(slab, slab_small, conv_w, conv_b, i_bias, f_bias, out_norm):
    bsz, s, _ = slab.shape
    L = MLSTM_CHUNK
    assert s % L == 0
    gate_bias = jnp.zeros((1, LANES), F32).at[0, :B_HEADS].set(i_bias).at[0, B_HEADS:2 * B_HEADS].set(f_bias)
    w2 = 2 * B_QK_W
    full = lambda shape: pl.BlockSpec(shape, lambda b, c: (0,) * len(shape))
    return pl.pallas_call(
        _mlstm_body,
        grid=(bsz, s // L),
        in_specs=[pl.BlockSpec((None, L, w2), lambda b, c: (b, c, OFF_QKB // w2)),
                  pl.BlockSpec((None, L, B_V_W), lambda b, c: (b, c, OFF_VB // B_V_W)),
                  pl.BlockSpec((None, L, B_V_W), lambda b, c: (b, c, OFF_OB // B_V_W)),
                  pl.BlockSpec((None, L, B_V_W), lambda b, c: (b, c, OFF_ZB // B_V_W)),
                  pl.BlockSpec((None, L, LANES), lambda b, c: (b, c, OFF_IF // LANES)),
                  full((B_CONV, w2)), full((1, w2)), full((1, LANES)), full((1, B_V_W))],
        out_specs=pl.BlockSpec((None, L, B_V_W), lambda b, c: (b, c, 0)),
        out_shape=jax.ShapeDtypeStruct((bsz, s, B_V_W), BF16),
        scratch_shapes=[pltpu.VMEM((CONV_HALO, w2), F32),
                        pltpu.VMEM((B_HEADS, B_QK_DIM, B_V_DIM), F32),
                        pltpu.VMEM((8, B_QK_DIM), F32),
                        pltpu.VMEM((8, LANES), F32)],
        compiler_params=_cparams(2),
        name="mlstm",
    )(slab, slab, slab, slab, slab_small, conv_w, conv_b.reshape(1, w2), gate_bias,
      out_norm.reshape(1, B_V_W))


def _mla_prep_body(cq_ref, ckv_ref, kr_ref, qg_ref, kg_ref, wqt_ref, wqst_ref, wkn_ref, wvt_ref,
                   cos_ref, sin_ref, cost_ref, sint_ref, qt_ref, kf_ref, vt_ref):
    def normed(ref, g_ref):
        t = ref[...].astype(F32)
        y = t * lax.rsqrt(jnp.mean(t * t, axis=-1, keepdims=True) + EPS)
        return (y * g_ref[...]).astype(BF16)

    cqn = normed(cq_ref, qg_ref)
    ckvn = normed(ckv_ref, kg_ref)

    qt_main = lax.dot_general(wqt_ref[...], cqn, _NT, preferred_element_type=F32)
    qt_swap = lax.dot_general(wqst_ref[...], cqn, _NT, preferred_element_type=F32)
    cos_t = cost_ref[...]
    sin_t = sint_ref[...]
    t = ATTN_TILE
    n_sub = cos_t.shape[1] // t
    for h in range(C_HEADS):
        lo = h * ATTN_QK_PAD
        nope = qt_main[lo:lo + LANES, :].astype(qt_ref.dtype)
        rope = (qt_main[lo + LANES:lo + 2 * LANES, :] * cos_t
                + qt_swap[h * LANES:(h + 1) * LANES, :] * sin_t).astype(qt_ref.dtype)
        for c in range(n_sub):
            qt_ref[c, lo:lo + LANES, :] = nope[:, c * t:(c + 1) * t]
            qt_ref[c, lo + LANES:lo + 2 * LANES, :] = rope[:, c * t:(c + 1) * t]

    k_nope = jnp.dot(ckvn, wkn_ref[...], preferred_element_type=F32)
    kr = kr_ref[...].astype(F32)
    half = C_ROPE // 2
    lane = lax.broadcasted_iota(jnp.int32, kr.shape, 1)
    swapped = jnp.where(lane < half, -pltpu.roll(kr, LANES - half, axis=1), pltpu.roll(kr, half, axis=1))
    k_rot = (kr * cos_ref[...] + swapped * sin_ref[...]).astype(kf_ref.dtype)
    for h in range(C_HEADS):
        lo = h * ATTN_QK_PAD
        kf_ref[:, lo:lo + LANES] = k_nope[:, h * LANES:(h + 1) * LANES].astype(kf_ref.dtype)
        kf_ref[:, lo + LANES:lo + 2 * LANES] = k_rot

    v_t = lax.dot_general(wvt_ref[...], ckvn, _NT, preferred_element_type=F32)
    hv = C_V_DIM + V_AUG_ROWS
    aug = _v_aug_rows(t, vt_ref.dtype)
    for h in range(C_HEADS):
        v_h = v_t[h * C_V_DIM:(h + 1) * C_V_DIM, :].astype(vt_ref.dtype)
        for c in range(n_sub):
            vt_ref[c, h * hv:h * hv + C_V_DIM, :] = v_h[:, c * t:(c + 1) * t]
            vt_ref[c, h * hv + C_V_DIM:(h + 1) * hv, :] = aug


def mla_prep(slab, q_norm, kv_norm, w_uq, w_ukv, tm=512):
    bsz, s, _ = slab.shape
    t = ATTN_TILE
    n_sub = tm // t
    vt_rows = C_HEADS * (C_V_DIM + V_AUG_ROWS)
    half = C_ROPE // 2
    scale = (C_NOPE + C_ROPE) ** -0.5 * LOG2E
    wq = (w_uq * scale).reshape(C_Q_RANK, C_HEADS, C_NOPE + C_ROPE)
    pad = jnp.zeros((C_Q_RANK, C_HEADS, LANES - C_ROPE), F32)
    x1, x2 = wq[..., C_NOPE:C_NOPE + half], wq[..., C_NOPE + half:]
    wqt = jnp.concatenate([wq, pad], axis=-1).reshape(C_Q_RANK, C_HEADS * ATTN_QK_PAD).T.astype(BF16)
    wqst = jnp.concatenate([-x2, x1, pad], axis=-1).reshape(C_Q_RANK, C_HEADS * LANES).T.astype(BF16)
    wkv = w_ukv.reshape(C_KV_RANK, C_HEADS, C_NOPE + C_V_DIM)
    wkn = wkv[..., :C_NOPE].reshape(C_KV_RANK, C_HEADS * C_NOPE).astype(BF16)
    wvt = wkv[..., C_NOPE:].reshape(C_KV_RANK, C_W).T.astype(BF16)

    pos = jnp.arange(s, dtype=jnp.int32)
    inv = ROPE_THETA ** (-jnp.arange(half, dtype=F32) / half)
    ang = pos.astype(F32)[:, None] * inv[None, :]
    zpad = jnp.zeros((s, LANES - C_ROPE), F32)
    cos_tab = jnp.concatenate([jnp.cos(ang), jnp.cos(ang), zpad], axis=-1)
    sin_tab = jnp.concatenate([jnp.sin(ang), jnp.sin(ang), zpad], axis=-1)

    nt = s // t
    full = lambda shape: pl.BlockSpec(shape, lambda b, i: (0,) * len(shape))
    return pl.pallas_call(
        _mla_prep_body,
        grid=(bsz, s // tm),
        in_specs=[pl.BlockSpec((None, tm, C_Q_RANK), lambda b, i: (b, i, OFF_CQ // C_Q_RANK)),
                  pl.BlockSpec((None, tm, C_KV_RANK), lambda b, i: (b, i, OFF_CKV // C_KV_RANK)),
                  pl.BlockSpec((None, tm, LANES), lambda b, i: (b, i, OFF_KR // LANES)),
                  full((1, C_Q_RANK)), full((1, C_KV_RANK)),
                  full(wqt.shape), full(wqst.shape), full(wkn.shape), full(wvt.shape),
                  pl.BlockSpec((tm, LANES), lambda b, i: (i, 0)),
                  pl.BlockSpec((tm, LANES), lambda b, i: (i, 0)),
                  pl.BlockSpec((LANES, tm), lambda b, i: (0, i)),
                  pl.BlockSpec((LANES, tm), lambda b, i: (0, i))],
        out_specs=[pl.BlockSpec((None, n_sub, C_HEADS * ATTN_QK_PAD, t), lambda b, i: (b, i, 0, 0)),
                   pl.BlockSpec((None, tm, C_HEADS * ATTN_QK_PAD), lambda b, i: (b, i, 0)),
                   pl.BlockSpec((None, n_sub, vt_rows, t), lambda b, i: (b, i, 0, 0))],
        out_shape=[jax.ShapeDtypeStruct((bsz, nt, C_HEADS * ATTN_QK_PAD, t), BF16),
                   jax.ShapeDtypeStruct((bsz, s, C_HEADS * ATTN_QK_PAD), BF16),
                   jax.ShapeDtypeStruct((bsz, nt, vt_rows, t), BF16)],
        compiler_params=_cparams(2),
        name="mla_prep",
    )(slab, slab, slab, q_norm.reshape(1, C_Q_RANK), kv_norm.reshape(1, C_KV_RANK),
      wqt, wqst, wkn, wvt, cos_tab, sin_tab, cos_tab.T, sin_tab.T)


def _mla_attn_body(qt_even_ref, qt_odd_ref, k_ref, vt_ref, z_even_ref, z_odd_ref, o_even_ref, o_odd_ref,
                   m_s, acc_s, *s_refs, g_heads, n_tiles):
    i = pl.program_id(2)
    t = ATTN_TILE
    dq, dv = ATTN_QK_PAD, C_V_DIM
    hv = dv + V_AUG_ROWS
    _softmax_init(m_s, acc_s)

    def query_tile(tile, qt_ref, z_ref, o_ref, which):
        state0 = which * g_heads

        def raw_scores(g, j):
            start = pl.multiple_of(j * t, t)
            return jnp.dot(k_ref[pl.ds(start, t), g * dq:(g + 1) * dq], qt_ref[g * dq:(g + 1) * dq, :],
                           preferred_element_type=F32)

        def diagonal_scores(slot):
            key = lax.broadcasted_iota(jnp.int32, (t, t), 0)
            qry = lax.broadcasted_iota(jnp.int32, (t, t), 1)
            for g in range(g_heads):
                s_refs[g][slot] = jnp.where(key <= qry, raw_scores(g, tile), NEG)

        def past_scores(j, slot):
            for g in range(g_heads):
                s_refs[g][slot] = raw_scores(g, j)

        def consume(j, slot):
            for g in range(g_heads):
                _softmax_step_t(s_refs[g][slot], vt_ref[j, g * hv:(g + 1) * hv, :], state0 + g, m_s, acc_s)

        finish = functools.partial(_attn_finish, z_ref, o_ref, acc_s, g_heads, dv, state0)
        return [(diagonal_scores, tile)], tile, past_scores, consume, finish

    carry = _skewed_blocks(*query_tile(2 * i, qt_even_ref, z_even_ref, o_even_ref, 0), drain=False)
    _skewed_blocks(*query_tile(n_tiles - 1 - 2 * i, qt_odd_ref, z_odd_ref, o_odd_ref, 1), carry_in=carry)


def mla_attention(qt, kf, vt, slab, g_heads=ATTN_HEADS_PER_STEP):
    bsz, nt, _, t = qt.shape
    s = nt * t
    dq, dv = ATTN_QK_PAD, C_V_DIM
    assert t == ATTN_TILE and nt % TILES_PER_STEP == 0
    assert C_HEADS % g_heads == 0 and OFF_ZC % (g_heads * dv) == 0
    cz = OFF_ZC // (g_heads * dv)
    qt_even, qt_odd = _tile_pair_specs(nt, (None, None, g_heads * dq, t), lambda b, h, tile: (b, tile, h, 0))
    z_even, z_odd = _tile_pair_specs(nt, (None, t, g_heads * dv), lambda b, h, tile: (b, tile, cz + h))
    out = jax.ShapeDtypeStruct((bsz, nt // 2, t, C_W), BF16)
    return pl.pallas_call(
        functools.partial(_mla_attn_body, g_heads=g_heads, n_tiles=nt),
        grid=(bsz, C_HEADS // g_heads, nt // 2),
        in_specs=[qt_even, qt_odd,
                  pl.BlockSpec((None, s, g_heads * dq), lambda b, h, i: (b, 0, h),
                               pipeline_mode=pl.Buffered(1)),
                  pl.BlockSpec((None, nt, g_heads * (dv + V_AUG_ROWS), t), lambda b, h, i: (b, 0, h, 0)),
                  z_even, z_odd],
        out_specs=[pl.BlockSpec((None, None, t, g_heads * dv), lambda b, h, i: (b, i, 0, h)),
                   pl.BlockSpec((None, None, t, g_heads * dv), lambda b, h, i: (b, nt // 2 - 1 - i, 0, h))],
        out_shape=[out, out],
        scratch_shapes=_attn_scratch(g_heads, t, dv),
        compiler_params=_cparams(3),
        name="mla_attn",
    )(qt, qt, kf, vt, slab, slab)


def _merge_body(ya_even_ref, ya_odd_ref, yb_ref, yc_even_ref, yc_odd_ref, wa_ref, wb_ref, wc_ref,
                ga_ref, gb_ref, gc_ref, o_ref, w_s):
    @pl.when(pl.program_id(1) == 0)
    def _cast_weights():
        for n, w_ref in enumerate((wa_ref, wb_ref, wc_ref)):
            w_s[n] = w_ref[...].astype(w_s.dtype)

    def rows(even_ref, odd_ref):
        return jnp.concatenate([ref[p] for p in range(even_ref.shape[0]) for ref in (even_ref, odd_ref)], axis=0)

    ya = rows(ya_even_ref, ya_odd_ref)
    yc = rows(yc_even_ref, yc_odd_ref)
    acc = None
    for n, (y, g_ref) in enumerate(((ya, ga_ref), (yb_ref[...], gb_ref), (yc, gc_ref))):
        term = jax.nn.sigmoid(g_ref[...].astype(F32)) * jnp.dot(y, w_s[n], preferred_element_type=F32)
        acc = term if acc is None else acc + term
    o_ref[...] = acc.astype(o_ref.dtype)


def branch_merge(ya_tiles, yb, yc_tiles, w_branch, layer, slab2d, tm=1024, tn=1024):
    m, w = yb.shape
    d = w_branch.shape[-1]
    t = ATTN_TILE
    pairs = tm // (TILES_PER_STEP * t)
    assert OFF_GT % tn == 0 and pairs * TILES_PER_STEP * t == tm
    g0 = OFF_GT // tn
    gper = d // tn
    halves = [y.reshape(m // (TILES_PER_STEP * t), t, w) for y in (*ya_tiles, *yc_tiles)]
    half_spec = pl.BlockSpec((pairs, t, w), lambda j, i: (i, 0, 0))
    w_specs = [pl.BlockSpec((None, None, w, tn), functools.partial(lambda j, i, n: (layer, n, 0, j), n=n),
                            pipeline_mode=pl.Buffered(1))
               for n in range(N_BRANCH)]
    g_specs = [pl.BlockSpec((tm, tn), functools.partial(lambda j, i, n: (i, g0 + n * gper + j), n=n))
               for n in range(N_BRANCH)]
    return pl.pallas_call(
        _merge_body,
        grid=(d // tn, m // tm),
        in_specs=[half_spec, half_spec, pl.BlockSpec((tm, w), lambda j, i: (i, 0)), half_spec, half_spec]
        + w_specs + g_specs,
        out_specs=pl.BlockSpec((tm, tn), lambda j, i: (i, j)),
        out_shape=jax.ShapeDtypeStruct((m, d), BF16),
        scratch_shapes=[pltpu.VMEM((N_BRANCH, w, tn), BF16)],
        compiler_params=_cparams(2),
        name="branch_merge",
    )(halves[0], halves[1], yb, halves[2], halves[3], w_branch, w_branch, w_branch, slab2d, slab2d, slab2d)


def _out_body(mg_ref, w_ref, x_ref, g_ref, *refs, last):
    out_refs, w_s = refs[:-1], refs[-1]

    @pl.when(pl.program_id(0) == 0)
    def _cast_weights():
        w_s[...] = w_ref[...].astype(w_s.dtype)

    x_new = x_ref[...] + jnp.dot(mg_ref[...], w_s[...], preferred_element_type=F32)
    y = x_new * lax.rsqrt(jnp.mean(x_new * x_new, axis=-1, keepdims=True) + EPS) * g_ref[...]
    if last:
        out_refs[0][...] = y
    else:
        out_refs[0][...] = x_new
        out_refs[1][...] = y.astype(out_refs[1].dtype)


def out_projection(merged, w_out, layer, x2d, gain, last, tm=512):
    m, d = x2d.shape
    row = pl.BlockSpec((tm, d), lambda i: (i, 0))
    if last:
        out_specs, out_shape = row, jax.ShapeDtypeStruct((m, d), F32)
    else:
        out_specs = [row, row]
        out_shape = [jax.ShapeDtypeStruct((m, d), F32), jax.ShapeDtypeStruct((m, d), BF16)]
    return pl.pallas_call(
        functools.partial(_out_body, last=last),
        grid=(m // tm,),
        in_specs=[row,
                  pl.BlockSpec((None, d, d), lambda i: (layer, 0, 0), pipeline_mode=pl.Buffered(1)),
                  row, pl.BlockSpec((1, d), lambda i: (0, 0))],
        out_specs=out_specs,
        out_shape=out_shape,
        scratch_shapes=[pltpu.VMEM((d, d), BF16)],
        compiler_params=_cparams(1),
        name="out_proj",
    )(merged, w_out, x2d, gain.reshape(1, d))


def kernel(x, norm_gain, w_in, t5_table, mlstm_conv_w, mlstm_conv_b, mlstm_i_bias, mlstm_f_bias,
           mlstm_out_norm, mla_q_norm, mla_kv_norm, mla_w_uq, mla_w_ukv, w_branch, w_out, final_norm):
    bsz, s, d = x.shape
    m = bsz * s
    x2d = x.reshape(m, d)
    h = rmsnorm_rows(x2d, norm_gain[0])
    w_in_t = jnp.swapaxes(w_in, 1, 2)
    out = None
    for l in range(DEPTH):
        slab2d = input_projection(h, w_in_t, l, SLAB_SRC_ROWS)
        slab = slab2d.reshape(bsz, s, D_SLAB)
        slab_small = input_projection_small(h, w_in_t, l).reshape(bsz, s, D_SLAB_SMALL)
        qt_a, vt_a = moba_qv_projection(h, w_in_t, l, bsz)
        ya = moba_branch(qt_a, vt_a, slab, t5_table)
        yb = mlstm_branch(slab, slab_small, mlstm_conv_w[l], mlstm_conv_b[l], mlstm_i_bias[l], mlstm_f_bias[l],
                          mlstm_out_norm[l])
        qt_c, k_c, vt_c = mla_prep(slab_small, mla_q_norm[l], mla_kv_norm[l], mla_w_uq[l], mla_w_ukv[l])
        yc = mla_attention(qt_c, k_c, vt_c, slab)
        merged = branch_merge(ya, yb.reshape(m, B_V_W), yc, w_branch, l, slab2d)
        last = l == DEPTH - 1
        gain = final_norm if last else norm_gain[l + 1]
        res = out_projection(merged, w_out, l, x2d, gain, last)
        if last:
            out = res
        else:
            x2d, h = res
    return out.reshape(bsz, s, d)
```

```python
import functools
import math

import jax
import jax.numpy as jnp
import numpy as np
from jax import lax
from jax.experimental import pallas as pl
from jax.experimental.pallas import tpu as pltpu

F32 = jnp.float32
BF16 = jnp.bfloat16

D_MODEL = 2048
DEPTH = 2
EPS = 1e-6
NEG = -1e30
LOG2E = math.log2(math.e)

A_HEADS = 8
A_HEAD_DIM = 128
A_BLOCK = 256
A_TOPK = 3
T5_BUCKETS = 32
T5_MAX_DIST = 128
B_HEADS = 4
B_QK_DIM = 128
B_V_DIM = 256
B_CONV = 4
C_HEADS = 8
C_Q_RANK = 512
C_KV_RANK = 256
C_NOPE = 128
C_ROPE = 64
C_V_DIM = 128
ROPE_THETA = 10000.0
N_BRANCH = 3

A_W = A_HEADS * A_HEAD_DIM
B_QK_W = B_HEADS * B_QK_DIM
B_V_W = B_HEADS * B_V_DIM
C_W = C_HEADS * C_V_DIM

LANES = 128
SUBLANES = 8
VMEM_LIMIT_BYTES = 58 * 1024 * 1024

SPLIT_SIZES = (A_W, A_W, A_W, A_W, B_QK_W, B_QK_W, B_V_W, B_HEADS, B_HEADS, B_V_W, B_V_W,
               C_Q_RANK, C_KV_RANK, C_ROPE, C_W, N_BRANCH * D_MODEL)
(SRC_QA, SRC_KA, SRC_VA, SRC_ZA, SRC_QB, SRC_KB, SRC_VB, SRC_IB, SRC_FB, SRC_OB, SRC_ZB,
 SRC_CQ, SRC_CKV, SRC_KR, SRC_ZC, SRC_GT) = (int(v) for v in np.cumsum((0,) + SPLIT_SIZES[:-1]))
D_IN = int(sum(SPLIT_SIZES))

W_BLOCK = 1024
SLAB_SRC_ROWS = ((SRC_KA, SRC_ZA, SRC_QB, SRC_VB, SRC_OB, SRC_ZB, SRC_ZC)
                 + tuple(SRC_GT + k * W_BLOCK for k in range(N_BRANCH * D_MODEL // W_BLOCK)))
assert SRC_KB == SRC_QB + B_QK_W and all(r % SUBLANES == 0 for r in SLAB_SRC_ROWS)
OFF_KA = 0
OFF_ZA = OFF_KA + W_BLOCK
OFF_QKB = OFF_ZA + W_BLOCK
OFF_VB = OFF_QKB + W_BLOCK
OFF_OB = OFF_VB + W_BLOCK
OFF_ZB = OFF_OB + W_BLOCK
OFF_ZC = OFF_ZB + W_BLOCK
OFF_GT = OFF_ZC + W_BLOCK
D_SLAB = len(SLAB_SRC_ROWS) * W_BLOCK
SMALL_WIN1_ROWS = C_Q_RANK + C_KV_RANK + LANES
SMALL_WIN2_ROWS = LANES
assert SRC_CKV == SRC_CQ + C_Q_RANK and SRC_KR == SRC_CKV + C_KV_RANK and SRC_FB == SRC_IB + B_HEADS
assert SRC_CQ % SUBLANES == 0 and SRC_IB % SUBLANES == 0
assert SRC_CQ + SMALL_WIN1_ROWS <= D_IN and SRC_IB + SMALL_WIN2_ROWS <= D_IN
OFF_CQ = 0
OFF_CKV = OFF_CQ + C_Q_RANK
OFF_KR = OFF_CKV + C_KV_RANK
OFF_IF = OFF_KR + LANES
D_SLAB_SMALL = SMALL_WIN1_ROWS + SMALL_WIN2_ROWS
assert D_SLAB_SMALL == W_BLOCK

ATTN_TILE = 256
ATTN_QK_PAD = 2 * LANES
ATTN_HEADS_PER_STEP = 8
assert ATTN_TILE == A_BLOCK


def _cparams(n_axes):
    return pltpu.CompilerParams(dimension_semantics=("arbitrary",) * n_axes,
                                vmem_limit_bytes=VMEM_LIMIT_BYTES)


def _silu(t):
    return t * jax.nn.sigmoid(t)


_NT = (((1,), (1,)), ((), ()))


def _rmsnorm_body(x_ref, g_ref, o_ref):
    xf = x_ref[...]
    y = xf * lax.rsqrt(jnp.mean(xf * xf, axis=-1, keepdims=True) + EPS)
    o_ref[...] = (y * g_ref[...]).astype(o_ref.dtype)


def rmsnorm_rows(x2d, gain, tm=1024):
    m, d = x2d.shape
    return pl.pallas_call(
        _rmsnorm_body,
        grid=(m // tm,),
        in_specs=[pl.BlockSpec((tm, d), lambda i: (i, 0)),
                  pl.BlockSpec((1, d), lambda i: (0, 0))],
        out_specs=pl.BlockSpec((tm, d), lambda i: (i, 0)),
        out_shape=jax.ShapeDtypeStruct((m, d), BF16),
        compiler_params=_cparams(1),
        name="rmsnorm",
    )(x2d, gain.reshape(1, d))


def _row_window(rows, d):
    return (pl.Squeezed(), pl.Element(rows), pl.Element(d))


def _proj_body(rows_ref, h_ref, w_ref, o_ref, wb_s):
    del rows_ref
    @pl.when(pl.program_id(1) == 0)
    def _cast_weights():
        for c in range(0, w_ref.shape[0], LANES):
            wb_s[:, c:c + LANES] = w_ref[c:c + LANES, :].T.astype(wb_s.dtype)

    o_ref[...] = jnp.dot(h_ref[...], wb_s[...], preferred_element_type=F32).astype(o_ref.dtype)


def input_projection(h2d, w_in_t, layer, src_rows, tm=2048):
    m, d = h2d.shape
    tn = W_BLOCK
    grid_spec = pltpu.PrefetchScalarGridSpec(
        num_scalar_prefetch=1,
        grid=(len(src_rows), m // tm),
        in_specs=[pl.BlockSpec((tm, d), lambda j, i, rows: (i, 0)),
                  pl.BlockSpec(_row_window(tn, d),
                               lambda j, i, rows: (layer, pl.multiple_of(rows[j], SUBLANES), 0))],
        out_specs=pl.BlockSpec((tm, tn), lambda j, i, rows: (i, j)),
        scratch_shapes=[pltpu.VMEM((d, tn), BF16)])
    return pl.pallas_call(
        _proj_body,
        grid_spec=grid_spec,
        out_shape=jax.ShapeDtypeStruct((m, len(src_rows) * tn), BF16),
        compiler_params=_cparams(2),
        name="input_proj",
    )(jnp.asarray(src_rows, jnp.int32), h2d, w_in_t)


def _proj_small_body(h_ref, w1_ref, w2_ref, o_ref, wb_s):
    @pl.when(pl.program_id(0) == 0)
    def _cast_weights():
        wb_s[:SMALL_WIN1_ROWS, :] = w1_ref[...].astype(wb_s.dtype)
        wb_s[SMALL_WIN1_ROWS:, :] = w2_ref[...].astype(wb_s.dtype)

    o_ref[...] = lax.dot_general(h_ref[...], wb_s[...], _NT, preferred_element_type=F32).astype(o_ref.dtype)


def input_projection_small(h2d, w_in_t, layer, tm=2048):
    m, d = h2d.shape
    return pl.pallas_call(
        _proj_small_body,
        grid=(m // tm,),
        in_specs=[pl.BlockSpec((tm, d), lambda i: (i, 0)),
                  pl.BlockSpec(_row_window(SMALL_WIN1_ROWS, d), lambda i: (layer, SRC_CQ, 0)),
                  pl.BlockSpec(_row_window(SMALL_WIN2_ROWS, d), lambda i: (layer, SRC_IB, 0))],
        out_specs=pl.BlockSpec((tm, D_SLAB_SMALL), lambda i: (i, 0)),
        out_shape=jax.ShapeDtypeStruct((m, D_SLAB_SMALL), BF16),
        scratch_shapes=[pltpu.VMEM((D_SLAB_SMALL, d), BF16)],
        compiler_params=_cparams(1),
        name="input_proj_small",
    )(h2d, w_in_t, w_in_t)


def _proj_t_body(h_ref, wq_ref, wv_ref, qt_ref, vt_ref, wt_s, *, n_sub):
    t = ATTN_TILE
    hd = A_HEAD_DIM
    hv = hd + V_AUG_ROWS

    @pl.when((pl.program_id(0) == 0) & (pl.program_id(1) == 0))
    def _cast_weights():
        q_scale = A_HEAD_DIM ** -0.5 * LOG2E
        wt_s[:A_W, :] = (wq_ref[...] * q_scale).astype(wt_s.dtype)
        wt_s[A_W:, :] = wv_ref[...].astype(wt_s.dtype)

    res = lax.dot_general(wt_s[...], h_ref[...], _NT, preferred_element_type=F32)
    aug = _v_aug_rows(t, vt_ref.dtype)
    for c in range(n_sub):
        qt_ref[c] = res[:A_W, c * t:(c + 1) * t].astype(qt_ref.dtype)
        for h in range(A_HEADS):
            vt_ref[c, h * hv:h * hv + hd, :] = res[A_W + h * hd:A_W + (h + 1) * hd, c * t:(c + 1) * t
                                                   ].astype(vt_ref.dtype)
            vt_ref[c, h * hv + hd:(h + 1) * hv, :] = aug


def moba_qv_projection(h2d, w_in_t, layer, bsz, tm=1024):
    m, d = h2d.shape
    assert SRC_QA % A_W == 0 and SRC_VA % A_W == 0
    t = ATTN_TILE
    tm = min(tm, m // bsz)
    n_sub = tm // t
    nt = m // bsz // t
    steps = m // bsz // tm
    vt_rows = A_HEADS * (A_HEAD_DIM + V_AUG_ROWS)
    return pl.pallas_call(
        functools.partial(_proj_t_body, n_sub=n_sub),
        grid=(bsz, steps),
        in_specs=[pl.BlockSpec((tm, d), lambda b, i: (b * steps + i, 0)),
                  pl.BlockSpec((None, A_W, d), lambda b, i: (layer, SRC_QA // A_W, 0),
                               pipeline_mode=pl.Buffered(1)),
                  pl.BlockSpec((None, A_W, d), lambda b, i: (layer, SRC_VA // A_W, 0),
                               pipeline_mode=pl.Buffered(1))],
        out_specs=[pl.BlockSpec((None, n_sub, A_W, t), lambda b, i: (b, i, 0, 0)),
                   pl.BlockSpec((None, n_sub, vt_rows, t), lambda b, i: (b, i, 0, 0))],
        out_shape=[jax.ShapeDtypeStruct((bsz, nt, A_W, t), BF16),
                   jax.ShapeDtypeStruct((bsz, nt, vt_rows, t), BF16)],
        scratch_shapes=[pltpu.VMEM((2 * A_W, d), BF16)],
        compiler_params=_cparams(2),
        name="moba_qv_proj",
    )(h2d, w_in_t, w_in_t)


V_AUG_ROWS = 16


def _v_aug_rows(t, dtype):
    return jnp.where(lax.broadcasted_iota(jnp.int32, (V_AUG_ROWS, t), 0) == 0, 1.0, 0.0).astype(dtype)


def _softmax_init(m_s, acc_s):
    m_s[...] = jnp.full(m_s.shape, -jnp.inf, F32)
    acc_s[...] = jnp.zeros(acc_s.shape, F32)


def _softmax_step_t(s, vt_blk, g, m_s, acc_s):
    m_prev = m_s[g]
    m_new = jnp.maximum(m_prev, jnp.max(s, axis=0, keepdims=True))
    alpha = jnp.exp2(m_prev - m_new)
    p = jnp.exp2(s - m_new)
    acc_s[g] = alpha * acc_s[g] + jnp.dot(vt_blk, p.astype(BF16), preferred_element_type=F32)
    m_s[g] = m_new


TILES_PER_STEP = 2


def _attn_scratch(g_heads, t, dv):
    n_state = TILES_PER_STEP * g_heads
    state = [pltpu.VMEM((n_state, 1, t), F32), pltpu.VMEM((n_state, dv + V_AUG_ROWS, t), F32)]
    return state + [pltpu.VMEM((2, t, t), F32)] * g_heads


def _tile_pair_specs(nt, block_shape, index_of_tile):
    even = pl.BlockSpec(block_shape, lambda x, y, p: index_of_tile(x, y, 2 * p))
    odd = pl.BlockSpec(block_shape, lambda x, y, p: index_of_tile(x, y, nt - 1 - 2 * p))
    return even, odd


def _skewed_blocks(lead, n_uniform, uniform_scores, consume, finish, carry_in=None, drain=True):
    if carry_in is None:
        lead[0][0](0)
        pend_blk, pend_slot = lead[0][1], 0
    else:
        prev_blk, prev_slot, prev_consume, prev_finish = carry_in
        lead[0][0](1 - prev_slot)
        prev_consume(prev_blk, prev_slot)
        prev_finish()
        pend_blk, pend_slot = lead[0][1], 1 - prev_slot
    for fn, blk in lead[1:]:
        fn(1 - pend_slot)
        consume(pend_blk, pend_slot)
        pend_blk, pend_slot = blk, 1 - pend_slot
    a, b = pend_slot, 1 - pend_slot

    def pair(first, pend):
        uniform_scores(first, b)
        consume(pend, a)
        uniform_scores(first + 1, a)
        consume(first, b)
        return first + 1

    def two_pairs(jj, pend):
        return pair(4 * jj + 2, pair(4 * jj, pend))

    n_quads = n_uniform // 4
    pend = lax.fori_loop(0, n_quads, two_pairs, pend_blk)
    rest = n_uniform - 4 * n_quads
    has_pair = rest >= 2
    first = 4 * n_quads

    @pl.when(has_pair)
    def _last_pair():
        pair(first, pend)

    pend = jnp.where(has_pair, first + 1, pend)
    if not drain:
        return pend, a, consume, finish
    last = jnp.where(has_pair, first + 2, first)

    @pl.when(rest % 2 == 1)
    def _odd_tail():
        uniform_scores(last, b)
        consume(pend, a)
        consume(last, b)
        finish()

    @pl.when(rest % 2 == 0)
    def _even_tail():
        consume(pend, a)
        finish()


def _attn_finish(z_ref, o_ref, acc_s, g_heads, dv, state0):
    for g in range(g_heads):
        out_t = acc_s[state0 + g, :dv, :] / acc_s[state0 + g, dv:dv + 1, :]
        o_ref[:, g * dv:(g + 1) * dv] = (out_t.T * _silu(z_ref[:, g * dv:(g + 1) * dv].astype(F32))
                                         ).astype(o_ref.dtype)


def _t5_thresholds():
    max_exact = T5_BUCKETS // 2
    d = np.arange(0, 4 * T5_MAX_DIST, dtype=np.int64)
    dd = np.maximum(d, 1).astype(np.float32)
    large = max_exact + (np.log(dd / np.float32(max_exact)) / np.float32(math.log(T5_MAX_DIST / max_exact))
                         * np.float32(T5_BUCKETS - max_exact)).astype(np.int32)
    large = np.minimum(large, T5_BUCKETS - 1)
    bucket = np.where(d < max_exact, d, large)
    assert np.all(np.diff(bucket) >= 0) and bucket[-1] == T5_BUCKETS - 1
    return [int(np.argmax(bucket >= k)) for k in range(T5_BUCKETS)]


_T5_THRESH = _t5_thresholds()

MASK_BIG = 1e30


def _moba_body(t5_ref, qt_even_ref, qt_odd_ref, k_ref, vt_ref, z_even_ref, z_odd_ref, o_even_ref, o_odd_ref,
               kmask_s, km_s, bias_own_s, bias_prev_s, qa_s, m_s, acc_s, *s_refs, n_blocks, g_heads):
    hg = pl.program_id(0)
    b = pl.program_id(1)
    i = pl.program_id(2)
    t, hd = A_BLOCK, A_HEAD_DIM
    hv = hd + V_AUG_ROWS
    seq = n_blocks * t
    nb_pad = -(-n_blocks // SUBLANES) * SUBLANES

    @pl.when((hg == 0) & (b == 0) & (i == 0))
    def _mask_columns():
        key_blk = lax.broadcasted_iota(jnp.int32, (seq, LANES), 0) // t
        lane = lax.broadcasted_iota(jnp.int32, (seq, LANES), 1)
        kmask_s[...] = jnp.where(lane == key_blk, -MASK_BIG, 0.0).astype(kmask_s.dtype)

    @pl.when((b == 0) & (i == 0))
    def _build_bias():
        key = lax.broadcasted_iota(jnp.int32, (t, t), 0)
        qry = lax.broadcasted_iota(jnp.int32, (t, t), 1)
        for g in range(g_heads):
            head = hg * g_heads + g
            far = t5_ref[T5_BUCKETS - 1, head]
            for dist, dst in ((qry - key, bias_own_s), (t + qry - key, bias_prev_s)):
                bias = jnp.full((t, t), t5_ref[0, head], F32)
                for kk in range(1, T5_BUCKETS):
                    bias = jnp.where(dist >= _T5_THRESH[kk], t5_ref[kk, head], bias)
                dst[g] = (bias - far) * LOG2E

    @pl.when(i == 0)
    def _new_sequence():
        for g in range(g_heads):
            km_s[g] = jnp.zeros(km_s.shape[1:], km_s.dtype)
            for nb in range(n_blocks):
                mean = jnp.mean(k_ref[nb * t:(nb + 1) * t, g * hd:(g + 1) * hd].astype(F32), axis=0, keepdims=True)
                hi = mean.astype(BF16)
                km_s[g, nb:nb + 1, :] = hi
                km_s[g, LANES + nb:LANES + nb + 1, :] = (mean - hi.astype(F32)).astype(BF16)

    _softmax_init(m_s, acc_s)
    blk_row = lax.broadcasted_iota(jnp.int32, (nb_pad, t), 0)

    def query_tile(tile, qt_ref, z_ref, o_ref, which):
        state0 = which * g_heads

        past = blk_row < tile
        for g in range(g_heads):
            qt = qt_ref[g * hd:(g + 1) * hd, :]
            g2 = jnp.dot(km_s[g], qt, preferred_element_type=F32)
            gate = g2[:nb_pad, :] + g2[LANES:LANES + nb_pad, :]
            cur = jnp.where(past, gate, NEG)
            keep = blk_row == tile
            for _ in range(A_TOPK):
                mx = jnp.max(cur, axis=0, keepdims=True)
                idx = jnp.min(jnp.where(cur == mx, blk_row, nb_pad), axis=0, keepdims=True)
                pick = blk_row == idx
                keep = keep | (pick & past)
                cur = jnp.where(pick, -jnp.inf, cur)
            unsel = jnp.where(keep, 0.0, 1.0)
            qa_s[state0 + g, :hd, :] = qt
            qa_s[state0 + g, hd:, :] = jnp.concatenate([unsel, jnp.zeros((LANES - nb_pad, t), F32)],
                                                       axis=0).astype(qa_s.dtype)

        def raw_scores(g, j):
            start = pl.multiple_of(j * t, t)
            keys = jnp.concatenate([k_ref[pl.ds(start, t), g * hd:(g + 1) * hd], kmask_s[pl.ds(start, t), :]],
                                   axis=1)
            return jnp.dot(keys, qa_s[state0 + g], preferred_element_type=F32)

        def own_scores(slot):
            key = lax.broadcasted_iota(jnp.int32, (t, t), 0)
            qry = lax.broadcasted_iota(jnp.int32, (t, t), 1)
            for g in range(g_heads):
                s_refs[g][slot] = jnp.where(key <= qry, raw_scores(g, tile) + bias_own_s[g], NEG)

        j_prev = jnp.maximum(tile - 1, 0)
        no_prev = jnp.where(tile == 0, -MASK_BIG, 0.0)

        def prev_scores(slot):
            for g in range(g_heads):
                s_refs[g][slot] = raw_scores(g, j_prev) + (bias_prev_s[g] + no_prev)

        def far_scores(j, slot):
            for g in range(g_heads):
                s_refs[g][slot] = raw_scores(g, j)

        def consume(j, slot):
            for g in range(g_heads):
                _softmax_step_t(s_refs[g][slot], vt_ref[j, g * hv:(g + 1) * hv, :], state0 + g, m_s, acc_s)

        finish = functools.partial(_attn_finish, z_ref, o_ref, acc_s, g_heads, hd, state0)
        return [(own_scores, tile), (prev_scores, j_prev)], j_prev, far_scores, consume, finish

    carry = _skewed_blocks(*query_tile(n_blocks - 1 - 2 * i, qt_odd_ref, z_odd_ref, o_odd_ref, 1), drain=False)
    _skewed_blocks(*query_tile(2 * i, qt_even_ref, z_even_ref, o_even_ref, 0), carry_in=carry)


def moba_branch(qt, vt, slab, t5_table, g_heads=ATTN_HEADS_PER_STEP):
    bsz, nb, _, t = qt.shape
    s = nb * t
    hd = A_HEAD_DIM
    gw = g_heads * hd
    assert t == A_BLOCK and nb <= LANES and nb % TILES_PER_STEP == 0 and A_HEADS % g_heads == 0
    assert OFF_KA % gw == 0 and OFF_ZA % gw == 0
    assert _T5_THRESH[T5_BUCKETS - 1] <= t + 1
    ck, cz = OFF_KA // gw, OFF_ZA // gw
    qt_even, qt_odd = _tile_pair_specs(nb, (None, None, gw, t), lambda h, b, tile: (b, tile, h, 0))
    z_even, z_odd = _tile_pair_specs(nb, (None, t, gw), lambda h, b, tile: (b, tile, cz + h))
    out = jax.ShapeDtypeStruct((bsz, nb // 2, t, A_W), BF16)
    return pl.pallas_call(
        functools.partial(_moba_body, n_blocks=nb, g_heads=g_heads),
        grid=(A_HEADS // g_heads, bsz, nb // 2),
        in_specs=[pl.BlockSpec(memory_space=pltpu.SMEM),
                  qt_even, qt_odd,
                  pl.BlockSpec((None, s, gw), lambda h, b, i: (b, 0, ck + h)),
                  pl.BlockSpec((None, nb, g_heads * (hd + V_AUG_ROWS), t), lambda h, b, i: (b, 0, h, 0)),
                  z_even, z_odd],
        out_specs=[pl.BlockSpec((None, None, t, gw), lambda h, b, i: (b, i, 0, h)),
                   pl.BlockSpec((None, None, t, gw), lambda h, b, i: (b, nb // 2 - 1 - i, 0, h))],
        out_shape=[out, out],
        scratch_shapes=[pltpu.VMEM((s, LANES), BF16),
                        pltpu.VMEM((g_heads, 2 * LANES, hd), BF16),
                        pltpu.VMEM((g_heads, t, t), F32),
                        pltpu.VMEM((g_heads, t, t), F32),
                        pltpu.VMEM((TILES_PER_STEP * g_heads, ATTN_QK_PAD, t), BF16)]
        + _attn_scratch(g_heads, t, hd),
        compiler_params=_cparams(3),
        name="moba",
    )(t5_table, qt, qt, slab, vt, slab, slab)


MLSTM_CHUNK = 256
MLSTM_SEQS_PER_STEP = 1
CONV_HALO = 8


def _log_sigmoid(t):
    return jnp.minimum(t, 0.0) - jnp.log1p(jnp.exp(-jnp.abs(t)))


def _mlstm_body(qk_ref, v_ref, ob_ref, zb_ref, if_ref, cw_ref, cb_ref, gb_ref, gn_ref, o_ref,
                xe_s, c_s, n_s, m_s):
    c = pl.program_id(1)

    @pl.when(c == 0)
    def _reset():
        xe_s[...] = jnp.zeros_like(xe_s)
        c_s[...] = jnp.zeros_like(c_s)
        n_s[...] = jnp.zeros_like(n_s)
        m_s[...] = jnp.zeros_like(m_s)

    for bi in range(qk_ref.shape[0]):
        _mlstm_chunk(bi, qk_ref, v_ref, ob_ref, zb_ref, if_ref, cw_ref, cb_ref, gb_ref, gn_ref, o_ref,
                     xe_s, c_s, n_s, m_s)


def _mlstm_chunk(bi, qk_ref, v_ref, ob_ref, zb_ref, if_ref, cw_ref, cb_ref, gb_ref, gn_ref, o_ref,
                 xe_s, c_s, n_s, m_s):
    L = MLSTM_CHUNK
    dk, dv = B_QK_DIM, B_V_DIM

    x = qk_ref[bi].astype(F32)
    xe = jnp.concatenate([xe_s[bi], x], axis=0)
    conv = cb_ref[...] + cw_ref[B_CONV - 1:B_CONV, :] * x
    for j in range(B_CONV - 1):
        back = B_CONV - 1 - j
        conv = conv + cw_ref[j:j + 1, :] * pltpu.roll(xe, back, axis=0)[CONV_HALO:, :]
    xe_s[bi] = x[L - CONV_HALO:, :]
    qk = _silu(conv)

    gi = if_ref[bi].astype(F32) + gb_ref[...]
    lf = _log_sigmoid(gi)
    row = lax.broadcasted_iota(jnp.int32, (L, L), 0)
    col = lax.broadcasted_iota(jnp.int32, (L, L), 1)
    causal = col <= row
    tri = jnp.where(causal, 1.0, 0.0).astype(F32)
    b_cols = jnp.dot(tri, lf, precision=lax.Precision.HIGHEST, preferred_element_type=F32)
    b_rows = b_cols.T
    li_rows = gi.T

    for h in range(B_HEADS):
        q_h = qk[:, h * dk:(h + 1) * dk]
        k_h = qk[:, B_QK_W + h * dk:B_QK_W + (h + 1) * dk] * (dk ** -0.5)
        v_h = v_ref[bi, :, h * dv:(h + 1) * dv]
        b_c = b_cols[:, B_HEADS + h:B_HEADS + h + 1]
        b_r = b_rows[B_HEADS + h:B_HEADS + h + 1, :]
        li_c = gi[:, h:h + 1]
        li_r = li_rows[h:h + 1, :]
        m_prev = m_s[bi, h:h + 1, 0:1]
        c_prev = c_s[bi * B_HEADS + h]
        n_prev = n_s[bi, h:h + 1, :]

        a_c = b_c + m_prev
        dmat = jnp.where(causal, b_c - b_r + li_r, -jnp.inf)
        m_t = jnp.maximum(a_c, jnp.max(dmat, axis=-1, keepdims=True))
        w_inter = jnp.exp(a_c - m_t)
        q_b = q_h.astype(BF16)
        k_b = k_h.astype(BF16)
        sc = (lax.dot_general(q_b, k_b, _NT, preferred_element_type=F32)
              * jnp.exp(dmat - m_t))
        num = (w_inter * jnp.dot(q_b, c_prev.astype(BF16), preferred_element_type=F32)
               + jnp.dot(sc.astype(BF16), v_h, preferred_element_type=F32))
        den = (w_inter * jnp.sum(q_h * n_prev, axis=-1, keepdims=True)
               + jnp.sum(sc, axis=-1, keepdims=True))
        hh = num / jnp.maximum(jnp.abs(den), jnp.exp(-m_t))

        b_last = b_c[L - 1:L, :]
        g_c = b_last - b_c + li_c
        m_new = jnp.maximum(b_last + m_prev, jnp.max(g_c, axis=0, keepdims=True))
        decay = jnp.exp(b_last + m_prev - m_new)
        wk = jnp.exp(g_c - m_new) * k_h
        c_s[bi * B_HEADS + h] = decay * c_prev + lax.dot_general(
            wk.astype(BF16), v_h, (((0,), (0,)), ((), ())), preferred_element_type=F32)
        n_s[bi, h:h + 1, :] = decay * n_prev + jnp.sum(wk, axis=0, keepdims=True)
        m_s[bi, h:h + 1, :] = jnp.broadcast_to(m_new, (1, LANES))

        mu = jnp.mean(hh, axis=-1, keepdims=True)
        dlt = hh - mu
        var = jnp.mean(dlt * dlt, axis=-1, keepdims=True)
        y = dlt * lax.rsqrt(var + EPS) * gn_ref[:, h * dv:(h + 1) * dv]
        y = (y * jax.nn.sigmoid(ob_ref[bi, :, h * dv:(h + 1) * dv].astype(F32))
             * _silu(zb_ref[bi, :, h * dv:(h + 1) * dv].astype(F32)))
        o_ref[bi, :, h * dv:(h + 1) * dv] = y.astype(o_ref.dtype)


def mlstm_branch(slab, slab_small, conv_w, conv_b, i_bias, f_bias, out_norm, nb=MLSTM_SEQS_PER_STEP):
    bsz, s, _ = slab.shape
    L = MLSTM_CHUNK
    nb = math.gcd(nb, bsz)
    assert s % L == 0
    gate_bias = jnp.zeros((1, LANES), F32).at[0, :B_HEADS].set(i_bias).at[0, B_HEADS:2 * B_HEADS].set(f_bias)
    w2 = 2 * B_QK_W
    full = lambda shape: pl.BlockSpec(shape, lambda b, c: (0,) * len(shape))
    return pl.pallas_call(
        _mlstm_body,
        grid=(bsz // nb, s // L),
        in_specs=[pl.BlockSpec((nb, L, w2), lambda b, c: (b, c, OFF_QKB // w2)),
                  pl.BlockSpec((nb, L, B_V_W), lambda b, c: (b, c, OFF_VB // B_V_W)),
                  pl.BlockSpec((nb, L, B_V_W), lambda b, c: (b, c, OFF_OB // B_V_W)),
                  pl.BlockSpec((nb, L, B_V_W), lambda b, c: (b, c, OFF_ZB // B_V_W)),
                  pl.BlockSpec((nb, L, LANES), lambda b, c: (b, c, OFF_IF // LANES)),
                  full((B_CONV, w2)), full((1, w2)), full((1, LANES)), full((1, B_V_W))],
        out_specs=pl.BlockSpec((nb, L, B_V_W), lambda b, c: (b, c, 0)),
        out_shape=jax.ShapeDtypeStruct((bsz, s, B_V_W), BF16),
        scratch_shapes=[pltpu.VMEM((nb, CONV_HALO, w2), F32),
                        pltpu.VMEM((nb * B_HEADS, B_QK_DIM, B_V_DIM), F32),
                        pltpu.VMEM((nb, SUBLANES, B_QK_DIM), F32),
                        pltpu.VMEM((nb, SUBLANES, LANES), F32)],
        compiler_params=_cparams(2),
        name="mlstm",
    )(slab, slab, slab, slab, slab_small, conv_w, conv_b.reshape(1, w2), gate_bias,
      out_norm.reshape(1, B_V_W))


def _mla_prep_body(cq_ref, ckv_ref, kr_ref, qg_ref, kg_ref, wqt_ref, wqst_ref, wkn_ref, wvt_ref,
                   cos_ref, sin_ref, cost_ref, sint_ref, qt_ref, kf_ref, vt_ref):
    def normed(ref, g_ref):
        t = ref[...].astype(F32)
        y = t * lax.rsqrt(jnp.mean(t * t, axis=-1, keepdims=True) + EPS)
        return (y * g_ref[...]).astype(BF16)

    cqn = normed(cq_ref, qg_ref)
    ckvn = normed(ckv_ref, kg_ref)

    qt_main = lax.dot_general(wqt_ref[...], cqn, _NT, preferred_element_type=F32)
    qt_swap = lax.dot_general(wqst_ref[...], cqn, _NT, preferred_element_type=F32)
    cos_t = cost_ref[...]
    sin_t = sint_ref[...]
    t = ATTN_TILE
    n_sub = cos_t.shape[1] // t
    for h in range(C_HEADS):
        lo = h * ATTN_QK_PAD
        nope = qt_main[lo:lo + LANES, :].astype(qt_ref.dtype)
        rope = (qt_main[lo + LANES:lo + 2 * LANES, :] * cos_t
                + qt_swap[h * LANES:(h + 1) * LANES, :] * sin_t).astype(qt_ref.dtype)
        for c in range(n_sub):
            qt_ref[c, lo:lo + LANES, :] = nope[:, c * t:(c + 1) * t]
            qt_ref[c, lo + LANES:lo + 2 * LANES, :] = rope[:, c * t:(c + 1) * t]

    k_nope = jnp.dot(ckvn, wkn_ref[...], preferred_element_type=F32)
    kr = kr_ref[...].astype(F32)
    half = C_ROPE // 2
    lane = lax.broadcasted_iota(jnp.int32, kr.shape, 1)
    swapped = jnp.where(lane < half, -pltpu.roll(kr, LANES - half, axis=1), pltpu.roll(kr, half, axis=1))
    k_rot = (kr * cos_ref[...] + swapped * sin_ref[...]).astype(kf_ref.dtype)
    for h in range(C_HEADS):
        lo = h * ATTN_QK_PAD
        kf_ref[:, lo:lo + LANES] = k_nope[:, h * LANES:(h + 1) * LANES].astype(kf_ref.dtype)
        kf_ref[:, lo + LANES:lo + 2 * LANES] = k_rot

    v_t = lax.dot_general(wvt_ref[...], ckvn, _NT, preferred_element_type=F32)
    hv = C_V_DIM + V_AUG_ROWS
    aug = _v_aug_rows(t, vt_ref.dtype)
    for h in range(C_HEADS):
        v_h = v_t[h * C_V_DIM:(h + 1) * C_V_DIM, :].astype(vt_ref.dtype)
        for c in range(n_sub):
            vt_ref[c, h * hv:h * hv + C_V_DIM, :] = v_h[:, c * t:(c + 1) * t]
            vt_ref[c, h * hv + C_V_DIM:(h + 1) * hv, :] = aug


def mla_prep(slab, q_norm, kv_norm, w_uq, w_ukv, tm=1024):
    bsz, s, _ = slab.shape
    t = ATTN_TILE
    n_sub = tm // t
    vt_rows = C_HEADS * (C_V_DIM + V_AUG_ROWS)
    half = C_ROPE // 2
    scale = (C_NOPE + C_ROPE) ** -0.5 * LOG2E
    wq = (w_uq * scale).reshape(C_Q_RANK, C_HEADS, C_NOPE + C_ROPE)
    pad = jnp.zeros((C_Q_RANK, C_HEADS, LANES - C_ROPE), F32)
    x1, x2 = wq[..., C_NOPE:C_NOPE + half], wq[..., C_NOPE + half:]
    wqt = jnp.concatenate([wq, pad], axis=-1).reshape(C_Q_RANK, C_HEADS * ATTN_QK_PAD).T.astype(BF16)
    wqst = jnp.concatenate([-x2, x1, pad], axis=-1).reshape(C_Q_RANK, C_HEADS * LANES).T.astype(BF16)
    wkv = w_ukv.reshape(C_KV_RANK, C_HEADS, C_NOPE + C_V_DIM)
    wkn = wkv[..., :C_NOPE].reshape(C_KV_RANK, C_HEADS * C_NOPE).astype(BF16)
    wvt = wkv[..., C_NOPE:].reshape(C_KV_RANK, C_W).T.astype(BF16)

    pos = jnp.arange(s, dtype=jnp.int32)
    inv = ROPE_THETA ** (-jnp.arange(half, dtype=F32) / half)
    ang = pos.astype(F32)[:, None] * inv[None, :]
    zpad = jnp.zeros((s, LANES - C_ROPE), F32)
    cos_tab = jnp.concatenate([jnp.cos(ang), jnp.cos(ang), zpad], axis=-1)
    sin_tab = jnp.concatenate([jnp.sin(ang), jnp.sin(ang), zpad], axis=-1)

    nt = s // t
    full = lambda shape: pl.BlockSpec(shape, lambda b, i: (0,) * len(shape))
    return pl.pallas_call(
        _mla_prep_body,
        grid=(bsz, s // tm),
        in_specs=[pl.BlockSpec((None, tm, C_Q_RANK), lambda b, i: (b, i, OFF_CQ // C_Q_RANK)),
                  pl.BlockSpec((None, tm, C_KV_RANK), lambda b, i: (b, i, OFF_CKV // C_KV_RANK)),
                  pl.BlockSpec((None, tm, LANES), lambda b, i: (b, i, OFF_KR // LANES)),
                  full((1, C_Q_RANK)), full((1, C_KV_RANK)),
                  full(wqt.shape), full(wqst.shape), full(wkn.shape), full(wvt.shape),
                  pl.BlockSpec((tm, LANES), lambda b, i: (i, 0)),
                  pl.BlockSpec((tm, LANES), lambda b, i: (i, 0)),
                  pl.BlockSpec((LANES, tm), lambda b, i: (0, i)),
                  pl.BlockSpec((LANES, tm), lambda b, i: (0, i))],
        out_specs=[pl.BlockSpec((None, n_sub, C_HEADS * ATTN_QK_PAD, t), lambda b, i: (b, i, 0, 0)),
                   pl.BlockSpec((None, tm, C_HEADS * ATTN_QK_PAD), lambda b, i: (b, i, 0)),
                   pl.BlockSpec((None, n_sub, vt_rows, t), lambda b, i: (b, i, 0, 0))],
        out_shape=[jax.ShapeDtypeStruct((bsz, nt, C_HEADS * ATTN_QK_PAD, t), BF16),
                   jax.ShapeDtypeStruct((bsz, s, C_HEADS * ATTN_QK_PAD), BF16),
                   jax.ShapeDtypeStruct((bsz, nt, vt_rows, t), BF16)],
        compiler_params=_cparams(2),
        name="mla_prep",
    )(slab, slab, slab, q_norm.reshape(1, C_Q_RANK), kv_norm.reshape(1, C_KV_RANK),
      wqt, wqst, wkn, wvt, cos_tab, sin_tab, cos_tab.T, sin_tab.T)


def _mla_attn_body(qt_even_ref, qt_odd_ref, k_ref, vt_ref, z_even_ref, z_odd_ref, o_even_ref, o_odd_ref,
                   m_s, acc_s, *s_refs, g_heads, n_tiles):
    i = pl.program_id(2)
    t = ATTN_TILE
    dq, dv = ATTN_QK_PAD, C_V_DIM
    hv = dv + V_AUG_ROWS
    _softmax_init(m_s, acc_s)

    def query_tile(tile, qt_ref, z_ref, o_ref, which):
        state0 = which * g_heads

        def raw_scores(g, j):
            start = pl.multiple_of(j * t, t)
            return jnp.dot(k_ref[pl.ds(start, t), g * dq:(g + 1) * dq], qt_ref[g * dq:(g + 1) * dq, :],
                           preferred_element_type=F32)

        def diagonal_scores(slot):
            key = lax.broadcasted_iota(jnp.int32, (t, t), 0)
            qry = lax.broadcasted_iota(jnp.int32, (t, t), 1)
            for g in range(g_heads):
                s_refs[g][slot] = jnp.where(key <= qry, raw_scores(g, tile), NEG)

        def past_scores(j, slot):
            for g in range(g_heads):
                s_refs[g][slot] = raw_scores(g, j)

        def consume(j, slot):
            for g in range(g_heads):
                _softmax_step_t(s_refs[g][slot], vt_ref[j, g * hv:(g + 1) * hv, :], state0 + g, m_s, acc_s)

        finish = functools.partial(_attn_finish, z_ref, o_ref, acc_s, g_heads, dv, state0)
        return [(diagonal_scores, tile)], tile, past_scores, consume, finish

    carry = _skewed_blocks(*query_tile(2 * i, qt_even_ref, z_even_ref, o_even_ref, 0), drain=False)
    _skewed_blocks(*query_tile(n_tiles - 1 - 2 * i, qt_odd_ref, z_odd_ref, o_odd_ref, 1), carry_in=carry)


def mla_attention(qt, kf, vt, slab, g_heads=ATTN_HEADS_PER_STEP):
    bsz, nt, _, t = qt.shape
    s = nt * t
    dq, dv = ATTN_QK_PAD, C_V_DIM
    assert t == ATTN_TILE and nt % TILES_PER_STEP == 0
    assert C_HEADS % g_heads == 0 and OFF_ZC % (g_heads * dv) == 0
    cz = OFF_ZC // (g_heads * dv)
    qt_even, qt_odd = _tile_pair_specs(nt, (None, None, g_heads * dq, t), lambda b, h, tile: (b, tile, h, 0))
    z_even, z_odd = _tile_pair_specs(nt, (None, t, g_heads * dv), lambda b, h, tile: (b, tile, cz + h))
    out = jax.ShapeDtypeStruct((bsz, nt // 2, t, C_W), BF16)
    return pl.pallas_call(
        functools.partial(_mla_attn_body, g_heads=g_heads, n_tiles=nt),
        grid=(bsz, C_HEADS // g_heads, nt // 2),
        in_specs=[qt_even, qt_odd,
                  pl.BlockSpec((None, s, g_heads * dq), lambda b, h, i: (b, 0, h),
                               pipeline_mode=pl.Buffered(1)),
                  pl.BlockSpec((None, nt, g_heads * (dv + V_AUG_ROWS), t), lambda b, h, i: (b, 0, h, 0)),
                  z_even, z_odd],
        out_specs=[pl.BlockSpec((None, None, t, g_heads * dv), lambda b, h, i: (b, i, 0, h)),
                   pl.BlockSpec((None, None, t, g_heads * dv), lambda b, h, i: (b, nt // 2 - 1 - i, 0, h))],
        out_shape=[out, out],
        scratch_shapes=_attn_scratch(g_heads, t, dv),
        compiler_params=_cparams(3),
        name="mla_attn",
    )(qt, qt, kf, vt, slab, slab)


def _merge_body(ya_even_ref, ya_odd_ref, yb_ref, yc_even_ref, yc_odd_ref, wa_ref, wb_ref, wc_ref,
                ga_ref, gb_ref, gc_ref, o_ref, w_s):
    @pl.when(pl.program_id(1) == 0)
    def _cast_weights():
        for n, w_ref in enumerate((wa_ref, wb_ref, wc_ref)):
            w_s[n] = w_ref[...].astype(w_s.dtype)

    def rows(even_ref, odd_ref):
        return jnp.concatenate([ref[p] for p in range(even_ref.shape[0]) for ref in (even_ref, odd_ref)], axis=0)

    ya = rows(ya_even_ref, ya_odd_ref)
    yc = rows(yc_even_ref, yc_odd_ref)
    acc = None
    for n, (y, g_ref) in enumerate(((ya, ga_ref), (yb_ref[...], gb_ref), (yc, gc_ref))):
        term = jax.nn.sigmoid(g_ref[...].astype(F32)) * jnp.dot(y, w_s[n], preferred_element_type=F32)
        acc = term if acc is None else acc + term
    o_ref[...] = acc.astype(o_ref.dtype)


def branch_merge(ya_tiles, yb, yc_tiles, w_branch, layer, slab2d, tm=1024, tn=1024):
    m, w = yb.shape
    d = w_branch.shape[-1]
    t = ATTN_TILE
    pairs = tm // (TILES_PER_STEP * t)
    assert OFF_GT % tn == 0 and pairs * TILES_PER_STEP * t == tm
    g0 = OFF_GT // tn
    gper = d // tn
    halves = [y.reshape(m // (TILES_PER_STEP * t), t, w) for y in (*ya_tiles, *yc_tiles)]
    half_spec = pl.BlockSpec((pairs, t, w), lambda j, i: (i, 0, 0))
    w_specs = [pl.BlockSpec((None, None, w, tn), functools.partial(lambda j, i, n: (layer, n, 0, j), n=n),
                            pipeline_mode=pl.Buffered(1))
               for n in range(N_BRANCH)]
    g_specs = [pl.BlockSpec((tm, tn), functools.partial(lambda j, i, n: (i, g0 + n * gper + j), n=n))
               for n in range(N_BRANCH)]
    return pl.pallas_call(
        _merge_body,
        grid=(d // tn, m // tm),
        in_specs=[half_spec, half_spec, pl.BlockSpec((tm, w), lambda j, i: (i, 0)), half_spec, half_spec]
        + w_specs + g_specs,
        out_specs=pl.BlockSpec((tm, tn), lambda j, i: (i, j)),
        out_shape=jax.ShapeDtypeStruct((m, d), BF16),
        scratch_shapes=[pltpu.VMEM((N_BRANCH, w, tn), BF16)],
        compiler_params=_cparams(2),
        name="branch_merge",
    )(halves[0], halves[1], yb, halves[2], halves[3], w_branch, w_branch, w_branch, slab2d, slab2d, slab2d)


def _out_body(mg_ref, w_ref, x_ref, g_ref, *refs, last):
    out_refs, w_s = refs[:-1], refs[-1]

    @pl.when(pl.program_id(0) == 0)
    def _cast_weights():
        w_s[...] = w_ref[...].astype(w_s.dtype)

    x_new = x_ref[...] + jnp.dot(mg_ref[...], w_s[...], preferred_element_type=F32)
    y = x_new * lax.rsqrt(jnp.mean(x_new * x_new, axis=-1, keepdims=True) + EPS) * g_ref[...]
    if last:
        out_refs[0][...] = y
    else:
        out_refs[0][...] = x_new
        out_refs[1][...] = y.astype(out_refs[1].dtype)


def out_projection(merged, w_out, layer, x2d, gain, last, tm=512):
    m, d = x2d.shape
    row = pl.BlockSpec((tm, d), lambda i: (i, 0))
    if last:
        out_specs, out_shape = row, jax.ShapeDtypeStruct((m, d), F32)
    else:
        out_specs = [row, row]
        out_shape = [jax.ShapeDtypeStruct((m, d), F32), jax.ShapeDtypeStruct((m, d), BF16)]
    return pl.pallas_call(
        functools.partial(_out_body, last=last),
        grid=(m // tm,),
        in_specs=[row,
                  pl.BlockSpec((None, d, d), lambda i: (layer, 0, 0), pipeline_mode=pl.Buffered(1)),
                  row, pl.BlockSpec((1, d), lambda i: (0, 0))],
        out_specs=out_specs,
        out_shape=out_shape,
        scratch_shapes=[pltpu.VMEM((d, d), BF16)],
        compiler_params=_cparams(1),
        name="out_proj",
    )(merged, w_out, x2d, gain.reshape(1, d))


def kernel(x, norm_gain, w_in, t5_table, mlstm_conv_w, mlstm_conv_b, mlstm_i_bias, mlstm_f_bias,
           mlstm_out_norm, mla_q_norm, mla_kv_norm, mla_w_uq, mla_w_ukv, w_branch, w_out, final_norm):
    bsz, s, d = x.shape
    m = bsz * s
    x2d = x.reshape(m, d)
    h = rmsnorm_rows(x2d, norm_gain[0])
    w_in_t = jnp.swapaxes(w_in, 1, 2)
    out = None
    for l in range(DEPTH):
        slab2d = input_projection(h, w_in_t, l, SLAB_SRC_ROWS)
        slab = slab2d.reshape(bsz, s, D_SLAB)
        slab_small = input_projection_small(h, w_in_t, l).reshape(bsz, s, D_SLAB_SMALL)
        qt_a, vt_a = moba_qv_projection(h, w_in_t, l, bsz)
        ya = moba_branch(qt_a, vt_a, slab, t5_table)
        yb = mlstm_branch(slab, slab_small, mlstm_conv_w[l], mlstm_conv_b[l], mlstm_i_bias[l], mlstm_f_bias[l],
                          mlstm_out_norm[l])
        qt_c, k_c, vt_c = mla_prep(slab_small, mla_q_norm[l], mla_kv_norm[l], mla_w_uq[l], mla_w_ukv[l])
        yc = mla_attention(qt_c, k_c, vt_c, slab)
        merged = branch_merge(ya, yb.reshape(m, B_V_W), yc, w_branch, l, slab2d)
        last = l == DEPTH - 1
        gain = final_norm if last else norm_gain[l + 1]
        res = out_projection(merged, w_out, l, x2d, gain, last)
        if last:
            out = res
        else:
            x2d, h = res
    return out.reshape(bsz, s, d)
```

```python
import functools
import math

import jax
import jax.numpy as jnp
import numpy as np
from jax import lax
from jax.experimental import pallas as pl
from jax.experimental.pallas import tpu as pltpu

F32 = jnp.float32
BF16 = jnp.bfloat16

D_MODEL = 2048
DEPTH = 2
EPS = 1e-6
NEG = -1e30
LOG2E = math.log2(math.e)

A_HEADS = 8
A_HEAD_DIM = 128
A_BLOCK = 256
A_TOPK = 3
T5_BUCKETS = 32
T5_MAX_DIST = 128
B_HEADS = 4
B_QK_DIM = 128
B_V_DIM = 256
B_CONV = 4
C_HEADS = 8
C_Q_RANK = 512
C_KV_RANK = 256
C_NOPE = 128
C_ROPE = 64
C_V_DIM = 128
ROPE_THETA = 10000.0
N_BRANCH = 3

A_W = A_HEADS * A_HEAD_DIM
B_QK_W = B_HEADS * B_QK_DIM
B_V_W = B_HEADS * B_V_DIM
C_W = C_HEADS * C_V_DIM

LANES = 128
SUBLANES = 8
VMEM_LIMIT_BYTES = 58 * 1024 * 1024

SPLIT_SIZES = (A_W, A_W, A_W, A_W, B_QK_W, B_QK_W, B_V_W, B_HEADS, B_HEADS, B_V_W, B_V_W,
               C_Q_RANK, C_KV_RANK, C_ROPE, C_W, N_BRANCH * D_MODEL)
(SRC_QA, SRC_KA, SRC_VA, SRC_ZA, SRC_QB, SRC_KB, SRC_VB, SRC_IB, SRC_FB, SRC_OB, SRC_ZB,
 SRC_CQ, SRC_CKV, SRC_KR, SRC_ZC, SRC_GT) = (int(v) for v in np.cumsum((0,) + SPLIT_SIZES[:-1]))
D_IN = int(sum(SPLIT_SIZES))

W_BLOCK = 1024
SLAB_SRC_ROWS = ((SRC_KA, SRC_ZA, SRC_QB, SRC_VB, SRC_OB, SRC_ZB, SRC_ZC)
                 + tuple(SRC_GT + k * W_BLOCK for k in range(N_BRANCH * D_MODEL // W_BLOCK)))
assert SRC_KB == SRC_QB + B_QK_W and all(r % SUBLANES == 0 for r in SLAB_SRC_ROWS)
OFF_KA = 0
OFF_ZA = OFF_KA + W_BLOCK
OFF_QKB = OFF_ZA + W_BLOCK
OFF_VB = OFF_QKB + W_BLOCK
OFF_OB = OFF_VB + W_BLOCK
OFF_ZB = OFF_OB + W_BLOCK
OFF_ZC = OFF_ZB + W_BLOCK
OFF_GT = OFF_ZC + W_BLOCK
D_SLAB = len(SLAB_SRC_ROWS) * W_BLOCK
SMALL_WIN1_ROWS = C_Q_RANK + C_KV_RANK + LANES
SMALL_WIN2_ROWS = LANES
assert SRC_CKV == SRC_CQ + C_Q_RANK and SRC_KR == SRC_CKV + C_KV_RANK and SRC_FB == SRC_IB + B_HEADS
assert SRC_CQ % SUBLANES == 0 and SRC_IB % SUBLANES == 0
assert SRC_CQ + SMALL_WIN1_ROWS <= D_IN and SRC_IB + SMALL_WIN2_ROWS <= D_IN
OFF_CQ = 0
OFF_CKV = OFF_CQ + C_Q_RANK
OFF_KR = OFF_CKV + C_KV_RANK
OFF_IF = OFF_KR + LANES
D_SLAB_SMALL = SMALL_WIN1_ROWS + SMALL_WIN2_ROWS
assert D_SLAB_SMALL == W_BLOCK

ATTN_TILE = 256
ATTN_QK_PAD = 2 * LANES
ATTN_HEADS_PER_STEP = 8
assert ATTN_TILE == A_BLOCK


def _cparams(n_axes):
    return pltpu.CompilerParams(dimension_semantics=("arbitrary",) * n_axes,
                                vmem_limit_bytes=VMEM_LIMIT_BYTES)


def _silu(t):
    return t * jax.nn.sigmoid(t)


_NT = (((1,), (1,)), ((), ()))


def _rmsnorm_body(x_ref, g_ref, o_ref):
    xf = x_ref[...]
    y = xf * lax.rsqrt(jnp.mean(xf * xf, axis=-1, keepdims=True) + EPS)
    o_ref[...] = (y * g_ref[...]).astype(o_ref.dtype)


def rmsnorm_rows(x2d, gain, tm=1024):
    m, d = x2d.shape
    return pl.pallas_call(
        _rmsnorm_body,
        grid=(m // tm,),
        in_specs=[pl.BlockSpec((tm, d), lambda i: (i, 0)),
                  pl.BlockSpec((1, d), lambda i: (0, 0))],
        out_specs=pl.BlockSpec((tm, d), lambda i: (i, 0)),
        out_shape=jax.ShapeDtypeStruct((m, d), BF16),
        compiler_params=_cparams(1),
        name="rmsnorm",
    )(x2d, gain.reshape(1, d))


def _row_window(rows, d):
    return (pl.Squeezed(), pl.Element(rows), pl.Element(d))


def _proj_body(rows_ref, h_ref, w_ref, o_ref, wb_s):
    del rows_ref
    @pl.when(pl.program_id(1) == 0)
    def _cast_weights():
        for c in range(0, w_ref.shape[0], LANES):
            wb_s[:, c:c + LANES] = w_ref[c:c + LANES, :].T.astype(wb_s.dtype)

    o_ref[...] = jnp.dot(h_ref[...], wb_s[...], preferred_element_type=F32).astype(o_ref.dtype)


def input_projection(h2d, w_in_t, layer, src_rows, tm=2048):
    m, d = h2d.shape
    tn = W_BLOCK
    grid_spec = pltpu.PrefetchScalarGridSpec(
        num_scalar_prefetch=1,
        grid=(len(src_rows), m // tm),
        in_specs=[pl.BlockSpec((tm, d), lambda j, i, rows: (i, 0)),
                  pl.BlockSpec(_row_window(tn, d),
                               lambda j, i, rows: (layer, pl.multiple_of(rows[j], SUBLANES), 0))],
        out_specs=pl.BlockSpec((tm, tn), lambda j, i, rows: (i, j)),
        scratch_shapes=[pltpu.VMEM((d, tn), BF16)])
    return pl.pallas_call(
        _proj_body,
        grid_spec=grid_spec,
        out_shape=jax.ShapeDtypeStruct((m, len(src_rows) * tn), BF16),
        compiler_params=_cparams(2),
        name="input_proj",
    )(jnp.asarray(src_rows, jnp.int32), h2d, w_in_t)


def _proj_small_body(h_ref, w1_ref, w2_ref, o_ref, wb_s):
    @pl.when(pl.program_id(0) == 0)
    def _cast_weights():
        wb_s[:SMALL_WIN1_ROWS, :] = w1_ref[...].astype(wb_s.dtype)
        wb_s[SMALL_WIN1_ROWS:, :] = w2_ref[...].astype(wb_s.dtype)

    o_ref[...] = lax.dot_general(h_ref[...], wb_s[...], _NT, preferred_element_type=F32).astype(o_ref.dtype)


def input_projection_small(h2d, w_in_t, layer, tm=2048):
    m, d = h2d.shape
    return pl.pallas_call(
        _proj_small_body,
        grid=(m // tm,),
        in_specs=[pl.BlockSpec((tm, d), lambda i: (i, 0)),
                  pl.BlockSpec(_row_window(SMALL_WIN1_ROWS, d), lambda i: (layer, SRC_CQ, 0)),
                  pl.BlockSpec(_row_window(SMALL_WIN2_ROWS, d), lambda i: (layer, SRC_IB, 0))],
        out_specs=pl.BlockSpec((tm, D_SLAB_SMALL), lambda i: (i, 0)),
        out_shape=jax.ShapeDtypeStruct((m, D_SLAB_SMALL), BF16),
        scratch_shapes=[pltpu.VMEM((D_SLAB_SMALL, d), BF16)],
        compiler_params=_cparams(1),
        name="input_proj_small",
    )(h2d, w_in_t, w_in_t)


def _proj_t_body(h_ref, wq_ref, wv_ref, qt_ref, vt_ref, wt_s, *, n_sub):
    t = ATTN_TILE
    hd = A_HEAD_DIM
    hv = hd + V_AUG_ROWS

    @pl.when((pl.program_id(0) == 0) & (pl.program_id(1) == 0))
    def _cast_weights():
        q_scale = A_HEAD_DIM ** -0.5 * LOG2E
        wt_s[:A_W, :] = (wq_ref[...] * q_scale).astype(wt_s.dtype)
        wt_s[A_W:, :] = wv_ref[...].astype(wt_s.dtype)

    res = lax.dot_general(wt_s[...], h_ref[...], _NT, preferred_element_type=F32)
    aug = _v_aug_rows(t, vt_ref.dtype)
    for c in range(n_sub):
        qt_ref[c] = res[:A_W, c * t:(c + 1) * t].astype(qt_ref.dtype)
        for h in range(A_HEADS):
            vt_ref[c, h * hv:h * hv + hd, :] = res[A_W + h * hd:A_W + (h + 1) * hd, c * t:(c + 1) * t
                                                   ].astype(vt_ref.dtype)
            vt_ref[c, h * hv + hd:(h + 1) * hv, :] = aug


def moba_qv_projection(h2d, w_in_t, layer, bsz, tm=1024):
    m, d = h2d.shape
    assert SRC_QA % A_W == 0 and SRC_VA % A_W == 0
    t = ATTN_TILE
    tm = min(tm, m // bsz)
    n_sub = tm // t
    nt = m // bsz // t
    steps = m // bsz // tm
    vt_rows = A_HEADS * (A_HEAD_DIM + V_AUG_ROWS)
    return pl.pallas_call(
        functools.partial(_proj_t_body, n_sub=n_sub),
        grid=(bsz, steps),
        in_specs=[pl.BlockSpec((tm, d), lambda b, i: (b * steps + i, 0)),
                  pl.BlockSpec((None, A_W, d), lambda b, i: (layer, SRC_QA // A_W, 0),
                               pipeline_mode=pl.Buffered(1)),
                  pl.BlockSpec((None, A_W, d), lambda b, i: (layer, SRC_VA // A_W, 0),
                               pipeline_mode=pl.Buffered(1))],
        out_specs=[pl.BlockSpec((None, n_sub, A_W, t), lambda b, i: (b, i, 0, 0)),
                   pl.BlockSpec((None, n_sub, vt_rows, t), lambda b, i: (b, i, 0, 0))],
        out_shape=[jax.ShapeDtypeStruct((bsz, nt, A_W, t), BF16),
                   jax.ShapeDtypeStruct((bsz, nt, vt_rows, t), BF16)],
        scratch_shapes=[pltpu.VMEM((2 * A_W, d), BF16)],
        compiler_params=_cparams(2),
        name="moba_qv_proj",
    )(h2d, w_in_t, w_in_t)


V_AUG_ROWS = 16


def _v_aug_rows(t, dtype):
    return jnp.where(lax.broadcasted_iota(jnp.int32, (V_AUG_ROWS, t), 0) == 0, 1.0, 0.0).astype(dtype)


def _softmax_init(m_s, acc_s):
    m_s[...] = jnp.full(m_s.shape, -jnp.inf, F32)
    acc_s[...] = jnp.zeros(acc_s.shape, F32)


def _softmax_step_t(s, vt_blk, g, m_s, acc_s):
    m_prev = m_s[g]
    m_new = jnp.maximum(m_prev, jnp.max(s, axis=0, keepdims=True))
    alpha = jnp.exp2(m_prev - m_new)
    p = jnp.exp2(s - m_new)
    acc_s[g] = alpha * acc_s[g] + jnp.dot(vt_blk, p.astype(BF16), preferred_element_type=F32)
    m_s[g] = m_new


TILES_PER_STEP = 2


def _attn_scratch(g_heads, t, dv):
    n_state = TILES_PER_STEP * g_heads
    state = [pltpu.VMEM((n_state, 1, t), F32), pltpu.VMEM((n_state, dv + V_AUG_ROWS, t), F32)]
    return state + [pltpu.VMEM((2, t, t), F32)] * g_heads


def _tile_pair_specs(nt, block_shape, index_of_tile):
    even = pl.BlockSpec(block_shape, lambda x, y, p: index_of_tile(x, y, 2 * p))
    odd = pl.BlockSpec(block_shape, lambda x, y, p: index_of_tile(x, y, nt - 1 - 2 * p))
    return even, odd


def _skewed_blocks(n_heads, lead, n_uniform, uniform_scores, consume, finish, carry_in=None, drain=True):
    def stage(score, fold):
        for g in range(n_heads):
            score((g,))
            fold((g,))

    if carry_in is None:
        lead[0][0](0)
        pend_blk, pend_slot = lead[0][1], 0
    else:
        prev_blk, prev_slot, prev_consume, prev_finish = carry_in
        stage(lambda hs: lead[0][0](1 - prev_slot, hs), lambda hs: prev_consume(prev_blk, prev_slot, hs))
        prev_finish()
        pend_blk, pend_slot = lead[0][1], 1 - prev_slot
    for fn, blk in lead[1:]:
        stage(functools.partial(fn, 1 - pend_slot), functools.partial(consume, pend_blk, pend_slot))
        pend_blk, pend_slot = blk, 1 - pend_slot
    a, b = pend_slot, 1 - pend_slot

    def pair(first, pend):
        stage(functools.partial(uniform_scores, first, b), functools.partial(consume, pend, a))
        stage(functools.partial(uniform_scores, first + 1, a), functools.partial(consume, first, b))
        return first + 1

    def two_pairs(jj, pend):
        return pair(4 * jj + 2, pair(4 * jj, pend))

    n_quads = n_uniform // 4
    pend = lax.fori_loop(0, n_quads, two_pairs, pend_blk)
    rest = n_uniform - 4 * n_quads
    has_pair = rest >= 2
    first = 4 * n_quads

    @pl.when(has_pair)
    def _last_pair():
        pair(first, pend)

    pend = jnp.where(has_pair, first + 1, pend)
    if not drain:
        return pend, a, consume, finish
    last = jnp.where(has_pair, first + 2, first)

    @pl.when(rest % 2 == 1)
    def _odd_tail():
        stage(functools.partial(uniform_scores, last, b), functools.partial(consume, pend, a))
        consume(last, b)
        finish()

    @pl.when(rest % 2 == 0)
    def _even_tail():
        consume(pend, a)
        finish()


def _attn_finish(z_ref, o_ref, acc_s, g_heads, dv, state0):
    for g in range(g_heads):
        out_t = acc_s[state0 + g, :dv, :] / acc_s[state0 + g, dv:dv + 1, :]
        o_ref[:, g * dv:(g + 1) * dv] = (out_t.T * _silu(z_ref[:, g * dv:(g + 1) * dv].astype(F32))
                                         ).astype(o_ref.dtype)


def _t5_thresholds():
    max_exact = T5_BUCKETS // 2
    d = np.arange(0, 4 * T5_MAX_DIST, dtype=np.int64)
    dd = np.maximum(d, 1).astype(np.float32)
    large = max_exact + (np.log(dd / np.float32(max_exact)) / np.float32(math.log(T5_MAX_DIST / max_exact))
                         * np.float32(T5_BUCKETS - max_exact)).astype(np.int32)
    large = np.minimum(large, T5_BUCKETS - 1)
    bucket = np.where(d < max_exact, d, large)
    assert np.all(np.diff(bucket) >= 0) and bucket[-1] == T5_BUCKETS - 1
    return [int(np.argmax(bucket >= k)) for k in range(T5_BUCKETS)]


_T5_THRESH = _t5_thresholds()

MASK_BIG = 1e30


def _moba_body(t5_ref, qt_even_ref, qt_odd_ref, k_ref, vt_ref, z_even_ref, z_odd_ref, o_even_ref, o_odd_ref,
               kmask_s, km_s, bias_own_s, bias_prev_s, qa_s, m_s, acc_s, *s_refs, n_blocks, g_heads):
    hg = pl.program_id(0)
    b = pl.program_id(1)
    i = pl.program_id(2)
    t, hd = A_BLOCK, A_HEAD_DIM
    hv = hd + V_AUG_ROWS
    seq = n_blocks * t
    nb_pad = -(-n_blocks // SUBLANES) * SUBLANES

    @pl.when((hg == 0) & (b == 0) & (i == 0))
    def _mask_columns():
        key_blk = lax.broadcasted_iota(jnp.int32, (seq, LANES), 0) // t
        lane = lax.broadcasted_iota(jnp.int32, (seq, LANES), 1)
        kmask_s[...] = jnp.where(lane == key_blk, -MASK_BIG, 0.0).astype(kmask_s.dtype)

    @pl.when((b == 0) & (i == 0))
    def _build_bias():
        key = lax.broadcasted_iota(jnp.int32, (t, t), 0)
        qry = lax.broadcasted_iota(jnp.int32, (t, t), 1)
        for g in range(g_heads):
            head = hg * g_heads + g
            far = t5_ref[T5_BUCKETS - 1, head]
            for dist, dst in ((qry - key, bias_own_s), (t + qry - key, bias_prev_s)):
                bias = jnp.full((t, t), t5_ref[0, head], F32)
                for kk in range(1, T5_BUCKETS):
                    bias = jnp.where(dist >= _T5_THRESH[kk], t5_ref[kk, head], bias)
                dst[g] = (bias - far) * LOG2E

    @pl.when(i == 0)
    def _new_sequence():
        for g in range(g_heads):
            km_s[g] = jnp.zeros(km_s.shape[1:], km_s.dtype)
            for nb in range(n_blocks):
                mean = jnp.mean(k_ref[nb * t:(nb + 1) * t, g * hd:(g + 1) * hd].astype(F32), axis=0, keepdims=True)
                hi = mean.astype(BF16)
                km_s[g, nb:nb + 1, :] = hi
                km_s[g, LANES + nb:LANES + nb + 1, :] = (mean - hi.astype(F32)).astype(BF16)

    _softmax_init(m_s, acc_s)
    blk_row = lax.broadcasted_iota(jnp.int32, (nb_pad, t), 0)

    def query_tile(tile, qt_ref, z_ref, o_ref, which):
        state0 = which * g_heads

        past = blk_row < tile
        for g in range(g_heads):
            qt = qt_ref[g * hd:(g + 1) * hd, :]
            g2 = jnp.dot(km_s[g], qt, preferred_element_type=F32)
            gate = g2[:nb_pad, :] + g2[LANES:LANES + nb_pad, :]
            cur = jnp.where(past, gate, NEG)
            keep = blk_row == tile
            for _ in range(A_TOPK):
                mx = jnp.max(cur, axis=0, keepdims=True)
                idx = jnp.min(jnp.where(cur == mx, blk_row, nb_pad), axis=0, keepdims=True)
                pick = blk_row == idx
                keep = keep | (pick & past)
                cur = jnp.where(pick, -jnp.inf, cur)
            unsel = jnp.where(keep, 0.0, 1.0)
            qa_s[state0 + g, :hd, :] = qt
            qa_s[state0 + g, hd:, :] = jnp.concatenate([unsel, jnp.zeros((LANES - nb_pad, t), F32)],
                                                       axis=0).astype(qa_s.dtype)

        def raw_scores(g, j):
            start = pl.multiple_of(j * t, t)
            keys = jnp.concatenate([k_ref[pl.ds(start, t), g * hd:(g + 1) * hd], kmask_s[pl.ds(start, t), :]],
                                   axis=1)
            return jnp.dot(keys, qa_s[state0 + g], preferred_element_type=F32)

        def own_scores(slot, heads=range(g_heads)):
            key = lax.broadcasted_iota(jnp.int32, (t, t), 0)
            qry = lax.broadcasted_iota(jnp.int32, (t, t), 1)
            for g in heads:
                s_refs[g][slot] = jnp.where(key <= qry, raw_scores(g, tile) + bias_own_s[g], NEG)

        j_prev = jnp.maximum(tile - 1, 0)
        no_prev = jnp.where(tile == 0, -MASK_BIG, 0.0)

        def prev_scores(slot, heads=range(g_heads)):
            for g in heads:
                s_refs[g][slot] = raw_scores(g, j_prev) + (bias_prev_s[g] + no_prev)

        def far_scores(j, slot, heads=range(g_heads)):
            for g in heads:
                s_refs[g][slot] = raw_scores(g, j)

        def consume(j, slot, heads=range(g_heads)):
            for g in heads:
                _softmax_step_t(s_refs[g][slot], vt_ref[j, g * hv:(g + 1) * hv, :], state0 + g, m_s, acc_s)

        finish = functools.partial(_attn_finish, z_ref, o_ref, acc_s, g_heads, hd, state0)
        return g_heads, [(own_scores, tile), (prev_scores, j_prev)], j_prev, far_scores, consume, finish

    carry = _skewed_blocks(*query_tile(n_blocks - 1 - 2 * i, qt_odd_ref, z_odd_ref, o_odd_ref, 1), drain=False)
    _skewed_blocks(*query_tile(2 * i, qt_even_ref, z_even_ref, o_even_ref, 0), carry_in=carry)


def moba_branch(qt, vt, slab, t5_table, g_heads=ATTN_HEADS_PER_STEP):
    bsz, nb, _, t = qt.shape
    s = nb * t
    hd = A_HEAD_DIM
    gw = g_heads * hd
    assert t == A_BLOCK and nb <= LANES and nb % TILES_PER_STEP == 0 and A_HEADS % g_heads == 0
    assert OFF_KA % gw == 0 and OFF_ZA % gw == 0
    assert _T5_THRESH[T5_BUCKETS - 1] <= t + 1
    ck, cz = OFF_KA // gw, OFF_ZA // gw
    qt_even, qt_odd = _tile_pair_specs(nb, (None, None, gw, t), lambda h, b, tile: (b, tile, h, 0))
    z_even, z_odd = _tile_pair_specs(nb, (None, t, gw), lambda h, b, tile: (b, tile, cz + h))
    out = jax.ShapeDtypeStruct((bsz, nb // 2, t, A_W), BF16)
    return pl.pallas_call(
        functools.partial(_moba_body, n_blocks=nb, g_heads=g_heads),
        grid=(A_HEADS // g_heads, bsz, nb // 2),
        in_specs=[pl.BlockSpec(memory_space=pltpu.SMEM),
                  qt_even, qt_odd,
                  pl.BlockSpec((None, s, gw), lambda h, b, i: (b, 0, ck + h)),
                  pl.BlockSpec((None, nb, g_heads * (hd + V_AUG_ROWS), t), lambda h, b, i: (b, 0, h, 0)),
                  z_even, z_odd],
        out_specs=[pl.BlockSpec((None, None, t, gw), lambda h, b, i: (b, i, 0, h)),
                   pl.BlockSpec((None, None, t, gw), lambda h, b, i: (b, nb // 2 - 1 - i, 0, h))],
        out_shape=[out, out],
        scratch_shapes=[pltpu.VMEM((s, LANES), BF16),
                        pltpu.VMEM((g_heads, 2 * LANES, hd), BF16),
                        pltpu.VMEM((g_heads, t, t), F32),
                        pltpu.VMEM((g_heads, t, t), F32),
                        pltpu.VMEM((TILES_PER_STEP * g_heads, ATTN_QK_PAD, t), BF16)]
        + _attn_scratch(g_heads, t, hd),
        compiler_params=_cparams(3),
        name="moba",
    )(t5_table, qt, qt, slab, vt, slab, slab)


MLSTM_CHUNK = 256
MLSTM_SEQS_PER_STEP = 1
CONV_HALO = 8


def _log_sigmoid(t):
    return jnp.minimum(t, 0.0) - jnp.log1p(jnp.exp(-jnp.abs(t)))


def _mlstm_body(qk_ref, v_ref, ob_ref, zb_ref, if_ref, cw_ref, cb_ref, gb_ref, gn_ref, o_ref,
                xe_s, c_s, n_s, m_s):
    c = pl.program_id(1)

    @pl.when(c == 0)
    def _reset():
        xe_s[...] = jnp.zeros_like(xe_s)
        c_s[...] = jnp.zeros_like(c_s)
        n_s[...] = jnp.zeros_like(n_s)
        m_s[...] = jnp.zeros_like(m_s)

    for bi in range(qk_ref.shape[0]):
        _mlstm_chunk(bi, qk_ref, v_ref, ob_ref, zb_ref, if_ref, cw_ref, cb_ref, gb_ref, gn_ref, o_ref,
                     xe_s, c_s, n_s, m_s)


def _mlstm_chunk(bi, qk_ref, v_ref, ob_ref, zb_ref, if_ref, cw_ref, cb_ref, gb_ref, gn_ref, o_ref,
                 xe_s, c_s, n_s, m_s):
    L = MLSTM_CHUNK
    dk, dv = B_QK_DIM, B_V_DIM

    x = qk_ref[bi].astype(F32)
    xe = jnp.concatenate([xe_s[bi], x], axis=0)
    conv = cb_ref[...] + cw_ref[B_CONV - 1:B_CONV, :] * x
    for j in range(B_CONV - 1):
        back = B_CONV - 1 - j
        conv = conv + cw_ref[j:j + 1, :] * pltpu.roll(xe, back, axis=0)[CONV_HALO:, :]
    xe_s[bi] = x[L - CONV_HALO:, :]
    qk = _silu(conv)

    gi = if_ref[bi].astype(F32) + gb_ref[...]
    lf = _log_sigmoid(gi)
    row = lax.broadcasted_iota(jnp.int32, (L, L), 0)
    col = lax.broadcasted_iota(jnp.int32, (L, L), 1)
    causal = col <= row
    tri = jnp.where(causal, 1.0, 0.0).astype(F32)
    b_cols = jnp.dot(tri, lf, precision=lax.Precision.HIGHEST, preferred_element_type=F32)
    b_rows = b_cols.T
    li_rows = gi.T

    for h in range(B_HEADS):
        q_h = qk[:, h * dk:(h + 1) * dk]
        k_h = qk[:, B_QK_W + h * dk:B_QK_W + (h + 1) * dk] * (dk ** -0.5)
        v_h = v_ref[bi, :, h * dv:(h + 1) * dv]
        b_c = b_cols[:, B_HEADS + h:B_HEADS + h + 1]
        b_r = b_rows[B_HEADS + h:B_HEADS + h + 1, :]
        li_c = gi[:, h:h + 1]
        li_r = li_rows[h:h + 1, :]
        m_prev = m_s[bi, h:h + 1, 0:1]
        c_prev = c_s[bi * B_HEADS + h]
        n_prev = n_s[bi, h:h + 1, :]

        a_c = b_c + m_prev
        dmat = jnp.where(causal, b_c - b_r + li_r, -jnp.inf)
        m_t = jnp.maximum(a_c, jnp.max(dmat, axis=-1, keepdims=True))
        w_inter = jnp.exp(a_c - m_t)
        q_b = q_h.astype(BF16)
        k_b = k_h.astype(BF16)
        sc = (lax.dot_general(q_b, k_b, _NT, preferred_element_type=F32)
              * jnp.exp(dmat - m_t))
        num = (w_inter * jnp.dot(q_b, c_prev.astype(BF16), preferred_element_type=F32)
               + jnp.dot(sc.astype(BF16), v_h, preferred_element_type=F32))
        den = (w_inter * jnp.sum(q_h * n_prev, axis=-1, keepdims=True)
               + jnp.sum(sc, axis=-1, keepdims=True))
        hh = num / jnp.maximum(jnp.abs(den), jnp.exp(-m_t))

        b_last = b_c[L - 1:L, :]
        g_c = b_last - b_c + li_c
        m_new = jnp.maximum(b_last + m_prev, jnp.max(g_c, axis=0, keepdims=True))
        decay = jnp.exp(b_last + m_prev - m_new)
        wk = jnp.exp(g_c - m_new) * k_h
        c_s[bi * B_HEADS + h] = decay * c_prev + lax.dot_general(
            wk.astype(BF16), v_h, (((0,), (0,)), ((), ())), preferred_element_type=F32)
        n_s[bi, h:h + 1, :] = decay * n_prev + jnp.sum(wk, axis=0, keepdims=True)
        m_s[bi, h:h + 1, :] = jnp.broadcast_to(m_new, (1, LANES))

        mu = jnp.mean(hh, axis=-1, keepdims=True)
        dlt = hh - mu
        var = jnp.mean(dlt * dlt, axis=-1, keepdims=True)
        y = dlt * lax.rsqrt(var + EPS) * gn_ref[:, h * dv:(h + 1) * dv]
        y = (y * jax.nn.sigmoid(ob_ref[bi, :, h * dv:(h + 1) * dv].astype(F32))
             * _silu(zb_ref[bi, :, h * dv:(h + 1) * dv].astype(F32)))
        o_ref[bi, :, h * dv:(h + 1) * dv] = y.astype(o_ref.dtype)


def mlstm_branch(slab, slab_small, conv_w, conv_b, i_bias, f_bias, out_norm, nb=MLSTM_SEQS_PER_STEP):
    bsz, s, _ = slab.shape
    L = MLSTM_CHUNK
    nb = math.gcd(nb, bsz)
    assert s % L == 0
    gate_bias = jnp.zeros((1, LANES), F32).at[0, :B_HEADS].set(i_bias).at[0, B_HEADS:2 * B_HEADS].set(f_bias)
    w2 = 2 * B_QK_W
    full = lambda shape: pl.BlockSpec(shape, lambda b, c: (0,) * len(shape))
    return pl.pallas_call(
        _mlstm_body,
        grid=(bsz // nb, s // L),
        in_specs=[pl.BlockSpec((nb, L, w2), lambda b, c: (b, c, OFF_QKB // w2)),
                  pl.BlockSpec((nb, L, B_V_W), lambda b, c: (b, c, OFF_VB // B_V_W)),
                  pl.BlockSpec((nb, L, B_V_W), lambda b, c: (b, c, OFF_OB // B_V_W)),
                  pl.BlockSpec((nb, L, B_V_W), lambda b, c: (b, c, OFF_ZB // B_V_W)),
                  pl.BlockSpec((nb, L, LANES), lambda b, c: (b, c, OFF_IF // LANES)),
                  full((B_CONV, w2)), full((1, w2)), full((1, LANES)), full((1, B_V_W))],
        out_specs=pl.BlockSpec((nb, L, B_V_W), lambda b, c: (b, c, 0)),
        out_shape=jax.ShapeDtypeStruct((bsz, s, B_V_W), BF16),
        scratch_shapes=[pltpu.VMEM((nb, CONV_HALO, w2), F32),
                        pltpu.VMEM((nb * B_HEADS, B_QK_DIM, B_V_DIM), F32),
                        pltpu.VMEM((nb, SUBLANES, B_QK_DIM), F32),
                        pltpu.VMEM((nb, SUBLANES, LANES), F32)],
        compiler_params=_cparams(2),
        name="mlstm",
    )(slab, slab, slab, slab, slab_small, conv_w, conv_b.reshape(1, w2), gate_bias,
      out_norm.reshape(1, B_V_W))


def _mla_prep_body(cq_ref, ckv_ref, kr_ref, qg_ref, kg_ref, wqt_ref, wqst_ref, wkn_ref, wvt_ref,
                   cos_ref, sin_ref, cost_ref, sint_ref, qt_ref, kf_ref, vt_ref):
    def normed(ref, g_ref):
        t = ref[...].astype(F32)
        y = t * lax.rsqrt(jnp.mean(t * t, axis=-1, keepdims=True) + EPS)
        return (y * g_ref[...]).astype(BF16)

    cqn = normed(cq_ref, qg_ref)
    ckvn = normed(ckv_ref, kg_ref)

    qt_main = lax.dot_general(wqt_ref[...], cqn, _NT, preferred_element_type=F32)
    qt_swap = lax.dot_general(wqst_ref[...], cqn, _NT, preferred_element_type=F32)
    cos_t = cost_ref[...]
    sin_t = sint_ref[...]
    t = ATTN_TILE
    n_sub = cos_t.shape[1] // t
    for h in range(C_HEADS):
        lo = h * ATTN_QK_PAD
        nope = qt_main[lo:lo + LANES, :].astype(qt_ref.dtype)
        rope = (qt_main[lo + LANES:lo + 2 * LANES, :] * cos_t
                + qt_swap[h * LANES:(h + 1) * LANES, :] * sin_t).astype(qt_ref.dtype)
        for c in range(n_sub):
            qt_ref[c, lo:lo + LANES, :] = nope[:, c * t:(c + 1) * t]
            qt_ref[c, lo + LANES:lo + 2 * LANES, :] = rope[:, c * t:(c + 1) * t]

    k_nope = jnp.dot(ckvn, wkn_ref[...], preferred_element_type=F32)
    kr = kr_ref[...].astype(F32)
    half = C_ROPE // 2
    lane = lax.broadcasted_iota(jnp.int32, kr.shape, 1)
    swapped = jnp.where(lane < half, -pltpu.roll(kr, LANES - half, axis=1), pltpu.roll(kr, half, axis=1))
    k_rot = (kr * cos_ref[...] + swapped * sin_ref[...]).astype(kf_ref.dtype)
    for h in range(C_HEADS):
        lo = h * ATTN_QK_PAD
        kf_ref[:, lo:lo + LANES] = k_nope[:, h * LANES:(h + 1) * LANES].astype(kf_ref.dtype)
        kf_ref[:, lo + LANES:lo + 2 * LANES] = k_rot

    v_t = lax.dot_general(wvt_ref[...], ckvn, _NT, preferred_element_type=F32)
    hv = C_V_DIM + V_AUG_ROWS
    aug = _v_aug_rows(t, vt_ref.dtype)
    for h in range(C_HEADS):
        v_h = v_t[h * C_V_DIM:(h + 1) * C_V_DIM, :].astype(vt_ref.dtype)
        for c in range(n_sub):
            vt_ref[c, h * hv:h * hv + C_V_DIM, :] = v_h[:, c * t:(c + 1) * t]
            vt_ref[c, h * hv + C_V_DIM:(h + 1) * hv, :] = aug


def mla_prep(slab, q_norm, kv_norm, w_uq, w_ukv, tm=1024):
    bsz, s, _ = slab.shape
    t = ATTN_TILE
    n_sub = tm // t
    vt_rows = C_HEADS * (C_V_DIM + V_AUG_ROWS)
    half = C_ROPE // 2
    scale = (C_NOPE + C_ROPE) ** -0.5 * LOG2E
    wq = (w_uq * scale).reshape(C_Q_RANK, C_HEADS, C_NOPE + C_ROPE)
    pad = jnp.zeros((C_Q_RANK, C_HEADS, LANES - C_ROPE), F32)
    x1, x2 = wq[..., C_NOPE:C_NOPE + half], wq[..., C_NOPE + half:]
    wqt = jnp.concatenate([wq, pad], axis=-1).reshape(C_Q_RANK, C_HEADS * ATTN_QK_PAD).T.astype(BF16)
    wqst = jnp.concatenate([-x2, x1, pad], axis=-1).reshape(C_Q_RANK, C_HEADS * LANES).T.astype(BF16)
    wkv = w_ukv.reshape(C_KV_RANK, C_HEADS, C_NOPE + C_V_DIM)
    wkn = wkv[..., :C_NOPE].reshape(C_KV_RANK, C_HEADS * C_NOPE).astype(BF16)
    wvt = wkv[..., C_NOPE:].reshape(C_KV_RANK, C_W).T.astype(BF16)

    pos = jnp.arange(s, dtype=jnp.int32)
    inv = ROPE_THETA ** (-jnp.arange(half, dtype=F32) / half)
    ang = pos.astype(F32)[:, None] * inv[None, :]
    zpad = jnp.zeros((s, LANES - C_ROPE), F32)
    cos_tab = jnp.concatenate([jnp.cos(ang), jnp.cos(ang), zpad], axis=-1)
    sin_tab = jnp.concatenate([jnp.sin(ang), jnp.sin(ang), zpad], axis=-1)

    nt = s // t
    full = lambda shape: pl.BlockSpec(shape, lambda b, i: (0,) * len(shape))
    return pl.pallas_call(
        _mla_prep_body,
        grid=(bsz, s // tm),
        in_specs=[pl.BlockSpec((None, tm, C_Q_RANK), lambda b, i: (b, i, OFF_CQ // C_Q_RANK)),
                  pl.BlockSpec((None, tm, C_KV_RANK), lambda b, i: (b, i, OFF_CKV // C_KV_RANK)),
                  pl.BlockSpec((None, tm, LANES), lambda b, i: (b, i, OFF_KR // LANES)),
                  full((1, C_Q_RANK)), full((1, C_KV_RANK)),
                  full(wqt.shape), full(wqst.shape), full(wkn.shape), full(wvt.shape),
                  pl.BlockSpec((tm, LANES), lambda b, i: (i, 0)),
                  pl.BlockSpec((tm, LANES), lambda b, i: (i, 0)),
                  pl.BlockSpec((LANES, tm), lambda b, i: (0, i)),
                  pl.BlockSpec((LANES, tm), lambda b, i: (0, i))],
        out_specs=[pl.BlockSpec((None, n_sub, C_HEADS * ATTN_QK_PAD, t), lambda b, i: (b, i, 0, 0)),
                   pl.BlockSpec((None, tm, C_HEADS * ATTN_QK_PAD), lambda b, i: (b, i, 0)),
                   pl.BlockSpec((None, n_sub, vt_rows, t), lambda b, i: (b, i, 0, 0))],
        out_shape=[jax.ShapeDtypeStruct((bsz, nt, C_HEADS * ATTN_QK_PAD, t), BF16),
                   jax.ShapeDtypeStruct((bsz, s, C_HEADS * ATTN_QK_PAD), BF16),
                   jax.ShapeDtypeStruct((bsz, nt, vt_rows, t), BF16)],
        compiler_params=_cparams(2),
        name="mla_prep",
    )(slab, slab, slab, q_norm.reshape(1, C_Q_RANK), kv_norm.reshape(1, C_KV_RANK),
      wqt, wqst, wkn, wvt, cos_tab, sin_tab, cos_tab.T, sin_tab.T)


def _mla_attn_body(qt_even_ref, qt_odd_ref, k_ref, vt_ref, z_even_ref, z_odd_ref, o_even_ref, o_odd_ref,
                   m_s, acc_s, *s_refs, g_heads, n_tiles):
    i = pl.program_id(2)
    t = ATTN_TILE
    dq, dv = ATTN_QK_PAD, C_V_DIM
    hv = dv + V_AUG_ROWS
    _softmax_init(m_s, acc_s)

    def query_tile(tile, qt_ref, z_ref, o_ref, which):
        state0 = which * g_heads

        def raw_scores(g, j):
            start = pl.multiple_of(j * t, t)
            return jnp.dot(k_ref[pl.ds(start, t), g * dq:(g + 1) * dq], qt_ref[g * dq:(g + 1) * dq, :],
                           preferred_element_type=F32)

        def diagonal_scores(slot, heads=range(g_heads)):
            key = lax.broadcasted_iota(jnp.int32, (t, t), 0)
            qry = lax.broadcasted_iota(jnp.int32, (t, t), 1)
            for g in heads:
                s_refs[g][slot] = jnp.where(key <= qry, raw_scores(g, tile), NEG)

        def past_scores(j, slot, heads=range(g_heads)):
            for g in heads:
                s_refs[g][slot] = raw_scores(g, j)

        def consume(j, slot, heads=range(g_heads)):
            for g in heads:
                _softmax_step_t(s_refs[g][slot], vt_ref[j, g * hv:(g + 1) * hv, :], state0 + g, m_s, acc_s)

        finish = functools.partial(_attn_finish, z_ref, o_ref, acc_s, g_heads, dv, state0)
        return g_heads, [(diagonal_scores, tile)], tile, past_scores, consume, finish

    carry = _skewed_blocks(*query_tile(2 * i, qt_even_ref, z_even_ref, o_even_ref, 0), drain=False)
    _skewed_blocks(*query_tile(n_tiles - 1 - 2 * i, qt_odd_ref, z_odd_ref, o_odd_ref, 1), carry_in=carry)


def mla_attention(qt, kf, vt, slab, g_heads=ATTN_HEADS_PER_STEP):
    bsz, nt, _, t = qt.shape
    s = nt * t
    dq, dv = ATTN_QK_PAD, C_V_DIM
    assert t == ATTN_TILE and nt % TILES_PER_STEP == 0
    assert C_HEADS % g_heads == 0 and OFF_ZC % (g_heads * dv) == 0
    cz = OFF_ZC // (g_heads * dv)
    qt_even, qt_odd = _tile_pair_specs(nt, (None, None, g_heads * dq, t), lambda b, h, tile: (b, tile, h, 0))
    z_even, z_odd = _tile_pair_specs(nt, (None, t, g_heads * dv), lambda b, h, tile: (b, tile, cz + h))
    out = jax.ShapeDtypeStruct((bsz, nt // 2, t, C_W), BF16)
    return pl.pallas_call(
        functools.partial(_mla_attn_body, g_heads=g_heads, n_tiles=nt),
        grid=(bsz, C_HEADS // g_heads, nt // 2),
        in_specs=[qt_even, qt_odd,
                  pl.BlockSpec((None, s, g_heads * dq), lambda b, h, i: (b, 0, h),
                               pipeline_mode=pl.Buffered(1)),
                  pl.BlockSpec((None, nt, g_heads * (dv + V_AUG_ROWS), t), lambda b, h, i: (b, 0, h, 0)),
                  z_even, z_odd],
        out_specs=[pl.BlockSpec((None, None, t, g_heads * dv), lambda b, h, i: (b, i, 0, h)),
                   pl.BlockSpec((None, None, t, g_heads * dv), lambda b, h, i: (b, nt // 2 - 1 - i, 0, h))],
        out_shape=[out, out],
        scratch_shapes=_attn_scratch(g_heads, t, dv),
        compiler_params=_cparams(3),
        name="mla_attn",
    )(qt, qt, kf, vt, slab, slab)


def _merge_body(ya_even_ref, ya_odd_ref, yb_ref, yc_even_ref, yc_odd_ref, wa_ref, wb_ref, wc_ref,
                ga_ref, gb_ref, gc_ref, o_ref, w_s):
    @pl.when(pl.program_id(1) == 0)
    def _cast_weights():
        for n, w_ref in enumerate((wa_ref, wb_ref, wc_ref)):
            w_s[n] = w_ref[...].astype(w_s.dtype)

    def rows(even_ref, odd_ref):
        return jnp.concatenate([ref[p] for p in range(even_ref.shape[0]) for ref in (even_ref, odd_ref)], axis=0)

    ya = rows(ya_even_ref, ya_odd_ref)
    yc = rows(yc_even_ref, yc_odd_ref)
    acc = None
    for n, (y, g_ref) in enumerate(((ya, ga_ref), (yb_ref[...], gb_ref), (yc, gc_ref))):
        term = jax.nn.sigmoid(g_ref[...].astype(F32)) * jnp.dot(y, w_s[n], preferred_element_type=F32)
        acc = term if acc is None else acc + term
    o_ref[...] = acc.astype(o_ref.dtype)


def branch_merge(ya_tiles, yb, yc_tiles, w_branch, layer, slab2d, tm=1024, tn=1024):
    m, w = yb.shape
    d = w_branch.shape[-1]
    t = ATTN_TILE
    pairs = tm // (TILES_PER_STEP * t)
    assert OFF_GT % tn == 0 and pairs * TILES_PER_STEP * t == tm
    g0 = OFF_GT // tn
    gper = d // tn
    halves = [y.reshape(m // (TILES_PER_STEP * t), t, w) for y in (*ya_tiles, *yc_tiles)]
    half_spec = pl.BlockSpec((pairs, t, w), lambda j, i: (i, 0, 0))
    w_specs = [pl.BlockSpec((None, None, w, tn), functools.partial(lambda j, i, n: (layer, n, 0, j), n=n),
                            pipeline_mode=pl.Buffered(1))
               for n in range(N_BRANCH)]
    g_specs = [pl.BlockSpec((tm, tn), functools.partial(lambda j, i, n: (i, g0 + n * gper + j), n=n))
               for n in range(N_BRANCH)]
    return pl.pallas_call(
        _merge_body,
        grid=(d // tn, m // tm),
        in_specs=[half_spec, half_spec, pl.BlockSpec((tm, w), lambda j, i: (i, 0)), half_spec, half_spec]
        + w_specs + g_specs,
        out_specs=pl.BlockSpec((tm, tn), lambda j, i: (i, j)),
        out_shape=jax.ShapeDtypeStruct((m, d), BF16),
        scratch_shapes=[pltpu.VMEM((N_BRANCH, w, tn), BF16)],
        compiler_params=_cparams(2),
        name="branch_merge",
    )(halves[0], halves[1], yb, halves[2], halves[3], w_branch, w_branch, w_branch, slab2d, slab2d, slab2d)


def _out_body(mg_ref, w_ref, x_ref, g_ref, *refs, last):
    out_refs, w_s = refs[:-1], refs[-1]

    @pl.when(pl.program_id(0) == 0)
    def _cast_weights():
        w_s[...] = w_ref[...].astype(w_s.dtype)

    x_new = x_ref[...] + jnp.dot(mg_ref[...], w_s[...], preferred_element_type=F32)
    y = x_new * lax.rsqrt(jnp.mean(x_new * x_new, axis=-1, keepdims=True) + EPS) * g_ref[...]
    if last:
        out_refs[0][...] = y
    else:
        out_refs[0][...] = x_new
        out_refs[1][...] = y.astype(out_refs[1].dtype)


def out_projection(merged, w_out, layer, x2d, gain, last, tm=512):
    m, d = x2d.shape
    row = pl.BlockSpec((tm, d), lambda i: (i, 0))
    if last:
        out_specs, out_shape = row, jax.ShapeDtypeStruct((m, d), F32)
    else:
        out_specs = [row, row]
        out_shape = [jax.ShapeDtypeStruct((m, d), F32), jax.ShapeDtypeStruct((m, d), BF16)]
    return pl.pallas_call(
        functools.partial(_out_body, last=last),
        grid=(m // tm,),
        in_specs=[row,
                  pl.BlockSpec((None, d, d), lambda i: (layer, 0, 0), pipeline_mode=pl.Buffered(1)),
                  row, pl.BlockSpec((1, d), lambda i: (0, 0))],
        out_specs=out_specs,
        out_shape=out_shape,
        scratch_shapes=[pltpu.VMEM((d, d), BF16)],
        compiler_params=_cparams(1),
        name="out_proj",
    )(merged, w_out, x2d, gain.reshape(1, d))


def kernel(x, norm_gain, w_in, t5_table, mlstm_conv_w, mlstm_conv_b, mlstm_i_bias, mlstm_f_bias,
           mlstm_out_norm, mla_q_norm, mla_kv_norm, mla_w_uq, mla_w_ukv, w_branch, w_out, final_norm):
    bsz, s, d = x.shape
    m = bsz * s
    x2d = x.reshape(m, d)
    h = rmsnorm_rows(x2d, norm_gain[0])
    w_in_t = jnp.swapaxes(w_in, 1, 2)
    out = None
    for l in range(DEPTH):
        slab2d = input_projection(h, w_in_t, l, SLAB_SRC_ROWS)
        slab = slab2d.reshape(bsz, s, D_SLAB)
        slab_small = input_projection_small(h, w_in_t, l).reshape(bsz, s, D_SLAB_SMALL)
        qt_a, vt_a = moba_qv_projection(h, w_in_t, l, bsz)
        ya = moba_branch(qt_a, vt_a, slab, t5_table)
        yb = mlstm_branch(slab, slab_small, mlstm_conv_w[l], mlstm_conv_b[l], mlstm_i_bias[l], mlstm_f_bias[l],
                          mlstm_out_norm[l])
        qt_c, k_c, vt_c = mla_prep(slab_small, mla_q_norm[l], mla_kv_norm[l], mla_w_uq[l], mla_w_ukv[l])
        yc = mla_attention(qt_c, k_c, vt_c, slab)
        merged = branch_merge(ya, yb.reshape(m, B_V_W), yc, w_branch, l, slab2d)
        last = l == DEPTH - 1
        gain = final_norm if last else norm_gain[l + 1]
        res = out_projection(merged, w_out, l, x2d, gain, last)
        if last:
            out = res
        else:
            x2d, h = res
    return out.reshape(bsz, s, d)
```

```python
import functools
import math

import jax
import jax.numpy as jnp
import numpy as np
from jax import lax
from jax.experimental import pallas as pl
from jax.experimental.pallas import tpu as pltpu

F32 = jnp.float32
BF16 = jnp.bfloat16

D_MODEL = 2048
DEPTH = 2
EPS = 1e-6
NEG = -1e30
LOG2E = math.log2(math.e)

A_HEADS = 8
A_HEAD_DIM = 128
A_BLOCK = 256
A_TOPK = 3
T5_BUCKETS = 32
T5_MAX_DIST = 128
B_HEADS = 4
B_QK_DIM = 128
B_V_DIM = 256
B_CONV = 4
C_HEADS = 8
C_Q_RANK = 512
C_KV_RANK = 256
C_NOPE = 128
C_ROPE = 64
C_V_DIM = 128
ROPE_THETA = 10000.0
N_BRANCH = 3

A_W = A_HEADS * A_HEAD_DIM
B_QK_W = B_HEADS * B_QK_DIM
B_V_W = B_HEADS * B_V_DIM
C_W = C_HEADS * C_V_DIM

LANES = 128
SUBLANES = 8
VMEM_LIMIT_BYTES = 58 * 1024 * 1024

SPLIT_SIZES = (A_W, A_W, A_W, A_W, B_QK_W, B_QK_W, B_V_W, B_HEADS, B_HEADS, B_V_W, B_V_W,
               C_Q_RANK, C_KV_RANK, C_ROPE, C_W, N_BRANCH * D_MODEL)
(SRC_QA, SRC_KA, SRC_VA, SRC_ZA, SRC_QB, SRC_KB, SRC_VB, SRC_IB, SRC_FB, SRC_OB, SRC_ZB,
 SRC_CQ, SRC_CKV, SRC_KR, SRC_ZC, SRC_GT) = (int(v) for v in np.cumsum((0,) + SPLIT_SIZES[:-1]))
D_IN = int(sum(SPLIT_SIZES))

W_BLOCK = 1024
SLAB_SRC_ROWS = ((SRC_KA, SRC_ZA, SRC_QB, SRC_VB, SRC_OB, SRC_ZB, SRC_ZC)
                 + tuple(SRC_GT + k * W_BLOCK for k in range(N_BRANCH * D_MODEL // W_BLOCK)))
assert SRC_KB == SRC_QB + B_QK_W and all(r % SUBLANES == 0 for r in SLAB_SRC_ROWS)
OFF_KA = 0
OFF_ZA = OFF_KA + W_BLOCK
OFF_QKB = OFF_ZA + W_BLOCK
OFF_VB = OFF_QKB + W_BLOCK
OFF_OB = OFF_VB + W_BLOCK
OFF_ZB = OFF_OB + W_BLOCK
OFF_ZC = OFF_ZB + W_BLOCK
OFF_GT = OFF_ZC + W_BLOCK
D_SLAB = len(SLAB_SRC_ROWS) * W_BLOCK
SMALL_WIN1_ROWS = C_Q_RANK + C_KV_RANK + LANES
SMALL_WIN2_ROWS = LANES
assert SRC_CKV == SRC_CQ + C_Q_RANK and SRC_KR == SRC_CKV + C_KV_RANK and SRC_FB == SRC_IB + B_HEADS
assert SRC_CQ % SUBLANES == 0 and SRC_IB % SUBLANES == 0
assert SRC_CQ + SMALL_WIN1_ROWS <= D_IN and SRC_IB + SMALL_WIN2_ROWS <= D_IN
OFF_CQ = 0
OFF_CKV = OFF_CQ + C_Q_RANK
OFF_KR = OFF_CKV + C_KV_RANK
OFF_IF = OFF_KR + LANES
D_SLAB_SMALL = SMALL_WIN1_ROWS + SMALL_WIN2_ROWS
assert D_SLAB_SMALL == W_BLOCK

ATTN_TILE = 256
ATTN_QK_PAD = 2 * LANES
ATTN_HEADS_PER_STEP = 8
assert ATTN_TILE == A_BLOCK


def _cparams(n_axes):
    return pltpu.CompilerParams(dimension_semantics=("arbitrary",) * n_axes,
                                vmem_limit_bytes=VMEM_LIMIT_BYTES)


def _silu(t):
    return t * jax.nn.sigmoid(t)


_NT = (((1,), (1,)), ((), ()))


def _rmsnorm_body(x_ref, g_ref, o_ref):
    xf = x_ref[...]
    y = xf * lax.rsqrt(jnp.mean(xf * xf, axis=-1, keepdims=True) + EPS)
    o_ref[...] = (y * g_ref[...]).astype(o_ref.dtype)


def rmsnorm_rows(x2d, gain, tm=1024):
    m, d = x2d.shape
    return pl.pallas_call(
        _rmsnorm_body,
        grid=(m // tm,),
        in_specs=[pl.BlockSpec((tm, d), lambda i: (i, 0)),
                  pl.BlockSpec((1, d), lambda i: (0, 0))],
        out_specs=pl.BlockSpec((tm, d), lambda i: (i, 0)),
        out_shape=jax.ShapeDtypeStruct((m, d), BF16),
        compiler_params=_cparams(1),
        name="rmsnorm",
    )(x2d, gain.reshape(1, d))


def _row_window(rows, d):
    return (pl.Squeezed(), pl.Element(rows), pl.Element(d))


def _proj_body(rows_ref, h_ref, w_ref, o_ref, wb_s):
    del rows_ref
    @pl.when(pl.program_id(1) == 0)
    def _cast_weights():
        for c in range(0, w_ref.shape[0], LANES):
            wb_s[:, c:c + LANES] = w_ref[c:c + LANES, :].T.astype(wb_s.dtype)

    o_ref[...] = jnp.dot(h_ref[...], wb_s[...], preferred_element_type=F32).astype(o_ref.dtype)


def input_projection(h2d, w_in_t, layer, src_rows, tm=2048):
    m, d = h2d.shape
    tn = W_BLOCK
    grid_spec = pltpu.PrefetchScalarGridSpec(
        num_scalar_prefetch=1,
        grid=(len(src_rows), m // tm),
        in_specs=[pl.BlockSpec((tm, d), lambda j, i, rows: (i, 0)),
                  pl.BlockSpec(_row_window(tn, d),
                               lambda j, i, rows: (layer, pl.multiple_of(rows[j], SUBLANES), 0))],
        out_specs=pl.BlockSpec((tm, tn), lambda j, i, rows: (i, j)),
        scratch_shapes=[pltpu.VMEM((d, tn), BF16)])
    return pl.pallas_call(
        _proj_body,
        grid_spec=grid_spec,
        out_shape=jax.ShapeDtypeStruct((m, len(src_rows) * tn), BF16),
        compiler_params=_cparams(2),
        name="input_proj",
    )(jnp.asarray(src_rows, jnp.int32), h2d, w_in_t)


def _proj_small_body(h_ref, w1_ref, w2_ref, o_ref, wb_s):
    @pl.when(pl.program_id(0) == 0)
    def _cast_weights():
        wb_s[:SMALL_WIN1_ROWS, :] = w1_ref[...].astype(wb_s.dtype)
        wb_s[SMALL_WIN1_ROWS:, :] = w2_ref[...].astype(wb_s.dtype)

    o_ref[...] = lax.dot_general(h_ref[...], wb_s[...], _NT, preferred_element_type=F32).astype(o_ref.dtype)


def input_projection_small(h2d, w_in_t, layer, tm=2048):
    m, d = h2d.shape
    return pl.pallas_call(
        _proj_small_body,
        grid=(m // tm,),
        in_specs=[pl.BlockSpec((tm, d), lambda i: (i, 0)),
                  pl.BlockSpec(_row_window(SMALL_WIN1_ROWS, d), lambda i: (layer, SRC_CQ, 0)),
                  pl.BlockSpec(_row_window(SMALL_WIN2_ROWS, d), lambda i: (layer, SRC_IB, 0))],
        out_specs=pl.BlockSpec((tm, D_SLAB_SMALL), lambda i: (i, 0)),
        out_shape=jax.ShapeDtypeStruct((m, D_SLAB_SMALL), BF16),
        scratch_shapes=[pltpu.VMEM((D_SLAB_SMALL, d), BF16)],
        compiler_params=_cparams(1),
        name="input_proj_small",
    )(h2d, w_in_t, w_in_t)


def _proj_t_body(h_ref, wq_ref, wv_ref, qt_ref, vt_ref, wt_s, *, n_sub):
    t = ATTN_TILE
    hd = A_HEAD_DIM
    hv = hd + V_AUG_ROWS

    @pl.when((pl.program_id(0) == 0) & (pl.program_id(1) == 0))
    def _cast_weights():
        q_scale = A_HEAD_DIM ** -0.5 * LOG2E
        wt_s[:A_W, :] = (wq_ref[...] * q_scale).astype(wt_s.dtype)
        wt_s[A_W:, :] = wv_ref[...].astype(wt_s.dtype)

    res = lax.dot_general(wt_s[...], h_ref[...], _NT, preferred_element_type=F32)
    aug = _v_aug_rows(t, vt_ref.dtype)
    for c in range(n_sub):
        qt_ref[c] = res[:A_W, c * t:(c + 1) * t].astype(qt_ref.dtype)
        for h in range(A_HEADS):
            vt_ref[c, h * hv:h * hv + hd, :] = res[A_W + h * hd:A_W + (h + 1) * hd, c * t:(c + 1) * t
                                                   ].astype(vt_ref.dtype)
            vt_ref[c, h * hv + hd:(h + 1) * hv, :] = aug


def moba_qv_projection(h2d, w_in_t, layer, bsz, tm=1024):
    m, d = h2d.shape
    assert SRC_QA % A_W == 0 and SRC_VA % A_W == 0
    t = ATTN_TILE
    tm = min(tm, m // bsz)
    n_sub = tm // t
    nt = m // bsz // t
    steps = m // bsz // tm
    vt_rows = A_HEADS * (A_HEAD_DIM + V_AUG_ROWS)
    return pl.pallas_call(
        functools.partial(_proj_t_body, n_sub=n_sub),
        grid=(bsz, steps),
        in_specs=[pl.BlockSpec((tm, d), lambda b, i: (b * steps + i, 0)),
                  pl.BlockSpec((None, A_W, d), lambda b, i: (layer, SRC_QA // A_W, 0),
                               pipeline_mode=pl.Buffered(1)),
                  pl.BlockSpec((None, A_W, d), lambda b, i: (layer, SRC_VA // A_W, 0),
                               pipeline_mode=pl.Buffered(1))],
        out_specs=[pl.BlockSpec((None, n_sub, A_W, t), lambda b, i: (b, i, 0, 0)),
                   pl.BlockSpec((None, n_sub, vt_rows, t), lambda b, i: (b, i, 0, 0))],
        out_shape=[jax.ShapeDtypeStruct((bsz, nt, A_W, t), BF16),
                   jax.ShapeDtypeStruct((bsz, nt, vt_rows, t), BF16)],
        scratch_shapes=[pltpu.VMEM((2 * A_W, d), BF16)],
        compiler_params=_cparams(2),
        name="moba_qv_proj",
    )(h2d, w_in_t, w_in_t)


V_AUG_ROWS = 16


def _v_aug_rows(t, dtype):
    return jnp.where(lax.broadcasted_iota(jnp.int32, (V_AUG_ROWS, t), 0) == 0, 1.0, 0.0).astype(dtype)


def _softmax_init(m_s, acc_s):
    m_s[...] = jnp.full(m_s.shape, -jnp.inf, F32)
    acc_s[...] = jnp.zeros(acc_s.shape, F32)


def _softmax_step_t(s, vt_blk, g, m_s, acc_s):
    m_prev = m_s[g]
    m_new = jnp.maximum(m_prev, jnp.max(s, axis=0, keepdims=True))
    alpha = jnp.exp2(m_prev - m_new)
    p = jnp.exp2(s - m_new)
    acc_s[g] = alpha * acc_s[g] + jnp.dot(vt_blk, p.astype(BF16), preferred_element_type=F32)
    m_s[g] = m_new


TILES_PER_STEP = 2


def _attn_scratch(g_heads, t, dv):
    n_state = TILES_PER_STEP * g_heads
    state = [pltpu.VMEM((n_state, 1, t), F32), pltpu.VMEM((n_state, dv + V_AUG_ROWS, t), F32)]
    return state + [pltpu.VMEM((2, t, t), F32)] * g_heads


def _tile_pair_specs(nt, block_shape, index_of_tile):
    even = pl.BlockSpec(block_shape, lambda x, y, p: index_of_tile(x, y, 2 * p))
    odd = pl.BlockSpec(block_shape, lambda x, y, p: index_of_tile(x, y, nt - 1 - 2 * p))
    return even, odd


def _skewed_blocks(n_heads, lead, n_uniform, uniform_scores, consume, finish, carry_in=None, drain=True):
    def stage(score, fold):
        score((0,))
        for g in range(n_heads):
            if g + 1 < n_heads:
                score((g + 1,))
            fold((g,))

    if carry_in is None:
        lead[0][0](0)
        pend_blk, pend_slot = lead[0][1], 0
    else:
        prev_blk, prev_slot, prev_consume, prev_finish = carry_in
        stage(lambda hs: lead[0][0](1 - prev_slot, hs), lambda hs: prev_consume(prev_blk, prev_slot, hs))
        prev_finish()
        pend_blk, pend_slot = lead[0][1], 1 - prev_slot
    for fn, blk in lead[1:]:
        stage(functools.partial(fn, 1 - pend_slot), functools.partial(consume, pend_blk, pend_slot))
        pend_blk, pend_slot = blk, 1 - pend_slot
    a, b = pend_slot, 1 - pend_slot

    def pair(first, pend):
        stage(functools.partial(uniform_scores, first, b), functools.partial(consume, pend, a))
        stage(functools.partial(uniform_scores, first + 1, a), functools.partial(consume, first, b))
        return first + 1

    def two_pairs(jj, pend):
        return pair(4 * jj + 2, pair(4 * jj, pend))

    n_quads = n_uniform // 4
    pend = lax.fori_loop(0, n_quads, two_pairs, pend_blk)
    rest = n_uniform - 4 * n_quads
    has_pair = rest >= 2
    first = 4 * n_quads

    @pl.when(has_pair)
    def _last_pair():
        pair(first, pend)

    pend = jnp.where(has_pair, first + 1, pend)
    if not drain:
        return pend, a, consume, finish
    last = jnp.where(has_pair, first + 2, first)

    @pl.when(rest % 2 == 1)
    def _odd_tail():
        stage(functools.partial(uniform_scores, last, b), functools.partial(consume, pend, a))
        consume(last, b)
        finish()

    @pl.when(rest % 2 == 0)
    def _even_tail():
        consume(pend, a)
        finish()


def _attn_finish(z_ref, o_ref, acc_s, g_heads, dv, state0):
    for g in range(g_heads):
        out_t = acc_s[state0 + g, :dv, :] / acc_s[state0 + g, dv:dv + 1, :]
        o_ref[:, g * dv:(g + 1) * dv] = (out_t.T * _silu(z_ref[:, g * dv:(g + 1) * dv].astype(F32))
                                         ).astype(o_ref.dtype)


def _t5_thresholds():
    max_exact = T5_BUCKETS // 2
    d = np.arange(0, 4 * T5_MAX_DIST, dtype=np.int64)
    dd = np.maximum(d, 1).astype(np.float32)
    large = max_exact + (np.log(dd / np.float32(max_exact)) / np.float32(math.log(T5_MAX_DIST / max_exact))
                         * np.float32(T5_BUCKETS - max_exact)).astype(np.int32)
    large = np.minimum(large, T5_BUCKETS - 1)
    bucket = np.where(d < max_exact, d, large)
    assert np.all(np.diff(bucket) >= 0) and bucket[-1] == T5_BUCKETS - 1
    return [int(np.argmax(bucket >= k)) for k in range(T5_BUCKETS)]


_T5_THRESH = _t5_thresholds()

MASK_BIG = 1e30


def _moba_body(t5_ref, qt_even_ref, qt_odd_ref, k_ref, vt_ref, z_even_ref, z_odd_ref, o_even_ref, o_odd_ref,
               kmask_s, km_s, bias_own_s, bias_prev_s, qa_s, m_s, acc_s, *s_refs, n_blocks, g_heads):
    hg = pl.program_id(0)
    b = pl.program_id(1)
    i = pl.program_id(2)
    t, hd = A_BLOCK, A_HEAD_DIM
    hv = hd + V_AUG_ROWS
    seq = n_blocks * t
    nb_pad = -(-n_blocks // SUBLANES) * SUBLANES

    @pl.when((hg == 0) & (b == 0) & (i == 0))
    def _mask_columns():
        key_blk = lax.broadcasted_iota(jnp.int32, (seq, LANES), 0) // t
        lane = lax.broadcasted_iota(jnp.int32, (seq, LANES), 1)
        kmask_s[...] = jnp.where(lane == key_blk, -MASK_BIG, 0.0).astype(kmask_s.dtype)

    @pl.when((b == 0) & (i == 0))
    def _build_bias():
        key = lax.broadcasted_iota(jnp.int32, (t, t), 0)
        qry = lax.broadcasted_iota(jnp.int32, (t, t), 1)
        for g in range(g_heads):
            head = hg * g_heads + g
            far = t5_ref[T5_BUCKETS - 1, head]
            for dist, dst in ((qry - key, bias_own_s), (t + qry - key, bias_prev_s)):
                bias = jnp.full((t, t), t5_ref[0, head], F32)
                for kk in range(1, T5_BUCKETS):
                    bias = jnp.where(dist >= _T5_THRESH[kk], t5_ref[kk, head], bias)
                dst[g] = (bias - far) * LOG2E

    @pl.when(i == 0)
    def _new_sequence():
        for g in range(g_heads):
            km_s[g] = jnp.zeros(km_s.shape[1:], km_s.dtype)
            for nb in range(n_blocks):
                mean = jnp.mean(k_ref[nb * t:(nb + 1) * t, g * hd:(g + 1) * hd].astype(F32), axis=0, keepdims=True)
                hi = mean.astype(BF16)
                km_s[g, nb:nb + 1, :] = hi
                km_s[g, LANES + nb:LANES + nb + 1, :] = (mean - hi.astype(F32)).astype(BF16)

    _softmax_init(m_s, acc_s)
    blk_row = lax.broadcasted_iota(jnp.int32, (nb_pad, t), 0)

    def query_tile(tile, qt_ref, z_ref, o_ref, which):
        state0 = which * g_heads

        past = blk_row < tile
        for g in range(g_heads):
            qt = qt_ref[g * hd:(g + 1) * hd, :]
            g2 = jnp.dot(km_s[g], qt, preferred_element_type=F32)
            gate = g2[:nb_pad, :] + g2[LANES:LANES + nb_pad, :]
            cur = jnp.where(past, gate, NEG)
            keep = blk_row == tile
            for _ in range(A_TOPK):
                mx = jnp.max(cur, axis=0, keepdims=True)
                idx = jnp.min(jnp.where(cur == mx, blk_row, nb_pad), axis=0, keepdims=True)
                pick = blk_row == idx
                keep = keep | (pick & past)
                cur = jnp.where(pick, -jnp.inf, cur)
            unsel = jnp.where(keep, 0.0, 1.0)
            qa_s[state0 + g, :hd, :] = qt
            qa_s[state0 + g, hd:, :] = jnp.concatenate([unsel, jnp.zeros((LANES - nb_pad, t), F32)],
                                                       axis=0).astype(qa_s.dtype)

        def raw_scores(g, j):
            start = pl.multiple_of(j * t, t)
            keys = jnp.concatenate([k_ref[pl.ds(start, t), g * hd:(g + 1) * hd], kmask_s[pl.ds(start, t), :]],
                                   axis=1)
            return jnp.dot(keys, qa_s[state0 + g], preferred_element_type=F32)

        def own_scores(slot, heads=range(g_heads)):
            key = lax.broadcasted_iota(jnp.int32, (t, t), 0)
            qry = lax.broadcasted_iota(jnp.int32, (t, t), 1)
            for g in heads:
                s_refs[g][slot] = jnp.where(key <= qry, raw_scores(g, tile) + bias_own_s[g], NEG)

        j_prev = jnp.maximum(tile - 1, 0)
        no_prev = jnp.where(tile == 0, -MASK_BIG, 0.0)

        def prev_scores(slot, heads=range(g_heads)):
            for g in heads:
                s_refs[g][slot] = raw_scores(g, j_prev) + (bias_prev_s[g] + no_prev)

        def far_scores(j, slot, heads=range(g_heads)):
            for g in heads:
                s_refs[g][slot] = raw_scores(g, j)

        def consume(j, slot, heads=range(g_heads)):
            for g in heads:
                _softmax_step_t(s_refs[g][slot], vt_ref[j, g * hv:(g + 1) * hv, :], state0 + g, m_s, acc_s)

        finish = functools.partial(_attn_finish, z_ref, o_ref, acc_s, g_heads, hd, state0)
        return g_heads, [(own_scores, tile), (prev_scores, j_prev)], j_prev, far_scores, consume, finish

    carry = _skewed_blocks(*query_tile(n_blocks - 1 - 2 * i, qt_odd_ref, z_odd_ref, o_odd_ref, 1), drain=False)
    _skewed_blocks(*query_tile(2 * i, qt_even_ref, z_even_ref, o_even_ref, 0), carry_in=carry)


def moba_branch(qt, vt, slab, t5_table, g_heads=ATTN_HEADS_PER_STEP):
    bsz, nb, _, t = qt.shape
    s = nb * t
    hd = A_HEAD_DIM
    gw = g_heads * hd
    assert t == A_BLOCK and nb <= LANES and nb % TILES_PER_STEP == 0 and A_HEADS % g_heads == 0
    assert OFF_KA % gw == 0 and OFF_ZA % gw == 0
    assert _T5_THRESH[T5_BUCKETS - 1] <= t + 1
    ck, cz = OFF_KA // gw, OFF_ZA // gw
    qt_even, qt_odd = _tile_pair_specs(nb, (None, None, gw, t), lambda h, b, tile: (b, tile, h, 0))
    z_even, z_odd = _tile_pair_specs(nb, (None, t, gw), lambda h, b, tile: (b, tile, cz + h))
    out = jax.ShapeDtypeStruct((bsz, nb // 2, t, A_W), BF16)
    return pl.pallas_call(
        functools.partial(_moba_body, n_blocks=nb, g_heads=g_heads),
        grid=(A_HEADS // g_heads, bsz, nb // 2),
        in_specs=[pl.BlockSpec(memory_space=pltpu.SMEM),
                  qt_even, qt_odd,
                  pl.BlockSpec((None, s, gw), lambda h, b, i: (b, 0, ck + h)),
                  pl.BlockSpec((None, nb, g_heads * (hd + V_AUG_ROWS), t), lambda h, b, i: (b, 0, h, 0)),
                  z_even, z_odd],
        out_specs=[pl.BlockSpec((None, None, t, gw), lambda h, b, i: (b, i, 0, h)),
                   pl.BlockSpec((None, None, t, gw), lambda h, b, i: (b, nb // 2 - 1 - i, 0, h))],
        out_shape=[out, out],
        scratch_shapes=[pltpu.VMEM((s, LANES), BF16),
                        pltpu.VMEM((g_heads, 2 * LANES, hd), BF16),
                        pltpu.VMEM((g_heads, t, t), F32),
                        pltpu.VMEM((g_heads, t, t), F32),
                        pltpu.VMEM((TILES_PER_STEP * g_heads, ATTN_QK_PAD, t), BF16)]
        + _attn_scratch(g_heads, t, hd),
        compiler_params=_cparams(3),
        name="moba",
    )(t5_table, qt, qt, slab, vt, slab, slab)


MLSTM_CHUNK = 256
MLSTM_SEQS_PER_STEP = 1
CONV_HALO = 8


def _log_sigmoid(t):
    return jnp.minimum(t, 0.0) - jnp.log1p(jnp.exp(-jnp.abs(t)))


def _mlstm_body(qk_ref, v_ref, ob_ref, zb_ref, if_ref, cw_ref, cb_ref, gb_ref, gn_ref, o_ref,
                xe_s, c_s, n_s, m_s):
    c = pl.program_id(1)

    @pl.when(c == 0)
    def _reset():
        xe_s[...] = jnp.zeros_like(xe_s)
        c_s[...] = jnp.zeros_like(c_s)
        n_s[...] = jnp.zeros_like(n_s)
        m_s[...] = jnp.zeros_like(m_s)

    for bi in range(qk_ref.shape[0]):
        _mlstm_chunk(bi, qk_ref, v_ref, ob_ref, zb_ref, if_ref, cw_ref, cb_ref, gb_ref, gn_ref, o_ref,
                     xe_s, c_s, n_s, m_s)


def _mlstm_chunk(bi, qk_ref, v_ref, ob_ref, zb_ref, if_ref, cw_ref, cb_ref, gb_ref, gn_ref, o_ref,
                 xe_s, c_s, n_s, m_s):
    L = MLSTM_CHUNK
    dk, dv = B_QK_DIM, B_V_DIM

    x = qk_ref[bi].astype(F32)
    xe = jnp.concatenate([xe_s[bi], x], axis=0)
    conv = cb_ref[...] + cw_ref[B_CONV - 1:B_CONV, :] * x
    for j in range(B_CONV - 1):
        back = B_CONV - 1 - j
        conv = conv + cw_ref[j:j + 1, :] * pltpu.roll(xe, back, axis=0)[CONV_HALO:, :]
    xe_s[bi] = x[L - CONV_HALO:, :]
    qk = _silu(conv)

    gi = if_ref[bi].astype(F32) + gb_ref[...]
    lf = _log_sigmoid(gi)
    row = lax.broadcasted_iota(jnp.int32, (L, L), 0)
    col = lax.broadcasted_iota(jnp.int32, (L, L), 1)
    causal = col <= row
    tri = jnp.where(causal, 1.0, 0.0).astype(F32)
    b_cols = jnp.dot(tri, lf, precision=lax.Precision.HIGHEST, preferred_element_type=F32)
    b_rows = b_cols.T
    li_rows = gi.T

    for h in range(B_HEADS):
        q_h = qk[:, h * dk:(h + 1) * dk]
        k_h = qk[:, B_QK_W + h * dk:B_QK_W + (h + 1) * dk] * (dk ** -0.5)
        v_h = v_ref[bi, :, h * dv:(h + 1) * dv]
        b_c = b_cols[:, B_HEADS + h:B_HEADS + h + 1]
        b_r = b_rows[B_HEADS + h:B_HEADS + h + 1, :]
        li_c = gi[:, h:h + 1]
        li_r = li_rows[h:h + 1, :]
        m_prev = m_s[bi, h:h + 1, 0:1]
        c_prev = c_s[bi * B_HEADS + h]
        n_prev = n_s[bi, h:h + 1, :]

        a_c = b_c + m_prev
        dmat = jnp.where(causal, b_c - b_r + li_r, -jnp.inf)
        m_t = jnp.maximum(a_c, jnp.max(dmat, axis=-1, keepdims=True))
        w_inter = jnp.exp(a_c - m_t)
        q_b = q_h.astype(BF16)
        k_b = k_h.astype(BF16)
        sc = (lax.dot_general(q_b, k_b, _NT, preferred_element_type=F32)
              * jnp.exp(dmat - m_t))
        num = (w_inter * jnp.dot(q_b, c_prev.astype(BF16), preferred_element_type=F32)
               + jnp.dot(sc.astype(BF16), v_h, preferred_element_type=F32))
        den = (w_inter * jnp.sum(q_h * n_prev, axis=-1, keepdims=True)
               + jnp.sum(sc, axis=-1, keepdims=True))
        hh = num / jnp.maximum(jnp.abs(den), jnp.exp(-m_t))

        b_last = b_c[L - 1:L, :]
        g_c = b_last - b_c + li_c
        m_new = jnp.maximum(b_last + m_prev, jnp.max(g_c, axis=0, keepdims=True))
        decay = jnp.exp(b_last + m_prev - m_new)
        wk = jnp.exp(g_c - m_new) * k_h
        c_s[bi * B_HEADS + h] = decay * c_prev + lax.dot_general(
            wk.astype(BF16), v_h, (((0,), (0,)), ((), ())), preferred_element_type=F32)
        n_s[bi, h:h + 1, :] = decay * n_prev + jnp.sum(wk, axis=0, keepdims=True)
        m_s[bi, h:h + 1, :] = jnp.broadcast_to(m_new, (1, LANES))

        mu = jnp.mean(hh, axis=-1, keepdims=True)
        dlt = hh - mu
        var = jnp.mean(dlt * dlt, axis=-1, keepdims=True)
        y = dlt * lax.rsqrt(var + EPS) * gn_ref[:, h * dv:(h + 1) * dv]
        y = (y * jax.nn.sigmoid(ob_ref[bi, :, h * dv:(h + 1) * dv].astype(F32))
             * _silu(zb_ref[bi, :, h * dv:(h + 1) * dv].astype(F32)))
        o_ref[bi, :, h * dv:(h + 1) * dv] = y.astype(o_ref.dtype)


def mlstm_branch(slab, slab_small, conv_w, conv_b, i_bias, f_bias, out_norm, nb=MLSTM_SEQS_PER_STEP):
    bsz, s, _ = slab.shape
    L = MLSTM_CHUNK
    nb = math.gcd(nb, bsz)
    assert s % L == 0
    gate_bias = jnp.zeros((1, LANES), F32).at[0, :B_HEADS].set(i_bias).at[0, B_HEADS:2 * B_HEADS].set(f_bias)
    w2 = 2 * B_QK_W
    full = lambda shape: pl.BlockSpec(shape, lambda b, c: (0,) * len(shape))
    return pl.pallas_call(
        _mlstm_body,
        grid=(bsz // nb, s // L),
        in_specs=[pl.BlockSpec((nb, L, w2), lambda b, c: (b, c, OFF_QKB // w2)),
                  pl.BlockSpec((nb, L, B_V_W), lambda b, c: (b, c, OFF_VB // B_V_W)),
                  pl.BlockSpec((nb, L, B_V_W), lambda b, c: (b, c, OFF_OB // B_V_W)),
                  pl.BlockSpec((nb, L, B_V_W), lambda b, c: (b, c, OFF_ZB // B_V_W)),
                  pl.BlockSpec((nb, L, LANES), lambda b, c: (b, c, OFF_IF // LANES)),
                  full((B_CONV, w2)), full((1, w2)), full((1, LANES)), full((1, B_V_W))],
        out_specs=pl.BlockSpec((nb, L, B_V_W), lambda b, c: (b, c, 0)),
        out_shape=jax.ShapeDtypeStruct((bsz, s, B_V_W), BF16),
        scratch_shapes=[pltpu.VMEM((nb, CONV_HALO, w2), F32),
                        pltpu.VMEM((nb * B_HEADS, B_QK_DIM, B_V_DIM), F32),
                        pltpu.VMEM((nb, SUBLANES, B_QK_DIM), F32),
                        pltpu.VMEM((nb, SUBLANES, LANES), F32)],
        compiler_params=_cparams(2),
        name="mlstm",
    )(slab, slab, slab, slab, slab_small, conv_w, conv_b.reshape(1, w2), gate_bias,
      out_norm.reshape(1, B_V_W))


def _mla_prep_body(cq_ref, ckv_ref, kr_ref, qg_ref, kg_ref, wqt_ref, wqst_ref, wkn_ref, wvt_ref,
                   cos_ref, sin_ref, cost_ref, sint_ref, qt_ref, kf_ref, vt_ref):
    def normed(ref, g_ref):
        t = ref[...].astype(F32)
        y = t * lax.rsqrt(jnp.mean(t * t, axis=-1, keepdims=True) + EPS)
        return (y * g_ref[...]).astype(BF16)

    cqn = normed(cq_ref, qg_ref)
    ckvn = normed(ckv_ref, kg_ref)

    qt_main = lax.dot_general(wqt_ref[...], cqn, _NT, preferred_element_type=F32)
    qt_swap = lax.dot_general(wqst_ref[...], cqn, _NT, preferred_element_type=F32)
    cos_t = cost_ref[...]
    sin_t = sint_ref[...]
    t = ATTN_TILE
    n_sub = cos_t.shape[1] // t
    for h in range(C_HEADS):
        lo = h * ATTN_QK_PAD
        nope = qt_main[lo:lo + LANES, :].astype(qt_ref.dtype)
        rope = (qt_main[lo + LANES:lo + 2 * LANES, :] * cos_t
                + qt_swap[h * LANES:(h + 1) * LANES, :] * sin_t).astype(qt_ref.dtype)
        for c in range(n_sub):
            qt_ref[c, lo:lo + LANES, :] = nope[:, c * t:(c + 1) * t]
            qt_ref[c, lo + LANES:lo + 2 * LANES, :] = rope[:, c * t:(c + 1) * t]

    k_nope = jnp.dot(ckvn, wkn_ref[...], preferred_element_type=F32)
    kr = kr_ref[...].astype(F32)
    half = C_ROPE // 2
    lane = lax.broadcasted_iota(jnp.int32, kr.shape, 1)
    swapped = jnp.where(lane < half, -pltpu.roll(kr, LANES - half, axis=1), pltpu.roll(kr, half, axis=1))
    k_rot = (kr * cos_ref[...] + swapped * sin_ref[...]).astype(kf_ref.dtype)
    for h in range(C_HEADS):
        lo = h * ATTN_QK_PAD
        kf_ref[:, lo:lo + LANES] = k_nope[:, h * LANES:(h + 1) * LANES].astype(kf_ref.dtype)
        kf_ref[:, lo + LANES:lo + 2 * LANES] = k_rot

    v_t = lax.dot_general(wvt_ref[...], ckvn, _NT, preferred_element_type=F32)
    hv = C_V_DIM + V_AUG_ROWS
    aug = _v_aug_rows(t, vt_ref.dtype)
    for h in range(C_HEADS):
        v_h = v_t[h * C_V_DIM:(h + 1) * C_V_DIM, :].astype(vt_ref.dtype)
        for c in range(n_sub):
            vt_ref[c, h * hv:h * hv + C_V_DIM, :] = v_h[:, c * t:(c + 1) * t]
            vt_ref[c, h * hv + C_V_DIM:(h + 1) * hv, :] = aug


def mla_prep(slab, q_norm, kv_norm, w_uq, w_ukv, tm=1024):
    bsz, s, _ = slab.shape
    t = ATTN_TILE
    n_sub = tm // t
    vt_rows = C_HEADS * (C_V_DIM + V_AUG_ROWS)
    half = C_ROPE // 2
    scale = (C_NOPE + C_ROPE) ** -0.5 * LOG2E
    wq = (w_uq * scale).reshape(C_Q_RANK, C_HEADS, C_NOPE + C_ROPE)
    pad = jnp.zeros((C_Q_RANK, C_HEADS, LANES - C_ROPE), F32)
    x1, x2 = wq[..., C_NOPE:C_NOPE + half], wq[..., C_NOPE + half:]
    wqt = jnp.concatenate([wq, pad], axis=-1).reshape(C_Q_RANK, C_HEADS * ATTN_QK_PAD).T.astype(BF16)
    wqst = jnp.concatenate([-x2, x1, pad], axis=-1).reshape(C_Q_RANK, C_HEADS * LANES).T.astype(BF16)
    wkv = w_ukv.reshape(C_KV_RANK, C_HEADS, C_NOPE + C_V_DIM)
    wkn = wkv[..., :C_NOPE].reshape(C_KV_RANK, C_HEADS * C_NOPE).astype(BF16)
    wvt = wkv[..., C_NOPE:].reshape(C_KV_RANK, C_W).T.astype(BF16)

    pos = jnp.arange(s, dtype=jnp.int32)
    inv = ROPE_THETA ** (-jnp.arange(half, dtype=F32) / half)
    ang = pos.astype(F32)[:, None] * inv[None, :]
    zpad = jnp.zeros((s, LANES - C_ROPE), F32)
    cos_tab = jnp.concatenate([jnp.cos(ang), jnp.cos(ang), zpad], axis=-1)
    sin_tab = jnp.concatenate([jnp.sin(ang), jnp.sin(ang), zpad], axis=-1)

    nt = s // t
    full = lambda shape: pl.BlockSpec(shape, lambda b, i: (0,) * len(shape))
    return pl.pallas_call(
        _mla_prep_body,
        grid=(bsz, s // tm),
        in_specs=[pl.BlockSpec((None, tm, C_Q_RANK), lambda b, i: (b, i, OFF_CQ // C_Q_RANK)),
                  pl.BlockSpec((None, tm, C_KV_RANK), lambda b, i: (b, i, OFF_CKV // C_KV_RANK)),
                  pl.BlockSpec((None, tm, LANES), lambda b, i: (b, i, OFF_KR // LANES)),
                  full((1, C_Q_RANK)), full((1, C_KV_RANK)),
                  full(wqt.shape), full(wqst.shape), full(wkn.shape), full(wvt.shape),
                  pl.BlockSpec((tm, LANES), lambda b, i: (i, 0)),
                  pl.BlockSpec((tm, LANES), lambda b, i: (i, 0)),
                  pl.BlockSpec((LANES, tm), lambda b, i: (0, i)),
                  pl.BlockSpec((LANES, tm), lambda b, i: (0, i))],
        out_specs=[pl.BlockSpec((None, n_sub, C_HEADS * ATTN_QK_PAD, t), lambda b, i: (b, i, 0, 0)),
                   pl.BlockSpec((None, tm, C_HEADS * ATTN_QK_PAD), lambda b, i: (b, i, 0)),
                   pl.BlockSpec((None, n_sub, vt_rows, t), lambda b, i: (b, i, 0, 0))],
        out_shape=[jax.ShapeDtypeStruct((bsz, nt, C_HEADS * ATTN_QK_PAD, t), BF16),
                   jax.ShapeDtypeStruct((bsz, s, C_HEADS * ATTN_QK_PAD), BF16),
                   jax.ShapeDtypeStruct((bsz, nt, vt_rows, t), BF16)],
        compiler_params=_cparams(2),
        name="mla_prep",
    )(slab, slab, slab, q_norm.reshape(1, C_Q_RANK), kv_norm.reshape(1, C_KV_RANK),
      wqt, wqst, wkn, wvt, cos_tab, sin_tab, cos_tab.T, sin_tab.T)


def _mla_attn_body(qt_even_ref, qt_odd_ref, k_ref, vt_ref, z_even_ref, z_odd_ref, o_even_ref, o_odd_ref,
                   m_s, acc_s, *s_refs, g_heads, n_tiles):
    i = pl.program_id(2)
    t = ATTN_TILE
    dq, dv = ATTN_QK_PAD, C_V_DIM
    hv = dv + V_AUG_ROWS
    _softmax_init(m_s, acc_s)

    def query_tile(tile, qt_ref, z_ref, o_ref, which):
        state0 = which * g_heads

        def raw_scores(g, j):
            start = pl.multiple_of(j * t, t)
            return jnp.dot(k_ref[pl.ds(start, t), g * dq:(g + 1) * dq], qt_ref[g * dq:(g + 1) * dq, :],
                           preferred_element_type=F32)

        def diagonal_scores(slot, heads=range(g_heads)):
            key = lax.broadcasted_iota(jnp.int32, (t, t), 0)
            qry = lax.broadcasted_iota(jnp.int32, (t, t), 1)
            for g in heads:
                s_refs[g][slot] = jnp.where(key <= qry, raw_scores(g, tile), NEG)

        def past_scores(j, slot, heads=range(g_heads)):
            for g in heads:
                s_refs[g][slot] = raw_scores(g, j)

        def consume(j, slot, heads=range(g_heads)):
            for g in heads:
                _softmax_step_t(s_refs[g][slot], vt_ref[j, g * hv:(g + 1) * hv, :], state0 + g, m_s, acc_s)

        finish = functools.partial(_attn_finish, z_ref, o_ref, acc_s, g_heads, dv, state0)
        return g_heads, [(diagonal_scores, tile)], tile, past_scores, consume, finish

    carry = _skewed_blocks(*query_tile(2 * i, qt_even_ref, z_even_ref, o_even_ref, 0), drain=False)
    _skewed_blocks(*query_tile(n_tiles - 1 - 2 * i, qt_odd_ref, z_odd_ref, o_odd_ref, 1), carry_in=carry)


def mla_attention(qt, kf, vt, slab, g_heads=ATTN_HEADS_PER_STEP):
    bsz, nt, _, t = qt.shape
    s = nt * t
    dq, dv = ATTN_QK_PAD, C_V_DIM
    assert t == ATTN_TILE and nt % TILES_PER_STEP == 0
    assert C_HEADS % g_heads == 0 and OFF_ZC % (g_heads * dv) == 0
    cz = OFF_ZC // (g_heads * dv)
    qt_even, qt_odd = _tile_pair_specs(nt, (None, None, g_heads * dq, t), lambda b, h, tile: (b, tile, h, 0))
    z_even, z_odd = _tile_pair_specs(nt, (None, t, g_heads * dv), lambda b, h, tile: (b, tile, cz + h))
    out = jax.ShapeDtypeStruct((bsz, nt // 2, t, C_W), BF16)
    return pl.pallas_call(
        functools.partial(_mla_attn_body, g_heads=g_heads, n_tiles=nt),
        grid=(bsz, C_HEADS // g_heads, nt // 2),
        in_specs=[qt_even, qt_odd,
                  pl.BlockSpec((None, s, g_heads * dq), lambda b, h, i: (b, 0, h),
                               pipeline_mode=pl.Buffered(1)),
                  pl.BlockSpec((None, nt, g_heads * (dv + V_AUG_ROWS), t), lambda b, h, i: (b, 0, h, 0)),
                  z_even, z_odd],
        out_specs=[pl.BlockSpec((None, None, t, g_heads * dv), lambda b, h, i: (b, i, 0, h)),
                   pl.BlockSpec((None, None, t, g_heads * dv), lambda b, h, i: (b, nt // 2 - 1 - i, 0, h))],
        out_shape=[out, out],
        scratch_shapes=_attn_scratch(g_heads, t, dv),
        compiler_params=_cparams(3),
        name="mla_attn",
    )(qt, qt, kf, vt, slab, slab)


def _merge_body(ya_even_ref, ya_odd_ref, yb_ref, yc_even_ref, yc_odd_ref, wa_ref, wb_ref, wc_ref,
                ga_ref, gb_ref, gc_ref, o_ref, w_s):
    @pl.when(pl.program_id(1) == 0)
    def _cast_weights():
        for n, w_ref in enumerate((wa_ref, wb_ref, wc_ref)):
            w_s[n] = w_ref[...].astype(w_s.dtype)

    def rows(even_ref, odd_ref):
        return jnp.concatenate([ref[p] for p in range(even_ref.shape[0]) for ref in (even_ref, odd_ref)], axis=0)

    ya = rows(ya_even_ref, ya_odd_ref)
    yc = rows(yc_even_ref, yc_odd_ref)
    acc = None
    for n, (y, g_ref) in enumerate(((ya, ga_ref), (yb_ref[...], gb_ref), (yc, gc_ref))):
        term = jax.nn.sigmoid(g_ref[...].astype(F32)) * jnp.dot(y, w_s[n], preferred_element_type=F32)
        acc = term if acc is None else acc + term
    o_ref[...] = acc.astype(o_ref.dtype)


def branch_merge(ya_tiles, yb, yc_tiles, w_branch, layer, slab2d, tm=1024, tn=1024):
    m, w = yb.shape
    d = w_branch.shape[-1]
    t = ATTN_TILE
    pairs = tm // (TILES_PER_STEP * t)
    assert OFF_GT % tn == 0 and pairs * TILES_PER_STEP * t == tm
    g0 = OFF_GT // tn
    gper = d // tn
    halves = [y.reshape(m // (TILES_PER_STEP * t), t, w) for y in (*ya_tiles, *yc_tiles)]
    half_spec = pl.BlockSpec((pairs, t, w), lambda j, i: (i, 0, 0))
    w_specs = [pl.BlockSpec((None, None, w, tn), functools.partial(lambda j, i, n: (layer, n, 0, j), n=n),
                            pipeline_mode=pl.Buffered(1))
               for n in range(N_BRANCH)]
    g_specs = [pl.BlockSpec((tm, tn), functools.partial(lambda j, i, n: (i, g0 + n * gper + j), n=n))
               for n in range(N_BRANCH)]
    return pl.pallas_call(
        _merge_body,
        grid=(d // tn, m // tm),
        in_specs=[half_spec, half_spec, pl.BlockSpec((tm, w), lambda j, i: (i, 0)), half_spec, half_spec]
        + w_specs + g_specs,
        out_specs=pl.BlockSpec((tm, tn), lambda j, i: (i, j)),
        out_shape=jax.ShapeDtypeStruct((m, d), BF16),
        scratch_shapes=[pltpu.VMEM((N_BRANCH, w, tn), BF16)],
        compiler_params=_cparams(2),
        name="branch_merge",
    )(halves[0], halves[1], yb, halves[2], halves[3], w_branch, w_branch, w_branch, slab2d, slab2d, slab2d)


def _out_body(mg_ref, w_ref, x_ref, g_ref, *refs, last):
    out_refs, w_s = refs[:-1], refs[-1]

    @pl.when(pl.program_id(0) == 0)
    def _cast_weights():
        w_s[...] = w_ref[...].astype(w_s.dtype)

    x_new = x_ref[...] + jnp.dot(mg_ref[...], w_s[...], preferred_element_type=F32)
    y = x_new * lax.rsqrt(jnp.mean(x_new * x_new, axis=-1, keepdims=True) + EPS) * g_ref[...]
    if last:
        out_refs[0][...] = y
    else:
        out_refs[0][...] = x_new
        out_refs[1][...] = y.astype(out_refs[1].dtype)


def out_projection(merged, w_out, layer, x2d, gain, last, tm=512):
    m, d = x2d.shape
    row = pl.BlockSpec((tm, d), lambda i: (i, 0))
    if last:
        out_specs, out_shape = row, jax.ShapeDtypeStruct((m, d), F32)
    else:
        out_specs = [row, row]
        out_shape = [jax.ShapeDtypeStruct((m, d), F32), jax.ShapeDtypeStruct((m, d), BF16)]
    return pl.pallas_call(
        functools.partial(_out_body, last=last),
        grid=(m // tm,),
        in_specs=[row,
                  pl.BlockSpec((None, d, d), lambda i: (layer, 0, 0), pipeline_mode=pl.Buffered(1)),
                  row, pl.BlockSpec((1, d), lambda i: (0, 0))],
        out_specs=out_specs,
        out_shape=out_shape,
        scratch_shapes=[pltpu.VMEM((d, d), BF16)],
        compiler_params=_cparams(1),
        name="out_proj",
    )(merged, w_out, x2d, gain.reshape(1, d))


def kernel(x, norm_gain, w_in, t5_table, mlstm_conv_w, mlstm_conv_b, mlstm_i_bias, mlstm_f_bias,
           mlstm_out_norm, mla_q_norm, mla_kv_norm, mla_w_uq, mla_w_ukv, w_branch, w_out, final_norm):
    bsz, s, d = x.shape
    m = bsz * s
    x2d = x.reshape(m, d)
    h = rmsnorm_rows(x2d, norm_gain[0])
    w_in_t = jnp.swapaxes(w_in, 1, 2)
    out = None
    for l in range(DEPTH):
        slab2d = input_projection(h, w_in_t, l, SLAB_SRC_ROWS)
        slab = slab2d.reshape(bsz, s, D_SLAB)
        slab_small = input_projection_small(h, w_in_t, l).reshape(bsz, s, D_SLAB_SMALL)
        qt_a, vt_a = moba_qv_projection(h, w_in_t, l, bsz)
        ya = moba_branch(qt_a, vt_a, slab, t5_table)
        yb = mlstm_branch(slab, slab_small, mlstm_conv_w[l], mlstm_conv_b[l], mlstm_i_bias[l], mlstm_f_bias[l],
                          mlstm_out_norm[l])
        qt_c, k_c, vt_c = mla_prep(slab_small, mla_q_norm[l], mla_kv_norm[l], mla_w_uq[l], mla_w_ukv[l])
        yc = mla_attention(qt_c, k_c, vt_c, slab)
        merged = branch_merge(ya, yb.reshape(m, B_V_W), yc, w_branch, l, slab2d)
        last = l == DEPTH - 1
        gain = final_norm if last else norm_gain[l + 1]
        res = out_projection(merged, w_out, l, x2d, gain, last)
        if last:
            out = res
        else:
            x2d, h = res
    return out.reshape(bsz, s, d)
```

```python
import functools
import math

import jax
import jax.numpy as jnp
import numpy as np
from jax import lax
from jax.experimental import pallas as pl
from jax.experimental.pallas import tpu as pltpu

F32 = jnp.float32
BF16 = jnp.bfloat16

D_MODEL = 2048
DEPTH = 2
EPS = 1e-6
NEG = -1e30
LOG2E = math.log2(math.e)

A_HEADS = 8
A_HEAD_DIM = 128
A_BLOCK = 256
A_TOPK = 3
T5_BUCKETS = 32
T5_MAX_DIST = 128
B_HEADS = 4
B_QK_DIM = 128
B_V_DIM = 256
B_CONV = 4
C_HEADS = 8
C_Q_RANK = 512
C_KV_RANK = 256
C_NOPE = 128
C_ROPE = 64
C_V_DIM = 128
ROPE_THETA = 10000.0
N_BRANCH = 3

A_W = A_HEADS * A_HEAD_DIM
B_QK_W = B_HEADS * B_QK_DIM
B_V_W = B_HEADS * B_V_DIM
C_W = C_HEADS * C_V_DIM

LANES = 128
SUBLANES = 8
VMEM_LIMIT_BYTES = 58 * 1024 * 1024

SPLIT_SIZES = (A_W, A_W, A_W, A_W, B_QK_W, B_QK_W, B_V_W, B_HEADS, B_HEADS, B_V_W, B_V_W,
               C_Q_RANK, C_KV_RANK, C_ROPE, C_W, N_BRANCH * D_MODEL)
(SRC_QA, SRC_KA, SRC_VA, SRC_ZA, SRC_QB, SRC_KB, SRC_VB, SRC_IB, SRC_FB, SRC_OB, SRC_ZB,
 SRC_CQ, SRC_CKV, SRC_KR, SRC_ZC, SRC_GT) = (int(v) for v in np.cumsum((0,) + SPLIT_SIZES[:-1]))
D_IN = int(sum(SPLIT_SIZES))

W_BLOCK = 1024
SLAB_SRC_ROWS = ((SRC_KA, SRC_ZA, SRC_QB, SRC_VB, SRC_OB, SRC_ZB, SRC_ZC)
                 + tuple(SRC_GT + k * W_BLOCK for k in range(N_BRANCH * D_MODEL // W_BLOCK)))
assert SRC_KB == SRC_QB + B_QK_W and all(r % SUBLANES == 0 for r in SLAB_SRC_ROWS)
OFF_KA = 0
OFF_ZA = OFF_KA + W_BLOCK
OFF_QKB = OFF_ZA + W_BLOCK
OFF_VB = OFF_QKB + W_BLOCK
OFF_OB = OFF_VB + W_BLOCK
OFF_ZB = OFF_OB + W_BLOCK
OFF_ZC = OFF_ZB + W_BLOCK
OFF_GT = OFF_ZC + W_BLOCK
D_SLAB = len(SLAB_SRC_ROWS) * W_BLOCK
SMALL_WIN1_ROWS = C_Q_RANK + C_KV_RANK + LANES
SMALL_WIN2_ROWS = LANES
assert SRC_CKV == SRC_CQ + C_Q_RANK and SRC_KR == SRC_CKV + C_KV_RANK and SRC_FB == SRC_IB + B_HEADS
assert SRC_CQ % SUBLANES == 0 and SRC_IB % SUBLANES == 0
assert SRC_CQ + SMALL_WIN1_ROWS <= D_IN and SRC_IB + SMALL_WIN2_ROWS <= D_IN
OFF_CQ = 0
OFF_CKV = OFF_CQ + C_Q_RANK
OFF_KR = OFF_CKV + C_KV_RANK
OFF_IF = OFF_KR + LANES
D_SLAB_SMALL = SMALL_WIN1_ROWS + SMALL_WIN2_ROWS
assert D_SLAB_SMALL == W_BLOCK

ATTN_TILE = 256
ATTN_QK_PAD = 2 * LANES
ATTN_HEADS_PER_STEP = 8
assert ATTN_TILE == A_BLOCK


def _cparams(n_axes):
    return pltpu.CompilerParams(dimension_semantics=("arbitrary",) * n_axes,
                                vmem_limit_bytes=VMEM_LIMIT_BYTES)


def _silu(t):
    return t * jax.nn.sigmoid(t)


_NT = (((1,), (1,)), ((), ()))


def _rmsnorm_body(x_ref, g_ref, o_ref):
    xf = x_ref[...]
    y = xf * lax.rsqrt(jnp.mean(xf * xf, axis=-1, keepdims=True) + EPS)
    o_ref[...] = (y * g_ref[...]).astype(o_ref.dtype)


def rmsnorm_rows(x2d, gain, tm=1024):
    m, d = x2d.shape
    return pl.pallas_call(
        _rmsnorm_body,
        grid=(m // tm,),
        in_specs=[pl.BlockSpec((tm, d), lambda i: (i, 0)),
                  pl.BlockSpec((1, d), lambda i: (0, 0))],
        out_specs=pl.BlockSpec((tm, d), lambda i: (i, 0)),
        out_shape=jax.ShapeDtypeStruct((m, d), BF16),
        compiler_params=_cparams(1),
        name="rmsnorm",
    )(x2d, gain.reshape(1, d))


def _row_window(rows, d):
    return (pl.Squeezed(), pl.Element(rows), pl.Element(d))


def _proj_body(rows_ref, h_ref, w_ref, o_ref, wb_s):
    del rows_ref
    @pl.when(pl.program_id(1) == 0)
    def _cast_weights():
        for c in range(0, w_ref.shape[0], LANES):
            wb_s[:, c:c + LANES] = w_ref[c:c + LANES, :].T.astype(wb_s.dtype)

    o_ref[...] = jnp.dot(h_ref[...], wb_s[...], preferred_element_type=F32).astype(o_ref.dtype)


def input_projection(h2d, w_in_t, layer, src_rows, tm=2048):
    m, d = h2d.shape
    tn = W_BLOCK
    grid_spec = pltpu.PrefetchScalarGridSpec(
        num_scalar_prefetch=1,
        grid=(len(src_rows), m // tm),
        in_specs=[pl.BlockSpec((tm, d), lambda j, i, rows: (i, 0)),
                  pl.BlockSpec(_row_window(tn, d),
                               lambda j, i, rows: (layer, pl.multiple_of(rows[j], SUBLANES), 0))],
        out_specs=pl.BlockSpec((tm, tn), lambda j, i, rows: (i, j)),
        scratch_shapes=[pltpu.VMEM((d, tn), BF16)])
    return pl.pallas_call(
        _proj_body,
        grid_spec=grid_spec,
        out_shape=jax.ShapeDtypeStruct((m, len(src_rows) * tn), BF16),
        compiler_params=_cparams(2),
        name="input_proj",
    )(jnp.asarray(src_rows, jnp.int32), h2d, w_in_t)


def _proj_small_body(h_ref, w1_ref, w2_ref, o_ref, wb_s):
    @pl.when(pl.program_id(0) == 0)
    def _cast_weights():
        wb_s[:SMALL_WIN1_ROWS, :] = w1_ref[...].astype(wb_s.dtype)
        wb_s[SMALL_WIN1_ROWS:, :] = w2_ref[...].astype(wb_s.dtype)

    o_ref[...] = lax.dot_general(h_ref[...], wb_s[...], _NT, preferred_element_type=F32).astype(o_ref.dtype)


def input_projection_small(h2d, w_in_t, layer, tm=2048):
    m, d = h2d.shape
    return pl.pallas_call(
        _proj_small_body,
        grid=(m // tm,),
        in_specs=[pl.BlockSpec((tm, d), lambda i: (i, 0)),
                  pl.BlockSpec(_row_window(SMALL_WIN1_ROWS, d), lambda i: (layer, SRC_CQ, 0)),
                  pl.BlockSpec(_row_window(SMALL_WIN2_ROWS, d), lambda i: (layer, SRC_IB, 0))],
        out_specs=pl.BlockSpec((tm, D_SLAB_SMALL), lambda i: (i, 0)),
        out_shape=jax.ShapeDtypeStruct((m, D_SLAB_SMALL), BF16),
        scratch_shapes=[pltpu.VMEM((D_SLAB_SMALL, d), BF16)],
        compiler_params=_cparams(1),
        name="input_proj_small",
    )(h2d, w_in_t, w_in_t)


def _proj_t_body(h_ref, wq_ref, wv_ref, qt_ref, vt_ref, wt_s, *, n_sub):
    t = ATTN_TILE
    hd = A_HEAD_DIM
    hv = hd + V_AUG_ROWS

    @pl.when((pl.program_id(0) == 0) & (pl.program_id(1) == 0))
    def _cast_weights():
        q_scale = A_HEAD_DIM ** -0.5 * LOG2E
        wt_s[:A_W, :] = (wq_ref[...] * q_scale).astype(wt_s.dtype)
        wt_s[A_W:, :] = wv_ref[...].astype(wt_s.dtype)

    res = lax.dot_general(wt_s[...], h_ref[...], _NT, preferred_element_type=F32)
    aug = _v_aug_rows(t, vt_ref.dtype)
    for c in range(n_sub):
        qt_ref[c] = res[:A_W, c * t:(c + 1) * t].astype(qt_ref.dtype)
        for h in range(A_HEADS):
            vt_ref[c, h * hv:h * hv + hd, :] = res[A_W + h * hd:A_W + (h + 1) * hd, c * t:(c + 1) * t
                                                   ].astype(vt_ref.dtype)
            vt_ref[c, h * hv + hd:(h + 1) * hv, :] = aug


def moba_qv_projection(h2d, w_in_t, layer, bsz, tm=1024):
    m, d = h2d.shape
    assert SRC_QA % A_W == 0 and SRC_VA % A_W == 0
    t = ATTN_TILE
    tm = min(tm, m // bsz)
    n_sub = tm // t
    nt = m // bsz // t
    steps = m // bsz // tm
    vt_rows = A_HEADS * (A_HEAD_DIM + V_AUG_ROWS)
    return pl.pallas_call(
        functools.partial(_proj_t_body, n_sub=n_sub),
        grid=(bsz, steps),
        in_specs=[pl.BlockSpec((tm, d), lambda b, i: (b * steps + i, 0)),
                  pl.BlockSpec((None, A_W, d), lambda b, i: (layer, SRC_QA // A_W, 0),
                               pipeline_mode=pl.Buffered(1)),
                  pl.BlockSpec((None, A_W, d), lambda b, i: (layer, SRC_VA // A_W, 0),
                               pipeline_mode=pl.Buffered(1))],
        out_specs=[pl.BlockSpec((None, n_sub, A_W, t), lambda b, i: (b, i, 0, 0)),
                   pl.BlockSpec((None, n_sub, vt_rows, t), lambda b, i: (b, i, 0, 0))],
        out_shape=[jax.ShapeDtypeStruct((bsz, nt, A_W, t), BF16),
                   jax.ShapeDtypeStruct((bsz, nt, vt_rows, t), BF16)],
        scratch_shapes=[pltpu.VMEM((2 * A_W, d), BF16)],
        compiler_params=_cparams(2),
        name="moba_qv_proj",
    )(h2d, w_in_t, w_in_t)


V_AUG_ROWS = 16


def _v_aug_rows(t, dtype):
    return jnp.where(lax.broadcasted_iota(jnp.int32, (V_AUG_ROWS, t), 0) == 0, 1.0, 0.0).astype(dtype)


def _softmax_init(m_s, acc_s):
    m_s[...] = jnp.full(m_s.shape, -jnp.inf, F32)
    acc_s[...] = jnp.zeros(acc_s.shape, F32)


def _softmax_step_t(s, vt_blk, g, m_s, acc_s):
    m_prev = m_s[g]
    m_new = jnp.maximum(m_prev, jnp.max(s, axis=0, keepdims=True))
    alpha = jnp.exp2(m_prev - m_new)
    p = jnp.exp2((s - m_new).astype(BF16))
    acc_s[g] = alpha * acc_s[g] + jnp.dot(vt_blk, p, preferred_element_type=F32)
    m_s[g] = m_new


TILES_PER_STEP = 2


def _attn_scratch(g_heads, t, dv):
    n_state = TILES_PER_STEP * g_heads
    state = [pltpu.VMEM((n_state, 1, t), F32), pltpu.VMEM((n_state, dv + V_AUG_ROWS, t), F32)]
    return state + [pltpu.VMEM((2, t, t), F32)] * g_heads


def _tile_pair_specs(nt, block_shape, index_of_tile):
    even = pl.BlockSpec(block_shape, lambda x, y, p: index_of_tile(x, y, 2 * p))
    odd = pl.BlockSpec(block_shape, lambda x, y, p: index_of_tile(x, y, nt - 1 - 2 * p))
    return even, odd


def _skewed_blocks(n_heads, lead, n_uniform, uniform_scores, consume, finish, carry_in=None, drain=True):
    def stage(score, fold):
        for g in range(n_heads):
            score((g,))
            fold((g,))

    if carry_in is None:
        lead[0][0](0)
        pend_blk, pend_slot = lead[0][1], 0
    else:
        prev_blk, prev_slot, prev_consume, prev_finish = carry_in
        stage(lambda hs: lead[0][0](1 - prev_slot, hs), lambda hs: prev_consume(prev_blk, prev_slot, hs))
        prev_finish()
        pend_blk, pend_slot = lead[0][1], 1 - prev_slot
    for fn, blk in lead[1:]:
        stage(functools.partial(fn, 1 - pend_slot), functools.partial(consume, pend_blk, pend_slot))
        pend_blk, pend_slot = blk, 1 - pend_slot
    a, b = pend_slot, 1 - pend_slot

    def pair(first, pend):
        stage(functools.partial(uniform_scores, first, b), functools.partial(consume, pend, a))
        stage(functools.partial(uniform_scores, first + 1, a), functools.partial(consume, first, b))
        return first + 1

    def two_pairs(jj, pend):
        return pair(4 * jj + 2, pair(4 * jj, pend))

    n_quads = n_uniform // 4
    pend = lax.fori_loop(0, n_quads, two_pairs, pend_blk)
    rest = n_uniform - 4 * n_quads
    has_pair = rest >= 2
    first = 4 * n_quads

    @pl.when(has_pair)
    def _last_pair():
        pair(first, pend)

    pend = jnp.where(has_pair, first + 1, pend)
    if not drain:
        return pend, a, consume, finish
    last = jnp.where(has_pair, first + 2, first)

    @pl.when(rest % 2 == 1)
    def _odd_tail():
        stage(functools.partial(uniform_scores, last, b), functools.partial(consume, pend, a))
        consume(last, b)
        finish()

    @pl.when(rest % 2 == 0)
    def _even_tail():
        consume(pend, a)
        finish()


def _attn_finish(z_ref, o_ref, acc_s, g_heads, dv, state0):
    for g in range(g_heads):
        out_t = acc_s[state0 + g, :dv, :] / acc_s[state0 + g, dv:dv + 1, :]
        o_ref[:, g * dv:(g + 1) * dv] = (out_t.T * _silu(z_ref[:, g * dv:(g + 1) * dv].astype(F32))
                                         ).astype(o_ref.dtype)


def _t5_thresholds():
    max_exact = T5_BUCKETS // 2
    d = np.arange(0, 4 * T5_MAX_DIST, dtype=np.int64)
    dd = np.maximum(d, 1).astype(np.float32)
    large = max_exact + (np.log(dd / np.float32(max_exact)) / np.float32(math.log(T5_MAX_DIST / max_exact))
                         * np.float32(T5_BUCKETS - max_exact)).astype(np.int32)
    large = np.minimum(large, T5_BUCKETS - 1)
    bucket = np.where(d < max_exact, d, large)
    assert np.all(np.diff(bucket) >= 0) and bucket[-1] == T5_BUCKETS - 1
    return [int(np.argmax(bucket >= k)) for k in range(T5_BUCKETS)]


_T5_THRESH = _t5_thresholds()

MASK_BIG = 1e30


def _moba_body(t5_ref, qt_even_ref, qt_odd_ref, k_ref, vt_ref, z_even_ref, z_odd_ref, o_even_ref, o_odd_ref,
               kmask_s, km_s, bias_own_s, bias_prev_s, qa_s, m_s, acc_s, *s_refs, n_blocks, g_heads):
    hg = pl.program_id(0)
    b = pl.program_id(1)
    i = pl.program_id(2)
    t, hd = A_BLOCK, A_HEAD_DIM
    hv = hd + V_AUG_ROWS
    seq = n_blocks * t
    nb_pad = -(-n_blocks // SUBLANES) * SUBLANES

    @pl.when((hg == 0) & (b == 0) & (i == 0))
    def _mask_columns():
        key_blk = lax.broadcasted_iota(jnp.int32, (seq, LANES), 0) // t
        lane = lax.broadcasted_iota(jnp.int32, (seq, LANES), 1)
        kmask_s[...] = jnp.where(lane == key_blk, -MASK_BIG, 0.0).astype(kmask_s.dtype)

    @pl.when((b == 0) & (i == 0))
    def _build_bias():
        key = lax.broadcasted_iota(jnp.int32, (t, t), 0)
        qry = lax.broadcasted_iota(jnp.int32, (t, t), 1)
        for g in range(g_heads):
            head = hg * g_heads + g
            far = t5_ref[T5_BUCKETS - 1, head]
            for dist, dst in ((qry - key, bias_own_s), (t + qry - key, bias_prev_s)):
                bias = jnp.full((t, t), t5_ref[0, head], F32)
                for kk in range(1, T5_BUCKETS):
                    bias = jnp.where(dist >= _T5_THRESH[kk], t5_ref[kk, head], bias)
                dst[g] = (bias - far) * LOG2E

    @pl.when(i == 0)
    def _new_sequence():
        for g in range(g_heads):
            km_s[g] = jnp.zeros(km_s.shape[1:], km_s.dtype)
            for nb in range(n_blocks):
                mean = jnp.mean(k_ref[nb * t:(nb + 1) * t, g * hd:(g + 1) * hd].astype(F32), axis=0, keepdims=True)
                hi = mean.astype(BF16)
                km_s[g, nb:nb + 1, :] = hi
                km_s[g, LANES + nb:LANES + nb + 1, :] = (mean - hi.astype(F32)).astype(BF16)

    _softmax_init(m_s, acc_s)
    blk_row = lax.broadcasted_iota(jnp.int32, (nb_pad, t), 0)

    def query_tile(tile, qt_ref, z_ref, o_ref, which):
        state0 = which * g_heads

        past = blk_row < tile
        for g in range(g_heads):
            qt = qt_ref[g * hd:(g + 1) * hd, :]
            g2 = jnp.dot(km_s[g], qt, preferred_element_type=F32)
            gate = g2[:nb_pad, :] + g2[LANES:LANES + nb_pad, :]
            cur = jnp.where(past, gate, NEG)
            keep = blk_row == tile
            for _ in range(A_TOPK):
                mx = jnp.max(cur, axis=0, keepdims=True)
                idx = jnp.min(jnp.where(cur == mx, blk_row, nb_pad), axis=0, keepdims=True)
                pick = blk_row == idx
                keep = keep | (pick & past)
                cur = jnp.where(pick, -jnp.inf, cur)
            unsel = jnp.where(keep, 0.0, 1.0)
            qa_s[state0 + g, :hd, :] = qt
            qa_s[state0 + g, hd:, :] = jnp.concatenate([unsel, jnp.zeros((LANES - nb_pad, t), F32)],
                                                       axis=0).astype(qa_s.dtype)

        def raw_scores(g, j):
            start = pl.multiple_of(j * t, t)
            keys = jnp.concatenate([k_ref[pl.ds(start, t), g * hd:(g + 1) * hd], kmask_s[pl.ds(start, t), :]],
                                   axis=1)
            return jnp.dot(keys, qa_s[state0 + g], preferred_element_type=F32)

        def own_scores(slot, heads=range(g_heads)):
            key = lax.broadcasted_iota(jnp.int32, (t, t), 0)
            qry = lax.broadcasted_iota(jnp.int32, (t, t), 1)
            for g in heads:
                s_refs[g][slot] = jnp.where(key <= qry, raw_scores(g, tile) + bias_own_s[g], NEG)

        j_prev = jnp.maximum(tile - 1, 0)
        no_prev = jnp.where(tile == 0, -MASK_BIG, 0.0)

        def prev_scores(slot, heads=range(g_heads)):
            for g in heads:
                s_refs[g][slot] = raw_scores(g, j_prev) + (bias_prev_s[g] + no_prev)

        def far_scores(j, slot, heads=range(g_heads)):
            for g in heads:
                s_refs[g][slot] = raw_scores(g, j)

        def consume(j, slot, heads=range(g_heads)):
            for g in heads:
                _softmax_step_t(s_refs[g][slot], vt_ref[j, g * hv:(g + 1) * hv, :], state0 + g, m_s, acc_s)

        finish = functools.partial(_attn_finish, z_ref, o_ref, acc_s, g_heads, hd, state0)
        return g_heads, [(own_scores, tile), (prev_scores, j_prev)], j_prev, far_scores, consume, finish

    carry = _skewed_blocks(*query_tile(n_blocks - 1 - 2 * i, qt_odd_ref, z_odd_ref, o_odd_ref, 1), drain=False)
    _skewed_blocks(*query_tile(2 * i, qt_even_ref, z_even_ref, o_even_ref, 0), carry_in=carry)


def moba_branch(qt, vt, slab, t5_table, g_heads=ATTN_HEADS_PER_STEP):
    bsz, nb, _, t = qt.shape
    s = nb * t
    hd = A_HEAD_DIM
    gw = g_heads * hd
    assert t == A_BLOCK and nb <= LANES and nb % TILES_PER_STEP == 0 and A_HEADS % g_heads == 0
    assert OFF_KA % gw == 0 and OFF_ZA % gw == 0
    assert _T5_THRESH[T5_BUCKETS - 1] <= t + 1
    ck, cz = OFF_KA // gw, OFF_ZA // gw
    qt_even, qt_odd = _tile_pair_specs(nb, (None, None, gw, t), lambda h, b, tile: (b, tile, h, 0))
    z_even, z_odd = _tile_pair_specs(nb, (None, t, gw), lambda h, b, tile: (b, tile, cz + h))
    out = jax.ShapeDtypeStruct((bsz, nb // 2, t, A_W), BF16)
    return pl.pallas_call(
        functools.partial(_moba_body, n_blocks=nb, g_heads=g_heads),
        grid=(A_HEADS // g_heads, bsz, nb // 2),
        in_specs=[pl.BlockSpec(memory_space=pltpu.SMEM),
                  qt_even, qt_odd,
                  pl.BlockSpec((None, s, gw), lambda h, b, i: (b, 0, ck + h)),
                  pl.BlockSpec((None, nb, g_heads * (hd + V_AUG_ROWS), t), lambda h, b, i: (b, 0, h, 0)),
                  z_even, z_odd],
        out_specs=[pl.BlockSpec((None, None, t, gw), lambda h, b, i: (b, i, 0, h)),
                   pl.BlockSpec((None, None, t, gw), lambda h, b, i: (b, nb // 2 - 1 - i, 0, h))],
        out_shape=[out, out],
        scratch_shapes=[pltpu.VMEM((s, LANES), BF16),
                        pltpu.VMEM((g_heads, 2 * LANES, hd), BF16),
                        pltpu.VMEM((g_heads, t, t), F32),
                        pltpu.VMEM((g_heads, t, t), F32),
                        pltpu.VMEM((TILES_PER_STEP * g_heads, ATTN_QK_PAD, t), BF16)]
        + _attn_scratch(g_heads, t, hd),
        compiler_params=_cparams(3),
        name="moba",
    )(t5_table, qt, qt, slab, vt, slab, slab)


MLSTM_CHUNK = 256
MLSTM_SEQS_PER_STEP = 1
CONV_HALO = 8


def _log_sigmoid(t):
    return jnp.minimum(t, 0.0) - jnp.log1p(jnp.exp(-jnp.abs(t)))


def _mlstm_body(qk_ref, v_ref, ob_ref, zb_ref, if_ref, cw_ref, cb_ref, gb_ref, gn_ref, o_ref,
                xe_s, c_s, n_s, m_s):
    c = pl.program_id(1)

    @pl.when(c == 0)
    def _reset():
        xe_s[...] = jnp.zeros_like(xe_s)
        c_s[...] = jnp.zeros_like(c_s)
        n_s[...] = jnp.zeros_like(n_s)
        m_s[...] = jnp.zeros_like(m_s)

    for bi in range(qk_ref.shape[0]):
        _mlstm_chunk(bi, qk_ref, v_ref, ob_ref, zb_ref, if_ref, cw_ref, cb_ref, gb_ref, gn_ref, o_ref,
                     xe_s, c_s, n_s, m_s)


def _mlstm_chunk(bi, qk_ref, v_ref, ob_ref, zb_ref, if_ref, cw_ref, cb_ref, gb_ref, gn_ref, o_ref,
                 xe_s, c_s, n_s, m_s):
    L = MLSTM_CHUNK
    dk, dv = B_QK_DIM, B_V_DIM

    x = qk_ref[bi].astype(F32)
    xe = jnp.concatenate([xe_s[bi], x], axis=0)
    conv = cb_ref[...] + cw_ref[B_CONV - 1:B_CONV, :] * x
    for j in range(B_CONV - 1):
        back = B_CONV - 1 - j
        conv = conv + cw_ref[j:j + 1, :] * pltpu.roll(xe, back, axis=0)[CONV_HALO:, :]
    xe_s[bi] = x[L - CONV_HALO:, :]
    qk = _silu(conv)

    gi = if_ref[bi].astype(F32) + gb_ref[...]
    lf = _log_sigmoid(gi)
    row = lax.broadcasted_iota(jnp.int32, (L, L), 0)
    col = lax.broadcasted_iota(jnp.int32, (L, L), 1)
    causal = col <= row
    tri = jnp.where(causal, 1.0, 0.0).astype(F32)
    b_cols = jnp.dot(tri, lf, precision=lax.Precision.HIGHEST, preferred_element_type=F32)
    b_rows = b_cols.T
    li_rows = gi.T

    for h in range(B_HEADS):
        q_h = qk[:, h * dk:(h + 1) * dk]
        k_h = qk[:, B_QK_W + h * dk:B_QK_W + (h + 1) * dk] * (dk ** -0.5)
        v_h = v_ref[bi, :, h * dv:(h + 1) * dv]
        b_c = b_cols[:, B_HEADS + h:B_HEADS + h + 1]
        b_r = b_rows[B_HEADS + h:B_HEADS + h + 1, :]
        li_c = gi[:, h:h + 1]
        li_r = li_rows[h:h + 1, :]
        m_prev = m_s[bi, h:h + 1, 0:1]
        c_prev = c_s[bi * B_HEADS + h]
        n_prev = n_s[bi, h:h + 1, :]

        a_c = b_c + m_prev
        dmat = jnp.where(causal, b_c - b_r + li_r, -jnp.inf)
        m_t = jnp.maximum(a_c, jnp.max(dmat, axis=-1, keepdims=True))
        w_inter = jnp.exp(a_c - m_t)
        q_b = q_h.astype(BF16)
        k_b = k_h.astype(BF16)
        sc = (lax.dot_general(q_b, k_b, _NT, preferred_element_type=F32)
              * jnp.exp(dmat - m_t))
        num = (w_inter * jnp.dot(q_b, c_prev.astype(BF16), preferred_element_type=F32)
               + jnp.dot(sc.astype(BF16), v_h, preferred_element_type=F32))
        den = (w_inter * jnp.sum(q_h * n_prev, axis=-1, keepdims=True)
               + jnp.sum(sc, axis=-1, keepdims=True))
        hh = num / jnp.maximum(jnp.abs(den), jnp.exp(-m_t))

        b_last = b_c[L - 1:L, :]
        g_c = b_last - b_c + li_c
        m_new = jnp.maximum(b_last + m_prev, jnp.max(g_c, axis=0, keepdims=True))
        decay = jnp.exp(b_last + m_prev - m_new)
        wk = jnp.exp(g_c - m_new) * k_h
        c_s[bi * B_HEADS + h] = decay * c_prev + lax.dot_general(
            wk.astype(BF16), v_h, (((0,), (0,)), ((), ())), preferred_element_type=F32)
        n_s[bi, h:h + 1, :] = decay * n_prev + jnp.sum(wk, axis=0, keepdims=True)
        m_s[bi, h:h + 1, :] = jnp.broadcast_to(m_new, (1, LANES))

        mu = jnp.mean(hh, axis=-1, keepdims=True)
        dlt = hh - mu
        var = jnp.mean(dlt * dlt, axis=-1, keepdims=True)
        y = dlt * lax.rsqrt(var + EPS) * gn_ref[:, h * dv:(h + 1) * dv]
        y = (y * jax.nn.sigmoid(ob_ref[bi, :, h * dv:(h + 1) * dv].astype(F32))
             * _silu(zb_ref[bi, :, h * dv:(h + 1) * dv].astype(F32)))
        o_ref[bi, :, h * dv:(h + 1) * dv] = y.astype(o_ref.dtype)


def mlstm_branch(slab, slab_small, conv_w, conv_b, i_bias, f_bias, out_norm, nb=MLSTM_SEQS_PER_STEP):
    bsz, s, _ = slab.shape
    L = MLSTM_CHUNK
    nb = math.gcd(nb, bsz)
    assert s % L == 0
    gate_bias = jnp.zeros((1, LANES), F32).at[0, :B_HEADS].set(i_bias).at[0, B_HEADS:2 * B_HEADS].set(f_bias)
    w2 = 2 * B_QK_W
    full = lambda shape: pl.BlockSpec(shape, lambda b, c: (0,) * len(shape))
    return pl.pallas_call(
        _mlstm_body,
        grid=(bsz // nb, s // L),
        in_specs=[pl.BlockSpec((nb, L, w2), lambda b, c: (b, c, OFF_QKB // w2)),
                  pl.BlockSpec((nb, L, B_V_W), lambda b, c: (b, c, OFF_VB // B_V_W)),
                  pl.BlockSpec((nb, L, B_V_W), lambda b, c: (b, c, OFF_OB // B_V_W)),
                  pl.BlockSpec((nb, L, B_V_W), lambda b, c: (b, c, OFF_ZB // B_V_W)),
                  pl.BlockSpec((nb, L, LANES), lambda b, c: (b, c, OFF_IF // LANES)),
                  full((B_CONV, w2)), full((1, w2)), full((1, LANES)), full((1, B_V_W))],
        out_specs=pl.BlockSpec((nb, L, B_V_W), lambda b, c: (b, c, 0)),
        out_shape=jax.ShapeDtypeStruct((bsz, s, B_V_W), BF16),
        scratch_shapes=[pltpu.VMEM((nb, CONV_HALO, w2), F32),
                        pltpu.VMEM((nb * B_HEADS, B_QK_DIM, B_V_DIM), F32),
                        pltpu.VMEM((nb, SUBLANES, B_QK_DIM), F32),
                        pltpu.VMEM((nb, SUBLANES, LANES), F32)],
        compiler_params=_cparams(2),
        name="mlstm",
    )(slab, slab, slab, slab, slab_small, conv_w, conv_b.reshape(1, w2), gate_bias,
      out_norm.reshape(1, B_V_W))


def _mla_prep_body(cq_ref, ckv_ref, kr_ref, qg_ref, kg_ref, wqt_ref, wqst_ref, wkn_ref, wvt_ref,
                   cos_ref, sin_ref, cost_ref, sint_ref, qt_ref, kf_ref, vt_ref):
    def normed(ref, g_ref):
        t = ref[...].astype(F32)
        y = t * lax.rsqrt(jnp.mean(t * t, axis=-1, keepdims=True) + EPS)
        return (y * g_ref[...]).astype(BF16)

    cqn = normed(cq_ref, qg_ref)
    ckvn = normed(ckv_ref, kg_ref)

    qt_main = lax.dot_general(wqt_ref[...], cqn, _NT, preferred_element_type=F32)
    qt_swap = lax.dot_general(wqst_ref[...], cqn, _NT, preferred_element_type=F32)
    cos_t = cost_ref[...]
    sin_t = sint_ref[...]
    t = ATTN_TILE
    n_sub = cos_t.shape[1] // t
    for h in range(C_HEADS):
        lo = h * ATTN_QK_PAD
        nope = qt_main[lo:lo + LANES, :].astype(qt_ref.dtype)
        rope = (qt_main[lo + LANES:lo + 2 * LANES, :] * cos_t
                + qt_swap[h * LANES:(h + 1) * LANES, :] * sin_t).astype(qt_ref.dtype)
        for c in range(n_sub):
            qt_ref[c, lo:lo + LANES, :] = nope[:, c * t:(c + 1) * t]
            qt_ref[c, lo + LANES:lo + 2 * LANES, :] = rope[:, c * t:(c + 1) * t]

    k_nope = jnp.dot(ckvn, wkn_ref[...], preferred_element_type=F32)
    kr = kr_ref[...].astype(F32)
    half = C_ROPE // 2
    lane = lax.broadcasted_iota(jnp.int32, kr.shape, 1)
    swapped = jnp.where(lane < half, -pltpu.roll(kr, LANES - half, axis=1), pltpu.roll(kr, half, axis=1))
    k_rot = (kr * cos_ref[...] + swapped * sin_ref[...]).astype(kf_ref.dtype)
    for h in range(C_HEADS):
        lo = h * ATTN_QK_PAD
        kf_ref[:, lo:lo + LANES] = k_nope[:, h * LANES:(h + 1) * LANES].astype(kf_ref.dtype)
        kf_ref[:, lo + LANES:lo + 2 * LANES] = k_rot

    v_t = lax.dot_general(wvt_ref[...], ckvn, _NT, preferred_element_type=F32)
    hv = C_V_DIM + V_AUG_ROWS
    aug = _v_aug_rows(t, vt_ref.dtype)
    for h in range(C_HEADS):
        v_h = v_t[h * C_V_DIM:(h + 1) * C_V_DIM, :].astype(vt_ref.dtype)
        for c in range(n_sub):
            vt_ref[c, h * hv:h * hv + C_V_DIM, :] = v_h[:, c * t:(c + 1) * t]
            vt_ref[c, h * hv + C_V_DIM:(h + 1) * hv, :] = aug


def mla_prep(slab, q_norm, kv_norm, w_uq, w_ukv, tm=1024):
    bsz, s, _ = slab.shape
    t = ATTN_TILE
    n_sub = tm // t
    vt_rows = C_HEADS * (C_V_DIM + V_AUG_ROWS)
    half = C_ROPE // 2
    scale = (C_NOPE + C_ROPE) ** -0.5 * LOG2E
    wq = (w_uq * scale).reshape(C_Q_RANK, C_HEADS, C_NOPE + C_ROPE)
    pad = jnp.zeros((C_Q_RANK, C_HEADS, LANES - C_ROPE), F32)
    x1, x2 = wq[..., C_NOPE:C_NOPE + half], wq[..., C_NOPE + half:]
    wqt = jnp.concatenate([wq, pad], axis=-1).reshape(C_Q_RANK, C_HEADS * ATTN_QK_PAD).T.astype(BF16)
    wqst = jnp.concatenate([-x2, x1, pad], axis=-1).reshape(C_Q_RANK, C_HEADS * LANES).T.astype(BF16)
    wkv = w_ukv.reshape(C_KV_RANK, C_HEADS, C_NOPE + C_V_DIM)
    wkn = wkv[..., :C_NOPE].reshape(C_KV_RANK, C_HEADS * C_NOPE).astype(BF16)
    wvt = wkv[..., C_NOPE:].reshape(C_KV_RANK, C_W).T.astype(BF16)

    pos = jnp.arange(s, dtype=jnp.int32)
    inv = ROPE_THETA ** (-jnp.arange(half, dtype=F32) / half)
    ang = pos.astype(F32)[:, None] * inv[None, :]
    zpad = jnp.zeros((s, LANES - C_ROPE), F32)
    cos_tab = jnp.concatenate([jnp.cos(ang), jnp.cos(ang), zpad], axis=-1)
    sin_tab = jnp.concatenate([jnp.sin(ang), jnp.sin(ang), zpad], axis=-1)

    nt = s // t
    full = lambda shape: pl.BlockSpec(shape, lambda b, i: (0,) * len(shape))
    return pl.pallas_call(
        _mla_prep_body,
        grid=(bsz, s // tm),
        in_specs=[pl.BlockSpec((None, tm, C_Q_RANK), lambda b, i: (b, i, OFF_CQ // C_Q_RANK)),
                  pl.BlockSpec((None, tm, C_KV_RANK), lambda b, i: (b, i, OFF_CKV // C_KV_RANK)),
                  pl.BlockSpec((None, tm, LANES), lambda b, i: (b, i, OFF_KR // LANES)),
                  full((1, C_Q_RANK)), full((1, C_KV_RANK)),
                  full(wqt.shape), full(wqst.shape), full(wkn.shape), full(wvt.shape),
                  pl.BlockSpec((tm, LANES), lambda b, i: (i, 0)),
                  pl.BlockSpec((tm, LANES), lambda b, i: (i, 0)),
                  pl.BlockSpec((LANES, tm), lambda b, i: (0, i)),
                  pl.BlockSpec((LANES, tm), lambda b, i: (0, i))],
        out_specs=[pl.BlockSpec((None, n_sub, C_HEADS * ATTN_QK_PAD, t), lambda b, i: (b, i, 0, 0)),
                   pl.BlockSpec((None, tm, C_HEADS * ATTN_QK_PAD), lambda b, i: (b, i, 0)),
                   pl.BlockSpec((None, n_sub, vt_rows, t), lambda b, i: (b, i, 0, 0))],
        out_shape=[jax.ShapeDtypeStruct((bsz, nt, C_HEADS * ATTN_QK_PAD, t), BF16),
                   jax.ShapeDtypeStruct((bsz, s, C_HEADS * ATTN_QK_PAD), BF16),
                   jax.ShapeDtypeStruct((bsz, nt, vt_rows, t), BF16)],
        compiler_params=_cparams(2),
        name="mla_prep",
    )(slab, slab, slab, q_norm.reshape(1, C_Q_RANK), kv_norm.reshape(1, C_KV_RANK),
      wqt, wqst, wkn, wvt, cos_tab, sin_tab, cos_tab.T, sin_tab.T)


def _mla_attn_body(qt_even_ref, qt_odd_ref, k_ref, vt_ref, z_even_ref, z_odd_ref, o_even_ref, o_odd_ref,
                   m_s, acc_s, *s_refs, g_heads, n_tiles):
    i = pl.program_id(2)
    t = ATTN_TILE
    dq, dv = ATTN_QK_PAD, C_V_DIM
    hv = dv + V_AUG_ROWS
    _softmax_init(m_s, acc_s)

    def query_tile(tile, qt_ref, z_ref, o_ref, which):
        state0 = which * g_heads

        def raw_scores(g, j):
            start = pl.multiple_of(j * t, t)
            return jnp.dot(k_ref[pl.ds(start, t), g * dq:(g + 1) * dq], qt_ref[g * dq:(g + 1) * dq, :],
                           preferred_element_type=F32)

        def diagonal_scores(slot, heads=range(g_heads)):
            key = lax.broadcasted_iota(jnp.int32, (t, t), 0)
            qry = lax.broadcasted_iota(jnp.int32, (t, t), 1)
            for g in heads:
                s_refs[g][slot] = jnp.where(key <= qry, raw_scores(g, tile), NEG)

        def past_scores(j, slot, heads=range(g_heads)):
            for g in heads:
                s_refs[g][slot] = raw_scores(g, j)

        def consume(j, slot, heads=range(g_heads)):
            for g in heads:
                _softmax_step_t(s_refs[g][slot], vt_ref[j, g * hv:(g + 1) * hv, :], state0 + g, m_s, acc_s)

        finish = functools.partial(_attn_finish, z_ref, o_ref, acc_s, g_heads, dv, state0)
        return g_heads, [(diagonal_scores, tile)], tile, past_scores, consume, finish

    carry = _skewed_blocks(*query_tile(2 * i, qt_even_ref, z_even_ref, o_even_ref, 0), drain=False)
    _skewed_blocks(*query_tile(n_tiles - 1 - 2 * i, qt_odd_ref, z_odd_ref, o_odd_ref, 1), carry_in=carry)


def mla_attention(qt, kf, vt, slab, g_heads=ATTN_HEADS_PER_STEP):
    bsz, nt, _, t = qt.shape
    s = nt * t
    dq, dv = ATTN_QK_PAD, C_V_DIM
    assert t == ATTN_TILE and nt % TILES_PER_STEP == 0
    assert C_HEADS % g_heads == 0 and OFF_ZC % (g_heads * dv) == 0
    cz = OFF_ZC // (g_heads * dv)
    qt_even, qt_odd = _tile_pair_specs(nt, (None, None, g_heads * dq, t), lambda b, h, tile: (b, tile, h, 0))
    z_even, z_odd = _tile_pair_specs(nt, (None, t, g_heads * dv), lambda b, h, tile: (b, tile, cz + h))
    out = jax.ShapeDtypeStruct((bsz, nt // 2, t, C_W), BF16)
    return pl.pallas_call(
        functools.partial(_mla_attn_body, g_heads=g_heads, n_tiles=nt),
        grid=(bsz, C_HEADS // g_heads, nt // 2),
        in_specs=[qt_even, qt_odd,
                  pl.BlockSpec((None, s, g_heads * dq), lambda b, h, i: (b, 0, h),
                               pipeline_mode=pl.Buffered(1)),
                  pl.BlockSpec((None, nt, g_heads * (dv + V_AUG_ROWS), t), lambda b, h, i: (b, 0, h, 0)),
                  z_even, z_odd],
        out_specs=[pl.BlockSpec((None, None, t, g_heads * dv), lambda b, h, i: (b, i, 0, h)),
                   pl.BlockSpec((None, None, t, g_heads * dv), lambda b, h, i: (b, nt // 2 - 1 - i, 0, h))],
        out_shape=[out, out],
        scratch_shapes=_attn_scratch(g_heads, t, dv),
        compiler_params=_cparams(3),
        name="mla_attn",
    )(qt, qt, kf, vt, slab, slab)


def _merge_body(ya_even_ref, ya_odd_ref, yb_ref, yc_even_ref, yc_odd_ref, wa_ref, wb_ref, wc_ref,
                ga_ref, gb_ref, gc_ref, o_ref, w_s):
    @pl.when(pl.program_id(1) == 0)
    def _cast_weights():
        for n, w_ref in enumerate((wa_ref, wb_ref, wc_ref)):
            w_s[n] = w_ref[...].astype(w_s.dtype)

    def rows(even_ref, odd_ref):
        return jnp.concatenate([ref[p] for p in range(even_ref.shape[0]) for ref in (even_ref, odd_ref)], axis=0)

    ya = rows(ya_even_ref, ya_odd_ref)
    yc = rows(yc_even_ref, yc_odd_ref)
    acc = None
    for n, (y, g_ref) in enumerate(((ya, ga_ref), (yb_ref[...], gb_ref), (yc, gc_ref))):
        term = jax.nn.sigmoid(g_ref[...].astype(F32)) * jnp.dot(y, w_s[n], preferred_element_type=F32)
        acc = term if acc is None else acc + term
    o_ref[...] = acc.astype(o_ref.dtype)


def branch_merge(ya_tiles, yb, yc_tiles, w_branch, layer, slab2d, tm=1024, tn=1024):
    m, w = yb.shape
    d = w_branch.shape[-1]
    t = ATTN_TILE
    pairs = tm // (TILES_PER_STEP * t)
    assert OFF_GT % tn == 0 and pairs * TILES_PER_STEP * t == tm
    g0 = OFF_GT // tn
    gper = d // tn
    halves = [y.reshape(m // (TILES_PER_STEP * t), t, w) for y in (*ya_tiles, *yc_tiles)]
    half_spec = pl.BlockSpec((pairs, t, w), lambda j, i: (i, 0, 0))
    w_specs = [pl.BlockSpec((None, None, w, tn), functools.partial(lambda j, i, n: (layer, n, 0, j), n=n),
                            pipeline_mode=pl.Buffered(1))
               for n in range(N_BRANCH)]
    g_specs = [pl.BlockSpec((tm, tn), functools.partial(lambda j, i, n: (i, g0 + n * gper + j), n=n))
               for n in range(N_BRANCH)]
    return pl.pallas_call(
        _merge_body,
        grid=(d // tn, m // tm),
        in_specs=[half_spec, half_spec, pl.BlockSpec((tm, w), lambda j, i: (i, 0)), half_spec, half_spec]
        + w_specs + g_specs,
        out_specs=pl.BlockSpec((tm, tn), lambda j, i: (i, j)),
        out_shape=jax.ShapeDtypeStruct((m, d), BF16),
        scratch_shapes=[pltpu.VMEM((N_BRANCH, w, tn), BF16)],
        compiler_params=_cparams(2),
        name="branch_merge",
    )(halves[0], halves[1], yb, halves[2], halves[3], w_branch, w_branch, w_branch, slab2d, slab2d, slab2d)


def _out_body(mg_ref, w_ref, x_ref, g_ref, *refs, last):
    out_refs, w_s = refs[:-1], refs[-1]

    @pl.when(pl.program_id(0) == 0)
    def _cast_weights():
        w_s[...] = w_ref[...].astype(w_s.dtype)

    x_new = x_ref[...] + jnp.dot(mg_ref[...], w_s[...], preferred_element_type=F32)
    y = x_new * lax.rsqrt(jnp.mean(x_new * x_new, axis=-1, keepdims=True) + EPS) * g_ref[...]
    if last:
        out_refs[0][...] = y
    else:
        out_refs[0][...] = x_new
        out_refs[1][...] = y.astype(out_refs[1].dtype)


def out_projection(merged, w_out, layer, x2d, gain, last, tm=512):
    m, d = x2d.shape
    row = pl.BlockSpec((tm, d), lambda i: (i, 0))
    if last:
        out_specs, out_shape = row, jax.ShapeDtypeStruct((m, d), F32)
    else:
        out_specs = [row, row]
        out_shape = [jax.ShapeDtypeStruct((m, d), F32), jax.ShapeDtypeStruct((m, d), BF16)]
    return pl.pallas_call(
        functools.partial(_out_body, last=last),
        grid=(m // tm,),
        in_specs=[row,
                  pl.BlockSpec((None, d, d), lambda i: (layer, 0, 0), pipeline_mode=pl.Buffered(1)),
                  row, pl.BlockSpec((1, d), lambda i: (0, 0))],
        out_specs=out_specs,
        out_shape=out_shape,
        scratch_shapes=[pltpu.VMEM((d, d), BF16)],
        compiler_params=_cparams(1),
        name="out_proj",
    )(merged, w_out, x2d, gain.reshape(1, d))


def kernel(x, norm_gain, w_in, t5_table, mlstm_conv_w, mlstm_conv_b, mlstm_i_bias, mlstm_f_bias,
           mlstm_out_norm, mla_q_norm, mla_kv_norm, mla_w_uq, mla_w_ukv, w_branch, w_out, final_norm):
    bsz, s, d = x.shape
    m = bsz * s
    x2d = x.reshape(m, d)
    h = rmsnorm_rows(x2d, norm_gain[0])
    w_in_t = jnp.swapaxes(w_in, 1, 2)
    out = None
    for l in range(DEPTH):
        slab2d = input_projection(h, w_in_t, l, SLAB_SRC_ROWS)
        slab = slab2d.reshape(bsz, s, D_SLAB)
        slab_small = input_projection_small(h, w_in_t, l).reshape(bsz, s, D_SLAB_SMALL)
        qt_a, vt_a = moba_qv_projection(h, w_in_t, l, bsz)
        ya = moba_branch(qt_a, vt_a, slab, t5_table)
        yb = mlstm_branch(slab, slab_small, mlstm_conv_w[l], mlstm_conv_b[l], mlstm_i_bias[l], mlstm_f_bias[l],
                          mlstm_out_norm[l])
        qt_c, k_c, vt_c = mla_prep(slab_small, mla_q_norm[l], mla_kv_norm[l], mla_w_uq[l], mla_w_ukv[l])
        yc = mla_attention(qt_c, k_c, vt_c, slab)
        merged = branch_merge(ya, yb.reshape(m, B_V_W), yc, w_branch, l, slab2d)
        last = l == DEPTH - 1
        gain = final_norm if last else norm_gain[l + 1]
        res = out_projection(merged, w_out, l, x2d, gain, last)
        if last:
            out = res
        else:
            x2d, h = res
    return out.reshape(bsz, s, d)
```
